```python
import math
import jax, jax.numpy as jnp
from jax import lax
import numpy as np

D_MODEL = 2048
BATCH = 1
SEQ = 8192
DEPTH = 2

CHUNK = 64
Q_BLOCK = 128
HEAD_DIM = 128
N_BRANCH = 4
BRANCH_WIDTH = D_MODEL // N_BRANCH
N_HEADS = BRANCH_WIDTH // HEAD_DIM
LRU_WIDTH = BRANCH_WIDTH
LRU_BLOCKS = N_HEADS
LRU_BLOCK = LRU_WIDTH // LRU_BLOCKS
CONV_WIDTH = 4
LRU_C = 8.0
LOOKBACK_CHUNKS = 8
BAND_CHUNKS = LOOKBACK_CHUNKS + 1
REL_CLIP = 256
REL_TABLE = REL_CLIP + CHUNK
D_FF = 4 * D_MODEL
ALPHA = (2.0 * DEPTH) ** 0.25
BETA = (8.0 * DEPTH) ** -0.25
LN_EPS = 1e-5

IN_SIZES = (
    BRANCH_WIDTH, BRANCH_WIDTH, BRANCH_WIDTH, N_HEADS,
    LRU_WIDTH, LRU_WIDTH,
    BRANCH_WIDTH, BRANCH_WIDTH, BRANCH_WIDTH,
    BRANCH_WIDTH, BRANCH_WIDTH, BRANCH_WIDTH,
)
D_IN = sum(IN_SIZES)

kernel_name = "chunk_causal_hybrid_fox_rglru_stickbreak_chunkattn"

F32 = jnp.float32


def layer_norm(x, g, b):
    xf = x.astype(F32)
    mu = jnp.mean(xf, axis=-1, keepdims=True)
    var = jnp.mean(jnp.square(xf - mu), axis=-1, keepdims=True)
    y = (xf - mu) * lax.rsqrt(var + LN_EPS) * g.astype(F32) + b.astype(F32)
    return y.astype(x.dtype)


def split_cols(u, sizes):
    outs, off = [], 0
    for n in sizes:
        outs.append(u[..., off:off + n])
        off += n
    return outs


def heads(t):
    b, s, _ = t.shape
    return t.reshape(b, s, N_HEADS, HEAD_DIM)


def fox_attention(q, k, v, f_logit):
    B, S, H, Dh = q.shape
    nb = S // Q_BLOCK
    cum_f = jnp.cumsum(jax.nn.log_sigmoid(f_logit.astype(F32)), axis=1)
    cum_f_k = cum_f.transpose(0, 2, 1)
    qb = q.reshape(B, nb, Q_BLOCK, H, Dh).transpose(1, 0, 2, 3, 4)
    fb = cum_f.reshape(B, nb, Q_BLOCK, H).transpose(1, 0, 3, 2)
    kpos = jnp.arange(S)
    scale = Dh ** -0.5

    def block(args):
        i, qi, fi = args
        s = jnp.einsum('bqhd,bkhd->bhqk', qi, k).astype(F32) * scale
        s = s + fi[..., :, None] - cum_f_k[:, :, None, :]
        qpos = i * Q_BLOCK + jnp.arange(Q_BLOCK)
        s = jnp.where(kpos[None, :] <= qpos[:, None], s, -jnp.inf)
        p = jax.nn.softmax(s, axis=-1)
        return jnp.einsum('bhqk,bkhd->bqhd', p.astype(v.dtype), v)

    out = lax.map(block, (jnp.arange(nb), qb, fb))
    return out.transpose(1, 0, 2, 3, 4).reshape(B, S, H * Dh)


def stick_breaking_attention(q, k, v):
    B, S, H, Dh = q.shape
    nb = S // Q_BLOCK
    qb = q.reshape(B, nb, Q_BLOCK, H, Dh).transpose(1, 0, 2, 3, 4)
    kpos = jnp.arange(S)
    scale = Dh ** -0.5

    def block(args):
        i, qi = args
        z = jnp.einsum('bqhd,bkhd->bhqk', qi, k).astype(F32) * scale
        qpos = i * Q_BLOCK + jnp.arange(Q_BLOCK)
        mask = kpos[None, :] < qpos[:, None]
        log_1m_beta = jnp.where(mask, jax.nn.log_sigmoid(-z), 0.0)
        later = lax.cumsum(log_1m_beta, axis=3, reverse=True) - log_1m_beta
        a = jnp.where(mask, jnp.exp(jax.nn.log_sigmoid(z) + later), 0.0)
        return jnp.einsum('bhqk,bkhd->bqhd', a.astype(v.dtype), v)

    out = lax.map(block, (jnp.arange(nb), qb))
    return out.transpose(1, 0, 2, 3, 4).reshape(B, S, H * Dh)


def chunk_band_attention(q, k, v, rel_bias):
    B, S, H, Dh = q.shape
    nc = S // CHUNK
    qc = q.reshape(B, nc, CHUNK, H, Dh)
    pad = ((0, 0), (LOOKBACK_CHUNKS, 0), (0, 0), (0, 0), (0, 0))
    kc = jnp.pad(k.reshape(B, nc, CHUNK, H, Dh), pad)
    vc = jnp.pad(v.reshape(B, nc, CHUNK, H, Dh), pad)
    kband = jnp.concatenate([kc[:, j:j + nc] for j in range(BAND_CHUNKS)], axis=2)
    vband = jnp.concatenate([vc[:, j:j + nc] for j in range(BAND_CHUNKS)], axis=2)
    kidx = jnp.arange(BAND_CHUNKS * CHUNK)
    dist = LOOKBACK_CHUNKS * CHUNK + jnp.arange(CHUNK)[:, None] - kidx[None, :]
    ridx = jnp.clip(dist, -(CHUNK - 1), REL_CLIP) + (CHUNK - 1)
    bias = rel_bias.astype(F32)[:, ridx]
    chunk_of_slot = jnp.arange(nc)[:, None] - LOOKBACK_CHUNKS + jnp.arange(BAND_CHUNKS)[None, :]
    valid = jnp.repeat(chunk_of_slot >= 0, CHUNK, axis=1)
    s = jnp.einsum('bcqhd,bckhd->bchqk', qc, kband).astype(F32) * (Dh ** -0.5)
    s = s + bias[None, None]
    s = jnp.where(valid[None, :, None, None, :], s, -jnp.inf)
    p = jax.nn.softmax(s, axis=-1)
    out = jnp.einsum('bchqk,bckhd->bcqhd', p.astype(v.dtype), vband)
    return out.reshape(B, S, H * Dh)


def recurrent_branch(xr, yr, conv_w, conv_b, w_r, b_r, w_i, b_i, lam):
    B, S, W = xr.shape
    xp = jnp.pad(xr, ((0, 0), (CONV_WIDTH - 1, 0), (0, 0)))
    xc = conv_b
    for j in range(CONV_WIDTH):
        xc = xc + xp[:, j:j + S] * conv_w[j]
    xg = xc.reshape(B, S, LRU_BLOCKS, LRU_BLOCK)
    r = jax.nn.sigmoid(jnp.einsum('bsnc,ncd->bsnd', xg, w_r).reshape(B, S, W) + b_r)
    gi = jax.nn.sigmoid(jnp.einsum('bsnc,ncd->bsnd', xg, w_i).reshape(B, S, W) + b_i)
    log_a = LRU_C * r.astype(F32) * jax.nn.log_sigmoid(lam.astype(F32))
    a = jnp.exp(log_a)
    inp = jnp.sqrt(-jnp.expm1(2.0 * log_a)) * (gi * xc).astype(F32)

    def combine(left, right):
        a1, b1 = left
        a2, b2 = right
        return a1 * a2, a2 * b1 + b2

    _, h = lax.associative_scan(combine, (a, inp), axis=1)
    return h.astype(xr.dtype) * jax.nn.gelu(yr)


def setup_inputs(seed: int = 0) -> dict:
    key = jax.random.key(seed)
    ks = jax.random.split(key, 24)
    L, D, Wb = DEPTH, D_MODEL, BRANCH_WIDTH

    def nrm(k, shape, scale):
        return jax.random.normal(k, shape, F32) * scale

    u = jax.random.uniform(ks[9], (L, LRU_WIDTH), F32, 0.9, 0.999)
    a0 = u ** (1.0 / LRU_C)
    lru_lambda = jnp.log(a0) - jnp.log1p(-a0)
    return {
        "x": nrm(ks[0], (BATCH, SEQ, D), 1.0),
        "ln_in_g": 1.0 + nrm(ks[1], (D,), 0.02),
        "ln_in_b": nrm(ks[2], (D,), 0.02),
        "w_in": nrm(ks[3], (L, D, D_IN), D ** -0.5),
        "b_forget": 3.0 + nrm(ks[4], (L, N_HEADS), 0.5),
        "conv_w": nrm(ks[5], (L, CONV_WIDTH, LRU_WIDTH), CONV_WIDTH ** -0.5),
        "conv_b": nrm(ks[6], (L, LRU_WIDTH), 0.02),
        "w_r": nrm(ks[7], (L, LRU_BLOCKS, LRU_BLOCK, LRU_BLOCK), LRU_BLOCK ** -0.5),
        "b_r": nrm(ks[8], (L, LRU_WIDTH), 0.02),
        "w_i": nrm(ks[10], (L, LRU_BLOCKS, LRU_BLOCK, LRU_BLOCK), LRU_BLOCK ** -0.5),
        "b_i": nrm(ks[11], (L, LRU_WIDTH), 0.02),
        "lru_lambda": lru_lambda,
        "rel_bias": nrm(ks[12], (L, N_HEADS, REL_TABLE), 0.1),
        "w_branch": nrm(ks[13], (L, N_BRANCH, Wb, D), Wb ** -0.5),
        "w_gate": nrm(ks[14], (L, N_BRANCH, D, D), D ** -0.5),
        "b_gate": nrm(ks[15], (L, N_BRANCH, D), 0.02),
        "w_out": nrm(ks[16], (L, D, D), BETA * D ** -0.5),
        "ln1_g": 1.0 + nrm(ks[17], (L, D), 0.02),
        "ln1_b": nrm(ks[18], (L, D), 0.02),
        "w_ff1": nrm(ks[19], (L, D, D_FF), D ** -0.5),
        "w_ff2": nrm(ks[20], (L, D_FF, D), BETA * D_FF ** -0.5),
        "ln2_g": 1.0 + nrm(ks[21], (L, D), 0.02),
        "ln2_b": nrm(ks[22], (L, D), 0.02),
    }


def reference(x, ln_in_g, ln_in_b, w_in, b_forget, conv_w, conv_b, w_r, b_r, w_i, b_i,
              lru_lambda, rel_bias, w_branch, w_gate, b_gate, w_out, ln1_g, ln1_b,
              w_ff1, w_ff2, ln2_g, ln2_b):
    x = layer_norm(x, ln_in_g, ln_in_b)
    for l in range(DEPTH):
        u = x @ w_in[l]
        (fq, fk, fv, ff, rx, ry, sq, sk, sv, cq, ck, cv) = split_cols(u, IN_SIZES)
        o_fox = fox_attention(heads(fq), heads(fk), heads(fv), ff + b_forget[l])
        o_lru = recurrent_branch(rx, ry, conv_w[l], conv_b[l], w_r[l], b_r[l],
                                 w_i[l], b_i[l], lru_lambda[l])
        o_sb = stick_breaking_attention(heads(sq), heads(sk), heads(sv))
        o_ch = chunk_band_attention(heads(cq), heads(ck), heads(cv), rel_bias[l])
        merged = None
        for g, o in enumerate((o_fox, o_lru, o_sb, o_ch)):
            gate = jax.nn.sigmoid(x @ w_gate[l, g] + b_gate[l, g])
            term = gate * (o @ w_branch[l, g])
            merged = term if merged is None else merged + term
        x = layer_norm(ALPHA * x + merged @ w_out[l], ln1_g[l], ln1_b[l])
        hid = jnp.square(jax.nn.relu(x @ w_ff1[l]))
        x = layer_norm(ALPHA * x + hid @ w_ff2[l], ln2_g[l], ln2_b[l])
    return x
```

```python
import functools
import math

import jax
import jax.numpy as jnp
from jax import lax
from jax.experimental import pallas as pl
from jax.experimental.pallas import tpu as pltpu

F32 = jnp.float32
BF16 = jnp.bfloat16

D_MODEL = 2048
SEQ = 8192
DEPTH = 2
CHUNK = 64
HEAD_DIM = 128
N_BRANCH = 4
BRANCH_WIDTH = D_MODEL // N_BRANCH
N_HEADS = BRANCH_WIDTH // HEAD_DIM
CONV_WIDTH = 4
LRU_C = 8.0
LOOKBACK_CHUNKS = 8
REL_CLIP = 256
D_FF = 4 * D_MODEL
ALPHA = (2.0 * DEPTH) ** 0.25
LN_EPS = 1e-5
QK_SCALE = HEAD_DIM ** -0.5

_OFF_FQ = 0
_OFF_FK = _OFF_FQ + BRANCH_WIDTH
_OFF_FV = _OFF_FK + BRANCH_WIDTH
_OFF_FF = _OFF_FV + BRANCH_WIDTH
_OFF_RX = _OFF_FF + N_HEADS
_OFF_RY = _OFF_RX + BRANCH_WIDTH
_OFF_SQ = _OFF_RY + BRANCH_WIDTH
_OFF_CQ = _OFF_SQ + 3 * BRANCH_WIDTH
_OFF_END = _OFF_CQ + 3 * BRANCH_WIDTH

LANES = 128
SUBLANES = 8
NEG_BIG = -1e30
SB_DEAD_LOG = -120.0

VMEM_LIMIT = 56 * 1024 * 1024


def _cparams(sem, vmem=VMEM_LIMIT):
    return pltpu.CompilerParams(dimension_semantics=sem, vmem_limit_bytes=vmem)


def _log_sigmoid(x):
    return jnp.minimum(x, 0.0) - jnp.log1p(jnp.exp(-jnp.abs(x)))


def _layer_norm_rows(y, g, b):
    mu = jnp.mean(y, axis=-1, keepdims=True)
    d = y - mu
    var = jnp.mean(d * d, axis=-1, keepdims=True)
    return d * lax.rsqrt(var + LN_EPS) * g + b


def _ln_kernel(x_ref, g_ref, b_ref, of_ref, ob_ref):
    y = _layer_norm_rows(x_ref[...], g_ref[...], b_ref[...])
    of_ref[...] = y
    ob_ref[...] = y.astype(BF16)


def _entry_ln(x, g, b, tm=512):
    s, d = x.shape
    return pl.pallas_call(
        _ln_kernel,
        grid=(s // tm,),
        in_specs=[pl.BlockSpec((tm, d), lambda i: (i, 0)),
                  pl.BlockSpec((1, d), lambda i: (0, 0)),
                  pl.BlockSpec((1, d), lambda i: (0, 0))],
        out_specs=[pl.BlockSpec((tm, d), lambda i: (i, 0)),
                   pl.BlockSpec((tm, d), lambda i: (i, 0))],
        out_shape=[jax.ShapeDtypeStruct((s, d), F32),
                   jax.ShapeDtypeStruct((s, d), BF16)],
        compiler_params=_cparams(("parallel",)),
        name="entry_ln",
    )(x, g.reshape(1, d), b.reshape(1, d))


def _proj_kernel(x_ref, w_ref, s_ref, o_ref):
    acc = jnp.dot(x_ref[...], w_ref[...], preferred_element_type=F32)
    o_ref[...] = (acc * s_ref[...]).astype(o_ref.dtype)


def _project(xb, w, colscale, out_dtype, tm, tn, name):
    m, k = xb.shape
    n = w.shape[1]
    return pl.pallas_call(
        _proj_kernel,
        grid=(m // tm, n // tn),
        in_specs=[pl.BlockSpec((tm, k), lambda i, j: (i, 0)),
                  pl.BlockSpec((k, tn), lambda i, j: (0, j)),
                  pl.BlockSpec((1, tn), lambda i, j: (0, j))],
        out_specs=pl.BlockSpec((tm, tn), lambda i, j: (i, j)),
        out_shape=jax.ShapeDtypeStruct((m, n), out_dtype),
        compiler_params=_cparams(("parallel", "arbitrary")),
        name=name,
    )(xb, w, colscale)


def _forget_cumsum_kernel(f_ref, b_ref, o_ref):
    rows = f_ref.shape[0]
    per_head = rows // N_HEADS
    ls = _log_sigmoid(f_ref[...] + b_ref[...])
    r = lax.broadcasted_iota(jnp.int32, (LANES, LANES), 0)
    c = lax.broadcasted_iota(jnp.int32, (LANES, LANES), 1)
    upper = (r <= c).astype(F32)
    within = jnp.dot(ls, upper, preferred_element_type=F32,
                     precision=lax.Precision.HIGHEST)
    total = within[:, LANES - 1:LANES]
    rr = lax.broadcasted_iota(jnp.int32, (rows, rows), 0)
    cc = lax.broadcasted_iota(jnp.int32, (rows, rows), 1)
    head_start = rr - (rr & (per_head - 1))
    before = ((cc >= head_start) & (cc < rr)).astype(F32)
    offs = jnp.dot(before, jnp.broadcast_to(total, (rows, LANES)),
                   preferred_element_type=F32, precision=lax.Precision.HIGHEST)
    o_ref[...] = within + offs


def _forget_cumsum(f_rows, b_rows):
    rows = f_rows.shape[0]
    return pl.pallas_call(
        _forget_cumsum_kernel,
        out_shape=jax.ShapeDtypeStruct((rows, LANES), F32),
        name="forget_cumsum",
    )(f_rows, b_rows)


def _fox_kernel(q_ref, k_ref, v_ref, cfq_ref, cfk_ref, o_ref, m_sc, l_sc, acc_sc, *, tq, tk):
    i = pl.program_id(0)
    j = pl.program_id(1)

    @pl.when(j == 0)
    def _init():
        m_sc[...] = jnp.full(m_sc.shape, NEG_BIG, F32)
        l_sc[...] = jnp.zeros(l_sc.shape, F32)
        acc_sc[...] = jnp.zeros(acc_sc.shape, F32)

    @pl.when(j <= i)
    def _compute():
        qpos = i * tq + lax.broadcasted_iota(jnp.int32, (tq, tk), 0)
        kpos = j * tk + lax.broadcasted_iota(jnp.int32, (tq, tk), 1)
        keep = kpos <= qpos
        for h in range(N_HEADS):
            hs = slice(h * HEAD_DIM, (h + 1) * HEAD_DIM)
            s = lax.dot_general(q_ref[:, hs], k_ref[:, hs], (((1,), (1,)), ((), ())),
                                preferred_element_type=F32)
            s = s + (cfq_ref[:, h:h + 1] - cfk_ref[h:h + 1, :])
            s = jnp.where(keep, s, NEG_BIG)
            m_old = m_sc[h]
            m_new = jnp.maximum(m_old, jnp.max(s, axis=-1, keepdims=True))
            alpha = jnp.exp(m_old - m_new)
            p = jnp.exp(s - m_new)
            l_sc[h] = alpha * l_sc[h] + jnp.sum(p, axis=-1, keepdims=True)
            acc_sc[:, hs] = alpha * acc_sc[:, hs] + jnp.dot(
                p.astype(BF16), v_ref[:, hs], preferred_element_type=F32)
            m_sc[h] = m_new

    @pl.when(j == i)
    def _finish():
        for h in range(N_HEADS):
            hs = slice(h * HEAD_DIM, (h + 1) * HEAD_DIM)
            o_ref[:, hs] = (acc_sc[:, hs] / l_sc[h]).astype(o_ref.dtype)


def _fox_attention(qkv, cfq, cfk, col0, tq=512):
    s = qkv.shape[0]
    tk = tq
    nq = s // tq
    w = BRANCH_WIDTH
    kern = functools.partial(_fox_kernel, tq=tq, tk=tk)
    return pl.pallas_call(
        kern,
        grid=(nq, nq),
        in_specs=[pl.BlockSpec((tq, w), lambda i, j: (i, col0)),
                  pl.BlockSpec((tk, w), lambda i, j: (jnp.minimum(j, i), col0 + 1)),
                  pl.BlockSpec((tk, w), lambda i, j: (jnp.minimum(j, i), col0 + 2)),
                  pl.BlockSpec((tq, SUBLANES), lambda i, j: (i, 0)),
                  pl.BlockSpec((SUBLANES, tk), lambda i, j: (0, jnp.minimum(j, i)))],
        out_specs=pl.BlockSpec((tq, w), lambda i, j: (i, 0)),
        out_shape=jax.ShapeDtypeStruct((s, w), BF16),
        scratch_shapes=[pltpu.VMEM((N_HEADS, tq, 1), F32),
                        pltpu.VMEM((N_HEADS, tq, 1), F32),
                        pltpu.VMEM((tq, w), F32)],
        compiler_params=_cparams(("parallel", "arbitrary")),
        name="fox_attention",
    )(qkv, qkv, qkv, cfq, cfk)


def _sb_kernel(q_ref, k_ref, v_ref, o_ref, r_sc, acc_sc, *, tq, tk):
    i = pl.program_id(0)
    j = pl.program_id(1)
    kb = i - j
    nsub = tk // LANES

    @pl.when(j == 0)
    def _init():
        r_sc[...] = jnp.zeros(r_sc.shape, F32)
        acc_sc[...] = jnp.zeros(acc_sc.shape, F32)

    @pl.when(j <= i)
    def _compute():
        jj = lax.broadcasted_iota(jnp.int32, (LANES, LANES), 0)
        ss = lax.broadcasted_iota(jnp.int32, (LANES, LANES), 1)
        tri = (jj > ss).astype(BF16)
        qpos = i * tq + lax.broadcasted_iota(jnp.int32, (tq, LANES), 0)
        kcol = lax.broadcasted_iota(jnp.int32, (tq, LANES), 1)
        for h in range(N_HEADS):
            hs = slice(h * HEAD_DIM, (h + 1) * HEAD_DIM)

            @pl.when(jnp.max(r_sc[h]) > SB_DEAD_LOG)
            def _head():
                q = q_ref[:, hs]
                run = r_sc[h]
                acc = acc_sc[:, hs]
                for c in range(nsub - 1, -1, -1):
                    ks = slice(c * LANES, (c + 1) * LANES)
                    z = lax.dot_general(q, k_ref[ks, hs], (((1,), (1,)), ((), ())),
                                        preferred_element_type=F32)
                    keep = (kb * tk + c * LANES + kcol) < qpos
                    lp = _log_sigmoid(z)
                    ln = jnp.where(keep, lp - z, 0.0)
                    ln_hi = ln.astype(BF16)
                    ln_lo = (ln - ln_hi.astype(F32)).astype(BF16)
                    later = (jnp.dot(ln_hi, tri, preferred_element_type=F32)
                             + jnp.dot(ln_lo, tri, preferred_element_type=F32))
                    a = jnp.where(keep, jnp.exp(lp + later + run), 0.0)
                    acc = acc + jnp.dot(a.astype(BF16), v_ref[ks, hs],
                                        preferred_element_type=F32)
                    run = run + later[:, 0:1] + ln[:, 0:1]
                r_sc[h] = run
                acc_sc[:, hs] = acc

    @pl.when(j == i)
    def _finish():
        o_ref[...] = acc_sc[...].astype(o_ref.dtype)


def _sb_attention(qkv, col0, tq=512):
    s = qkv.shape[0]
    tk = tq
    nq = s // tq
    w = BRANCH_WIDTH
    kern = functools.partial(_sb_kernel, tq=tq, tk=tk)
    return pl.pallas_call(
        kern,
        grid=(nq, nq),
        in_specs=[pl.BlockSpec((tq, w), lambda i, j: (i, col0)),
                  pl.BlockSpec((tk, w), lambda i, j: (jnp.maximum(i - j, 0), col0 + 1)),
                  pl.BlockSpec((tk, w), lambda i, j: (jnp.maximum(i - j, 0), col0 + 2))],
        out_specs=pl.BlockSpec((tq, w), lambda i, j: (i, 0)),
        out_shape=jax.ShapeDtypeStruct((s, w), BF16),
        scratch_shapes=[pltpu.VMEM((N_HEADS, tq, 1), F32),
                        pltpu.VMEM((tq, w), F32)],
        compiler_params=_cparams(("parallel", "arbitrary")),
        name="sb_attention",
    )(qkv, qkv, qkv)


def _band_kernel(q_ref, k2_ref, k1_ref, k0_ref, v2_ref, v1_ref, v0_ref, bias_ref, o_ref, *, tq):
    i = pl.program_id(0)
    col = lax.broadcasted_iota(jnp.int32, (tq, tq), 1)
    k_refs = (k2_ref, k1_ref, k0_ref)
    v_refs = (v2_ref, v1_ref, v0_ref)
    for h in range(N_HEADS):
        hs = slice(h * HEAD_DIM, (h + 1) * HEAD_DIM)
        q = q_ref[:, hs]
        scores = []
        for p in range(3):
            s = lax.dot_general(q, k_refs[p][:, hs], (((1,), (1,)), ((), ())),
                                preferred_element_type=F32)
            s = s + bias_ref[h, p]
            if p < 2:
                s = jnp.where((i - 2 + p) * tq + col >= 0, s, NEG_BIG)
            scores.append(s)
        m = jnp.maximum(jnp.maximum(jnp.max(scores[0], axis=-1, keepdims=True),
                                    jnp.max(scores[1], axis=-1, keepdims=True)),
                        jnp.max(scores[2], axis=-1, keepdims=True))
        l = jnp.zeros((tq, 1), F32)
        acc = jnp.zeros((tq, HEAD_DIM), F32)
        for p in range(3):
            e = jnp.exp(scores[p] - m)
            l = l + jnp.sum(e, axis=-1, keepdims=True)
            acc = acc + jnp.dot(e.astype(BF16), v_refs[p][:, hs], preferred_element_type=F32)
        o_ref[:, hs] = (acc / l).astype(o_ref.dtype)


def _band_bias_table(rel_bias_l, tq):
    t = jnp.arange(tq)[:, None]
    s = jnp.arange(3 * tq)[None, :] - 2 * tq
    dist = t - s
    in_band = ((t // CHUNK) - (s // CHUNK) <= LOOKBACK_CHUNKS) & ((s // CHUNK) <= (t // CHUNK))
    ridx = jnp.clip(dist, -(CHUNK - 1), REL_CLIP) + (CHUNK - 1)
    table = jnp.where(in_band[None], rel_bias_l.astype(F32)[:, ridx], NEG_BIG)
    return table.reshape(N_HEADS, tq, 3, tq).transpose(0, 2, 1, 3)


def _band_attention(qkv, bias_table, col0, tq=256):
    s = qkv.shape[0]
    w = BRANCH_WIDTH
    kern = functools.partial(_band_kernel, tq=tq)

    def kv_spec(back, col):
        return pl.BlockSpec((tq, w), lambda i: (jnp.maximum(i - back, 0), col))

    return pl.pallas_call(
        kern,
        grid=(s // tq,),
        in_specs=[pl.BlockSpec((tq, w), lambda i: (i, col0)),
                  kv_spec(2, col0 + 1), kv_spec(1, col0 + 1), kv_spec(0, col0 + 1),
                  kv_spec(2, col0 + 2), kv_spec(1, col0 + 2), kv_spec(0, col0 + 2),
                  pl.BlockSpec((N_HEADS, 3, tq, tq), lambda i: (0, 0, 0, 0))],
        out_specs=pl.BlockSpec((tq, w), lambda i: (i, 0)),
        out_shape=jax.ShapeDtypeStruct((s, w), BF16),
        compiler_params=_cparams(("parallel",)),
        name="band_attention",
    )(qkv, qkv, qkv, qkv, qkv, qkv, qkv, bias_table)


def _gelu_tanh(x):
    c = math.sqrt(2.0 / math.pi)
    return 0.5 * x * (1.0 + jnp.tanh(c * (x + 0.044715 * (x * x * x))))


def _lru_kernel(rx_ref, ry_ref, cw_ref, cb_ref, wr_ref, br_ref, wi_ref, bi_ref, lam_ref,
                o_ref, xext_sc, a_sc, b_sc, h_sc, carry_sc, *, tm):
    i = pl.program_id(0)
    w = BRANCH_WIDTH
    halo = SUBLANES

    @pl.when(i == 0)
    def _first():
        xext_sc[0:halo, :] = jnp.zeros((halo, w), F32)
        carry_sc[...] = jnp.zeros(carry_sc.shape, F32)

    @pl.when(i > 0)
    def _shift_halo():
        xext_sc[0:halo, :] = xext_sc[tm:tm + halo, :]

    xext_sc[halo:halo + tm, :] = rx_ref[...]
    xc = cb_ref[...] + jnp.zeros((tm, w), F32)
    for t in range(CONV_WIDTH):
        start = halo - (CONV_WIDTH - 1) + t
        xc = xc + xext_sc[start:start + tm, :] * cw_ref[t:t + 1, :]

    xcb = xc.astype(BF16)
    r_parts, i_parts = [], []
    for n in range(N_HEADS):
        ns = slice(n * HEAD_DIM, (n + 1) * HEAD_DIM)
        r_parts.append(jnp.dot(xcb[:, ns], wr_ref[n], preferred_element_type=F32))
        i_parts.append(jnp.dot(xcb[:, ns], wi_ref[n], preferred_element_type=F32))
    r = jax.nn.sigmoid(jnp.concatenate(r_parts, axis=1) + br_ref[...])
    gi = jax.nn.sigmoid(jnp.concatenate(i_parts, axis=1) + bi_ref[...])
    log_a = LRU_C * r * _log_sigmoid(lam_ref[...])
    a = jnp.exp(log_a)
    a_sc[...] = a
    b_sc[...] = jnp.sqrt(-jnp.tanh(log_a) * (a * a + 1.0)) * (gi * xc)

    row = lax.broadcasted_iota(jnp.int32, (SUBLANES, w), 0)

    def group(g, carry):
        r0 = pl.multiple_of(g * SUBLANES, SUBLANES)
        a = a_sc[pl.ds(r0, SUBLANES), :]
        b = b_sc[pl.ds(r0, SUBLANES), :]
        for k in (1, 2, 4):
            a_prev = pltpu.roll(a, k, axis=0)
            b_prev = pltpu.roll(b, k, axis=0)
            ok = row >= k
            b = jnp.where(ok, a * b_prev + b, b)
            a = jnp.where(ok, a * a_prev, a)
        hgrp = a * carry + b
        h_sc[pl.ds(r0, SUBLANES), :] = hgrp
        return jnp.broadcast_to(hgrp[SUBLANES - 1:SUBLANES, :], (SUBLANES, w))

    carry_sc[...] = lax.fori_loop(0, tm // SUBLANES, group, carry_sc[...])
    o_ref[...] = (h_sc[...] * _gelu_tanh(ry_ref[...])).astype(o_ref.dtype)


def _recurrent_branch(uf, conv_w, conv_b, w_r, b_r, w_i, b_i, lam, tm=512):
    s = uf.shape[0]
    w = BRANCH_WIDTH
    kern = functools.partial(_lru_kernel, tm=tm)
    row = lambda v: v.reshape(1, w)
    full2 = lambda shape: pl.BlockSpec(shape, lambda i: (0, 0))
    full3 = lambda shape: pl.BlockSpec(shape, lambda i: (0, 0, 0))
    return pl.pallas_call(
        kern,
        grid=(s // tm,),
        in_specs=[pl.BlockSpec((tm, w), lambda i: (i, 0)),
                  pl.BlockSpec((tm, w), lambda i: (i, 1)),
                  full2((CONV_WIDTH, w)), full2((1, w)),
                  full3((N_HEADS, HEAD_DIM, HEAD_DIM)), full2((1, w)),
                  full3((N_HEADS, HEAD_DIM, HEAD_DIM)), full2((1, w)),
                  full2((1, w))],
        out_specs=pl.BlockSpec((tm, w), lambda i: (i, 0)),
        out_shape=jax.ShapeDtypeStruct((s, w), BF16),
        scratch_shapes=[pltpu.VMEM((tm + 2 * SUBLANES, w), F32),
                        pltpu.VMEM((tm, w), F32),
                        pltpu.VMEM((tm, w), F32),
                        pltpu.VMEM((tm, w), F32),
                        pltpu.VMEM((SUBLANES, w), F32)],
        compiler_params=_cparams(("arbitrary",)),
        name="recurrent_branch",
    )(uf, uf, conv_w, row(conv_b), w_r.astype(BF16), row(b_r), w_i.astype(BF16), row(b_i), row(lam))


def _merge_kernel(x_ref, o0_ref, o1_ref, o2_ref, o3_ref, wg_ref, bg_ref, wb_ref, out_ref):
    x = x_ref[...]
    merged = None
    for g, o_ref in enumerate((o0_ref, o1_ref, o2_ref, o3_ref)):
        gate = jax.nn.sigmoid(jnp.dot(x, wg_ref[g], preferred_element_type=F32) + bg_ref[g:g + 1, :])
        term = gate * jnp.dot(o_ref[...], wb_ref[g], preferred_element_type=F32)
        merged = term if merged is None else merged + term
    out_ref[...] = merged.astype(out_ref.dtype)


def _merge(xb, branches, wg, bg, wb, tm=1024, tn=256):
    s, d = xb.shape
    w = BRANCH_WIDTH
    o_spec = pl.BlockSpec((tm, w), lambda i, j: (i, 0))
    return pl.pallas_call(
        _merge_kernel,
        grid=(s // tm, d // tn),
        in_specs=[pl.BlockSpec((tm, d), lambda i, j: (i, 0)),
                  o_spec, o_spec, o_spec, o_spec,
                  pl.BlockSpec((N_BRANCH, d, tn), lambda i, j: (0, 0, j)),
                  pl.BlockSpec((N_BRANCH, tn), lambda i, j: (0, j)),
                  pl.BlockSpec((N_BRANCH, w, tn), lambda i, j: (0, 0, j))],
        out_specs=pl.BlockSpec((tm, tn), lambda i, j: (i, j)),
        out_shape=jax.ShapeDtypeStruct((s, d), BF16),
        compiler_params=_cparams(("parallel", "arbitrary")),
        name="gated_merge",
    )(xb, *branches, wg, bg, wb)


def _outproj_kernel(m_ref, w_ref, x_ref, g_ref, b_ref, of_ref, ob_ref):
    y = ALPHA * x_ref[...] + jnp.dot(m_ref[...], w_ref[...], preferred_element_type=F32)
    y = _layer_norm_rows(y, g_ref[...], b_ref[...])
    of_ref[...] = y
    ob_ref[...] = y.astype(BF16)


def _outproj_ln(merged, w_out, x, g, b, tm=512):
    s, d = x.shape
    row_spec = pl.BlockSpec((tm, d), lambda i: (i, 0))
    vec_spec = pl.BlockSpec((1, d), lambda i: (0, 0))
    return pl.pallas_call(
        _outproj_kernel,
        grid=(s // tm,),
        in_specs=[row_spec, pl.BlockSpec((d, d), lambda i: (0, 0)), row_spec, vec_spec, vec_spec],
        out_specs=[row_spec, row_spec],
        out_shape=[jax.ShapeDtypeStruct((s, d), F32), jax.ShapeDtypeStruct((s, d), BF16)],
        compiler_params=_cparams(("parallel",)),
        name="outproj_ln",
    )(merged, w_out, x, g.reshape(1, d), b.reshape(1, d))


def _ffn_kernel(xb_ref, xf_ref, w1_ref, w2_ref, g_ref, b_ref, of_ref, ob_ref, acc_sc):
    f = pl.program_id(1)

    @pl.when(f == 0)
    def _init():
        acc_sc[...] = jnp.zeros(acc_sc.shape, F32)

    hid = jnp.maximum(jnp.dot(xb_ref[...], w1_ref[...], preferred_element_type=F32), 0.0)
    hid = (hid * hid).astype(BF16)
    acc_sc[...] += jnp.dot(hid, w2_ref[...], preferred_element_type=F32)

    @pl.when(f == pl.num_programs(1) - 1)
    def _finish():
        y = _layer_norm_rows(ALPHA * xf_ref[...] + acc_sc[...], g_ref[...], b_ref[...])
        of_ref[...] = y
        ob_ref[...] = y.astype(BF16)


def _ffn_ln(xb, xf, w1, w2, g, b, tm=512, tf=1024):
    s, d = xf.shape
    dff = w1.shape[1]
    row_spec = pl.BlockSpec((tm, d), lambda i, f: (i, 0))
    vec_spec = pl.BlockSpec((1, d), lambda i, f: (0, 0))
    return pl.pallas_call(
        _ffn_kernel,
        grid=(s // tm, dff // tf),
        in_specs=[row_spec, row_spec,
                  pl.BlockSpec((d, tf), lambda i, f: (0, f)),
                  pl.BlockSpec((tf, d), lambda i, f: (f, 0)),
                  vec_spec, vec_spec],
        out_specs=[row_spec, row_spec],
        out_shape=[jax.ShapeDtypeStruct((s, d), F32), jax.ShapeDtypeStruct((s, d), BF16)],
        scratch_shapes=[pltpu.VMEM((tm, d), F32)],
        compiler_params=_cparams(("parallel", "arbitrary")),
        name="ffn_ln",
    )(xb, xf, w1, w2, g.reshape(1, d), b.reshape(1, d))


def _split_in_proj(w):
    w = w.astype(BF16)
    qkv = jnp.concatenate([w[:, _OFF_FQ:_OFF_FF], w[:, _OFF_SQ:_OFF_END]], axis=1)
    pad = jnp.zeros((w.shape[0], LANES - N_HEADS), BF16)
    rest = jnp.concatenate([w[:, _OFF_RX:_OFF_SQ], w[:, _OFF_FF:_OFF_RX], pad], axis=1)
    return qkv, rest


def kernel(x, ln_in_g, ln_in_b, w_in, b_forget, conv_w, conv_b, w_r, b_r, w_i, b_i, lru_lambda,
           rel_bias, w_branch, w_gate, b_gate, w_out, ln1_g, ln1_b, w_ff1, w_ff2, ln2_g, ln2_b):
    batch, s, d = x.shape
    assert (batch, s, d) == (1, SEQ, D_MODEL)
    w = BRANCH_WIDTH
    band_tq = 256

    q_cols = jnp.zeros((9 * w,), F32).at[0:w].set(1.0).at[3 * w:4 * w].set(1.0).at[6 * w:7 * w].set(1.0)
    qkv_scale = jnp.where(q_cols > 0, QK_SCALE, 1.0).reshape(1, 9 * w).astype(F32)
    rest_scale = jnp.ones((1, 2 * w + LANES), F32)

    xf, xb = _entry_ln(x.reshape(s, d), ln_in_g, ln_in_b)
    for l in range(DEPTH):
        w_qkv, w_rest = _split_in_proj(w_in[l])
        qkv = _project(xb, w_qkv, qkv_scale, BF16, 1024, 512, "in_proj_qkv")
        uf = _project(xb, w_rest, rest_scale, F32, 1024, 384, "in_proj_rest")

        f_rows = uf[:, 2 * w:2 * w + N_HEADS].T.reshape(N_HEADS * (s // LANES), LANES)
        b_rows = jnp.repeat(b_forget[l].astype(F32), s // LANES).reshape(-1, 1)
        cf = _forget_cumsum(f_rows, b_rows).reshape(N_HEADS, s)
        cfk = jnp.pad(cf, ((0, SUBLANES - N_HEADS), (0, 0)))
        cfq = cfk.T

        o_fox = _fox_attention(qkv, cfq, cfk, 0)
        o_sb = _sb_attention(qkv, 3)
        o_ch = _band_attention(qkv, _band_bias_table(rel_bias[l], band_tq), 6, band_tq)
        o_lru = _recurrent_branch(uf, conv_w[l], conv_b[l], w_r[l], b_r[l], w_i[l], b_i[l],
                                  lru_lambda[l])

        merged = _merge(xb, (o_fox, o_lru, o_sb, o_ch), w_gate[l].astype(BF16), b_gate[l],
                        w_branch[l].astype(BF16))
        xf, xb = _outproj_ln(merged, w_out[l].astype(BF16), xf, ln1_g[l], ln1_b[l])
        xf, xb = _ffn_ln(xb, xf, w_ff1[l].astype(BF16), w_ff2[l].astype(BF16), ln2_g[l], ln2_b[l])
    return xf.reshape(batch, s, d)
```

```python
import functools
import math

import jax
import jax.numpy as jnp
import numpy as np
from jax import lax
from jax.experimental import pallas as pl
from jax.experimental.pallas import tpu as pltpu

F32 = jnp.float32
BF16 = jnp.bfloat16

D_MODEL = 2048
SEQ = 8192
DEPTH = 2
CHUNK = 64
HEAD_DIM = 128
N_BRANCH = 4
BRANCH_WIDTH = D_MODEL // N_BRANCH
N_HEADS = BRANCH_WIDTH // HEAD_DIM
CONV_WIDTH = 4
LRU_C = 8.0
LOOKBACK_CHUNKS = 8
REL_CLIP = 256
D_FF = 4 * D_MODEL
ALPHA = (2.0 * DEPTH) ** 0.25
LN_EPS = 1e-5
QK_SCALE = HEAD_DIM ** -0.5
LOG2_E = math.log2(math.e)

_OFF_FQ = 0
_OFF_FK = _OFF_FQ + BRANCH_WIDTH
_OFF_FV = _OFF_FK + BRANCH_WIDTH
_OFF_FF = _OFF_FV + BRANCH_WIDTH
_OFF_RX = _OFF_FF + N_HEADS
_OFF_RY = _OFF_RX + BRANCH_WIDTH
_OFF_SQ = _OFF_RY + BRANCH_WIDTH
_OFF_CQ = _OFF_SQ + 3 * BRANCH_WIDTH
_OFF_END = _OFF_CQ + 3 * BRANCH_WIDTH

LANES = 128
SUBLANES = 8
NEG_BIG = -1e30
SB_DEAD_LOG = -120.0

VMEM_LIMIT = 56 * 1024 * 1024


def _cparams(sem, vmem=VMEM_LIMIT):
    return pltpu.CompilerParams(dimension_semantics=sem, vmem_limit_bytes=vmem)


def _log_sigmoid(x):
    return jnp.minimum(x, 0.0) - jnp.log1p(jnp.exp(-jnp.abs(x)))


def _layer_norm_rows(y, g, b):
    mu = jnp.mean(y, axis=-1, keepdims=True)
    d = y - mu
    var = jnp.mean(d * d, axis=-1, keepdims=True)
    return d * lax.rsqrt(var + LN_EPS) * g + b


def _ln_kernel(x_ref, g_ref, b_ref, of_ref, ob_ref):
    y = _layer_norm_rows(x_ref[...], g_ref[...], b_ref[...])
    of_ref[...] = y
    ob_ref[...] = y.astype(BF16)


def _entry_ln(x, g, b, tm=512):
    s, d = x.shape
    return pl.pallas_call(
        _ln_kernel,
        grid=(s // tm,),
        in_specs=[pl.BlockSpec((tm, d), lambda i: (i, 0)),
                  pl.BlockSpec((1, d), lambda i: (0, 0)),
                  pl.BlockSpec((1, d), lambda i: (0, 0))],
        out_specs=[pl.BlockSpec((tm, d), lambda i: (i, 0)),
                   pl.BlockSpec((tm, d), lambda i: (i, 0))],
        out_shape=[jax.ShapeDtypeStruct((s, d), F32),
                   jax.ShapeDtypeStruct((s, d), BF16)],
        compiler_params=_cparams(("parallel",)),
        name="entry_ln",
    )(x, g.reshape(1, d), b.reshape(1, d))


def _proj_kernel(x_ref, w_ref, s_ref, o_ref):
    acc = jnp.dot(x_ref[...], w_ref[...], preferred_element_type=F32)
    o_ref[...] = (acc * s_ref[...]).astype(o_ref.dtype)


def _project(xb, w_all, layer, colscale, out_dtype, tm, tn, name):
    m, k = xb.shape
    n = w_all.shape[2]
    return pl.pallas_call(
        _proj_kernel,
        grid=(m // tm, n // tn),
        in_specs=[pl.BlockSpec((tm, k), lambda i, j: (i, 0)),
                  pl.BlockSpec((None, k, tn), lambda i, j: (layer, 0, j)),
                  pl.BlockSpec((1, tn), lambda i, j: (0, j))],
        out_specs=pl.BlockSpec((tm, tn), lambda i, j: (i, j)),
        out_shape=jax.ShapeDtypeStruct((m, n), out_dtype),
        compiler_params=_cparams(("parallel", "arbitrary")),
        name=name,
    )(xb, w_all, colscale)


def _forget_cumsum_kernel(f_ref, b_ref, o_ref):
    rows = f_ref.shape[0]
    per_head = rows // N_HEADS
    ls = _log_sigmoid(f_ref[...] + b_ref[...])
    r = lax.broadcasted_iota(jnp.int32, (LANES, LANES), 0)
    c = lax.broadcasted_iota(jnp.int32, (LANES, LANES), 1)
    upper = (r <= c).astype(F32)
    within = jnp.dot(ls, upper, preferred_element_type=F32,
                     precision=lax.Precision.HIGHEST)
    total = within[:, LANES - 1:LANES]
    rr = lax.broadcasted_iota(jnp.int32, (rows, rows), 0)
    cc = lax.broadcasted_iota(jnp.int32, (rows, rows), 1)
    head_start = rr - (rr & (per_head - 1))
    before = ((cc >= head_start) & (cc < rr)).astype(F32)
    offs = jnp.dot(before, jnp.broadcast_to(total, (rows, LANES)),
                   preferred_element_type=F32, precision=lax.Precision.HIGHEST)
    o_ref[...] = (within + offs) * LOG2_E


def _forget_cumsum(f_rows, b_rows):
    rows = f_rows.shape[0]
    return pl.pallas_call(
        _forget_cumsum_kernel,
        out_shape=jax.ShapeDtypeStruct((rows, LANES), F32),
        name="forget_cumsum",
    )(f_rows, b_rows)


def _fox_kernel(q_ref, k_ref, v_ref, cfk_ref, o_ref, m_sc, acc_sc, vaug_sc, *, tq, tk):
    i = pl.program_id(0)
    m_sc[...] = jnp.full(m_sc.shape, NEG_BIG, F32)
    acc_sc[...] = jnp.zeros(acc_sc.shape, F32)
    vaug_sc[:, :, HEAD_DIM:] = jnp.ones((N_HEADS, tk, HEAD_DIM), BF16)
    rep = tk // LANES

    def key_tile(j, masked):
        k0 = pl.multiple_of(j * tk, tk)
        if masked:
            keep = (lax.broadcasted_iota(jnp.int32, (tq, tk), 1)
                    <= lax.broadcasted_iota(jnp.int32, (tq, tk), 0))
        for h in range(N_HEADS):
            hs = slice(h * HEAD_DIM, (h + 1) * HEAD_DIM)
            s = lax.dot_general(q_ref[:, hs], k_ref[pl.ds(k0, tk), hs], (((1,), (1,)), ((), ())),
                                preferred_element_type=F32)
            s = s - cfk_ref[h:h + 1, pl.ds(k0, tk)]
            if masked:
                s = jnp.where(keep, s, NEG_BIG)
            m_old = m_sc[h]
            m_new = jnp.maximum(m_old, jnp.max(s, axis=-1, keepdims=True))
            alpha = jnp.exp2(m_old - m_new)
            p = jnp.exp2(s - pltpu.repeat(m_new, rep, axis=1))
            vaug_sc[h, :, :HEAD_DIM] = v_ref[pl.ds(k0, tk), hs]
            pv = jnp.dot(p.astype(BF16), vaug_sc[h], preferred_element_type=F32)
            acc_sc[h] = pltpu.repeat(alpha, 2, axis=1) * acc_sc[h] + pv
            m_sc[h] = m_new

    def off_diagonal(j, carry):
        key_tile(j, False)
        return carry

    lax.fori_loop(0, i, off_diagonal, 0)
    key_tile(i, True)
    for h in range(N_HEADS):
        hs = slice(h * HEAD_DIM, (h + 1) * HEAD_DIM)
        o_ref[:, hs] = (acc_sc[h, :, :HEAD_DIM] / acc_sc[h, :, HEAD_DIM:]).astype(o_ref.dtype)


def _fox_attention(qkv, cfk, col0, tq=512):
    s = qkv.shape[0]
    tk = tq
    w = BRANCH_WIDTH
    kern = functools.partial(_fox_kernel, tq=tq, tk=tk)
    resident = pl.Buffered(1)
    return pl.pallas_call(
        kern,
        grid=(s // tq,),
        in_specs=[pl.BlockSpec((tq, w), lambda i: (i, col0)),
                  pl.BlockSpec((s, w), lambda i: (0, col0 + 1), pipeline_mode=resident),
                  pl.BlockSpec((s, w), lambda i: (0, col0 + 2), pipeline_mode=resident),
                  pl.BlockSpec((SUBLANES, s), lambda i: (0, 0), pipeline_mode=resident)],
        out_specs=pl.BlockSpec((tq, w), lambda i: (i, 0)),
        out_shape=jax.ShapeDtypeStruct((s, w), BF16),
        scratch_shapes=[pltpu.VMEM((N_HEADS, tq, LANES), F32),
                        pltpu.VMEM((N_HEADS, tq, 2 * HEAD_DIM), F32),
                        pltpu.VMEM((N_HEADS, tk, 2 * HEAD_DIM), BF16)],
        compiler_params=_cparams(("arbitrary",)),
        name="fox_attention",
    )(qkv, qkv, qkv, cfk)


def _sb_kernel(q_ref, k_ref, v_ref, o_ref, r_sc, acc_sc, *, tq, tk):
    i = pl.program_id(0)
    j = pl.program_id(1)
    kb = i - j
    nsub = tk // LANES

    @pl.when(j == 0)
    def _init():
        r_sc[...] = jnp.zeros(r_sc.shape, F32)
        acc_sc[...] = jnp.zeros(acc_sc.shape, F32)

    @pl.when(j <= i)
    def _compute():
        jj = lax.broadcasted_iota(jnp.int32, (LANES, LANES), 0)
        ss = lax.broadcasted_iota(jnp.int32, (LANES, LANES), 1)
        tri = (jj > ss).astype(BF16)
        qpos = i * tq + lax.broadcasted_iota(jnp.int32, (tq, LANES), 0)
        kcol = lax.broadcasted_iota(jnp.int32, (tq, LANES), 1)
        for h in range(N_HEADS):
            hs = slice(h * HEAD_DIM, (h + 1) * HEAD_DIM)

            @pl.when(jnp.max(r_sc[h]) > SB_DEAD_LOG)
            def _head():
                q = q_ref[:, hs]
                run = r_sc[h]
                acc = acc_sc[:, hs]
                for c in range(nsub - 1, -1, -1):
                    ks = slice(c * LANES, (c + 1) * LANES)
                    z = lax.dot_general(q, k_ref[ks, hs], (((1,), (1,)), ((), ())),
                                        preferred_element_type=F32)
                    keep = (kb * tk + c * LANES + kcol) < qpos
                    lp = _log_sigmoid(z)
                    ln = jnp.where(keep, lp - z, 0.0)
                    ln_hi = ln.astype(BF16)
                    ln_lo = (ln - ln_hi.astype(F32)).astype(BF16)
                    later = (jnp.dot(ln_hi, tri, preferred_element_type=F32)
                             + jnp.dot(ln_lo, tri, preferred_element_type=F32))
                    a = jnp.where(keep, jnp.exp(lp + later + run), 0.0)
                    acc = acc + jnp.dot(a.astype(BF16), v_ref[ks, hs],
                                        preferred_element_type=F32)
                    run = run + later[:, 0:1] + ln[:, 0:1]
                r_sc[h] = run
                acc_sc[:, hs] = acc

    @pl.when(j == i)
    def _finish():
        o_ref[...] = acc_sc[...].astype(o_ref.dtype)


def _sb_attention(qkv, col0, tq=512):
    s = qkv.shape[0]
    tk = tq
    nq = s // tq
    w = BRANCH_WIDTH
    kern = functools.partial(_sb_kernel, tq=tq, tk=tk)
    return pl.pallas_call(
        kern,
        grid=(nq, nq),
        in_specs=[pl.BlockSpec((tq, w), lambda i, j: (i, col0)),
                  pl.BlockSpec((tk, w), lambda i, j: (jnp.maximum(i - j, 0), col0 + 1)),
                  pl.BlockSpec((tk, w), lambda i, j: (jnp.maximum(i - j, 0), col0 + 2))],
        out_specs=pl.BlockSpec((tq, w), lambda i, j: (i, 0)),
        out_shape=jax.ShapeDtypeStruct((s, w), BF16),
        scratch_shapes=[pltpu.VMEM((N_HEADS, tq, 1), F32),
                        pltpu.VMEM((tq, w), F32)],
        compiler_params=_cparams(("parallel", "arbitrary")),
        name="sb_attention",
    )(qkv, qkv, qkv)


def _band_kernel(q_ref, k2_ref, k1_ref, k0_ref, v2_ref, v1_ref, v0_ref, ext_ref, o_ref, bias_sc, *, tq):
    i = pl.program_id(0)
    width = 4 * tq

    @pl.when(i == 0)
    def _build_bias():
        trow = lax.broadcasted_iota(jnp.int32, (tq, width), 0)
        t = lax.broadcasted_iota(jnp.int32, (tq, 3 * tq), 0)
        s = lax.broadcasted_iota(jnp.int32, (tq, 3 * tq), 1)
        shift = CHUNK.bit_length() - 1
        t_chunk = t >> shift
        s_chunk = (s >> shift) - (2 * tq) // CHUNK
        in_band = (t_chunk - s_chunk <= LOOKBACK_CHUNKS) & (s_chunk <= t_chunk)
        for h in range(N_HEADS):
            x = jnp.broadcast_to(ext_ref[h:h + 1, :], (tq, width))
            for b in range(tq.bit_length() - 1):
                x = jnp.where(((trow >> b) & 1) == 1, pltpu.roll(x, 1 << b, axis=1), x)
            bias_sc[h] = jnp.where(in_band, x[:, :3 * tq], NEG_BIG)

    col = lax.broadcasted_iota(jnp.int32, (tq, tq), 1)
    k_refs = (k2_ref, k1_ref, k0_ref)
    v_refs = (v2_ref, v1_ref, v0_ref)
    for h in range(N_HEADS):
        hs = slice(h * HEAD_DIM, (h + 1) * HEAD_DIM)
        q = q_ref[:, hs]
        scores = []
        for p in range(3):
            s = lax.dot_general(q, k_refs[p][:, hs], (((1,), (1,)), ((), ())),
                                preferred_element_type=F32)
            s = s + bias_sc[h, :, p * tq:(p + 1) * tq]
            if p < 2:
                s = jnp.where((i - 2 + p) * tq + col >= 0, s, NEG_BIG)
            scores.append(s)
        m = jnp.maximum(jnp.maximum(jnp.max(scores[0], axis=-1, keepdims=True),
                                    jnp.max(scores[1], axis=-1, keepdims=True)),
                        jnp.max(scores[2], axis=-1, keepdims=True))
        l = jnp.zeros((tq, 1), F32)
        acc = jnp.zeros((tq, HEAD_DIM), F32)
        for p in range(3):
            e = jnp.exp(scores[p] - m)
            l = l + jnp.sum(e, axis=-1, keepdims=True)
            acc = acc + jnp.dot(e.astype(BF16), v_refs[p][:, hs], preferred_element_type=F32)
        o_ref[:, hs] = (acc / l).astype(o_ref.dtype)


def _band_bias_vector(rel_bias_l, tq):
    n = np.arange(4 * tq)
    dist = np.where(n < 3 * tq, 2 * tq - n, 6 * tq - n)
    ridx = np.clip(dist, -(CHUNK - 1), REL_CLIP) + (CHUNK - 1)
    return rel_bias_l.astype(F32)[:, ridx]


def _band_attention(qkv, bias_ext, col0, tq=256):
    s = qkv.shape[0]
    w = BRANCH_WIDTH
    assert 2 * tq >= LOOKBACK_CHUNKS * CHUNK and tq % CHUNK == 0 and tq & (tq - 1) == 0
    kern = functools.partial(_band_kernel, tq=tq)

    def kv_spec(back, col):
        return pl.BlockSpec((tq, w), lambda i: (jnp.maximum(i - back, 0), col))

    return pl.pallas_call(
        kern,
        grid=(s // tq,),
        in_specs=[pl.BlockSpec((tq, w), lambda i: (i, col0)),
                  kv_spec(2, col0 + 1), kv_spec(1, col0 + 1), kv_spec(0, col0 + 1),
                  kv_spec(2, col0 + 2), kv_spec(1, col0 + 2), kv_spec(0, col0 + 2),
                  pl.BlockSpec((N_HEADS, 4 * tq), lambda i: (0, 0))],
        out_specs=pl.BlockSpec((tq, w), lambda i: (i, 0)),
        out_shape=jax.ShapeDtypeStruct((s, w), BF16),
        scratch_shapes=[pltpu.VMEM((N_HEADS, tq, 3 * tq), F32)],
        compiler_params=_cparams(("arbitrary",)),
        name="band_attention",
    )(qkv, qkv, qkv, qkv, qkv, qkv, qkv, bias_ext)


def _gelu_tanh(x):
    c = math.sqrt(2.0 / math.pi)
    return 0.5 * x * (1.0 + jnp.tanh(c * (x + 0.044715 * (x * x * x))))


def _lru_kernel(rx_ref, ry_ref, cw_ref, cb_ref, wr_ref, br_ref, wi_ref, bi_ref, lam_ref,
                o_ref, xext_sc, a_sc, b_sc, h_sc, carry_sc, *, tm):
    i = pl.program_id(0)
    w = BRANCH_WIDTH
    halo = SUBLANES

    @pl.when(i == 0)
    def _first():
        xext_sc[0:halo, :] = jnp.zeros((halo, w), F32)
        carry_sc[...] = jnp.zeros(carry_sc.shape, F32)

    @pl.when(i > 0)
    def _shift_halo():
        xext_sc[0:halo, :] = xext_sc[tm:tm + halo, :]

    xext_sc[halo:halo + tm, :] = rx_ref[...]
    xc = cb_ref[...] + jnp.zeros((tm, w), F32)
    for t in range(CONV_WIDTH):
        start = halo - (CONV_WIDTH - 1) + t
        xc = xc + xext_sc[start:start + tm, :] * cw_ref[t:t + 1, :]

    xcb = xc.astype(BF16)
    r_parts, i_parts = [], []
    for n in range(N_HEADS):
        ns = slice(n * HEAD_DIM, (n + 1) * HEAD_DIM)
        r_parts.append(jnp.dot(xcb[:, ns], wr_ref[n], preferred_element_type=F32))
        i_parts.append(jnp.dot(xcb[:, ns], wi_ref[n], preferred_element_type=F32))
    r = jax.nn.sigmoid(jnp.concatenate(r_parts, axis=1) + br_ref[...])
    gi = jax.nn.sigmoid(jnp.concatenate(i_parts, axis=1) + bi_ref[...])
    log_a = LRU_C * r * _log_sigmoid(lam_ref[...])
    a = jnp.exp(log_a)
    a_sc[...] = a
    b_sc[...] = jnp.sqrt(-jnp.tanh(log_a) * (a * a + 1.0)) * (gi * xc)

    row = lax.broadcasted_iota(jnp.int32, (SUBLANES, w), 0)

    def group(g, carry):
        r0 = pl.multiple_of(g * SUBLANES, SUBLANES)
        a = a_sc[pl.ds(r0, SUBLANES), :]
        b = b_sc[pl.ds(r0, SUBLANES), :]
        for k in (1, 2, 4):
            a_prev = pltpu.roll(a, k, axis=0)
            b_prev = pltpu.roll(b, k, axis=0)
            ok = row >= k
            b = jnp.where(ok, a * b_prev + b, b)
            a = jnp.where(ok, a * a_prev, a)
        hgrp = a * carry + b
        h_sc[pl.ds(r0, SUBLANES), :] = hgrp
        return jnp.broadcast_to(hgrp[SUBLANES - 1:SUBLANES, :], (SUBLANES, w))

    carry_sc[...] = lax.fori_loop(0, tm // SUBLANES, group, carry_sc[...])
    o_ref[...] = (h_sc[...] * _gelu_tanh(ry_ref[...])).astype(o_ref.dtype)


def _recurrent_branch(uf, conv_w, conv_b, w_r, b_r, w_i, b_i, lam, tm=512):
    s = uf.shape[0]
    w = BRANCH_WIDTH
    kern = functools.partial(_lru_kernel, tm=tm)
    row = lambda v: v.reshape(1, w)
    full2 = lambda shape: pl.BlockSpec(shape, lambda i: (0, 0))
    full3 = lambda shape: pl.BlockSpec(shape, lambda i: (0, 0, 0))
    return pl.pallas_call(
        kern,
        grid=(s // tm,),
        in_specs=[pl.BlockSpec((tm, w), lambda i: (i, 0)),
                  pl.BlockSpec((tm, w), lambda i: (i, 1)),
                  full2((CONV_WIDTH, w)), full2((1, w)),
                  full3((N_HEADS, HEAD_DIM, HEAD_DIM)), full2((1, w)),
                  full3((N_HEADS, HEAD_DIM, HEAD_DIM)), full2((1, w)),
                  full2((1, w))],
        out_specs=pl.BlockSpec((tm, w), lambda i: (i, 0)),
        out_shape=jax.ShapeDtypeStruct((s, w), BF16),
        scratch_shapes=[pltpu.VMEM((tm + 2 * SUBLANES, w), F32),
                        pltpu.VMEM((tm, w), F32),
                        pltpu.VMEM((tm, w), F32),
                        pltpu.VMEM((tm, w), F32),
                        pltpu.VMEM((SUBLANES, w), F32)],
        compiler_params=_cparams(("arbitrary",)),
        name="recurrent_branch",
    )(uf, uf, conv_w, row(conv_b), w_r.astype(BF16), row(b_r), w_i.astype(BF16), row(b_i), row(lam))


def _merge_kernel(x_ref, o0_ref, o1_ref, o2_ref, o3_ref, wg_ref, bg_ref, wb_ref, out_ref):
    x = x_ref[...]
    merged = None
    for g, o_ref in enumerate((o0_ref, o1_ref, o2_ref, o3_ref)):
        gate = jax.nn.sigmoid(jnp.dot(x, wg_ref[g], preferred_element_type=F32) + bg_ref[g:g + 1, :])
        term = gate * jnp.dot(o_ref[...], wb_ref[g], preferred_element_type=F32)
        merged = term if merged is None else merged + term
    out_ref[...] = merged.astype(out_ref.dtype)


def _merge(xb, branches, wg_all, bg_all, wb_all, layer, tm=1024, tn=256):
    s, d = xb.shape
    w = BRANCH_WIDTH
    o_spec = pl.BlockSpec((tm, w), lambda i, j: (i, 0))
    return pl.pallas_call(
        _merge_kernel,
        grid=(s // tm, d // tn),
        in_specs=[pl.BlockSpec((tm, d), lambda i, j: (i, 0)),
                  o_spec, o_spec, o_spec, o_spec,
                  pl.BlockSpec((None, N_BRANCH, d, tn), lambda i, j: (layer, 0, 0, j)),
                  pl.BlockSpec((None, N_BRANCH, tn), lambda i, j: (layer, 0, j)),
                  pl.BlockSpec((None, N_BRANCH, w, tn), lambda i, j: (layer, 0, 0, j))],
        out_specs=pl.BlockSpec((tm, tn), lambda i, j: (i, j)),
        out_shape=jax.ShapeDtypeStruct((s, d), BF16),
        compiler_params=_cparams(("parallel", "arbitrary")),
        name="gated_merge",
    )(xb, *branches, wg_all, bg_all, wb_all)


def _outproj_kernel(m_ref, w_ref, x_ref, g_ref, b_ref, of_ref, ob_ref):
    y = ALPHA * x_ref[...] + jnp.dot(m_ref[...], w_ref[...], preferred_element_type=F32)
    y = _layer_norm_rows(y, g_ref[...], b_ref[...])
    of_ref[...] = y
    ob_ref[...] = y.astype(BF16)


def _outproj_ln(merged, w_out_all, layer, x, g, b, tm=512):
    s, d = x.shape
    row_spec = pl.BlockSpec((tm, d), lambda i: (i, 0))
    vec_spec = pl.BlockSpec((1, d), lambda i: (0, 0))
    return pl.pallas_call(
        _outproj_kernel,
        grid=(s // tm,),
        in_specs=[row_spec, pl.BlockSpec((None, d, d), lambda i: (layer, 0, 0)), row_spec,
                  vec_spec, vec_spec],
        out_specs=[row_spec, row_spec],
        out_shape=[jax.ShapeDtypeStruct((s, d), F32), jax.ShapeDtypeStruct((s, d), BF16)],
        compiler_params=_cparams(("parallel",)),
        name="outproj_ln",
    )(merged, w_out_all, x, g.reshape(1, d), b.reshape(1, d))


def _ffn_kernel(xb_ref, xf_ref, w1_ref, w2_ref, g_ref, b_ref, of_ref, ob_ref, acc_sc):
    f = pl.program_id(1)

    @pl.when(f == 0)
    def _init():
        acc_sc[...] = jnp.zeros(acc_sc.shape, F32)

    hid = jnp.maximum(jnp.dot(xb_ref[...], w1_ref[...], preferred_element_type=F32), 0.0)
    hid = (hid * hid).astype(BF16)
    acc_sc[...] += jnp.dot(hid, w2_ref[...], preferred_element_type=F32)

    @pl.when(f == pl.num_programs(1) - 1)
    def _finish():
        y = _layer_norm_rows(ALPHA * xf_ref[...] + acc_sc[...], g_ref[...], b_ref[...])
        of_ref[...] = y
        ob_ref[...] = y.astype(BF16)


def _ffn_ln(xb, xf, w1_all, w2_all, layer, g, b, tm=512, tf=1024):
    s, d = xf.shape
    dff = w1_all.shape[2]
    row_spec = pl.BlockSpec((tm, d), lambda i, f: (i, 0))
    vec_spec = pl.BlockSpec((1, d), lambda i, f: (0, 0))
    return pl.pallas_call(
        _ffn_kernel,
        grid=(s // tm, dff // tf),
        in_specs=[row_spec, row_spec,
                  pl.BlockSpec((None, d, tf), lambda i, f: (layer, 0, f)),
                  pl.BlockSpec((None, tf, d), lambda i, f: (layer, f, 0)),
                  vec_spec, vec_spec],
        out_specs=[row_spec, row_spec],
        out_shape=[jax.ShapeDtypeStruct((s, d), F32), jax.ShapeDtypeStruct((s, d), BF16)],
        scratch_shapes=[pltpu.VMEM((tm, d), F32)],
        compiler_params=_cparams(("parallel", "arbitrary")),
        name="ffn_ln",
    )(xb, xf, w1_all, w2_all, g.reshape(1, d), b.reshape(1, d))


def _split_in_proj(w_in):
    qkv = jnp.concatenate([w_in[:, :, _OFF_FQ:_OFF_FF], w_in[:, :, _OFF_SQ:_OFF_END]], axis=2)
    pad = jnp.zeros(w_in.shape[:2] + (LANES - N_HEADS,), w_in.dtype)
    rest = jnp.concatenate([w_in[:, :, _OFF_RX:_OFF_SQ], w_in[:, :, _OFF_FF:_OFF_RX], pad], axis=2)
    return qkv.astype(BF16), rest.astype(BF16)


def kernel(x, ln_in_g, ln_in_b, w_in, b_forget, conv_w, conv_b, w_r, b_r, w_i, b_i, lru_lambda,
           rel_bias, w_branch, w_gate, b_gate, w_out, ln1_g, ln1_b, w_ff1, w_ff2, ln2_g, ln2_b):
    batch, s, d = x.shape
    assert (batch, s, d) == (1, SEQ, D_MODEL)
    w = BRANCH_WIDTH
    band_tq = 256

    col_scale = np.ones((1, 9 * w), np.float32)
    for q_block, scale in ((0, QK_SCALE * LOG2_E), (3, QK_SCALE), (6, QK_SCALE)):
        col_scale[:, q_block * w:(q_block + 1) * w] = scale
    qkv_scale = jnp.asarray(col_scale)
    rest_scale = jnp.ones((1, 2 * w + LANES), F32)

    w_qkv, w_rest = _split_in_proj(w_in)
    wg_b, wb_b, wo_b = w_gate.astype(BF16), w_branch.astype(BF16), w_out.astype(BF16)
    w1_b, w2_b = w_ff1.astype(BF16), w_ff2.astype(BF16)

    xf, xb = _entry_ln(x.reshape(s, d), ln_in_g, ln_in_b)
    for l in range(DEPTH):
        qkv = _project(xb, w_qkv, l, qkv_scale, BF16, 1024, 512, "in_proj_qkv")
        uf = _project(xb, w_rest, l, rest_scale, F32, 1024, 384, "in_proj_rest")

        f_rows = uf[:, 2 * w:2 * w + N_HEADS].T.reshape(N_HEADS * (s // LANES), LANES)
        b_rows = jnp.repeat(b_forget[l].astype(F32), s // LANES).reshape(-1, 1)
        cf = _forget_cumsum(f_rows, b_rows).reshape(N_HEADS, s)
        cfk = jnp.pad(cf, ((0, SUBLANES - N_HEADS), (0, 0)))

        o_fox = _fox_attention(qkv, cfk, 0)
        o_sb = _sb_attention(qkv, 3)
        o_ch = _band_attention(qkv, _band_bias_vector(rel_bias[l], band_tq), 6, band_tq)
        o_lru = _recurrent_branch(uf, conv_w[l], conv_b[l], w_r[l], b_r[l], w_i[l], b_i[l],
                                  lru_lambda[l])

        merged = _merge(xb, (o_fox, o_lru, o_sb, o_ch), wg_b, b_gate, wb_b, l)
        xf, xb = _outproj_ln(merged, wo_b, l, xf, ln1_g[l], ln1_b[l])
        xf, xb = _ffn_ln(xb, xf, w1_b, w2_b, l, ln2_g[l], ln2_b[l])
    return xf.reshape(batch, s, d)
```

```python
import functools
import math

import jax
import jax.numpy as jnp
import numpy as np
from jax import lax
from jax.experimental import pallas as pl
from jax.experimental.pallas import tpu as pltpu

F32 = jnp.float32
BF16 = jnp.bfloat16

D_MODEL = 2048
SEQ = 8192
DEPTH = 2
CHUNK = 64
HEAD_DIM = 128
N_BRANCH = 4
BRANCH_WIDTH = D_MODEL // N_BRANCH
N_HEADS = BRANCH_WIDTH // HEAD_DIM
CONV_WIDTH = 4
LRU_C = 8.0
LOOKBACK_CHUNKS = 8
REL_CLIP = 256
D_FF = 4 * D_MODEL
ALPHA = (2.0 * DEPTH) ** 0.25
LN_EPS = 1e-5
QK_SCALE = HEAD_DIM ** -0.5
LOG2_E = math.log2(math.e)

_OFF_FQ = 0
_OFF_FK = _OFF_FQ + BRANCH_WIDTH
_OFF_FV = _OFF_FK + BRANCH_WIDTH
_OFF_FF = _OFF_FV + BRANCH_WIDTH
_OFF_RX = _OFF_FF + N_HEADS
_OFF_RY = _OFF_RX + BRANCH_WIDTH
_OFF_SQ = _OFF_RY + BRANCH_WIDTH
_OFF_CQ = _OFF_SQ + 3 * BRANCH_WIDTH
_OFF_END = _OFF_CQ + 3 * BRANCH_WIDTH

LANES = 128
SUBLANES = 8
NEG_BIG = -1e30
SB_DEAD_LOG2 = -180.0

VMEM_LIMIT = 56 * 1024 * 1024


def _cparams(sem, vmem=VMEM_LIMIT):
    return pltpu.CompilerParams(dimension_semantics=sem, vmem_limit_bytes=vmem)


def _log_sigmoid(x):
    return jnp.minimum(x, 0.0) - jnp.log1p(jnp.exp(-jnp.abs(x)))


def _layer_norm_rows(y, g, b):
    mu = jnp.mean(y, axis=-1, keepdims=True)
    d = y - mu
    var = jnp.mean(d * d, axis=-1, keepdims=True)
    return d * lax.rsqrt(var + LN_EPS) * g + b


def _ln_kernel(x_ref, g_ref, b_ref, of_ref, ob_ref):
    y = _layer_norm_rows(x_ref[...], g_ref[...], b_ref[...])
    of_ref[...] = y
    ob_ref[...] = y.astype(BF16)


def _entry_ln(x, g, b, tm=512):
    s, d = x.shape
    return pl.pallas_call(
        _ln_kernel,
        grid=(s // tm,),
        in_specs=[pl.BlockSpec((tm, d), lambda i: (i, 0)),
                  pl.BlockSpec((1, d), lambda i: (0, 0)),
                  pl.BlockSpec((1, d), lambda i: (0, 0))],
        out_specs=[pl.BlockSpec((tm, d), lambda i: (i, 0)),
                   pl.BlockSpec((tm, d), lambda i: (i, 0))],
        out_shape=[jax.ShapeDtypeStruct((s, d), F32),
                   jax.ShapeDtypeStruct((s, d), BF16)],
        compiler_params=_cparams(("parallel",)),
        name="entry_ln",
    )(x, g.reshape(1, d), b.reshape(1, d))


def _proj_kernel(x_ref, w_ref, s_ref, o_ref):
    acc = jnp.dot(x_ref[...], w_ref[...], preferred_element_type=F32)
    o_ref[...] = (acc * s_ref[...]).astype(o_ref.dtype)


def _project(xb, w_all, layer, colscale, out_dtype, tm, tn, name):
    m, k = xb.shape
    n = w_all.shape[2]
    return pl.pallas_call(
        _proj_kernel,
        grid=(m // tm, n // tn),
        in_specs=[pl.BlockSpec((tm, k), lambda i, j: (i, 0)),
                  pl.BlockSpec((None, k, tn), lambda i, j: (layer, 0, j)),
                  pl.BlockSpec((1, tn), lambda i, j: (0, j))],
        out_specs=pl.BlockSpec((tm, tn), lambda i, j: (i, j)),
        out_shape=jax.ShapeDtypeStruct((m, n), out_dtype),
        compiler_params=_cparams(("parallel", "arbitrary")),
        name=name,
    )(xb, w_all, colscale)


def _forget_cumsum_kernel(f_ref, b_ref, o_ref):
    rows = f_ref.shape[0]
    per_head = rows // N_HEADS
    ls = _log_sigmoid(f_ref[...] + b_ref[...])
    r = lax.broadcasted_iota(jnp.int32, (LANES, LANES), 0)
    c = lax.broadcasted_iota(jnp.int32, (LANES, LANES), 1)
    upper = (r <= c).astype(F32)
    within = jnp.dot(ls, upper, preferred_element_type=F32,
                     precision=lax.Precision.HIGHEST)
    total = within[:, LANES - 1:LANES]
    rr = lax.broadcasted_iota(jnp.int32, (rows, rows), 0)
    cc = lax.broadcasted_iota(jnp.int32, (rows, rows), 1)
    head_start = rr - (rr & (per_head - 1))
    before = ((cc >= head_start) & (cc < rr)).astype(F32)
    offs = jnp.dot(before, jnp.broadcast_to(total, (rows, LANES)),
                   preferred_element_type=F32, precision=lax.Precision.HIGHEST)
    o_ref[...] = (within + offs) * LOG2_E


def _forget_cumsum(f_rows, b_rows):
    rows = f_rows.shape[0]
    return pl.pallas_call(
        _forget_cumsum_kernel,
        out_shape=jax.ShapeDtypeStruct((rows, LANES), F32),
        name="forget_cumsum",
    )(f_rows, b_rows)


def _fox_kernel(q_ref, k_ref, v_ref, cfk_ref, o_ref, m_sc, acc_sc, vaug_sc, *, tq, tk):
    i = pl.program_id(0)
    m_sc[...] = jnp.full(m_sc.shape, NEG_BIG, F32)
    acc_sc[...] = jnp.zeros(acc_sc.shape, F32)
    vaug_sc[:, :, HEAD_DIM:] = jnp.ones((N_HEADS, tk, HEAD_DIM), BF16)
    rep = tk // LANES

    def key_tile(j, masked):
        k0 = pl.multiple_of(j * tk, tk)
        if masked:
            keep = (lax.broadcasted_iota(jnp.int32, (tq, tk), 1)
                    <= lax.broadcasted_iota(jnp.int32, (tq, tk), 0))
        for h in range(N_HEADS):
            hs = slice(h * HEAD_DIM, (h + 1) * HEAD_DIM)
            s = lax.dot_general(q_ref[:, hs], k_ref[pl.ds(k0, tk), hs], (((1,), (1,)), ((), ())),
                                preferred_element_type=F32)
            s = s - cfk_ref[h:h + 1, pl.ds(k0, tk)]
            if masked:
                s = jnp.where(keep, s, NEG_BIG)
            m_old = m_sc[h]
            m_new = jnp.maximum(m_old, jnp.max(s, axis=-1, keepdims=True))
            alpha = jnp.exp2(m_old - m_new)
            p = jnp.exp2(s - jnp.concatenate([m_new] * rep, axis=1))
            vaug_sc[h, :, :HEAD_DIM] = v_ref[pl.ds(k0, tk), hs]
            pv = jnp.dot(p.astype(BF16), vaug_sc[h], preferred_element_type=F32)
            acc_sc[h] = jnp.concatenate([alpha, alpha], axis=1) * acc_sc[h] + pv
            m_sc[h] = m_new

    def off_diagonal(j, carry):
        key_tile(j, False)
        return carry

    lax.fori_loop(0, i, off_diagonal, 0)
    key_tile(i, True)
    for h in range(N_HEADS):
        hs = slice(h * HEAD_DIM, (h + 1) * HEAD_DIM)
        o_ref[:, hs] = (acc_sc[h, :, :HEAD_DIM] / acc_sc[h, :, HEAD_DIM:]).astype(o_ref.dtype)


def _fox_attention(qkv, cfk, col0, tq=512):
    s = qkv.shape[0]
    tk = tq
    w = BRANCH_WIDTH
    kern = functools.partial(_fox_kernel, tq=tq, tk=tk)
    resident = pl.Buffered(1)
    return pl.pallas_call(
        kern,
        grid=(s // tq,),
        in_specs=[pl.BlockSpec((tq, w), lambda i: (i, col0)),
                  pl.BlockSpec((s, w), lambda i: (0, col0 + 1), pipeline_mode=resident),
                  pl.BlockSpec((s, w), lambda i: (0, col0 + 2), pipeline_mode=resident),
                  pl.BlockSpec((SUBLANES, s), lambda i: (0, 0), pipeline_mode=resident)],
        out_specs=pl.BlockSpec((tq, w), lambda i: (i, 0)),
        out_shape=jax.ShapeDtypeStruct((s, w), BF16),
        scratch_shapes=[pltpu.VMEM((N_HEADS, tq, LANES), F32),
                        pltpu.VMEM((N_HEADS, tq, 2 * HEAD_DIM), F32),
                        pltpu.VMEM((N_HEADS, tk, 2 * HEAD_DIM), BF16)],
        compiler_params=_cparams(("arbitrary",)),
        name="fox_attention",
    )(qkv, qkv, qkv, cfk)


def _sb_kernel(q_ref, k_ref, v_ref, o_ref, run_sc, acc_sc, *, tq):
    i = pl.program_id(0)
    nsub = tq // LANES
    r = lax.broadcasted_iota(jnp.int32, (2 * LANES, 2 * LANES), 0) & (LANES - 1)
    c = lax.broadcasted_iota(jnp.int32, (2 * LANES, 2 * LANES), 1)
    tri_aug = ((c >= LANES) | (r > c)).astype(BF16)

    def sub_block(h, k0, r0, masked):
        hs = slice(h * HEAD_DIM, (h + 1) * HEAD_DIM)
        rows = tq - r0
        z = lax.dot_general(q_ref[r0:, hs], k_ref[pl.ds(k0, LANES), hs], (((1,), (1,)), ((), ())),
                            preferred_element_type=F32)
        lp = jnp.minimum(z, 0.0) - jnp.log2(1.0 + jnp.exp2(-jnp.abs(z)))
        ln = lp - z
        if masked:
            keep = (lax.broadcasted_iota(jnp.int32, (rows, LANES), 1)
                    < lax.broadcasted_iota(jnp.int32, (rows, LANES), 0))
            ln = jnp.where(keep, ln, 0.0)
        ln_hi = ln.astype(BF16)
        ln_lo = (ln - ln_hi.astype(F32)).astype(BF16)
        la = jnp.dot(jnp.concatenate([ln_hi, ln_lo], axis=1), tri_aug,
                     preferred_element_type=F32)
        run = run_sc[h, r0:, :]
        a = jnp.exp2(lp + la[:, :LANES] + run)
        if masked:
            a = jnp.where(keep, a, 0.0)
        acc_sc[h, r0:, :] += jnp.dot(a.astype(BF16), v_ref[pl.ds(k0, LANES), hs],
                                     preferred_element_type=F32)
        run_sc[h, r0:, :] = run + la[:, LANES:]

    run_sc[...] = jnp.zeros(run_sc.shape, F32)
    acc_sc[...] = jnp.zeros(acc_sc.shape, F32)
    for cc in range(nsub - 1, -1, -1):
        for h in range(N_HEADS):
            sub_block(h, pl.multiple_of(i * tq + cc * LANES, LANES), cc * LANES, True)

    def alive():
        return (jnp.max(run_sc[...]) > SB_DEAD_LOG2).astype(jnp.int32)

    def earlier(state):
        cb, _ = state
        for back in range(2):
            for h in range(N_HEADS):
                sub_block(h, pl.multiple_of((cb - back) * LANES, LANES), 0, False)
        return cb - 2, alive()

    lax.while_loop(lambda st: jnp.logical_and(st[0] >= 0, st[1] > 0), earlier,
                   (i * nsub - 1, alive()))
    for h in range(N_HEADS):
        o_ref[:, h * HEAD_DIM:(h + 1) * HEAD_DIM] = acc_sc[h].astype(o_ref.dtype)


def _sb_attention(qkv, col0, tq=512):
    s = qkv.shape[0]
    w = BRANCH_WIDTH
    assert (tq // LANES) % 2 == 0
    kern = functools.partial(_sb_kernel, tq=tq)
    resident = pl.Buffered(1)
    return pl.pallas_call(
        kern,
        grid=(s // tq,),
        in_specs=[pl.BlockSpec((tq, w), lambda i: (i, col0)),
                  pl.BlockSpec((s, w), lambda i: (0, col0 + 1), pipeline_mode=resident),
                  pl.BlockSpec((s, w), lambda i: (0, col0 + 2), pipeline_mode=resident)],
        out_specs=pl.BlockSpec((tq, w), lambda i: (i, 0)),
        out_shape=jax.ShapeDtypeStruct((s, w), BF16),
        scratch_shapes=[pltpu.VMEM((N_HEADS, tq, LANES), F32),
                        pltpu.VMEM((N_HEADS, tq, HEAD_DIM), F32)],
        compiler_params=_cparams(("arbitrary",)),
        name="sb_attention",
    )(qkv, qkv, qkv)


def _band_kernel(q_ref, k2_ref, k1_ref, k0_ref, v2_ref, v1_ref, v0_ref, ext_ref, o_ref, bias_sc, *, tq):
    i = pl.program_id(0)
    width = 4 * tq

    @pl.when(i == 0)
    def _build_bias():
        trow = lax.broadcasted_iota(jnp.int32, (tq, width), 0)
        t = lax.broadcasted_iota(jnp.int32, (tq, 3 * tq), 0)
        s = lax.broadcasted_iota(jnp.int32, (tq, 3 * tq), 1)
        shift = CHUNK.bit_length() - 1
        t_chunk = t >> shift
        s_chunk = (s >> shift) - (2 * tq) // CHUNK
        in_band = (t_chunk - s_chunk <= LOOKBACK_CHUNKS) & (s_chunk <= t_chunk)
        for h in range(N_HEADS):
            x = jnp.broadcast_to(ext_ref[h:h + 1, :], (tq, width))
            for b in range(tq.bit_length() - 1):
                x = jnp.where(((trow >> b) & 1) == 1, pltpu.roll(x, 1 << b, axis=1), x)
            bias_sc[h] = jnp.where(in_band, x[:, :3 * tq], NEG_BIG)

    col = lax.broadcasted_iota(jnp.int32, (tq, tq), 1)
    k_refs = (k2_ref, k1_ref, k0_ref)
    v_refs = (v2_ref, v1_ref, v0_ref)
    for h in range(N_HEADS):
        hs = slice(h * HEAD_DIM, (h + 1) * HEAD_DIM)
        q = q_ref[:, hs]
        scores = []
        for p in range(3):
            s = lax.dot_general(q, k_refs[p][:, hs], (((1,), (1,)), ((), ())),
                                preferred_element_type=F32)
            s = s + bias_sc[h, :, p * tq:(p + 1) * tq]
            if p < 2:
                s = jnp.where((i - 2 + p) * tq + col >= 0, s, NEG_BIG)
            scores.append(s)
        m = jnp.maximum(jnp.maximum(jnp.max(scores[0], axis=-1, keepdims=True),
                                    jnp.max(scores[1], axis=-1, keepdims=True)),
                        jnp.max(scores[2], axis=-1, keepdims=True))
        l = jnp.zeros((tq, 1), F32)
        acc = jnp.zeros((tq, HEAD_DIM), F32)
        for p in range(3):
            e = jnp.exp(scores[p] - m)
            l = l + jnp.sum(e, axis=-1, keepdims=True)
            acc = acc + jnp.dot(e.astype(BF16), v_refs[p][:, hs], preferred_element_type=F32)
        o_ref[:, hs] = (acc / l).astype(o_ref.dtype)


def _band_bias_vector(rel_bias_l, tq):
    n = np.arange(4 * tq)
    dist = np.where(n < 3 * tq, 2 * tq - n, 6 * tq - n)
    ridx = np.clip(dist, -(CHUNK - 1), REL_CLIP) + (CHUNK - 1)
    return rel_bias_l.astype(F32)[:, ridx]


def _band_attention(qkv, bias_ext, col0, tq=256):
    s = qkv.shape[0]
    w = BRANCH_WIDTH
    assert 2 * tq >= LOOKBACK_CHUNKS * CHUNK and tq % CHUNK == 0 and tq & (tq - 1) == 0
    kern = functools.partial(_band_kernel, tq=tq)

    def kv_spec(back, col):
        return pl.BlockSpec((tq, w), lambda i: (jnp.maximum(i - back, 0), col))

    return pl.pallas_call(
        kern,
        grid=(s // tq,),
        in_specs=[pl.BlockSpec((tq, w), lambda i: (i, col0)),
                  kv_spec(2, col0 + 1), kv_spec(1, col0 + 1), kv_spec(0, col0 + 1),
                  kv_spec(2, col0 + 2), kv_spec(1, col0 + 2), kv_spec(0, col0 + 2),
                  pl.BlockSpec((N_HEADS, 4 * tq), lambda i: (0, 0))],
        out_specs=pl.BlockSpec((tq, w), lambda i: (i, 0)),
        out_shape=jax.ShapeDtypeStruct((s, w), BF16),
        scratch_shapes=[pltpu.VMEM((N_HEADS, tq, 3 * tq), F32)],
        compiler_params=_cparams(("arbitrary",)),
        name="band_attention",
    )(qkv, qkv, qkv, qkv, qkv, qkv, qkv, bias_ext)


def _gelu_tanh(x):
    c = math.sqrt(2.0 / math.pi)
    return 0.5 * x * (1.0 + jnp.tanh(c * (x + 0.044715 * (x * x * x))))


def _lru_kernel(rx_ref, ry_ref, cw_ref, cb_ref, wr_ref, br_ref, wi_ref, bi_ref, lam_ref,
                o_ref, xext_sc, a_sc, b_sc, h_sc, carry_sc, *, tm):
    i = pl.program_id(0)
    w = BRANCH_WIDTH
    halo = SUBLANES

    @pl.when(i == 0)
    def _first():
        xext_sc[0:halo, :] = jnp.zeros((halo, w), F32)
        carry_sc[...] = jnp.zeros(carry_sc.shape, F32)

    @pl.when(i > 0)
    def _shift_halo():
        xext_sc[0:halo, :] = xext_sc[tm:tm + halo, :]

    xext_sc[halo:halo + tm, :] = rx_ref[...]
    xc = cb_ref[...] + jnp.zeros((tm, w), F32)
    for t in range(CONV_WIDTH):
        start = halo - (CONV_WIDTH - 1) + t
        xc = xc + xext_sc[start:start + tm, :] * cw_ref[t:t + 1, :]

    xcb = xc.astype(BF16)
    r_parts, i_parts = [], []
    for n in range(N_HEADS):
        ns = slice(n * HEAD_DIM, (n + 1) * HEAD_DIM)
        r_parts.append(jnp.dot(xcb[:, ns], wr_ref[n], preferred_element_type=F32))
        i_parts.append(jnp.dot(xcb[:, ns], wi_ref[n], preferred_element_type=F32))
    r = jax.nn.sigmoid(jnp.concatenate(r_parts, axis=1) + br_ref[...])
    gi = jax.nn.sigmoid(jnp.concatenate(i_parts, axis=1) + bi_ref[...])
    log_a = LRU_C * r * _log_sigmoid(lam_ref[...])
    a = jnp.exp(log_a)
    a_sc[...] = a
    b_sc[...] = jnp.sqrt(-jnp.tanh(log_a) * (a * a + 1.0)) * (gi * xc)

    row = lax.broadcasted_iota(jnp.int32, (SUBLANES, w), 0)

    def group(g, carry):
        r0 = pl.multiple_of(g * SUBLANES, SUBLANES)
        a = a_sc[pl.ds(r0, SUBLANES), :]
        b = b_sc[pl.ds(r0, SUBLANES), :]
        for k in (1, 2, 4):
            a_prev = pltpu.roll(a, k, axis=0)
            b_prev = pltpu.roll(b, k, axis=0)
            ok = row >= k
            b = jnp.where(ok, a * b_prev + b, b)
            a = jnp.where(ok, a * a_prev, a)
        hgrp = a * carry + b
        h_sc[pl.ds(r0, SUBLANES), :] = hgrp
        return jnp.broadcast_to(hgrp[SUBLANES - 1:SUBLANES, :], (SUBLANES, w))

    carry_sc[...] = lax.fori_loop(0, tm // SUBLANES, group, carry_sc[...])
    o_ref[...] = (h_sc[...] * _gelu_tanh(ry_ref[...])).astype(o_ref.dtype)


def _recurrent_branch(uf, conv_w, conv_b, w_r, b_r, w_i, b_i, lam, tm=512):
    s = uf.shape[0]
    w = BRANCH_WIDTH
    kern = functools.partial(_lru_kernel, tm=tm)
    row = lambda v: v.reshape(1, w)
    full2 = lambda shape: pl.BlockSpec(shape, lambda i: (0, 0))
    full3 = lambda shape: pl.BlockSpec(shape, lambda i: (0, 0, 0))
    return pl.pallas_call(
        kern,
        grid=(s // tm,),
        in_specs=[pl.BlockSpec((tm, w), lambda i: (i, 0)),
                  pl.BlockSpec((tm, w), lambda i: (i, 1)),
                  full2((CONV_WIDTH, w)), full2((1, w)),
                  full3((N_HEADS, HEAD_DIM, HEAD_DIM)), full2((1, w)),
                  full3((N_HEADS, HEAD_DIM, HEAD_DIM)), full2((1, w)),
                  full2((1, w))],
        out_specs=pl.BlockSpec((tm, w), lambda i: (i, 0)),
        out_shape=jax.ShapeDtypeStruct((s, w), BF16),
        scratch_shapes=[pltpu.VMEM((tm + 2 * SUBLANES, w), F32),
                        pltpu.VMEM((tm, w), F32),
                        pltpu.VMEM((tm, w), F32),
                        pltpu.VMEM((tm, w), F32),
                        pltpu.VMEM((SUBLANES, w), F32)],
        compiler_params=_cparams(("arbitrary",)),
        name="recurrent_branch",
    )(uf, uf, conv_w, row(conv_b), w_r.astype(BF16), row(b_r), w_i.astype(BF16), row(b_i), row(lam))


def _merge_kernel(x_ref, o0_ref, o1_ref, o2_ref, o3_ref, wg_ref, bg_ref, wb_ref, out_ref):
    x = x_ref[...]
    merged = None
    for g, o_ref in enumerate((o0_ref, o1_ref, o2_ref, o3_ref)):
        gate = jax.nn.sigmoid(jnp.dot(x, wg_ref[g], preferred_element_type=F32) + bg_ref[g:g + 1, :])
        term = gate * jnp.dot(o_ref[...], wb_ref[g], preferred_element_type=F32)
        merged = term if merged is None else merged + term
    out_ref[...] = merged.astype(out_ref.dtype)


def _merge(xb, branches, wg_all, bg_all, wb_all, layer, tm=1024, tn=256):
    s, d = xb.shape
    w = BRANCH_WIDTH
    o_spec = pl.BlockSpec((tm, w), lambda i, j: (i, 0))
    return pl.pallas_call(
        _merge_kernel,
        grid=(s // tm, d // tn),
        in_specs=[pl.BlockSpec((tm, d), lambda i, j: (i, 0)),
                  o_spec, o_spec, o_spec, o_spec,
                  pl.BlockSpec((None, N_BRANCH, d, tn), lambda i, j: (layer, 0, 0, j)),
                  pl.BlockSpec((None, N_BRANCH, tn), lambda i, j: (layer, 0, j)),
                  pl.BlockSpec((None, N_BRANCH, w, tn), lambda i, j: (layer, 0, 0, j))],
        out_specs=pl.BlockSpec((tm, tn), lambda i, j: (i, j)),
        out_shape=jax.ShapeDtypeStruct((s, d), BF16),
        compiler_params=_cparams(("parallel", "arbitrary")),
        name="gated_merge",
    )(xb, *branches, wg_all, bg_all, wb_all)


def _outproj_kernel(m_ref, w_ref, x_ref, g_ref, b_ref, of_ref, ob_ref):
    y = ALPHA * x_ref[...] + jnp.dot(m_ref[...], w_ref[...], preferred_element_type=F32)
    y = _layer_norm_rows(y, g_ref[...], b_ref[...])
    of_ref[...] = y
    ob_ref[...] = y.astype(BF16)


def _outproj_ln(merged, w_out_all, layer, x, g, b, tm=512):
    s, d = x.shape
    row_spec = pl.BlockSpec((tm, d), lambda i: (i, 0))
    vec_spec = pl.BlockSpec((1, d), lambda i: (0, 0))
    return pl.pallas_call(
        _outproj_kernel,
        grid=(s // tm,),
        in_specs=[row_spec, pl.BlockSpec((None, d, d), lambda i: (layer, 0, 0)), row_spec,
                  vec_spec, vec_spec],
        out_specs=[row_spec, row_spec],
        out_shape=[jax.ShapeDtypeStruct((s, d), F32), jax.ShapeDtypeStruct((s, d), BF16)],
        compiler_params=_cparams(("parallel",)),
        name="outproj_ln",
    )(merged, w_out_all, x, g.reshape(1, d), b.reshape(1, d))


def _ffn_kernel(xb_ref, xf_ref, w1_ref, w2_ref, g_ref, b_ref, of_ref, ob_ref, acc_sc):
    f = pl.program_id(1)

    @pl.when(f == 0)
    def _init():
        acc_sc[...] = jnp.zeros(acc_sc.shape, F32)

    hid = jnp.maximum(jnp.dot(xb_ref[...], w1_ref[...], preferred_element_type=F32), 0.0)
    hid = (hid * hid).astype(BF16)
    acc_sc[...] += jnp.dot(hid, w2_ref[...], preferred_element_type=F32)

    @pl.when(f == pl.num_programs(1) - 1)
    def _finish():
        y = _layer_norm_rows(ALPHA * xf_ref[...] + acc_sc[...], g_ref[...], b_ref[...])
        of_ref[...] = y
        ob_ref[...] = y.astype(BF16)


def _ffn_ln(xb, xf, w1_all, w2_all, layer, g, b, tm=512, tf=1024):
    s, d = xf.shape
    dff = w1_all.shape[2]
    row_spec = pl.BlockSpec((tm, d), lambda i, f: (i, 0))
    vec_spec = pl.BlockSpec((1, d), lambda i, f: (0, 0))
    return pl.pallas_call(
        _ffn_kernel,
        grid=(s // tm, dff // tf),
        in_specs=[row_spec, row_spec,
                  pl.BlockSpec((None, d, tf), lambda i, f: (layer, 0, f)),
                  pl.BlockSpec((None, tf, d), lambda i, f: (layer, f, 0)),
                  vec_spec, vec_spec],
        out_specs=[row_spec, row_spec],
        out_shape=[jax.ShapeDtypeStruct((s, d), F32), jax.ShapeDtypeStruct((s, d), BF16)],
        scratch_shapes=[pltpu.VMEM((tm, d), F32)],
        compiler_params=_cparams(("parallel", "arbitrary")),
        name="ffn_ln",
    )(xb, xf, w1_all, w2_all, g.reshape(1, d), b.reshape(1, d))


def _split_in_proj(w_in):
    qkv = jnp.concatenate([w_in[:, :, _OFF_FQ:_OFF_FF], w_in[:, :, _OFF_SQ:_OFF_END]], axis=2)
    pad = jnp.zeros(w_in.shape[:2] + (LANES - N_HEADS,), w_in.dtype)
    rest = jnp.concatenate([w_in[:, :, _OFF_RX:_OFF_SQ], w_in[:, :, _OFF_FF:_OFF_RX], pad], axis=2)
    return qkv.astype(BF16), rest.astype(BF16)


def kernel(x, ln_in_g, ln_in_b, w_in, b_forget, conv_w, conv_b, w_r, b_r, w_i, b_i, lru_lambda,
           rel_bias, w_branch, w_gate, b_gate, w_out, ln1_g, ln1_b, w_ff1, w_ff2, ln2_g, ln2_b):
    batch, s, d = x.shape
    assert (batch, s, d) == (1, SEQ, D_MODEL)
    w = BRANCH_WIDTH
    band_tq = 256

    col_scale = np.ones((1, 9 * w), np.float32)
    for q_block, scale in ((0, QK_SCALE * LOG2_E), (3, QK_SCALE * LOG2_E), (6, QK_SCALE)):
        col_scale[:, q_block * w:(q_block + 1) * w] = scale
    qkv_scale = jnp.asarray(col_scale)
    rest_scale = jnp.ones((1, 2 * w + LANES), F32)

    w_qkv, w_rest = _split_in_proj(w_in)
    wg_b, wb_b, wo_b = w_gate.astype(BF16), w_branch.astype(BF16), w_out.astype(BF16)
    w1_b, w2_b = w_ff1.astype(BF16), w_ff2.astype(BF16)

    xf, xb = _entry_ln(x.reshape(s, d), ln_in_g, ln_in_b)
    for l in range(DEPTH):
        qkv = _project(xb, w_qkv, l, qkv_scale, BF16, 1024, 512, "in_proj_qkv")
        uf = _project(xb, w_rest, l, rest_scale, F32, 1024, 384, "in_proj_rest")

        f_rows = uf[:, 2 * w:2 * w + N_HEADS].T.reshape(N_HEADS * (s // LANES), LANES)
        b_rows = jnp.repeat(b_forget[l].astype(F32), s // LANES).reshape(-1, 1)
        cf = _forget_cumsum(f_rows, b_rows).reshape(N_HEADS, s)
        cfk = jnp.pad(cf, ((0, SUBLANES - N_HEADS), (0, 0)))

        o_fox = _fox_attention(qkv, cfk, 0)
        o_sb = _sb_attention(qkv, 3)
        o_ch = _band_attention(qkv, _band_bias_vector(rel_bias[l], band_tq), 6, band_tq)
        o_lru = _recurrent_branch(uf, conv_w[l], conv_b[l], w_r[l], b_r[l], w_i[l], b_i[l],
                                  lru_lambda[l])

        merged = _merge(xb, (o_fox, o_lru, o_sb, o_ch), wg_b, b_gate, wb_b, l)
        xf, xb = _outproj_ln(merged, wo_b, l, xf, ln1_g[l], ln1_b[l])
        xf, xb = _ffn_ln(xb, xf, w1_b, w2_b, l, ln2_g[l], ln2_b[l])
    return xf.reshape(batch, s, d)
```

```python
import functools
import math

import jax
import jax.numpy as jnp
import numpy as np
from jax import lax
from jax.experimental import pallas as pl
from jax.experimental.pallas import tpu as pltpu

F32 = jnp.float32
BF16 = jnp.bfloat16

D_MODEL = 2048
SEQ = 8192
DEPTH = 2
CHUNK = 64
HEAD_DIM = 128
N_BRANCH = 4
BRANCH_WIDTH = D_MODEL // N_BRANCH
N_HEADS = BRANCH_WIDTH // HEAD_DIM
CONV_WIDTH = 4
LRU_C = 8.0
LOOKBACK_CHUNKS = 8
REL_CLIP = 256
D_FF = 4 * D_MODEL
ALPHA = (2.0 * DEPTH) ** 0.25
LN_EPS = 1e-5
QK_SCALE = HEAD_DIM ** -0.5
LOG2_E = math.log2(math.e)

_OFF_FQ = 0
_OFF_FK = _OFF_FQ + BRANCH_WIDTH
_OFF_FV = _OFF_FK + BRANCH_WIDTH
_OFF_FF = _OFF_FV + BRANCH_WIDTH
_OFF_RX = _OFF_FF + N_HEADS
_OFF_RY = _OFF_RX + BRANCH_WIDTH
_OFF_SQ = _OFF_RY + BRANCH_WIDTH
_OFF_CQ = _OFF_SQ + 3 * BRANCH_WIDTH
_OFF_END = _OFF_CQ + 3 * BRANCH_WIDTH

LANES = 128
SUBLANES = 8
NEG_BIG = -1e30
SB_DEAD_LOG2 = -180.0

VMEM_LIMIT = 56 * 1024 * 1024


def _cparams(sem, vmem=VMEM_LIMIT):
    return pltpu.CompilerParams(dimension_semantics=sem, vmem_limit_bytes=vmem)


def _log_sigmoid(x):
    return jnp.minimum(x, 0.0) - jnp.log1p(jnp.exp(-jnp.abs(x)))


def _layer_norm_rows(y, g, b):
    mu = jnp.mean(y, axis=-1, keepdims=True)
    d = y - mu
    var = jnp.mean(d * d, axis=-1, keepdims=True)
    return d * lax.rsqrt(var + LN_EPS) * g + b


def _ln_kernel(x_ref, g_ref, b_ref, of_ref, ob_ref):
    y = _layer_norm_rows(x_ref[...], g_ref[...], b_ref[...])
    of_ref[...] = y
    ob_ref[...] = y.astype(BF16)


def _entry_ln(x, g, b, tm=512):
    s, d = x.shape
    return pl.pallas_call(
        _ln_kernel,
        grid=(s // tm,),
        in_specs=[pl.BlockSpec((tm, d), lambda i: (i, 0)),
                  pl.BlockSpec((1, d), lambda i: (0, 0)),
                  pl.BlockSpec((1, d), lambda i: (0, 0))],
        out_specs=[pl.BlockSpec((tm, d), lambda i: (i, 0)),
                   pl.BlockSpec((tm, d), lambda i: (i, 0))],
        out_shape=[jax.ShapeDtypeStruct((s, d), F32),
                   jax.ShapeDtypeStruct((s, d), BF16)],
        compiler_params=_cparams(("parallel",)),
        name="entry_ln",
    )(x, g.reshape(1, d), b.reshape(1, d))


def _proj_kernel(x_ref, w_ref, s_ref, o_ref):
    acc = jnp.dot(x_ref[...], w_ref[...], preferred_element_type=F32)
    o_ref[...] = (acc * s_ref[...]).astype(o_ref.dtype)


def _project(xb, w_all, layer, colscale, out_dtype, tm, tn, name):
    m, k = xb.shape
    n = w_all.shape[2]
    return pl.pallas_call(
        _proj_kernel,
        grid=(m // tm, n // tn),
        in_specs=[pl.BlockSpec((tm, k), lambda i, j: (i, 0)),
                  pl.BlockSpec((None, k, tn), lambda i, j: (layer, 0, j)),
                  pl.BlockSpec((1, tn), lambda i, j: (0, j))],
        out_specs=pl.BlockSpec((tm, tn), lambda i, j: (i, j)),
        out_shape=jax.ShapeDtypeStruct((m, n), out_dtype),
        compiler_params=_cparams(("parallel", "arbitrary")),
        name=name,
    )(xb, w_all, colscale)


def _forget_cumsum_kernel(f_ref, b_ref, o_ref):
    rows = f_ref.shape[0]
    per_head = rows // N_HEADS
    ls = _log_sigmoid(f_ref[...] + b_ref[...])
    r = lax.broadcasted_iota(jnp.int32, (LANES, LANES), 0)
    c = lax.broadcasted_iota(jnp.int32, (LANES, LANES), 1)
    upper = (r <= c).astype(F32)
    within = jnp.dot(ls, upper, preferred_element_type=F32,
                     precision=lax.Precision.HIGHEST)
    total = within[:, LANES - 1:LANES]
    rr = lax.broadcasted_iota(jnp.int32, (rows, rows), 0)
    cc = lax.broadcasted_iota(jnp.int32, (rows, rows), 1)
    head_start = rr - (rr & (per_head - 1))
    before = ((cc >= head_start) & (cc < rr)).astype(F32)
    offs = jnp.dot(before, jnp.broadcast_to(total, (rows, LANES)),
                   preferred_element_type=F32, precision=lax.Precision.HIGHEST)
    o_ref[...] = (within + offs) * LOG2_E


def _forget_cumsum(f_rows, b_rows):
    rows = f_rows.shape[0]
    return pl.pallas_call(
        _forget_cumsum_kernel,
        out_shape=jax.ShapeDtypeStruct((rows, LANES), F32),
        name="forget_cumsum",
    )(f_rows, b_rows)


def _fox_kernel(q_ref, k_ref, v_ref, cfk_ref, o_ref, m_sc, acc_sc, vaug_sc, *, tq, tk):
    i = pl.program_id(0)
    m_sc[...] = jnp.full(m_sc.shape, NEG_BIG, F32)
    acc_sc[...] = jnp.zeros(acc_sc.shape, F32)
    vaug_sc[:, :, HEAD_DIM:] = jnp.ones((N_HEADS, tk, HEAD_DIM), BF16)
    rep = tk // LANES

    def key_tile(j, masked):
        k0 = pl.multiple_of(j * tk, tk)
        if masked:
            keep = (lax.broadcasted_iota(jnp.int32, (tq, tk), 1)
                    <= lax.broadcasted_iota(jnp.int32, (tq, tk), 0))
        heads = [slice(h * HEAD_DIM, (h + 1) * HEAD_DIM) for h in range(N_HEADS)]
        scores = [lax.dot_general(q_ref[:, hs], k_ref[pl.ds(k0, tk), hs], (((1,), (1,)), ((), ())),
                                  preferred_element_type=F32) for hs in heads]
        probs, alphas = [], []
        for h, s in enumerate(scores):
            s = s - cfk_ref[h:h + 1, pl.ds(k0, tk)]
            if masked:
                s = jnp.where(keep, s, NEG_BIG)
            m_old = m_sc[h]
            m_new = jnp.maximum(m_old, jnp.max(s, axis=-1, keepdims=True))
            alphas.append(jnp.exp2(m_old - m_new))
            probs.append(jnp.exp2(s - jnp.concatenate([m_new] * rep, axis=1)).astype(BF16))
            m_sc[h] = m_new
            vaug_sc[h, :, :HEAD_DIM] = v_ref[pl.ds(k0, tk), heads[h]]
        for h, (p, alpha) in enumerate(zip(probs, alphas)):
            pv = jnp.dot(p, vaug_sc[h], preferred_element_type=F32)
            acc_sc[h] = jnp.concatenate([alpha, alpha], axis=1) * acc_sc[h] + pv

    def off_diagonal(j, carry):
        key_tile(j, False)
        return carry

    lax.fori_loop(0, i, off_diagonal, 0)
    key_tile(i, True)
    for h in range(N_HEADS):
        hs = slice(h * HEAD_DIM, (h + 1) * HEAD_DIM)
        o_ref[:, hs] = (acc_sc[h, :, :HEAD_DIM] / acc_sc[h, :, HEAD_DIM:]).astype(o_ref.dtype)


def _fox_attention(qkv, cfk, col0, tq=512):
    s = qkv.shape[0]
    tk = tq
    w = BRANCH_WIDTH
    kern = functools.partial(_fox_kernel, tq=tq, tk=tk)
    resident = pl.Buffered(1)
    return pl.pallas_call(
        kern,
        grid=(s // tq,),
        in_specs=[pl.BlockSpec((tq, w), lambda i: (i, col0)),
                  pl.BlockSpec((s, w), lambda i: (0, col0 + 1), pipeline_mode=resident),
                  pl.BlockSpec((s, w), lambda i: (0, col0 + 2), pipeline_mode=resident),
                  pl.BlockSpec((SUBLANES, s), lambda i: (0, 0), pipeline_mode=resident)],
        out_specs=pl.BlockSpec((tq, w), lambda i: (i, 0)),
        out_shape=jax.ShapeDtypeStruct((s, w), BF16),
        scratch_shapes=[pltpu.VMEM((N_HEADS, tq, LANES), F32),
                        pltpu.VMEM((N_HEADS, tq, 2 * HEAD_DIM), F32),
                        pltpu.VMEM((N_HEADS, tk, 2 * HEAD_DIM), BF16)],
        compiler_params=_cparams(("arbitrary",)),
        name="fox_attention",
    )(qkv, qkv, qkv, cfk)


def _sb_kernel(q_ref, k_ref, v_ref, o_ref, run_sc, acc_sc, *, tq):
    i = pl.program_id(0)
    nsub = tq // LANES
    r = lax.broadcasted_iota(jnp.int32, (2 * LANES, 2 * LANES), 0) & (LANES - 1)
    c = lax.broadcasted_iota(jnp.int32, (2 * LANES, 2 * LANES), 1)
    tri_aug = ((c >= LANES) | (r > c)).astype(BF16)

    def sub_blocks(items, masked):
        scores = []
        for h, k0, r0 in items:
            hs = slice(h * HEAD_DIM, (h + 1) * HEAD_DIM)
            scores.append(lax.dot_general(q_ref[r0:, hs], k_ref[pl.ds(k0, LANES), hs],
                                          (((1,), (1,)), ((), ())), preferred_element_type=F32))
        log_beta, sums, keeps = [], [], []
        for (h, k0, r0), z in zip(items, scores):
            rows = tq - r0
            lp = jnp.minimum(z, 0.0) - jnp.log2(1.0 + jnp.exp2(-jnp.abs(z)))
            ln = lp - z
            keep = None
            if masked:
                keep = (lax.broadcasted_iota(jnp.int32, (rows, LANES), 1)
                        < lax.broadcasted_iota(jnp.int32, (rows, LANES), 0))
                ln = jnp.where(keep, ln, 0.0)
            ln_hi = ln.astype(BF16)
            ln_lo = (ln - ln_hi.astype(F32)).astype(BF16)
            log_beta.append(lp)
            keeps.append(keep)
            sums.append(jnp.dot(jnp.concatenate([ln_hi, ln_lo], axis=1), tri_aug,
                                preferred_element_type=F32))
        weights = []
        for (h, k0, r0), lp, la, keep in zip(items, log_beta, sums, keeps):
            run = run_sc[h, r0:, :]
            a = jnp.exp2(lp + la[:, :LANES] + run)
            if masked:
                a = jnp.where(keep, a, 0.0)
            run_sc[h, r0:, :] = run + la[:, LANES:]
            weights.append(a.astype(BF16))
        for (h, k0, r0), a in zip(items, weights):
            hs = slice(h * HEAD_DIM, (h + 1) * HEAD_DIM)
            acc_sc[h, r0:, :] += jnp.dot(a, v_ref[pl.ds(k0, LANES), hs], preferred_element_type=F32)

    run_sc[...] = jnp.zeros(run_sc.shape, F32)
    acc_sc[...] = jnp.zeros(acc_sc.shape, F32)
    for cc in range(nsub - 1, -1, -1):
        k0 = pl.multiple_of(i * tq + cc * LANES, LANES)
        sub_blocks([(h, k0, cc * LANES) for h in range(N_HEADS)], True)

    def alive():
        return (jnp.max(run_sc[...]) > SB_DEAD_LOG2).astype(jnp.int32)

    def earlier(state):
        cb, _ = state
        sub_blocks([(h, pl.multiple_of((cb - back) * LANES, LANES), 0)
                    for back in range(2) for h in range(N_HEADS)], False)
        return cb - 2, alive()

    lax.while_loop(lambda st: jnp.logical_and(st[0] >= 0, st[1] > 0), earlier,
                   (i * nsub - 1, alive()))
    for h in range(N_HEADS):
        o_ref[:, h * HEAD_DIM:(h + 1) * HEAD_DIM] = acc_sc[h].astype(o_ref.dtype)


def _sb_attention(qkv, col0, tq=512):
    s = qkv.shape[0]
    w = BRANCH_WIDTH
    assert (tq // LANES) % 2 == 0
    kern = functools.partial(_sb_kernel, tq=tq)
    resident = pl.Buffered(1)
    return pl.pallas_call(
        kern,
        grid=(s // tq,),
        in_specs=[pl.BlockSpec((tq, w), lambda i: (i, col0)),
                  pl.BlockSpec((s, w), lambda i: (0, col0 + 1), pipeline_mode=resident),
                  pl.BlockSpec((s, w), lambda i: (0, col0 + 2), pipeline_mode=resident)],
        out_specs=pl.BlockSpec((tq, w), lambda i: (i, 0)),
        out_shape=jax.ShapeDtypeStruct((s, w), BF16),
        scratch_shapes=[pltpu.VMEM((N_HEADS, tq, LANES), F32),
                        pltpu.VMEM((N_HEADS, tq, HEAD_DIM), F32)],
        compiler_params=_cparams(("arbitrary",)),
        name="sb_attention",
    )(qkv, qkv, qkv)


def _band_kernel(q_ref, k2_ref, k1_ref, k0_ref, v2_ref, v1_ref, v0_ref, ext_ref, o_ref,
                 bias_sc, vaug_sc, *, tq):
    i = pl.program_id(0)
    width = 4 * tq

    @pl.when(i == 0)
    def _build_tables():
        trow = lax.broadcasted_iota(jnp.int32, (tq, width), 0)
        t = lax.broadcasted_iota(jnp.int32, (tq, 3 * tq), 0)
        s = lax.broadcasted_iota(jnp.int32, (tq, 3 * tq), 1)
        shift = CHUNK.bit_length() - 1
        t_chunk = t >> shift
        s_chunk = (s >> shift) - (2 * tq) // CHUNK
        in_band = (t_chunk - s_chunk <= LOOKBACK_CHUNKS) & (s_chunk <= t_chunk)
        for h in range(N_HEADS):
            x = jnp.broadcast_to(ext_ref[h:h + 1, :], (tq, width))
            for b in range(tq.bit_length() - 1):
                x = jnp.where(((trow >> b) & 1) == 1, pltpu.roll(x, 1 << b, axis=1), x)
            bias_sc[h] = jnp.where(in_band, x[:, :3 * tq] * LOG2_E, NEG_BIG)
        vaug_sc[:, :, HEAD_DIM:] = jnp.ones((N_HEADS, 3 * tq, HEAD_DIM), BF16)

    k_refs = (k2_ref, k1_ref, k0_ref)
    v_refs = (v2_ref, v1_ref, v0_ref)
    heads = [slice(h * HEAD_DIM, (h + 1) * HEAD_DIM) for h in range(N_HEADS)]

    def tile(first_tiles):
        scores = [[lax.dot_general(q_ref[:, hs], k_refs[p][:, hs], (((1,), (1,)), ((), ())),
                                   preferred_element_type=F32) for p in range(3)] for hs in heads]
        probs = []
        for h, pieces in enumerate(scores):
            pieces = [s + bias_sc[h, :, p * tq:(p + 1) * tq] for p, s in enumerate(pieces)]
            if first_tiles:
                pieces = [jnp.where(i - 2 + p >= 0, s, NEG_BIG) for p, s in enumerate(pieces)]
            m = jnp.max(jnp.maximum(jnp.maximum(pieces[0], pieces[1]), pieces[2]),
                        axis=-1, keepdims=True)
            probs.append(jnp.concatenate([jnp.exp2(s - m).astype(BF16) for s in pieces], axis=1))
            for p in range(3):
                vaug_sc[h, p * tq:(p + 1) * tq, :HEAD_DIM] = v_refs[p][:, heads[h]]
        for h, p in enumerate(probs):
            acc = jnp.dot(p, vaug_sc[h], preferred_element_type=F32)
            o_ref[:, heads[h]] = (acc[:, :HEAD_DIM] / acc[:, HEAD_DIM:]).astype(o_ref.dtype)

    @pl.when(i < 2)
    def _first_tiles():
        tile(True)

    @pl.when(i >= 2)
    def _other_tiles():
        tile(False)


def _band_bias_vector(rel_bias_l, tq):
    n = np.arange(4 * tq)
    dist = np.where(n < 3 * tq, 2 * tq - n, 6 * tq - n)
    ridx = np.clip(dist, -(CHUNK - 1), REL_CLIP) + (CHUNK - 1)
    return rel_bias_l.astype(F32)[:, ridx]


def _band_attention(qkv, bias_ext, col0, tq=256):
    s = qkv.shape[0]
    w = BRANCH_WIDTH
    assert 2 * tq >= LOOKBACK_CHUNKS * CHUNK and tq % CHUNK == 0 and tq & (tq - 1) == 0
    kern = functools.partial(_band_kernel, tq=tq)

    def kv_spec(back, col):
        return pl.BlockSpec((tq, w), lambda i: (jnp.maximum(i - back, 0), col))

    return pl.pallas_call(
        kern,
        grid=(s // tq,),
        in_specs=[pl.BlockSpec((tq, w), lambda i: (i, col0)),
                  kv_spec(2, col0 + 1), kv_spec(1, col0 + 1), kv_spec(0, col0 + 1),
                  kv_spec(2, col0 + 2), kv_spec(1, col0 + 2), kv_spec(0, col0 + 2),
                  pl.BlockSpec((N_HEADS, 4 * tq), lambda i: (0, 0))],
        out_specs=pl.BlockSpec((tq, w), lambda i: (i, 0)),
        out_shape=jax.ShapeDtypeStruct((s, w), BF16),
        scratch_shapes=[pltpu.VMEM((N_HEADS, tq, 3 * tq), F32),
                        pltpu.VMEM((N_HEADS, 3 * tq, 2 * HEAD_DIM), BF16)],
        compiler_params=_cparams(("arbitrary",)),
        name="band_attention",
    )(qkv, qkv, qkv, qkv, qkv, qkv, qkv, bias_ext)


def _gelu_tanh(x):
    c = math.sqrt(2.0 / math.pi)
    return 0.5 * x * (1.0 + jnp.tanh(c * (x + 0.044715 * (x * x * x))))


def _lru_kernel(rx_ref, ry_ref, cw_ref, cb_ref, wr_ref, br_ref, wi_ref, bi_ref, lam_ref,
                o_ref, xext_sc, a_sc, b_sc, h_sc, carry_sc, *, tm):
    i = pl.program_id(0)
    w = BRANCH_WIDTH
    halo = SUBLANES

    @pl.when(i == 0)
    def _first():
        xext_sc[0:halo, :] = jnp.zeros((halo, w), F32)
        carry_sc[...] = jnp.zeros(carry_sc.shape, F32)

    @pl.when(i > 0)
    def _shift_halo():
        xext_sc[0:halo, :] = xext_sc[tm:tm + halo, :]

    xext_sc[halo:halo + tm, :] = rx_ref[...]
    xc = cb_ref[...] + jnp.zeros((tm, w), F32)
    for t in range(CONV_WIDTH):
        start = halo - (CONV_WIDTH - 1) + t
        xc = xc + xext_sc[start:start + tm, :] * cw_ref[t:t + 1, :]

    xcb = xc.astype(BF16)
    r_parts, i_parts = [], []
    for n in range(N_HEADS):
        ns = slice(n * HEAD_DIM, (n + 1) * HEAD_DIM)
        r_parts.append(jnp.dot(xcb[:, ns], wr_ref[n], preferred_element_type=F32))
        i_parts.append(jnp.dot(xcb[:, ns], wi_ref[n], preferred_element_type=F32))
    r = jax.nn.sigmoid(jnp.concatenate(r_parts, axis=1) + br_ref[...])
    gi = jax.nn.sigmoid(jnp.concatenate(i_parts, axis=1) + bi_ref[...])
    log_a = LRU_C * r * _log_sigmoid(lam_ref[...])
    a = jnp.exp(log_a)
    a_sc[...] = a
    b_sc[...] = jnp.sqrt(-jnp.tanh(log_a) * (a * a + 1.0)) * (gi * xc)

    row = lax.broadcasted_iota(jnp.int32, (SUBLANES, w), 0)

    def group(g, carry):
        r0 = pl.multiple_of(g * SUBLANES, SUBLANES)
        a = a_sc[pl.ds(r0, SUBLANES), :]
        b = b_sc[pl.ds(r0, SUBLANES), :]
        for k in (1, 2, 4):
            a_prev = pltpu.roll(a, k, axis=0)
            b_prev = pltpu.roll(b, k, axis=0)
            ok = row >= k
            b = jnp.where(ok, a * b_prev + b, b)
            a = jnp.where(ok, a * a_prev, a)
        hgrp = a * carry + b
        h_sc[pl.ds(r0, SUBLANES), :] = hgrp
        return jnp.broadcast_to(hgrp[SUBLANES - 1:SUBLANES, :], (SUBLANES, w))

    carry_sc[...] = lax.fori_loop(0, tm // SUBLANES, group, carry_sc[...])
    o_ref[...] = (h_sc[...] * _gelu_tanh(ry_ref[...])).astype(o_ref.dtype)


def _recurrent_branch(uf, conv_w, conv_b, w_r, b_r, w_i, b_i, lam, tm=512):
    s = uf.shape[0]
    w = BRANCH_WIDTH
    kern = functools.partial(_lru_kernel, tm=tm)
    row = lambda v: v.reshape(1, w)
    full2 = lambda shape: pl.BlockSpec(shape, lambda i: (0, 0))
    full3 = lambda shape: pl.BlockSpec(shape, lambda i: (0, 0, 0))
    return pl.pallas_call(
        kern,
        grid=(s // tm,),
        in_specs=[pl.BlockSpec((tm, w), lambda i: (i, 0)),
                  pl.BlockSpec((tm, w), lambda i: (i, 1)),
                  full2((CONV_WIDTH, w)), full2((1, w)),
                  full3((N_HEADS, HEAD_DIM, HEAD_DIM)), full2((1, w)),
                  full3((N_HEADS, HEAD_DIM, HEAD_DIM)), full2((1, w)),
                  full2((1, w))],
        out_specs=pl.BlockSpec((tm, w), lambda i: (i, 0)),
        out_shape=jax.ShapeDtypeStruct((s, w), BF16),
        scratch_shapes=[pltpu.VMEM((tm + 2 * SUBLANES, w), F32),
                        pltpu.VMEM((tm, w), F32),
                        pltpu.VMEM((tm, w), F32),
                        pltpu.VMEM((tm, w), F32),
                        pltpu.VMEM((SUBLANES, w), F32)],
        compiler_params=_cparams(("arbitrary",)),
        name="recurrent_branch",
    )(uf, uf, conv_w, row(conv_b), w_r.astype(BF16), row(b_r), w_i.astype(BF16), row(b_i), row(lam))


def _merge_kernel(x_ref, o0_ref, o1_ref, o2_ref, o3_ref, wg_ref, bg_ref, wb_ref, out_ref):
    x = x_ref[...]
    merged = None
    for g, o_ref in enumerate((o0_ref, o1_ref, o2_ref, o3_ref)):
        gate = jax.nn.sigmoid(jnp.dot(x, wg_ref[g], preferred_element_type=F32) + bg_ref[g:g + 1, :])
        term = gate * jnp.dot(o_ref[...], wb_ref[g], preferred_element_type=F32)
        merged = term if merged is None else merged + term
    out_ref[...] = merged.astype(out_ref.dtype)


def _merge(xb, branches, wg_all, bg_all, wb_all, layer, tm=1024, tn=512):
    s, d = xb.shape
    w = BRANCH_WIDTH
    o_spec = pl.BlockSpec((tm, w), lambda i, j: (i, 0))
    return pl.pallas_call(
        _merge_kernel,
        grid=(s // tm, d // tn),
        in_specs=[pl.BlockSpec((tm, d), lambda i, j: (i, 0)),
                  o_spec, o_spec, o_spec, o_spec,
                  pl.BlockSpec((None, N_BRANCH, d, tn), lambda i, j: (layer, 0, 0, j)),
                  pl.BlockSpec((None, N_BRANCH, tn), lambda i, j: (layer, 0, j)),
                  pl.BlockSpec((None, N_BRANCH, w, tn), lambda i, j: (layer, 0, 0, j))],
        out_specs=pl.BlockSpec((tm, tn), lambda i, j: (i, j)),
        out_shape=jax.ShapeDtypeStruct((s, d), BF16),
        compiler_params=_cparams(("parallel", "arbitrary")),
        name="gated_merge",
    )(xb, *branches, wg_all, bg_all, wb_all)


def _outproj_kernel(m_ref, w_ref, x_ref, g_ref, b_ref, of_ref, ob_ref):
    half = m_ref.shape[0] // 2
    rows = (slice(0, half), slice(half, 2 * half))
    proj = [jnp.dot(m_ref[r, :], w_ref[...], preferred_element_type=F32) for r in rows]
    for r, p in zip(rows, proj):
        y = _layer_norm_rows(ALPHA * x_ref[r, :] + p, g_ref[...], b_ref[...])
        of_ref[r, :] = y
        ob_ref[r, :] = y.astype(BF16)


def _outproj_ln(merged, w_out_all, layer, x, g, b, tm=512):
    s, d = x.shape
    row_spec = pl.BlockSpec((tm, d), lambda i: (i, 0))
    vec_spec = pl.BlockSpec((1, d), lambda i: (0, 0))
    return pl.pallas_call(
        _outproj_kernel,
        grid=(s // tm,),
        in_specs=[row_spec, pl.BlockSpec((None, d, d), lambda i: (layer, 0, 0)), row_spec,
                  vec_spec, vec_spec],
        out_specs=[row_spec, row_spec],
        out_shape=[jax.ShapeDtypeStruct((s, d), F32), jax.ShapeDtypeStruct((s, d), BF16)],
        compiler_params=_cparams(("parallel",)),
        name="outproj_ln",
    )(merged, w_out_all, x, g.reshape(1, d), b.reshape(1, d))


def _ffn_kernel(xb_ref, xf_ref, w1_ref, w2_ref, g_ref, b_ref, of_ref, ob_ref, acc_sc):
    f = pl.program_id(1)

    @pl.when(f == 0)
    def _init():
        acc_sc[...] = jnp.zeros(acc_sc.shape, F32)

    hid = jnp.maximum(jnp.dot(xb_ref[...], w1_ref[...], preferred_element_type=F32), 0.0)
    hid = (hid * hid).astype(BF16)
    acc_sc[...] += jnp.dot(hid, w2_ref[...], preferred_element_type=F32)

    @pl.when(f == pl.num_programs(1) - 1)
    def _finish():
        y = _layer_norm_rows(ALPHA * xf_ref[...] + acc_sc[...], g_ref[...], b_ref[...])
        of_ref[...] = y
        ob_ref[...] = y.astype(BF16)


def _ffn_ln(xb, xf, w1_all, w2_all, layer, g, b, tm=512, tf=1024):
    s, d = xf.shape
    dff = w1_all.shape[2]
    row_spec = pl.BlockSpec((tm, d), lambda i, f: (i, 0))
    vec_spec = pl.BlockSpec((1, d), lambda i, f: (0, 0))
    return pl.pallas_call(
        _ffn_kernel,
        grid=(s // tm, dff // tf),
        in_specs=[row_spec, row_spec,
                  pl.BlockSpec((None, d, tf), lambda i, f: (layer, 0, f)),
                  pl.BlockSpec((None, tf, d), lambda i, f: (layer, f, 0)),
                  vec_spec, vec_spec],
        out_specs=[row_spec, row_spec],
        out_shape=[jax.ShapeDtypeStruct((s, d), F32), jax.ShapeDtypeStruct((s, d), BF16)],
        scratch_shapes=[pltpu.VMEM((tm, d), F32)],
        compiler_params=_cparams(("parallel", "arbitrary")),
        name="ffn_ln",
    )(xb, xf, w1_all, w2_all, g.reshape(1, d), b.reshape(1, d))


def _split_in_proj(w_in):
    qkv = jnp.concatenate([w_in[:, :, _OFF_FQ:_OFF_FF], w_in[:, :, _OFF_SQ:_OFF_END]], axis=2)
    pad = jnp.zeros(w_in.shape[:2] + (LANES - N_HEADS,), w_in.dtype)
    rest = jnp.concatenate([w_in[:, :, _OFF_RX:_OFF_SQ], w_in[:, :, _OFF_FF:_OFF_RX], pad], axis=2)
    return qkv.astype(BF16), rest.astype(BF16)


def kernel(x, ln_in_g, ln_in_b, w_in, b_forget, conv_w, conv_b, w_r, b_r, w_i, b_i, lru_lambda,
           rel_bias, w_branch, w_gate, b_gate, w_out, ln1_g, ln1_b, w_ff1, w_ff2, ln2_g, ln2_b):
    batch, s, d = x.shape
    assert (batch, s, d) == (1, SEQ, D_MODEL)
    w = BRANCH_WIDTH
    band_tq = 256

    col_scale = np.ones((1, 9 * w), np.float32)
    for q_block in (0, 3, 6):
        col_scale[:, q_block * w:(q_block + 1) * w] = QK_SCALE * LOG2_E
    qkv_scale = jnp.asarray(col_scale)
    rest_scale = jnp.ones((1, 2 * w + LANES), F32)

    w_qkv, w_rest = _split_in_proj(w_in)
    wg_b, wb_b, wo_b = w_gate.astype(BF16), w_branch.astype(BF16), w_out.astype(BF16)
    w1_b, w2_b = w_ff1.astype(BF16), w_ff2.astype(BF16)

    xf, xb = _entry_ln(x.reshape(s, d), ln_in_g, ln_in_b)
    for l in range(DEPTH):
        qkv = _project(xb, w_qkv, l, qkv_scale, BF16, 1024, 1536, "in_proj_qkv")
        uf = _project(xb, w_rest, l, rest_scale, F32, 1024, 1152, "in_proj_rest")

        f_rows = uf[:, 2 * w:2 * w + N_HEADS].T.reshape(N_HEADS * (s // LANES), LANES)
        b_rows = jnp.repeat(b_forget[l].astype(F32), s // LANES).reshape(-1, 1)
        cf = _forget_cumsum(f_rows, b_rows).reshape(N_HEADS, s)
        cfk = jnp.pad(cf, ((0, SUBLANES - N_HEADS), (0, 0)))

        o_fox = _fox_attention(qkv, cfk, 0)
        o_sb = _sb_attention(qkv, 3)
        o_ch = _band_attention(qkv, _band_bias_vector(rel_bias[l], band_tq), 6, band_tq)
        o_lru = _recurrent_branch(uf, conv_w[l], conv_b[l], w_r[l], b_r[l], w_i[l], b_i[l],
                                  lru_lambda[l])

        merged = _merge(xb, (o_fox, o_lru, o_sb, o_ch), wg_b, b_gate, wb_b, l)
        xf, xb = _outproj_ln(merged, wo_b, l, xf, ln1_g[l], ln1_b[l])
        xf, xb = _ffn_ln(xb, xf, w1_b, w2_b, l, ln2_g[l], ln2_b[l])
    return xf.reshape(batch, s, d)
```

```python
import functools
import math

import jax
import jax.numpy as jnp
import numpy as np
from jax import lax
from jax.experimental import pallas as pl
from jax.experimental.pallas import tpu as pltpu

F32 = jnp.float32
BF16 = jnp.bfloat16

D_MODEL = 2048
SEQ = 8192
DEPTH = 2
CHUNK = 64
HEAD_DIM = 128
N_BRANCH = 4
BRANCH_WIDTH = D_MODEL // N_BRANCH
N_HEADS = BRANCH_WIDTH // HEAD_DIM
CONV_WIDTH = 4
LRU_C = 8.0
LOOKBACK_CHUNKS = 8
REL_CLIP = 256
D_FF = 4 * D_MODEL
ALPHA = (2.0 * DEPTH) ** 0.25
LN_EPS = 1e-5
QK_SCALE = HEAD_DIM ** -0.5
LOG2_E = math.log2(math.e)

_OFF_FQ = 0
_OFF_FK = _OFF_FQ + BRANCH_WIDTH
_OFF_FV = _OFF_FK + BRANCH_WIDTH
_OFF_FF = _OFF_FV + BRANCH_WIDTH
_OFF_RX = _OFF_FF + N_HEADS
_OFF_RY = _OFF_RX + BRANCH_WIDTH
_OFF_SQ = _OFF_RY + BRANCH_WIDTH
_OFF_CQ = _OFF_SQ + 3 * BRANCH_WIDTH
_OFF_END = _OFF_CQ + 3 * BRANCH_WIDTH

LANES = 128
SUBLANES = 8
NEG_BIG = -1e30
SB_DEAD_LOG2 = -180.0

VMEM_LIMIT = 56 * 1024 * 1024


def _cparams(sem, vmem=VMEM_LIMIT):
    return pltpu.CompilerParams(dimension_semantics=sem, vmem_limit_bytes=vmem)


def _log_sigmoid(x):
    return jnp.minimum(x, 0.0) - jnp.log1p(jnp.exp(-jnp.abs(x)))


def _layer_norm_rows(y, g, b):
    mu = jnp.mean(y, axis=-1, keepdims=True)
    d = y - mu
    var = jnp.mean(d * d, axis=-1, keepdims=True)
    return d * lax.rsqrt(var + LN_EPS) * g + b


def _ln_kernel(x_ref, g_ref, b_ref, of_ref, ob_ref):
    y = _layer_norm_rows(x_ref[...], g_ref[...], b_ref[...])
    of_ref[...] = y
    ob_ref[...] = y.astype(BF16)


def _entry_ln(x, g, b, tm=512):
    s, d = x.shape
    return pl.pallas_call(
        _ln_kernel,
        grid=(s // tm,),
        in_specs=[pl.BlockSpec((tm, d), lambda i: (i, 0)),
                  pl.BlockSpec((1, d), lambda i: (0, 0)),
                  pl.BlockSpec((1, d), lambda i: (0, 0))],
        out_specs=[pl.BlockSpec((tm, d), lambda i: (i, 0)),
                   pl.BlockSpec((tm, d), lambda i: (i, 0))],
        out_shape=[jax.ShapeDtypeStruct((s, d), F32),
                   jax.ShapeDtypeStruct((s, d), BF16)],
        compiler_params=_cparams(("parallel",)),
        name="entry_ln",
    )(x, g.reshape(1, d), b.reshape(1, d))


def _proj_kernel(x_ref, w_ref, s_ref, o_ref):
    acc = jnp.dot(x_ref[...], w_ref[...], preferred_element_type=F32)
    o_ref[...] = (acc * s_ref[...]).astype(o_ref.dtype)


def _project(xb, w_all, layer, colscale, out_dtype, tm, tn, name):
    m, k = xb.shape
    n = w_all.shape[2]
    return pl.pallas_call(
        _proj_kernel,
        grid=(m // tm, n // tn),
        in_specs=[pl.BlockSpec((tm, k), lambda i, j: (i, 0)),
                  pl.BlockSpec((None, k, tn), lambda i, j: (layer, 0, j)),
                  pl.BlockSpec((1, tn), lambda i, j: (0, j))],
        out_specs=pl.BlockSpec((tm, tn), lambda i, j: (i, j)),
        out_shape=jax.ShapeDtypeStruct((m, n), out_dtype),
        compiler_params=_cparams(("parallel", "arbitrary")),
        name=name,
    )(xb, w_all, colscale)


def _forget_cumsum_kernel(f_ref, b_ref, o_ref):
    rows = f_ref.shape[0]
    per_head = rows // N_HEADS
    ls = _log_sigmoid(f_ref[...] + b_ref[...])
    r = lax.broadcasted_iota(jnp.int32, (LANES, LANES), 0)
    c = lax.broadcasted_iota(jnp.int32, (LANES, LANES), 1)
    upper = (r <= c).astype(F32)
    within = jnp.dot(ls, upper, preferred_element_type=F32,
                     precision=lax.Precision.HIGHEST)
    total = within[:, LANES - 1:LANES]
    rr = lax.broadcasted_iota(jnp.int32, (rows, rows), 0)
    cc = lax.broadcasted_iota(jnp.int32, (rows, rows), 1)
    head_start = rr - (rr & (per_head - 1))
    before = ((cc >= head_start) & (cc < rr)).astype(F32)
    offs = jnp.dot(before, jnp.broadcast_to(total, (rows, LANES)),
                   preferred_element_type=F32, precision=lax.Precision.HIGHEST)
    o_ref[...] = (within + offs) * LOG2_E


def _forget_cumsum(f_rows, b_rows):
    rows = f_rows.shape[0]
    return pl.pallas_call(
        _forget_cumsum_kernel,
        out_shape=jax.ShapeDtypeStruct((rows, LANES), F32),
        name="forget_cumsum",
    )(f_rows, b_rows)


def _fox_kernel(q_ref, k_ref, v_ref, cfk_ref, o_ref, m_sc, acc_sc, vaug_sc, *, tq, tk):
    i = pl.program_id(0)
    m_sc[...] = jnp.full(m_sc.shape, NEG_BIG, F32)
    acc_sc[...] = jnp.zeros(acc_sc.shape, F32)
    vaug_sc[:, :, HEAD_DIM:] = jnp.ones((N_HEADS, tk, HEAD_DIM), BF16)
    rep = tk // LANES

    def key_tile(j, masked):
        k0 = pl.multiple_of(j * tk, tk)
        if masked:
            keep = (lax.broadcasted_iota(jnp.int32, (tq, tk), 1)
                    <= lax.broadcasted_iota(jnp.int32, (tq, tk), 0))
        heads = [slice(h * HEAD_DIM, (h + 1) * HEAD_DIM) for h in range(N_HEADS)]
        scores = [lax.dot_general(q_ref[:, hs], k_ref[pl.ds(k0, tk), hs], (((1,), (1,)), ((), ())),
                                  preferred_element_type=F32) for hs in heads]
        probs, alphas = [], []
        for h, s in enumerate(scores):
            s = s - cfk_ref[h:h + 1, pl.ds(k0, tk)]
            if masked:
                s = jnp.where(keep, s, NEG_BIG)
            m_old = m_sc[h]
            m_new = jnp.maximum(m_old, jnp.max(s, axis=-1, keepdims=True))
            alphas.append(jnp.exp2(m_old - m_new))
            probs.append(jnp.exp2(s - jnp.concatenate([m_new] * rep, axis=1)).astype(BF16))
            m_sc[h] = m_new
            vaug_sc[h, :, :HEAD_DIM] = v_ref[pl.ds(k0, tk), heads[h]]
        for h, (p, alpha) in enumerate(zip(probs, alphas)):
            pv = jnp.dot(p, vaug_sc[h], preferred_element_type=F32)
            acc_sc[h] = jnp.concatenate([alpha, alpha], axis=1) * acc_sc[h] + pv

    def off_diagonal(j, carry):
        key_tile(j, False)
        return carry

    lax.fori_loop(0, i, off_diagonal, 0)
    key_tile(i, True)
    for h in range(N_HEADS):
        hs = slice(h * HEAD_DIM, (h + 1) * HEAD_DIM)
        o_ref[:, hs] = (acc_sc[h, :, :HEAD_DIM] / acc_sc[h, :, HEAD_DIM:]).astype(o_ref.dtype)


def _fox_attention(qkv, cfk, col0, tq=512):
    s = qkv.shape[0]
    tk = tq
    w = BRANCH_WIDTH
    kern = functools.partial(_fox_kernel, tq=tq, tk=tk)
    resident = pl.Buffered(1)
    return pl.pallas_call(
        kern,
        grid=(s // tq,),
        in_specs=[pl.BlockSpec((tq, w), lambda i: (i, col0)),
                  pl.BlockSpec((s, w), lambda i: (0, col0 + 1), pipeline_mode=resident),
                  pl.BlockSpec((s, w), lambda i: (0, col0 + 2), pipeline_mode=resident),
                  pl.BlockSpec((SUBLANES, s), lambda i: (0, 0), pipeline_mode=resident)],
        out_specs=pl.BlockSpec((tq, w), lambda i: (i, 0)),
        out_shape=jax.ShapeDtypeStruct((s, w), BF16),
        scratch_shapes=[pltpu.VMEM((N_HEADS, tq, LANES), F32),
                        pltpu.VMEM((N_HEADS, tq, 2 * HEAD_DIM), F32),
                        pltpu.VMEM((N_HEADS, tk, 2 * HEAD_DIM), BF16)],
        compiler_params=_cparams(("arbitrary",)),
        name="fox_attention",
    )(qkv, qkv, qkv, cfk)


def _sb_kernel(q_ref, k_ref, v_ref, o_ref, run_sc, acc_sc, *, tq):
    i = pl.program_id(0)
    nsub = tq // LANES
    r = lax.broadcasted_iota(jnp.int32, (2 * LANES, 2 * LANES), 0) & (LANES - 1)
    c = lax.broadcasted_iota(jnp.int32, (2 * LANES, 2 * LANES), 1)
    tri_aug = ((c >= LANES) | (r > c)).astype(BF16)

    def sub_blocks(items, masked):
        scores = []
        for h, k0, r0 in items:
            hs = slice(h * HEAD_DIM, (h + 1) * HEAD_DIM)
            scores.append(lax.dot_general(q_ref[r0:, hs], k_ref[pl.ds(k0, LANES), hs],
                                          (((1,), (1,)), ((), ())), preferred_element_type=F32))
        log_beta, sums, keeps = [], [], []
        for (h, k0, r0), z in zip(items, scores):
            rows = tq - r0
            lp = jnp.minimum(z, 0.0) - jnp.log2(1.0 + jnp.exp2(-jnp.abs(z)))
            ln = lp - z
            keep = None
            if masked:
                keep = (lax.broadcasted_iota(jnp.int32, (rows, LANES), 1)
                        < lax.broadcasted_iota(jnp.int32, (rows, LANES), 0))
                ln = jnp.where(keep, ln, 0.0)
            ln_hi = ln.astype(BF16)
            ln_lo = (ln - ln_hi.astype(F32)).astype(BF16)
            log_beta.append(lp)
            keeps.append(keep)
            sums.append(jnp.dot(jnp.concatenate([ln_hi, ln_lo], axis=1), tri_aug,
                                preferred_element_type=F32))
        weights = []
        for (h, k0, r0), lp, la, keep in zip(items, log_beta, sums, keeps):
            run = run_sc[h, r0:, :]
            a = jnp.exp2(lp + la[:, :LANES] + run)
            if masked:
                a = jnp.where(keep, a, 0.0)
            run_sc[h, r0:, :] = run + la[:, LANES:]
            weights.append(a.astype(BF16))
        for (h, k0, r0), a in zip(items, weights):
            hs = slice(h * HEAD_DIM, (h + 1) * HEAD_DIM)
            acc_sc[h, r0:, :] += jnp.dot(a, v_ref[pl.ds(k0, LANES), hs], preferred_element_type=F32)

    run_sc[...] = jnp.zeros(run_sc.shape, F32)
    acc_sc[...] = jnp.zeros(acc_sc.shape, F32)
    for cc in range(nsub - 1, -1, -1):
        k0 = pl.multiple_of(i * tq + cc * LANES, LANES)
        sub_blocks([(h, k0, cc * LANES) for h in range(N_HEADS)], True)

    def alive():
        return (jnp.max(run_sc[...]) > SB_DEAD_LOG2).astype(jnp.int32)

    def earlier(state):
        cb, _ = state
        sub_blocks([(h, pl.multiple_of((cb - back) * LANES, LANES), 0)
                    for back in range(2) for h in range(N_HEADS)], False)
        return cb - 2, alive()

    lax.while_loop(lambda st: jnp.logical_and(st[0] >= 0, st[1] > 0), earlier,
                   (i * nsub - 1, alive()))
    for h in range(N_HEADS):
        o_ref[:, h * HEAD_DIM:(h + 1) * HEAD_DIM] = acc_sc[h].astype(o_ref.dtype)


def _sb_attention(qkv, col0, tq=512):
    s = qkv.shape[0]
    w = BRANCH_WIDTH
    assert (tq // LANES) % 2 == 0
    kern = functools.partial(_sb_kernel, tq=tq)
    resident = pl.Buffered(1)
    return pl.pallas_call(
        kern,
        grid=(s // tq,),
        in_specs=[pl.BlockSpec((tq, w), lambda i: (i, col0)),
                  pl.BlockSpec((s, w), lambda i: (0, col0 + 1), pipeline_mode=resident),
                  pl.BlockSpec((s, w), lambda i: (0, col0 + 2), pipeline_mode=resident)],
        out_specs=pl.BlockSpec((tq, w), lambda i: (i, 0)),
        out_shape=jax.ShapeDtypeStruct((s, w), BF16),
        scratch_shapes=[pltpu.VMEM((N_HEADS, tq, LANES), F32),
                        pltpu.VMEM((N_HEADS, tq, HEAD_DIM), F32)],
        compiler_params=_cparams(("arbitrary",)),
        name="sb_attention",
    )(qkv, qkv, qkv)


def _band_kernel(q_ref, k2_ref, k1_ref, k0_ref, v2_ref, v1_ref, v0_ref, ext_ref, o_ref,
                 bias_sc, vaug_sc, *, tq):
    i = pl.program_id(0)
    width = 4 * tq

    @pl.when(i == 0)
    def _build_tables():
        trow = lax.broadcasted_iota(jnp.int32, (tq, width), 0)
        t = lax.broadcasted_iota(jnp.int32, (tq, 3 * tq), 0)
        s = lax.broadcasted_iota(jnp.int32, (tq, 3 * tq), 1)
        shift = CHUNK.bit_length() - 1
        t_chunk = t >> shift
        s_chunk = (s >> shift) - (2 * tq) // CHUNK
        in_band = (t_chunk - s_chunk <= LOOKBACK_CHUNKS) & (s_chunk <= t_chunk)
        for h in range(N_HEADS):
            x = jnp.broadcast_to(ext_ref[h:h + 1, :], (tq, width))
            for b in range(tq.bit_length() - 1):
                x = jnp.where(((trow >> b) & 1) == 1, pltpu.roll(x, 1 << b, axis=1), x)
            bias_sc[h] = jnp.where(in_band, x[:, :3 * tq] * LOG2_E, NEG_BIG)
        vaug_sc[:, :, HEAD_DIM:] = jnp.ones((N_HEADS, 3 * tq, HEAD_DIM), BF16)

    k_refs = (k2_ref, k1_ref, k0_ref)
    v_refs = (v2_ref, v1_ref, v0_ref)
    heads = [slice(h * HEAD_DIM, (h + 1) * HEAD_DIM) for h in range(N_HEADS)]

    def tile(first_tiles):
        scores = [[lax.dot_general(q_ref[:, hs], k_refs[p][:, hs], (((1,), (1,)), ((), ())),
                                   preferred_element_type=F32) for p in range(3)] for hs in heads]
        probs = []
        for h, pieces in enumerate(scores):
            pieces = [s + bias_sc[h, :, p * tq:(p + 1) * tq] for p, s in enumerate(pieces)]
            if first_tiles:
                pieces = [jnp.where(i - 2 + p >= 0, s, NEG_BIG) for p, s in enumerate(pieces)]
            m = jnp.max(jnp.maximum(jnp.maximum(pieces[0], pieces[1]), pieces[2]),
                        axis=-1, keepdims=True)
            probs.append(jnp.concatenate([jnp.exp2(s - m).astype(BF16) for s in pieces], axis=1))
            for p in range(3):
                vaug_sc[h, p * tq:(p + 1) * tq, :HEAD_DIM] = v_refs[p][:, heads[h]]
        for h, p in enumerate(probs):
            acc = jnp.dot(p, vaug_sc[h], preferred_element_type=F32)
            o_ref[:, heads[h]] = (acc[:, :HEAD_DIM] / acc[:, HEAD_DIM:]).astype(o_ref.dtype)

    @pl.when(i < 2)
    def _first_tiles():
        tile(True)

    @pl.when(i >= 2)
    def _other_tiles():
        tile(False)


def _band_bias_vector(rel_bias_l, tq):
    n = np.arange(4 * tq)
    dist = np.where(n < 3 * tq, 2 * tq - n, 6 * tq - n)
    ridx = np.clip(dist, -(CHUNK - 1), REL_CLIP) + (CHUNK - 1)
    return rel_bias_l.astype(F32)[:, ridx]


def _band_attention(qkv, bias_ext, col0, tq=256):
    s = qkv.shape[0]
    w = BRANCH_WIDTH
    assert 2 * tq >= LOOKBACK_CHUNKS * CHUNK and tq % CHUNK == 0 and tq & (tq - 1) == 0
    kern = functools.partial(_band_kernel, tq=tq)

    def kv_spec(back, col):
        return pl.BlockSpec((tq, w), lambda i: (jnp.maximum(i - back, 0), col))

    return pl.pallas_call(
        kern,
        grid=(s // tq,),
        in_specs=[pl.BlockSpec((tq, w), lambda i: (i, col0)),
                  kv_spec(2, col0 + 1), kv_spec(1, col0 + 1), kv_spec(0, col0 + 1),
                  kv_spec(2, col0 + 2), kv_spec(1, col0 + 2), kv_spec(0, col0 + 2),
                  pl.BlockSpec((N_HEADS, 4 * tq), lambda i: (0, 0))],
        out_specs=pl.BlockSpec((tq, w), lambda i: (i, 0)),
        out_shape=jax.ShapeDtypeStruct((s, w), BF16),
        scratch_shapes=[pltpu.VMEM((N_HEADS, tq, 3 * tq), F32),
                        pltpu.VMEM((N_HEADS, 3 * tq, 2 * HEAD_DIM), BF16)],
        compiler_params=_cparams(("arbitrary",)),
        name="band_attention",
    )(qkv, qkv, qkv, qkv, qkv, qkv, qkv, bias_ext)


def _gelu_tanh(x):
    c = math.sqrt(2.0 / math.pi)
    return 0.5 * x * (1.0 + jnp.tanh(c * (x + 0.044715 * (x * x * x))))


def _lru_kernel(rx_ref, ry_ref, cw_ref, cb_ref, wr_ref, br_ref, wi_ref, bi_ref, lam_ref,
                o_ref, xext_sc, a_sc, b_sc, h_sc, carry_sc, *, tm):
    i = pl.program_id(0)
    w = BRANCH_WIDTH
    halo = SUBLANES

    @pl.when(i == 0)
    def _first():
        xext_sc[0:halo, :] = jnp.zeros((halo, w), F32)
        carry_sc[...] = jnp.zeros(carry_sc.shape, F32)

    @pl.when(i > 0)
    def _shift_halo():
        xext_sc[0:halo, :] = xext_sc[tm:tm + halo, :]

    xext_sc[halo:halo + tm, :] = rx_ref[...]
    xext = xext_sc[0:halo + tm, :]
    xc = cb_ref[...] + rx_ref[...] * cw_ref[CONV_WIDTH - 1:CONV_WIDTH, :]
    for back in range(1, CONV_WIDTH):
        t = CONV_WIDTH - 1 - back
        xc = xc + pltpu.roll(xext, back, axis=0)[halo:, :] * cw_ref[t:t + 1, :]

    xcb = xc.astype(BF16)
    r_parts, i_parts = [], []
    for n in range(N_HEADS):
        ns = slice(n * HEAD_DIM, (n + 1) * HEAD_DIM)
        r_parts.append(jnp.dot(xcb[:, ns], wr_ref[n], preferred_element_type=F32))
        i_parts.append(jnp.dot(xcb[:, ns], wi_ref[n], preferred_element_type=F32))
    r = jax.nn.sigmoid(jnp.concatenate(r_parts, axis=1) + br_ref[...])
    gi = jax.nn.sigmoid(jnp.concatenate(i_parts, axis=1) + bi_ref[...])
    log_a = LRU_C * r * _log_sigmoid(lam_ref[...])
    a = jnp.exp(log_a)
    a_sc[...] = a
    b_sc[...] = jnp.sqrt(-jnp.tanh(log_a) * (a * a + 1.0)) * (gi * xc)

    row = lax.broadcasted_iota(jnp.int32, (SUBLANES, w), 0)

    def group(g, carry):
        r0 = pl.multiple_of(g * SUBLANES, SUBLANES)
        a = a_sc[pl.ds(r0, SUBLANES), :]
        b = b_sc[pl.ds(r0, SUBLANES), :]
        for k in (1, 2, 4):
            a_prev = pltpu.roll(a, k, axis=0)
            b_prev = pltpu.roll(b, k, axis=0)
            ok = row >= k
            b = jnp.where(ok, a * b_prev + b, b)
            a = jnp.where(ok, a * a_prev, a)
        hgrp = a * carry + b
        h_sc[pl.ds(r0, SUBLANES), :] = hgrp
        return jnp.broadcast_to(hgrp[SUBLANES - 1:SUBLANES, :], (SUBLANES, w))

    carry_sc[...] = lax.fori_loop(0, tm // SUBLANES, group, carry_sc[...])
    o_ref[...] = (h_sc[...] * _gelu_tanh(ry_ref[...])).astype(o_ref.dtype)


def _recurrent_branch(uf, conv_w, conv_b, w_r, b_r, w_i, b_i, lam, tm=512):
    s = uf.shape[0]
    w = BRANCH_WIDTH
    kern = functools.partial(_lru_kernel, tm=tm)
    row = lambda v: v.reshape(1, w)
    full2 = lambda shape: pl.BlockSpec(shape, lambda i: (0, 0))
    full3 = lambda shape: pl.BlockSpec(shape, lambda i: (0, 0, 0))
    return pl.pallas_call(
        kern,
        grid=(s // tm,),
        in_specs=[pl.BlockSpec((tm, w), lambda i: (i, 0)),
                  pl.BlockSpec((tm, w), lambda i: (i, 1)),
                  full2((CONV_WIDTH, w)), full2((1, w)),
                  full3((N_HEADS, HEAD_DIM, HEAD_DIM)), full2((1, w)),
                  full3((N_HEADS, HEAD_DIM, HEAD_DIM)), full2((1, w)),
                  full2((1, w))],
        out_specs=pl.BlockSpec((tm, w), lambda i: (i, 0)),
        out_shape=jax.ShapeDtypeStruct((s, w), BF16),
        scratch_shapes=[pltpu.VMEM((tm + 2 * SUBLANES, w), F32),
                        pltpu.VMEM((tm, w), F32),
                        pltpu.VMEM((tm, w), F32),
                        pltpu.VMEM((tm, w), F32),
                        pltpu.VMEM((SUBLANES, w), F32)],
        compiler_params=_cparams(("arbitrary",)),
        name="recurrent_branch",
    )(uf, uf, conv_w, row(conv_b), w_r.astype(BF16), row(b_r), w_i.astype(BF16), row(b_i), row(lam))


def _merge_kernel(x_ref, o0_ref, o1_ref, o2_ref, o3_ref, wg_ref, bg_ref, wb_ref, out_ref):
    x = x_ref[...]
    merged = None
    for g, o_ref in enumerate((o0_ref, o1_ref, o2_ref, o3_ref)):
        gate = jax.nn.sigmoid(jnp.dot(x, wg_ref[g].astype(BF16), preferred_element_type=F32)
                              + bg_ref[g:g + 1, :])
        term = gate * jnp.dot(o_ref[...], wb_ref[g].astype(BF16), preferred_element_type=F32)
        merged = term if merged is None else merged + term
    out_ref[...] = merged.astype(out_ref.dtype)


def _merge(xb, branches, wg_all, bg_all, wb_all, layer, tm=1024, tn=256):
    s, d = xb.shape
    w = BRANCH_WIDTH
    o_spec = pl.BlockSpec((tm, w), lambda i, j: (i, 0))
    return pl.pallas_call(
        _merge_kernel,
        grid=(s // tm, d // tn),
        in_specs=[pl.BlockSpec((tm, d), lambda i, j: (i, 0)),
                  o_spec, o_spec, o_spec, o_spec,
                  pl.BlockSpec((None, N_BRANCH, d, tn), lambda i, j: (layer, 0, 0, j)),
                  pl.BlockSpec((None, N_BRANCH, tn), lambda i, j: (layer, 0, j)),
                  pl.BlockSpec((None, N_BRANCH, w, tn), lambda i, j: (layer, 0, 0, j))],
        out_specs=pl.BlockSpec((tm, tn), lambda i, j: (i, j)),
        out_shape=jax.ShapeDtypeStruct((s, d), BF16),
        compiler_params=_cparams(("parallel", "arbitrary")),
        name="gated_merge",
    )(xb, *branches, wg_all, bg_all, wb_all)


def _outproj_kernel(m_ref, w_ref, x_ref, g_ref, b_ref, of_ref, ob_ref):
    half = m_ref.shape[0] // 2
    rows = (slice(0, half), slice(half, 2 * half))
    proj = [jnp.dot(m_ref[r, :], w_ref[...], preferred_element_type=F32) for r in rows]
    for r, p in zip(rows, proj):
        y = _layer_norm_rows(ALPHA * x_ref[r, :] + p, g_ref[...], b_ref[...])
        of_ref[r, :] = y
        ob_ref[r, :] = y.astype(BF16)


def _outproj_ln(merged, w_out_all, layer, x, g, b, tm=512):
    s, d = x.shape
    row_spec = pl.BlockSpec((tm, d), lambda i: (i, 0))
    vec_spec = pl.BlockSpec((1, d), lambda i: (0, 0))
    return pl.pallas_call(
        _outproj_kernel,
        grid=(s // tm,),
        in_specs=[row_spec, pl.BlockSpec((None, d, d), lambda i: (layer, 0, 0)), row_spec,
                  vec_spec, vec_spec],
        out_specs=[row_spec, row_spec],
        out_shape=[jax.ShapeDtypeStruct((s, d), F32), jax.ShapeDtypeStruct((s, d), BF16)],
        compiler_params=_cparams(("parallel",)),
        name="outproj_ln",
    )(merged, w_out_all, x, g.reshape(1, d), b.reshape(1, d))


def _ffn_kernel(xb_ref, xf_ref, w1_ref, w2_ref, g_ref, b_ref, of_ref, ob_ref, acc_sc):
    f = pl.program_id(1)

    @pl.when(f == 0)
    def _init():
        acc_sc[...] = jnp.zeros(acc_sc.shape, F32)

    hid = jnp.maximum(jnp.dot(xb_ref[...], w1_ref[...], preferred_element_type=F32), 0.0)
    hid = (hid * hid).astype(BF16)
    acc_sc[...] += jnp.dot(hid, w2_ref[...], preferred_element_type=F32)

    @pl.when(f == pl.num_programs(1) - 1)
    def _finish():
        y = _layer_norm_rows(ALPHA * xf_ref[...] + acc_sc[...], g_ref[...], b_ref[...])
        of_ref[...] = y
        ob_ref[...] = y.astype(BF16)


def _ffn_ln(xb, xf, w1_all, w2_all, layer, g, b, tm=512, tf=1024):
    s, d = xf.shape
    dff = w1_all.shape[2]
    row_spec = pl.BlockSpec((tm, d), lambda i, f: (i, 0))
    vec_spec = pl.BlockSpec((1, d), lambda i, f: (0, 0))
    return pl.pallas_call(
        _ffn_kernel,
        grid=(s // tm, dff // tf),
        in_specs=[row_spec, row_spec,
                  pl.BlockSpec((None, d, tf), lambda i, f: (layer, 0, f)),
                  pl.BlockSpec((None, tf, d), lambda i, f: (layer, f, 0)),
                  vec_spec, vec_spec],
        out_specs=[row_spec, row_spec],
        out_shape=[jax.ShapeDtypeStruct((s, d), F32), jax.ShapeDtypeStruct((s, d), BF16)],
        scratch_shapes=[pltpu.VMEM((tm, d), F32)],
        compiler_params=_cparams(("parallel", "arbitrary")),
        name="ffn_ln",
    )(xb, xf, w1_all, w2_all, g.reshape(1, d), b.reshape(1, d))


def _split_in_proj_kernel(w_ref, qkv_ref, rest_ref):
    w2 = 2 * BRANCH_WIDTH
    qkv_ref[:, :_OFF_FF] = w_ref[:, :_OFF_FF].astype(BF16)
    qkv_ref[:, _OFF_FF:] = w_ref[:, _OFF_SQ:_OFF_END].astype(BF16)
    rest_ref[:, :w2] = w_ref[:, _OFF_RX:_OFF_SQ].astype(BF16)
    lane = lax.broadcasted_iota(jnp.int32, (w_ref.shape[0], LANES), 1)
    forget = w_ref[:, _OFF_FF:_OFF_FF + LANES]
    rest_ref[:, w2:] = jnp.where(lane < N_HEADS, forget, 0.0).astype(BF16)


def _split_in_proj(w_in, tk=256):
    layers, d, d_in = w_in.shape
    n_qkv, n_rest = 9 * BRANCH_WIDTH, 2 * BRANCH_WIDTH + LANES
    return pl.pallas_call(
        _split_in_proj_kernel,
        grid=(layers, d // tk),
        in_specs=[pl.BlockSpec((None, tk, d_in), lambda l, i: (l, i, 0))],
        out_specs=[pl.BlockSpec((None, tk, n_qkv), lambda l, i: (l, i, 0)),
                   pl.BlockSpec((None, tk, n_rest), lambda l, i: (l, i, 0))],
        out_shape=[jax.ShapeDtypeStruct((layers, d, n_qkv), BF16),
                   jax.ShapeDtypeStruct((layers, d, n_rest), BF16)],
        compiler_params=_cparams(("parallel", "parallel")),
        name="split_in_proj",
    )(w_in)


def kernel(x, ln_in_g, ln_in_b, w_in, b_forget, conv_w, conv_b, w_r, b_r, w_i, b_i, lru_lambda,
           rel_bias, w_branch, w_gate, b_gate, w_out, ln1_g, ln1_b, w_ff1, w_ff2, ln2_g, ln2_b):
    batch, s, d = x.shape
    assert (batch, s, d) == (1, SEQ, D_MODEL)
    w = BRANCH_WIDTH
    band_tq = 256

    col_scale = np.ones((1, 9 * w), np.float32)
    for q_block in (0, 3, 6):
        col_scale[:, q_block * w:(q_block + 1) * w] = QK_SCALE * LOG2_E
    qkv_scale = jnp.asarray(col_scale)
    rest_scale = jnp.ones((1, 2 * w + LANES), F32)

    w_qkv, w_rest = _split_in_proj(w_in)
    wo_b = w_out.astype(BF16)
    w1_b, w2_b = w_ff1.astype(BF16), w_ff2.astype(BF16)

    xf, xb = _entry_ln(x.reshape(s, d), ln_in_g, ln_in_b)
    for l in range(DEPTH):
        qkv = _project(xb, w_qkv, l, qkv_scale, BF16, 1024, 1536, "in_proj_qkv")
        uf = _project(xb, w_rest, l, rest_scale, F32, 1024, 1152, "in_proj_rest")

        f_rows = uf[:, 2 * w:2 * w + N_HEADS].T.reshape(N_HEADS * (s // LANES), LANES)
        b_rows = jnp.repeat(b_forget[l].astype(F32), s // LANES).reshape(-1, 1)
        cf = _forget_cumsum(f_rows, b_rows).reshape(N_HEADS, s)
        cfk = jnp.pad(cf, ((0, SUBLANES - N_HEADS), (0, 0)))

        o_fox = _fox_attention(qkv, cfk, 0)
        o_sb = _sb_attention(qkv, 3)
        o_ch = _band_attention(qkv, _band_bias_vector(rel_bias[l], band_tq), 6, band_tq)
        o_lru = _recurrent_branch(uf, conv_w[l], conv_b[l], w_r[l], b_r[l], w_i[l], b_i[l],
                                  lru_lambda[l])

        merged = _merge(xb, (o_fox, o_lru, o_sb, o_ch), w_gate, b_gate, w_branch, l)
        xf, xb = _outproj_ln(merged, wo_b, l, xf, ln1_g[l], ln1_b[l])
        xf, xb = _ffn_ln(xb, xf, w1_b, w2_b, l, ln2_g[l], ln2_b[l])
    return xf.reshape(batch, s, d)
```

```python
import functools
import math

import jax
import jax.numpy as jnp
import numpy as np
from jax import lax
from jax.experimental import pallas as pl
from jax.experimental.pallas import tpu as pltpu

F32 = jnp.float32
BF16 = jnp.bfloat16

D_MODEL = 2048
SEQ = 8192
DEPTH = 2
CHUNK = 64
HEAD_DIM = 128
N_BRANCH = 4
BRANCH_WIDTH = D_MODEL // N_BRANCH
N_HEADS = BRANCH_WIDTH // HEAD_DIM
CONV_WIDTH = 4
LRU_C = 8.0
LOOKBACK_CHUNKS = 8
REL_CLIP = 256
D_FF = 4 * D_MODEL
ALPHA = (2.0 * DEPTH) ** 0.25
LN_EPS = 1e-5
QK_SCALE = HEAD_DIM ** -0.5
LOG2_E = math.log2(math.e)

_OFF_FQ = 0
_OFF_FK = _OFF_FQ + BRANCH_WIDTH
_OFF_FV = _OFF_FK + BRANCH_WIDTH
_OFF_FF = _OFF_FV + BRANCH_WIDTH
_OFF_RX = _OFF_FF + N_HEADS
_OFF_RY = _OFF_RX + BRANCH_WIDTH
_OFF_SQ = _OFF_RY + BRANCH_WIDTH
_OFF_CQ = _OFF_SQ + 3 * BRANCH_WIDTH
_OFF_END = _OFF_CQ + 3 * BRANCH_WIDTH

LANES = 128
SUBLANES = 8
NEG_BIG = -1e30
SB_DEAD_LOG2 = -180.0
FOX_DEAD_LOG2 = -170.0

VMEM_LIMIT = 56 * 1024 * 1024


def _cparams(sem, vmem=VMEM_LIMIT):
    return pltpu.CompilerParams(dimension_semantics=sem, vmem_limit_bytes=vmem)


def _log_sigmoid(x):
    return jnp.minimum(x, 0.0) - jnp.log1p(jnp.exp(-jnp.abs(x)))


def _layer_norm_rows(y, g, b):
    mu = jnp.mean(y, axis=-1, keepdims=True)
    d = y - mu
    var = jnp.mean(d * d, axis=-1, keepdims=True)
    return d * lax.rsqrt(var + LN_EPS) * g + b


def _ln_kernel(x_ref, g_ref, b_ref, of_ref, ob_ref):
    y = _layer_norm_rows(x_ref[...], g_ref[...], b_ref[...])
    of_ref[...] = y
    ob_ref[...] = y.astype(BF16)


def _entry_ln(x, g, b, tm=512):
    s, d = x.shape
    return pl.pallas_call(
        _ln_kernel,
        grid=(s // tm,),
        in_specs=[pl.BlockSpec((tm, d), lambda i: (i, 0)),
                  pl.BlockSpec((1, d), lambda i: (0, 0)),
                  pl.BlockSpec((1, d), lambda i: (0, 0))],
        out_specs=[pl.BlockSpec((tm, d), lambda i: (i, 0)),
                   pl.BlockSpec((tm, d), lambda i: (i, 0))],
        out_shape=[jax.ShapeDtypeStruct((s, d), F32),
                   jax.ShapeDtypeStruct((s, d), BF16)],
        compiler_params=_cparams(("parallel",)),
        name="entry_ln",
    )(x, g.reshape(1, d), b.reshape(1, d))


def _proj_kernel(x_ref, w_ref, s_ref, o_ref):
    acc = jnp.dot(x_ref[...], w_ref[...], preferred_element_type=F32)
    o_ref[...] = (acc * s_ref[...]).astype(o_ref.dtype)


def _project(xb, w_all, layer, colscale, out_dtype, tm, tn, name):
    m, k = xb.shape
    n = w_all.shape[2]
    return pl.pallas_call(
        _proj_kernel,
        grid=(m // tm, n // tn),
        in_specs=[pl.BlockSpec((tm, k), lambda i, j: (i, 0)),
                  pl.BlockSpec((None, k, tn), lambda i, j: (layer, 0, j)),
                  pl.BlockSpec((1, tn), lambda i, j: (0, j))],
        out_specs=pl.BlockSpec((tm, tn), lambda i, j: (i, j)),
        out_shape=jax.ShapeDtypeStruct((m, n), out_dtype),
        compiler_params=_cparams(("parallel", "arbitrary")),
        name=name,
    )(xb, w_all, colscale)


def _forget_cumsum_kernel(f_ref, b_ref, o_ref):
    rows = f_ref.shape[0]
    per_head = rows // N_HEADS
    ls = _log_sigmoid(f_ref[...] + b_ref[...])
    r = lax.broadcasted_iota(jnp.int32, (LANES, LANES), 0)
    c = lax.broadcasted_iota(jnp.int32, (LANES, LANES), 1)
    upper = (r <= c).astype(F32)
    within = jnp.dot(ls, upper, preferred_element_type=F32,
                     precision=lax.Precision.HIGHEST)
    total = within[:, LANES - 1:LANES]
    rr = lax.broadcasted_iota(jnp.int32, (rows, rows), 0)
    cc = lax.broadcasted_iota(jnp.int32, (rows, rows), 1)
    head_start = rr - (rr & (per_head - 1))
    before = ((cc >= head_start) & (cc < rr)).astype(F32)
    offs = jnp.dot(before, jnp.broadcast_to(total, (rows, LANES)),
                   preferred_element_type=F32, precision=lax.Precision.HIGHEST)
    o_ref[...] = (within + offs) * LOG2_E


def _forget_cumsum(f_rows, b_rows):
    rows = f_rows.shape[0]
    return pl.pallas_call(
        _forget_cumsum_kernel,
        out_shape=jax.ShapeDtypeStruct((rows, LANES), F32),
        name="forget_cumsum",
    )(f_rows, b_rows)


def _fox_kernel(q_ref, k_ref, v_ref, cfk_ref, o_ref, m_sc, acc_sc, vaug_sc, qn_sc, kn_sc, *, tq, tk):
    i = pl.program_id(0)
    heads = [slice(h * HEAD_DIM, (h + 1) * HEAD_DIM) for h in range(N_HEADS)]
    norm_rows = 1024

    @pl.when(i == 0)
    def _largest_key_norm():
        for h, hs in enumerate(heads):
            def chunk(c, best, hs=hs):
                rows = k_ref[pl.ds(pl.multiple_of(c * norm_rows, norm_rows), norm_rows), hs].astype(F32)
                return jnp.maximum(best, jnp.max(jnp.sum(rows * rows, axis=-1, keepdims=True)))
            best = lax.fori_loop(0, k_ref.shape[0] // norm_rows, chunk, jnp.zeros((SUBLANES, LANES), F32))
            kn_sc[h] = jnp.sqrt(best)

    m_sc[...] = jnp.full(m_sc.shape, NEG_BIG, F32)
    acc_sc[...] = jnp.zeros(acc_sc.shape, F32)
    vaug_sc[:, :, HEAD_DIM:] = jnp.ones((N_HEADS, tk, HEAD_DIM), BF16)
    for h, hs in enumerate(heads):
        q = q_ref[:, hs].astype(F32)
        qn_sc[h] = jnp.broadcast_to(jnp.sqrt(jnp.sum(q * q, axis=-1, keepdims=True)), (tq, LANES))
    rep = tk // LANES

    def key_tile(j, masked):
        k0 = pl.multiple_of(j * tk, tk)
        if masked:
            keep = (lax.broadcasted_iota(jnp.int32, (tq, tk), 1)
                    <= lax.broadcasted_iota(jnp.int32, (tq, tk), 0))
        scores = [lax.dot_general(q_ref[:, hs], k_ref[pl.ds(k0, tk), hs], (((1,), (1,)), ((), ())),
                                  preferred_element_type=F32) for hs in heads]
        probs, alphas = [], []
        for h, s in enumerate(scores):
            s = s - cfk_ref[h:h + 1, pl.ds(k0, tk)]
            if masked:
                s = jnp.where(keep, s, NEG_BIG)
            m_old = m_sc[h]
            m_new = jnp.maximum(m_old, jnp.max(s, axis=-1, keepdims=True))
            alphas.append(jnp.exp2(m_old - m_new))
            probs.append(jnp.exp2(s - jnp.concatenate([m_new] * rep, axis=1)).astype(BF16))
            m_sc[h] = m_new
            vaug_sc[h, :, :HEAD_DIM] = v_ref[pl.ds(k0, tk), heads[h]]
        for h, (p, alpha) in enumerate(zip(probs, alphas)):
            pv = jnp.dot(p, vaug_sc[h], preferred_element_type=F32)
            acc_sc[h] = jnp.concatenate([alpha, alpha], axis=1) * acc_sc[h] + pv

    def reach(j):
        newest = pl.multiple_of(j * tk + tk - LANES, LANES)
        bound = None
        for h in range(N_HEADS):
            decay = -cfk_ref[h:h + 1, pl.ds(newest, LANES)][:, LANES - 1:]
            t = qn_sc[h] * kn_sc[h, 0:1, :] + decay - m_sc[h]
            bound = t if bound is None else jnp.maximum(bound, t)
        return jnp.max(bound)

    def alive(j):
        return (reach(jnp.maximum(j, 0)) > FOX_DEAD_LOG2).astype(jnp.int32)

    def earlier(state):
        j, _ = state
        key_tile(j, False)
        return j - 1, alive(j - 1)

    key_tile(i, True)
    lax.while_loop(lambda st: jnp.logical_and(st[0] >= 0, st[1] > 0), earlier, (i - 1, alive(i - 1)))
    for h in range(N_HEADS):
        hs = slice(h * HEAD_DIM, (h + 1) * HEAD_DIM)
        o_ref[:, hs] = (acc_sc[h, :, :HEAD_DIM] / acc_sc[h, :, HEAD_DIM:]).astype(o_ref.dtype)


def _fox_attention(qkv, cfk, col0, tq=512):
    s = qkv.shape[0]
    tk = tq
    w = BRANCH_WIDTH
    kern = functools.partial(_fox_kernel, tq=tq, tk=tk)
    resident = pl.Buffered(1)
    return pl.pallas_call(
        kern,
        grid=(s // tq,),
        in_specs=[pl.BlockSpec((tq, w), lambda i: (i, col0)),
                  pl.BlockSpec((s, w), lambda i: (0, col0 + 1), pipeline_mode=resident),
                  pl.BlockSpec((s, w), lambda i: (0, col0 + 2), pipeline_mode=resident),
                  pl.BlockSpec((SUBLANES, s), lambda i: (0, 0), pipeline_mode=resident)],
        out_specs=pl.BlockSpec((tq, w), lambda i: (i, 0)),
        out_shape=jax.ShapeDtypeStruct((s, w), BF16),
        scratch_shapes=[pltpu.VMEM((N_HEADS, tq, LANES), F32),
                        pltpu.VMEM((N_HEADS, tq, 2 * HEAD_DIM), F32),
                        pltpu.VMEM((N_HEADS, tk, 2 * HEAD_DIM), BF16),
                        pltpu.VMEM((N_HEADS, tq, LANES), F32),
                        pltpu.VMEM((N_HEADS, SUBLANES, LANES), F32)],
        compiler_params=_cparams(("arbitrary",)),
        name="fox_attention",
    )(qkv, qkv, qkv, cfk)


def _sb_kernel(q_ref, k_ref, v_ref, o_ref, run_sc, acc_sc, *, tq):
    i = pl.program_id(0)
    nsub = tq // LANES
    r = lax.broadcasted_iota(jnp.int32, (2 * LANES, 2 * LANES), 0) & (LANES - 1)
    c = lax.broadcasted_iota(jnp.int32, (2 * LANES, 2 * LANES), 1)
    tri_aug = ((c >= LANES) | (r > c)).astype(BF16)

    def sub_blocks(items, masked):
        scores = []
        for h, k0, r0 in items:
            hs = slice(h * HEAD_DIM, (h + 1) * HEAD_DIM)
            scores.append(lax.dot_general(q_ref[r0:, hs], k_ref[pl.ds(k0, LANES), hs],
                                          (((1,), (1,)), ((), ())), preferred_element_type=F32))
        log_beta, sums, keeps = [], [], []
        for (h, k0, r0), z in zip(items, scores):
            rows = tq - r0
            lp = jnp.minimum(z, 0.0) - jnp.log2(1.0 + jnp.exp2(-jnp.abs(z)))
            ln = lp - z
            keep = None
            if masked:
                keep = (lax.broadcasted_iota(jnp.int32, (rows, LANES), 1)
                        < lax.broadcasted_iota(jnp.int32, (rows, LANES), 0))
                ln = jnp.where(keep, ln, 0.0)
            ln_hi = ln.astype(BF16)
            ln_lo = (ln - ln_hi.astype(F32)).astype(BF16)
            log_beta.append(lp)
            keeps.append(keep)
            sums.append(jnp.dot(jnp.concatenate([ln_hi, ln_lo], axis=1), tri_aug,
                                preferred_element_type=F32))
        weights = []
        for (h, k0, r0), lp, la, keep in zip(items, log_beta, sums, keeps):
            run = run_sc[h, r0:, :]
            a = jnp.exp2(lp + la[:, :LANES] + run)
            if masked:
                a = jnp.where(keep, a, 0.0)
            run_sc[h, r0:, :] = run + la[:, LANES:]
            weights.append(a.astype(BF16))
        for (h, k0, r0), a in zip(items, weights):
            hs = slice(h * HEAD_DIM, (h + 1) * HEAD_DIM)
            acc_sc[h, r0:, :] += jnp.dot(a, v_ref[pl.ds(k0, LANES), hs], preferred_element_type=F32)

    run_sc[...] = jnp.zeros(run_sc.shape, F32)
    acc_sc[...] = jnp.zeros(acc_sc.shape, F32)
    for cc in range(nsub - 1, -1, -1):
        k0 = pl.multiple_of(i * tq + cc * LANES, LANES)
        sub_blocks([(h, k0, cc * LANES) for h in range(N_HEADS)], True)

    def alive():
        return (jnp.max(run_sc[...]) > SB_DEAD_LOG2).astype(jnp.int32)

    def earlier(state):
        cb, _ = state
        sub_blocks([(h, pl.multiple_of((cb - back) * LANES, LANES), 0)
                    for back in range(2) for h in range(N_HEADS)], False)
        return cb - 2, alive()

    lax.while_loop(lambda st: jnp.logical_and(st[0] >= 0, st[1] > 0), earlier,
                   (i * nsub - 1, alive()))
    for h in range(N_HEADS):
        o_ref[:, h * HEAD_DIM:(h + 1) * HEAD_DIM] = acc_sc[h].astype(o_ref.dtype)


def _sb_attention(qkv, col0, tq=512):
    s = qkv.shape[0]
    w = BRANCH_WIDTH
    assert (tq // LANES) % 2 == 0
    kern = functools.partial(_sb_kernel, tq=tq)
    resident = pl.Buffered(1)
    return pl.pallas_call(
        kern,
        grid=(s // tq,),
        in_specs=[pl.BlockSpec((tq, w), lambda i: (i, col0)),
                  pl.BlockSpec((s, w), lambda i: (0, col0 + 1), pipeline_mode=resident),
                  pl.BlockSpec((s, w), lambda i: (0, col0 + 2), pipeline_mode=resident)],
        out_specs=pl.BlockSpec((tq, w), lambda i: (i, 0)),
        out_shape=jax.ShapeDtypeStruct((s, w), BF16),
        scratch_shapes=[pltpu.VMEM((N_HEADS, tq, LANES), F32),
                        pltpu.VMEM((N_HEADS, tq, HEAD_DIM), F32)],
        compiler_params=_cparams(("arbitrary",)),
        name="sb_attention",
    )(qkv, qkv, qkv)


def _band_kernel(q_ref, k2_ref, k1_ref, k0_ref, v2_ref, v1_ref, v0_ref, ext_ref, o_ref,
                 bias_sc, vaug_sc, *, tq):
    i = pl.program_id(0)
    width = 4 * tq

    @pl.when(i == 0)
    def _build_tables():
        trow = lax.broadcasted_iota(jnp.int32, (tq, width), 0)
        t = lax.broadcasted_iota(jnp.int32, (tq, 3 * tq), 0)
        s = lax.broadcasted_iota(jnp.int32, (tq, 3 * tq), 1)
        shift = CHUNK.bit_length() - 1
        t_chunk = t >> shift
        s_chunk = (s >> shift) - (2 * tq) // CHUNK
        in_band = (t_chunk - s_chunk <= LOOKBACK_CHUNKS) & (s_chunk <= t_chunk)
        for h in range(N_HEADS):
            x = jnp.broadcast_to(ext_ref[h:h + 1, :], (tq, width))
            for b in range(tq.bit_length() - 1):
                x = jnp.where(((trow >> b) & 1) == 1, pltpu.roll(x, 1 << b, axis=1), x)
            bias_sc[h] = jnp.where(in_band, x[:, :3 * tq] * LOG2_E, NEG_BIG)
        vaug_sc[:, :, HEAD_DIM:] = jnp.ones((N_HEADS, 3 * tq, HEAD_DIM), BF16)

    k_refs = (k2_ref, k1_ref, k0_ref)
    v_refs = (v2_ref, v1_ref, v0_ref)
    heads = [slice(h * HEAD_DIM, (h + 1) * HEAD_DIM) for h in range(N_HEADS)]

    def tile(first_tiles):
        scores = [[lax.dot_general(q_ref[:, hs], k_refs[p][:, hs], (((1,), (1,)), ((), ())),
                                   preferred_element_type=F32) for p in range(3)] for hs in heads]
        probs = []
        for h, pieces in enumerate(scores):
            pieces = [s + bias_sc[h, :, p * tq:(p + 1) * tq] for p, s in enumerate(pieces)]
            if first_tiles:
                pieces = [jnp.where(i - 2 + p >= 0, s, NEG_BIG) for p, s in enumerate(pieces)]
            m = jnp.max(jnp.maximum(jnp.maximum(pieces[0], pieces[1]), pieces[2]),
                        axis=-1, keepdims=True)
            probs.append(jnp.concatenate([jnp.exp2(s - m).astype(BF16) for s in pieces], axis=1))
            for p in range(3):
                vaug_sc[h, p * tq:(p + 1) * tq, :HEAD_DIM] = v_refs[p][:, heads[h]]
        for h, p in enumerate(probs):
            acc = jnp.dot(p, vaug_sc[h], preferred_element_type=F32)
            o_ref[:, heads[h]] = (acc[:, :HEAD_DIM] / acc[:, HEAD_DIM:]).astype(o_ref.dtype)

    @pl.when(i < 2)
    def _first_tiles():
        tile(True)

    @pl.when(i >= 2)
    def _other_tiles():
        tile(False)


def _band_bias_vector(rel_bias_l, tq):
    n = np.arange(4 * tq)
    dist = np.where(n < 3 * tq, 2 * tq - n, 6 * tq - n)
    ridx = np.clip(dist, -(CHUNK - 1), REL_CLIP) + (CHUNK - 1)
    return rel_bias_l.astype(F32)[:, ridx]


def _band_attention(qkv, bias_ext, col0, tq=256):
    s = qkv.shape[0]
    w = BRANCH_WIDTH
    assert 2 * tq >= LOOKBACK_CHUNKS * CHUNK and tq % CHUNK == 0 and tq & (tq - 1) == 0
    kern = functools.partial(_band_kernel, tq=tq)

    def kv_spec(back, col):
        return pl.BlockSpec((tq, w), lambda i: (jnp.maximum(i - back, 0), col))

    return pl.pallas_call(
        kern,
        grid=(s // tq,),
        in_specs=[pl.BlockSpec((tq, w), lambda i: (i, col0)),
                  kv_spec(2, col0 + 1), kv_spec(1, col0 + 1), kv_spec(0, col0 + 1),
                  kv_spec(2, col0 + 2), kv_spec(1, col0 + 2), kv_spec(0, col0 + 2),
                  pl.BlockSpec((N_HEADS, 4 * tq), lambda i: (0, 0))],
        out_specs=pl.BlockSpec((tq, w), lambda i: (i, 0)),
        out_shape=jax.ShapeDtypeStruct((s, w), BF16),
        scratch_shapes=[pltpu.VMEM((N_HEADS, tq, 3 * tq), F32),
                        pltpu.VMEM((N_HEADS, 3 * tq, 2 * HEAD_DIM), BF16)],
        compiler_params=_cparams(("arbitrary",)),
        name="band_attention",
    )(qkv, qkv, qkv, qkv, qkv, qkv, qkv, bias_ext)


def _gelu_tanh(x):
    c = math.sqrt(2.0 / math.pi)
    return 0.5 * x * (1.0 + jnp.tanh(c * (x + 0.044715 * (x * x * x))))


def _lru_kernel(rx_ref, ry_ref, cw_ref, cb_ref, wr_ref, br_ref, wi_ref, bi_ref, lam_ref,
                o_ref, xext_sc, a_sc, b_sc, h_sc, carry_sc, *, tm):
    i = pl.program_id(0)
    w = BRANCH_WIDTH
    halo = SUBLANES

    @pl.when(i == 0)
    def _first():
        xext_sc[0:halo, :] = jnp.zeros((halo, w), F32)
        carry_sc[...] = jnp.zeros(carry_sc.shape, F32)

    @pl.when(i > 0)
    def _shift_halo():
        xext_sc[0:halo, :] = xext_sc[tm:tm + halo, :]

    xext_sc[halo:halo + tm, :] = rx_ref[...]
    xext = xext_sc[0:halo + tm, :]
    xc = cb_ref[...] + rx_ref[...] * cw_ref[CONV_WIDTH - 1:CONV_WIDTH, :]
    for back in range(1, CONV_WIDTH):
        t = CONV_WIDTH - 1 - back
        xc = xc + pltpu.roll(xext, back, axis=0)[halo:, :] * cw_ref[t:t + 1, :]

    xcb = xc.astype(BF16)
    r_parts, i_parts = [], []
    for n in range(N_HEADS):
        ns = slice(n * HEAD_DIM, (n + 1) * HEAD_DIM)
        r_parts.append(jnp.dot(xcb[:, ns], wr_ref[n], preferred_element_type=F32))
        i_parts.append(jnp.dot(xcb[:, ns], wi_ref[n], preferred_element_type=F32))
    r = jax.nn.sigmoid(jnp.concatenate(r_parts, axis=1) + br_ref[...])
    gi = jax.nn.sigmoid(jnp.concatenate(i_parts, axis=1) + bi_ref[...])
    log_a = LRU_C * r * _log_sigmoid(lam_ref[...])
    a = jnp.exp(log_a)
    a_sc[...] = a
    b_sc[...] = jnp.sqrt(-jnp.tanh(log_a) * (a * a + 1.0)) * (gi * xc)

    row = lax.broadcasted_iota(jnp.int32, (SUBLANES, w), 0)

    def group(g, carry):
        r0 = pl.multiple_of(g * SUBLANES, SUBLANES)
        a = a_sc[pl.ds(r0, SUBLANES), :]
        b = b_sc[pl.ds(r0, SUBLANES), :]
        for k in (1, 2, 4):
            a_prev = pltpu.roll(a, k, axis=0)
            b_prev = pltpu.roll(b, k, axis=0)
            ok = row >= k
            b = jnp.where(ok, a * b_prev + b, b)
            a = jnp.where(ok, a * a_prev, a)
        hgrp = a * carry + b
        h_sc[pl.ds(r0, SUBLANES), :] = hgrp
        return jnp.broadcast_to(hgrp[SUBLANES - 1:SUBLANES, :], (SUBLANES, w))

    carry_sc[...] = lax.fori_loop(0, tm // SUBLANES, group, carry_sc[...])
    o_ref[...] = (h_sc[...] * _gelu_tanh(ry_ref[...])).astype(o_ref.dtype)


def _recurrent_branch(uf, conv_w, conv_b, w_r, b_r, w_i, b_i, lam, tm=512):
    s = uf.shape[0]
    w = BRANCH_WIDTH
    kern = functools.partial(_lru_kernel, tm=tm)
    row = lambda v: v.reshape(1, w)
    full2 = lambda shape: pl.BlockSpec(shape, lambda i: (0, 0))
    full3 = lambda shape: pl.BlockSpec(shape, lambda i: (0, 0, 0))
    return pl.pallas_call(
        kern,
        grid=(s // tm,),
        in_specs=[pl.BlockSpec((tm, w), lambda i: (i, 0)),
                  pl.BlockSpec((tm, w), lambda i: (i, 1)),
                  full2((CONV_WIDTH, w)), full2((1, w)),
                  full3((N_HEADS, HEAD_DIM, HEAD_DIM)), full2((1, w)),
                  full3((N_HEADS, HEAD_DIM, HEAD_DIM)), full2((1, w)),
                  full2((1, w))],
        out_specs=pl.BlockSpec((tm, w), lambda i: (i, 0)),
        out_shape=jax.ShapeDtypeStruct((s, w), BF16),
        scratch_shapes=[pltpu.VMEM((tm + 2 * SUBLANES, w), F32),
                        pltpu.VMEM((tm, w), F32),
                        pltpu.VMEM((tm, w), F32),
                        pltpu.VMEM((tm, w), F32),
                        pltpu.VMEM((SUBLANES, w), F32)],
        compiler_params=_cparams(("arbitrary",)),
        name="recurrent_branch",
    )(uf, uf, conv_w, row(conv_b), w_r.astype(BF16), row(b_r), w_i.astype(BF16), row(b_i), row(lam))


def _merge_kernel(x_ref, o0_ref, o1_ref, o2_ref, o3_ref, wg_ref, bg_ref, wb_ref, out_ref):
    x = x_ref[...]
    merged = None
    for g, o_ref in enumerate((o0_ref, o1_ref, o2_ref, o3_ref)):
        gate = jax.nn.sigmoid(jnp.dot(x, wg_ref[g].astype(BF16), preferred_element_type=F32)
                              + bg_ref[g:g + 1, :])
        term = gate * jnp.dot(o_ref[...], wb_ref[g].astype(BF16), preferred_element_type=F32)
        merged = term if merged is None else merged + term
    out_ref[...] = merged.astype(out_ref.dtype)


def _merge(xb, branches, wg_all, bg_all, wb_all, layer, tm=1024, tn=256):
    s, d = xb.shape
    w = BRANCH_WIDTH
    o_spec = pl.BlockSpec((tm, w), lambda i, j: (i, 0))
    return pl.pallas_call(
        _merge_kernel,
        grid=(s // tm, d // tn),
        in_specs=[pl.BlockSpec((tm, d), lambda i, j: (i, 0)),
                  o_spec, o_spec, o_spec, o_spec,
                  pl.BlockSpec((None, N_BRANCH, d, tn), lambda i, j: (layer, 0, 0, j)),
                  pl.BlockSpec((None, N_BRANCH, tn), lambda i, j: (layer, 0, j)),
                  pl.BlockSpec((None, N_BRANCH, w, tn), lambda i, j: (layer, 0, 0, j))],
        out_specs=pl.BlockSpec((tm, tn), lambda i, j: (i, j)),
        out_shape=jax.ShapeDtypeStruct((s, d), BF16),
        compiler_params=_cparams(("parallel", "arbitrary")),
        name="gated_merge",
    )(xb, *branches, wg_all, bg_all, wb_all)


def _outproj_kernel(m_ref, w_ref, x_ref, g_ref, b_ref, of_ref, ob_ref):
    half = m_ref.shape[0] // 2
    rows = (slice(0, half), slice(half, 2 * half))
    proj = [jnp.dot(m_ref[r, :], w_ref[...], preferred_element_type=F32) for r in rows]
    for r, p in zip(rows, proj):
        y = _layer_norm_rows(ALPHA * x_ref[r, :] + p, g_ref[...], b_ref[...])
        of_ref[r, :] = y
        ob_ref[r, :] = y.astype(BF16)


def _outproj_ln(merged, w_out_all, layer, x, g, b, tm=512):
    s, d = x.shape
    row_spec = pl.BlockSpec((tm, d), lambda i: (i, 0))
    vec_spec = pl.BlockSpec((1, d), lambda i: (0, 0))
    return pl.pallas_call(
        _outproj_kernel,
        grid=(s // tm,),
        in_specs=[row_spec, pl.BlockSpec((None, d, d), lambda i: (layer, 0, 0)), row_spec,
                  vec_spec, vec_spec],
        out_specs=[row_spec, row_spec],
        out_shape=[jax.ShapeDtypeStruct((s, d), F32), jax.ShapeDtypeStruct((s, d), BF16)],
        compiler_params=_cparams(("parallel",)),
        name="outproj_ln",
    )(merged, w_out_all, x, g.reshape(1, d), b.reshape(1, d))


def _ffn_kernel(xb_ref, xf_ref, w1_ref, w2_ref, g_ref, b_ref, of_ref, ob_ref, acc_sc):
    f = pl.program_id(1)

    @pl.when(f == 0)
    def _init():
        acc_sc[...] = jnp.zeros(acc_sc.shape, F32)

    hid = jnp.maximum(jnp.dot(xb_ref[...], w1_ref[...], preferred_element_type=F32), 0.0)
    hid = (hid * hid).astype(BF16)
    acc_sc[...] += jnp.dot(hid, w2_ref[...], preferred_element_type=F32)

    @pl.when(f == pl.num_programs(1) - 1)
    def _finish():
        y = _layer_norm_rows(ALPHA * xf_ref[...] + acc_sc[...], g_ref[...], b_ref[...])
        of_ref[...] = y
        ob_ref[...] = y.astype(BF16)


def _ffn_ln(xb, xf, w1_all, w2_all, layer, g, b, tm=512, tf=1024):
    s, d = xf.shape
    dff = w1_all.shape[2]
    row_spec = pl.BlockSpec((tm, d), lambda i, f: (i, 0))
    vec_spec = pl.BlockSpec((1, d), lambda i, f: (0, 0))
    return pl.pallas_call(
        _ffn_kernel,
        grid=(s // tm, dff // tf),
        in_specs=[row_spec, row_spec,
                  pl.BlockSpec((None, d, tf), lambda i, f: (layer, 0, f)),
                  pl.BlockSpec((None, tf, d), lambda i, f: (layer, f, 0)),
                  vec_spec, vec_spec],
        out_specs=[row_spec, row_spec],
        out_shape=[jax.ShapeDtypeStruct((s, d), F32), jax.ShapeDtypeStruct((s, d), BF16)],
        scratch_shapes=[pltpu.VMEM((tm, d), F32)],
        compiler_params=_cparams(("parallel", "arbitrary")),
        name="ffn_ln",
    )(xb, xf, w1_all, w2_all, g.reshape(1, d), b.reshape(1, d))


def _split_in_proj(w_in):
    qkv = jnp.concatenate([w_in[:, :, _OFF_FQ:_OFF_FF], w_in[:, :, _OFF_SQ:_OFF_END]], axis=2)
    pad = jnp.zeros(w_in.shape[:2] + (LANES - N_HEADS,), w_in.dtype)
    rest = jnp.concatenate([w_in[:, :, _OFF_RX:_OFF_SQ], w_in[:, :, _OFF_FF:_OFF_RX], pad], axis=2)
    return qkv.astype(BF16), rest.astype(BF16)


def kernel(x, ln_in_g, ln_in_b, w_in, b_forget, conv_w, conv_b, w_r, b_r, w_i, b_i, lru_lambda,
           rel_bias, w_branch, w_gate, b_gate, w_out, ln1_g, ln1_b, w_ff1, w_ff2, ln2_g, ln2_b):
    batch, s, d = x.shape
    assert (batch, s, d) == (1, SEQ, D_MODEL)
    w = BRANCH_WIDTH
    band_tq = 256

    col_scale = np.ones((1, 9 * w), np.float32)
    for q_block in (0, 3, 6):
        col_scale[:, q_block * w:(q_block + 1) * w] = QK_SCALE * LOG2_E
    qkv_scale = jnp.asarray(col_scale)
    rest_scale = jnp.ones((1, 2 * w + LANES), F32)

    w_qkv, w_rest = _split_in_proj(w_in)
    wo_b = w_out.astype(BF16)
    w1_b, w2_b = w_ff1.astype(BF16), w_ff2.astype(BF16)

    xf, xb = _entry_ln(x.reshape(s, d), ln_in_g, ln_in_b)
    for l in range(DEPTH):
        qkv = _project(xb, w_qkv, l, qkv_scale, BF16, 1024, 1536, "in_proj_qkv")
        uf = _project(xb, w_rest, l, rest_scale, F32, 1024, 1152, "in_proj_rest")

        f_rows = uf[:, 2 * w:2 * w + N_HEADS].T.reshape(N_HEADS * (s // LANES), LANES)
        b_rows = jnp.repeat(b_forget[l].astype(F32), s // LANES).reshape(-1, 1)
        cf = _forget_cumsum(f_rows, b_rows).reshape(N_HEADS, s)
        cfk = jnp.pad(cf, ((0, SUBLANES - N_HEADS), (0, 0)))

        o_fox = _fox_attention(qkv, cfk, 0)
        o_sb = _sb_attention(qkv, 3)
        o_ch = _band_attention(qkv, _band_bias_vector(rel_bias[l], band_tq), 6, band_tq)
        o_lru = _recurrent_branch(uf, conv_w[l], conv_b[l], w_r[l], b_r[l], w_i[l], b_i[l],
                                  lru_lambda[l])

        merged = _merge(xb, (o_fox, o_lru, o_sb, o_ch), w_gate, b_gate, w_branch, l)
        xf, xb = _outproj_ln(merged, wo_b, l, xf, ln1_g[l], ln1_b[l])
        xf, xb = _ffn_ln(xb, xf, w1_b, w2_b, l, ln2_g[l], ln2_b[l])
    return xf.reshape(batch, s, d)
```

```python
import functools
import math

import jax
import jax.numpy as jnp
import numpy as np
from jax import lax
from jax.experimental import pallas as pl
from jax.experimental.pallas import tpu as pltpu

F32 = jnp.float32
BF16 = jnp.bfloat16

D_MODEL = 2048
SEQ = 8192
DEPTH = 2
CHUNK = 64
HEAD_DIM = 128
N_BRANCH = 4
BRANCH_WIDTH = D_MODEL // N_BRANCH
N_HEADS = BRANCH_WIDTH // HEAD_DIM
CONV_WIDTH = 4
LRU_C = 8.0
LOOKBACK_CHUNKS = 8
REL_CLIP = 256
D_FF = 4 * D_MODEL
ALPHA = (2.0 * DEPTH) ** 0.25
LN_EPS = 1e-5
QK_SCALE = HEAD_DIM ** -0.5
LOG2_E = math.log2(math.e)

_OFF_FQ = 0
_OFF_FK = _OFF_FQ + BRANCH_WIDTH
_OFF_FV = _OFF_FK + BRANCH_WIDTH
_OFF_FF = _OFF_FV + BRANCH_WIDTH
_OFF_RX = _OFF_FF + N_HEADS
_OFF_RY = _OFF_RX + BRANCH_WIDTH
_OFF_SQ = _OFF_RY + BRANCH_WIDTH
_OFF_CQ = _OFF_SQ + 3 * BRANCH_WIDTH
_OFF_END = _OFF_CQ + 3 * BRANCH_WIDTH

LANES = 128
SUBLANES = 8
NEG_BIG = -1e30
SB_DEAD_LOG2 = -180.0
FOX_DEAD_LOG2 = -170.0

VMEM_LIMIT = 56 * 1024 * 1024


def _cparams(sem, vmem=VMEM_LIMIT):
    return pltpu.CompilerParams(dimension_semantics=sem, vmem_limit_bytes=vmem)


def _log_sigmoid(x):
    return jnp.minimum(x, 0.0) - jnp.log1p(jnp.exp(-jnp.abs(x)))


def _layer_norm_rows(y, g, b):
    mu = jnp.mean(y, axis=-1, keepdims=True)
    d = y - mu
    var = jnp.mean(d * d, axis=-1, keepdims=True)
    return d * lax.rsqrt(var + LN_EPS) * g + b


def _ln_kernel(x_ref, g_ref, b_ref, of_ref, ob_ref):
    y = _layer_norm_rows(x_ref[...], g_ref[...], b_ref[...])
    of_ref[...] = y
    ob_ref[...] = y.astype(BF16)


def _entry_ln(x, g, b, tm=512):
    s, d = x.shape
    return pl.pallas_call(
        _ln_kernel,
        grid=(s // tm,),
        in_specs=[pl.BlockSpec((tm, d), lambda i: (i, 0)),
                  pl.BlockSpec((1, d), lambda i: (0, 0)),
                  pl.BlockSpec((1, d), lambda i: (0, 0))],
        out_specs=[pl.BlockSpec((tm, d), lambda i: (i, 0)),
                   pl.BlockSpec((tm, d), lambda i: (i, 0))],
        out_shape=[jax.ShapeDtypeStruct((s, d), F32),
                   jax.ShapeDtypeStruct((s, d), BF16)],
        compiler_params=_cparams(("parallel",)),
        name="entry_ln",
    )(x, g.reshape(1, d), b.reshape(1, d))


def _proj_kernel(x_ref, w_ref, s_ref, o_ref):
    acc = jnp.dot(x_ref[...], w_ref[...], preferred_element_type=F32)
    o_ref[...] = (acc * s_ref[...]).astype(o_ref.dtype)


def _project(xb, w_all, layer, colscale, out_dtype, tm, tn, name):
    m, k = xb.shape
    n = w_all.shape[2]
    return pl.pallas_call(
        _proj_kernel,
        grid=(m // tm, n // tn),
        in_specs=[pl.BlockSpec((tm, k), lambda i, j: (i, 0)),
                  pl.BlockSpec((None, k, tn), lambda i, j: (layer, 0, j)),
                  pl.BlockSpec((1, tn), lambda i, j: (0, j))],
        out_specs=pl.BlockSpec((tm, tn), lambda i, j: (i, j)),
        out_shape=jax.ShapeDtypeStruct((m, n), out_dtype),
        compiler_params=_cparams(("parallel", "arbitrary")),
        name=name,
    )(xb, w_all, colscale)


def _forget_cumsum_kernel(f_ref, b_ref, o_ref):
    rows = f_ref.shape[0]
    per_head = rows // N_HEADS
    ls = _log_sigmoid(f_ref[...] + b_ref[...])
    r = lax.broadcasted_iota(jnp.int32, (LANES, LANES), 0)
    c = lax.broadcasted_iota(jnp.int32, (LANES, LANES), 1)
    upper = (r <= c).astype(F32)
    within = jnp.dot(ls, upper, preferred_element_type=F32,
                     precision=lax.Precision.HIGHEST)
    total = within[:, LANES - 1:LANES]
    rr = lax.broadcasted_iota(jnp.int32, (rows, rows), 0)
    cc = lax.broadcasted_iota(jnp.int32, (rows, rows), 1)
    head_start = rr - (rr & (per_head - 1))
    before = ((cc >= head_start) & (cc < rr)).astype(F32)
    offs = jnp.dot(before, jnp.broadcast_to(total, (rows, LANES)),
                   preferred_element_type=F32, precision=lax.Precision.HIGHEST)
    o_ref[...] = (within + offs) * LOG2_E


def _forget_cumsum(f_rows, b_rows):
    rows = f_rows.shape[0]
    return pl.pallas_call(
        _forget_cumsum_kernel,
        out_shape=jax.ShapeDtypeStruct((rows, LANES), F32),
        name="forget_cumsum",
    )(f_rows, b_rows)


def _fox_kernel(q_ref, k_ref, v_ref, cfk_ref, o_ref, m_sc, acc_sc, vaug_sc, qn_sc, kn_sc, *, tq, tk):
    i = pl.program_id(0)
    heads = [slice(h * HEAD_DIM, (h + 1) * HEAD_DIM) for h in range(N_HEADS)]
    norm_rows = 1024

    @pl.when(i == 0)
    def _largest_key_norm():
        for h, hs in enumerate(heads):
            def chunk(c, best, hs=hs):
                rows = k_ref[pl.ds(pl.multiple_of(c * norm_rows, norm_rows), norm_rows), hs].astype(F32)
                return jnp.maximum(best, jnp.max(jnp.sum(rows * rows, axis=-1, keepdims=True)))
            best = lax.fori_loop(0, k_ref.shape[0] // norm_rows, chunk, jnp.zeros((SUBLANES, LANES), F32))
            kn_sc[h] = jnp.sqrt(best)

    m_sc[...] = jnp.full(m_sc.shape, NEG_BIG, F32)
    acc_sc[...] = jnp.zeros(acc_sc.shape, F32)
    vaug_sc[:, :, HEAD_DIM:] = jnp.ones((N_HEADS, tk, HEAD_DIM), BF16)
    for h, hs in enumerate(heads):
        q = q_ref[:, hs].astype(F32)
        qn_sc[h] = jnp.broadcast_to(jnp.sqrt(jnp.sum(q * q, axis=-1, keepdims=True)), (tq, LANES))
    rep = tk // LANES

    def key_tile(j, masked, hlist):
        k0 = pl.multiple_of(j * tk, tk)
        if masked:
            keep = (lax.broadcasted_iota(jnp.int32, (tq, tk), 1)
                    <= lax.broadcasted_iota(jnp.int32, (tq, tk), 0))
        scores = [lax.dot_general(q_ref[:, heads[h]], k_ref[pl.ds(k0, tk), heads[h]],
                                  (((1,), (1,)), ((), ())), preferred_element_type=F32) for h in hlist]
        probs, alphas = [], []
        for h, s in zip(hlist, scores):
            s = s - cfk_ref[h:h + 1, pl.ds(k0, tk)]
            if masked:
                s = jnp.where(keep, s, NEG_BIG)
            m_old = m_sc[h]
            m_new = jnp.maximum(m_old, jnp.max(s, axis=-1, keepdims=True))
            alphas.append(jnp.exp2(m_old - m_new))
            probs.append(jnp.exp2(s - jnp.concatenate([m_new] * rep, axis=1)).astype(BF16))
            m_sc[h] = m_new
            vaug_sc[h, :, :HEAD_DIM] = v_ref[pl.ds(k0, tk), heads[h]]
        for h, p, alpha in zip(hlist, probs, alphas):
            pv = jnp.dot(p, vaug_sc[h], preferred_element_type=F32)
            acc_sc[h] = jnp.concatenate([alpha, alpha], axis=1) * acc_sc[h] + pv

    def alive(j, hlist):
        newest = pl.multiple_of(jnp.maximum(j, 0) * tk + tk - LANES, LANES)
        reach = None
        for h in hlist:
            decay = -cfk_ref[h:h + 1, pl.ds(newest, LANES)][:, LANES - 1:]
            bound = jnp.max(qn_sc[h] * kn_sc[h, 0:1, :] + decay - m_sc[h], axis=0, keepdims=True)
            reach = bound if reach is None else jnp.minimum(reach, bound)
        return (jnp.max(reach) > FOX_DEAD_LOG2).astype(jnp.int32)

    def walk_back(j_start, hlist):
        def earlier(state):
            j, _ = state
            key_tile(j, False, hlist)
            return j - 1, alive(j - 1, hlist)
        return lax.while_loop(lambda st: jnp.logical_and(st[0] >= 0, st[1] > 0), earlier,
                              (j_start, alive(j_start, hlist)))[0]

    all_heads = list(range(N_HEADS))
    key_tile(i, True, all_heads)
    j_split = walk_back(i - 1, all_heads)
    for h in all_heads:
        walk_back(j_split, [h])
    for h in range(N_HEADS):
        hs = slice(h * HEAD_DIM, (h + 1) * HEAD_DIM)
        o_ref[:, hs] = (acc_sc[h, :, :HEAD_DIM] / acc_sc[h, :, HEAD_DIM:]).astype(o_ref.dtype)


def _fox_attention(qkv, cfk, col0, tq=512):
    s = qkv.shape[0]
    tk = tq
    w = BRANCH_WIDTH
    kern = functools.partial(_fox_kernel, tq=tq, tk=tk)
    resident = pl.Buffered(1)
    return pl.pallas_call(
        kern,
        grid=(s // tq,),
        in_specs=[pl.BlockSpec((tq, w), lambda i: (i, col0)),
                  pl.BlockSpec((s, w), lambda i: (0, col0 + 1), pipeline_mode=resident),
                  pl.BlockSpec((s, w), lambda i: (0, col0 + 2), pipeline_mode=resident),
                  pl.BlockSpec((SUBLANES, s), lambda i: (0, 0), pipeline_mode=resident)],
        out_specs=pl.BlockSpec((tq, w), lambda i: (i, 0)),
        out_shape=jax.ShapeDtypeStruct((s, w), BF16),
        scratch_shapes=[pltpu.VMEM((N_HEADS, tq, LANES), F32),
                        pltpu.VMEM((N_HEADS, tq, 2 * HEAD_DIM), F32),
                        pltpu.VMEM((N_HEADS, tk, 2 * HEAD_DIM), BF16),
                        pltpu.VMEM((N_HEADS, tq, LANES), F32),
                        pltpu.VMEM((N_HEADS, SUBLANES, LANES), F32)],
        compiler_params=_cparams(("arbitrary",)),
        name="fox_attention",
    )(qkv, qkv, qkv, cfk)


def _sb_kernel(q_ref, k_ref, v_ref, o_ref, run_sc, acc_sc, *, tq):
    i = pl.program_id(0)
    nsub = tq // LANES
    r = lax.broadcasted_iota(jnp.int32, (2 * LANES, 2 * LANES), 0) & (LANES - 1)
    c = lax.broadcasted_iota(jnp.int32, (2 * LANES, 2 * LANES), 1)
    tri_aug = ((c >= LANES) | (r > c)).astype(BF16)

    def sub_blocks(items, masked):
        scores = []
        for h, k0, r0 in items:
            hs = slice(h * HEAD_DIM, (h + 1) * HEAD_DIM)
            scores.append(lax.dot_general(q_ref[r0:, hs], k_ref[pl.ds(k0, LANES), hs],
                                          (((1,), (1,)), ((), ())), preferred_element_type=F32))
        log_beta, sums, keeps = [], [], []
        for (h, k0, r0), z in zip(items, scores):
            rows = tq - r0
            lp = jnp.minimum(z, 0.0) - jnp.log2(1.0 + jnp.exp2(-jnp.abs(z)))
            ln = lp - z
            keep = None
            if masked:
                keep = (lax.broadcasted_iota(jnp.int32, (rows, LANES), 1)
                        < lax.broadcasted_iota(jnp.int32, (rows, LANES), 0))
                ln = jnp.where(keep, ln, 0.0)
            ln_hi = ln.astype(BF16)
            ln_lo = (ln - ln_hi.astype(F32)).astype(BF16)
            log_beta.append(lp)
            keeps.append(keep)
            sums.append(jnp.dot(jnp.concatenate([ln_hi, ln_lo], axis=1), tri_aug,
                                preferred_element_type=F32))
        weights = []
        for (h, k0, r0), lp, la, keep in zip(items, log_beta, sums, keeps):
            run = run_sc[h, r0:, :]
            a = jnp.exp2(lp + la[:, :LANES] + run)
            if masked:
                a = jnp.where(keep, a, 0.0)
            run_sc[h, r0:, :] = run + la[:, LANES:]
            weights.append(a.astype(BF16))
        for (h, k0, r0), a in zip(items, weights):
            hs = slice(h * HEAD_DIM, (h + 1) * HEAD_DIM)
            acc_sc[h, r0:, :] += jnp.dot(a, v_ref[pl.ds(k0, LANES), hs], preferred_element_type=F32)

    run_sc[...] = jnp.zeros(run_sc.shape, F32)
    acc_sc[...] = jnp.zeros(acc_sc.shape, F32)
    for cc in range(nsub - 1, -1, -1):
        k0 = pl.multiple_of(i * tq + cc * LANES, LANES)
        sub_blocks([(h, k0, cc * LANES) for h in range(N_HEADS)], True)

    def alive():
        return (jnp.max(run_sc[...]) > SB_DEAD_LOG2).astype(jnp.int32)

    def earlier(state):
        cb, _ = state
        sub_blocks([(h, pl.multiple_of((cb - back) * LANES, LANES), 0)
                    for back in range(2) for h in range(N_HEADS)], False)
        return cb - 2, alive()

    lax.while_loop(lambda st: jnp.logical_and(st[0] >= 0, st[1] > 0), earlier,
                   (i * nsub - 1, alive()))
    for h in range(N_HEADS):
        o_ref[:, h * HEAD_DIM:(h + 1) * HEAD_DIM] = acc_sc[h].astype(o_ref.dtype)


def _sb_attention(qkv, col0, tq=512):
    s = qkv.shape[0]
    w = BRANCH_WIDTH
    assert (tq // LANES) % 2 == 0
    kern = functools.partial(_sb_kernel, tq=tq)
    resident = pl.Buffered(1)
    return pl.pallas_call(
        kern,
        grid=(s // tq,),
        in_specs=[pl.BlockSpec((tq, w), lambda i: (i, col0)),
                  pl.BlockSpec((s, w), lambda i: (0, col0 + 1), pipeline_mode=resident),
                  pl.BlockSpec((s, w), lambda i: (0, col0 + 2), pipeline_mode=resident)],
        out_specs=pl.BlockSpec((tq, w), lambda i: (i, 0)),
        out_shape=jax.ShapeDtypeStruct((s, w), BF16),
        scratch_shapes=[pltpu.VMEM((N_HEADS, tq, LANES), F32),
                        pltpu.VMEM((N_HEADS, tq, HEAD_DIM), F32)],
        compiler_params=_cparams(("arbitrary",)),
        name="sb_attention",
    )(qkv, qkv, qkv)


def _band_kernel(q_ref, k2_ref, k1_ref, k0_ref, v2_ref, v1_ref, v0_ref, ext_ref, o_ref,
                 bias_sc, vaug_sc, *, tq):
    i = pl.program_id(0)
    width = 4 * tq

    @pl.when(i == 0)
    def _build_tables():
        trow = lax.broadcasted_iota(jnp.int32, (tq, width), 0)
        t = lax.broadcasted_iota(jnp.int32, (tq, 3 * tq), 0)
        s = lax.broadcasted_iota(jnp.int32, (tq, 3 * tq), 1)
        shift = CHUNK.bit_length() - 1
        t_chunk = t >> shift
        s_chunk = (s >> shift) - (2 * tq) // CHUNK
        in_band = (t_chunk - s_chunk <= LOOKBACK_CHUNKS) & (s_chunk <= t_chunk)
        for h in range(N_HEADS):
            x = jnp.broadcast_to(ext_ref[h:h + 1, :], (tq, width))
            for b in range(tq.bit_length() - 1):
                x = jnp.where(((trow >> b) & 1) == 1, pltpu.roll(x, 1 << b, axis=1), x)
            bias_sc[h] = jnp.where(in_band, x[:, :3 * tq] * LOG2_E, NEG_BIG)
        vaug_sc[:, :, HEAD_DIM:] = jnp.ones((N_HEADS, 3 * tq, HEAD_DIM), BF16)

    k_refs = (k2_ref, k1_ref, k0_ref)
    v_refs = (v2_ref, v1_ref, v0_ref)
    heads = [slice(h * HEAD_DIM, (h + 1) * HEAD_DIM) for h in range(N_HEADS)]

    def tile(first_tiles):
        scores = [[lax.dot_general(q_ref[:, hs], k_refs[p][:, hs], (((1,), (1,)), ((), ())),
                                   preferred_element_type=F32) for p in range(3)] for hs in heads]
        probs = []
        for h, pieces in enumerate(scores):
            pieces = [s + bias_sc[h, :, p * tq:(p + 1) * tq] for p, s in enumerate(pieces)]
            if first_tiles:
                pieces = [jnp.where(i - 2 + p >= 0, s, NEG_BIG) for p, s in enumerate(pieces)]
            m = jnp.max(jnp.maximum(jnp.maximum(pieces[0], pieces[1]), pieces[2]),
                        axis=-1, keepdims=True)
            probs.append(jnp.concatenate([jnp.exp2(s - m).astype(BF16) for s in pieces], axis=1))
            for p in range(3):
                vaug_sc[h, p * tq:(p + 1) * tq, :HEAD_DIM] = v_refs[p][:, heads[h]]
        for h, p in enumerate(probs):
            acc = jnp.dot(p, vaug_sc[h], preferred_element_type=F32)
            o_ref[:, heads[h]] = (acc[:, :HEAD_DIM] / acc[:, HEAD_DIM:]).astype(o_ref.dtype)

    @pl.when(i < 2)
    def _first_tiles():
        tile(True)

    @pl.when(i >= 2)
    def _other_tiles():
        tile(False)


def _band_bias_vector(rel_bias_l, tq):
    n = np.arange(4 * tq)
    dist = np.where(n < 3 * tq, 2 * tq - n, 6 * tq - n)
    ridx = np.clip(dist, -(CHUNK - 1), REL_CLIP) + (CHUNK - 1)
    return rel_bias_l.astype(F32)[:, ridx]


def _band_attention(qkv, bias_ext, col0, tq=256):
    s = qkv.shape[0]
    w = BRANCH_WIDTH
    assert 2 * tq >= LOOKBACK_CHUNKS * CHUNK and tq % CHUNK == 0 and tq & (tq - 1) == 0
    kern = functools.partial(_band_kernel, tq=tq)

    def kv_spec(back, col):
        return pl.BlockSpec((tq, w), lambda i: (jnp.maximum(i - back, 0), col))

    return pl.pallas_call(
        kern,
        grid=(s // tq,),
        in_specs=[pl.BlockSpec((tq, w), lambda i: (i, col0)),
                  kv_spec(2, col0 + 1), kv_spec(1, col0 + 1), kv_spec(0, col0 + 1),
                  kv_spec(2, col0 + 2), kv_spec(1, col0 + 2), kv_spec(0, col0 + 2),
                  pl.BlockSpec((N_HEADS, 4 * tq), lambda i: (0, 0))],
        out_specs=pl.BlockSpec((tq, w), lambda i: (i, 0)),
        out_shape=jax.ShapeDtypeStruct((s, w), BF16),
        scratch_shapes=[pltpu.VMEM((N_HEADS, tq, 3 * tq), F32),
                        pltpu.VMEM((N_HEADS, 3 * tq, 2 * HEAD_DIM), BF16)],
        compiler_params=_cparams(("arbitrary",)),
        name="band_attention",
    )(qkv, qkv, qkv, qkv, qkv, qkv, qkv, bias_ext)


def _gelu_tanh(x):
    c = math.sqrt(2.0 / math.pi)
    return 0.5 * x * (1.0 + jnp.tanh(c * (x + 0.044715 * (x * x * x))))


def _lru_kernel(rx_ref, ry_ref, cw_ref, cb_ref, wr_ref, br_ref, wi_ref, bi_ref, lam_ref,
                o_ref, xext_sc, a_sc, b_sc, h_sc, carry_sc, *, tm):
    i = pl.program_id(0)
    w = BRANCH_WIDTH
    halo = SUBLANES

    @pl.when(i == 0)
    def _first():
        xext_sc[0:halo, :] = jnp.zeros((halo, w), F32)
        carry_sc[...] = jnp.zeros(carry_sc.shape, F32)

    @pl.when(i > 0)
    def _shift_halo():
        xext_sc[0:halo, :] = xext_sc[tm:tm + halo, :]

    xext_sc[halo:halo + tm, :] = rx_ref[...]
    xext = xext_sc[0:halo + tm, :]
    xc = cb_ref[...] + rx_ref[...] * cw_ref[CONV_WIDTH - 1:CONV_WIDTH, :]
    for back in range(1, CONV_WIDTH):
        t = CONV_WIDTH - 1 - back
        xc = xc + pltpu.roll(xext, back, axis=0)[halo:, :] * cw_ref[t:t + 1, :]

    xcb = xc.astype(BF16)
    r_parts, i_parts = [], []
    for n in range(N_HEADS):
        ns = slice(n * HEAD_DIM, (n + 1) * HEAD_DIM)
        r_parts.append(jnp.dot(xcb[:, ns], wr_ref[n], preferred_element_type=F32))
        i_parts.append(jnp.dot(xcb[:, ns], wi_ref[n], preferred_element_type=F32))
    r = jax.nn.sigmoid(jnp.concatenate(r_parts, axis=1) + br_ref[...])
    gi = jax.nn.sigmoid(jnp.concatenate(i_parts, axis=1) + bi_ref[...])
    log_a = LRU_C * r * _log_sigmoid(lam_ref[...])
    a = jnp.exp(log_a)
    a_sc[...] = a
    b_sc[...] = jnp.sqrt(-jnp.tanh(log_a) * (a * a + 1.0)) * (gi * xc)

    row = lax.broadcasted_iota(jnp.int32, (SUBLANES, w), 0)

    def group(g, carry):
        r0 = pl.multiple_of(g * SUBLANES, SUBLANES)
        a = a_sc[pl.ds(r0, SUBLANES), :]
        b = b_sc[pl.ds(r0, SUBLANES), :]
        for k in (1, 2, 4):
            a_prev = pltpu.roll(a, k, axis=0)
            b_prev = pltpu.roll(b, k, axis=0)
            ok = row >= k
            b = jnp.where(ok, a * b_prev + b, b)
            a = jnp.where(ok, a * a_prev, a)
        hgrp = a * carry + b
        h_sc[pl.ds(r0, SUBLANES), :] = hgrp
        return jnp.broadcast_to(hgrp[SUBLANES - 1:SUBLANES, :], (SUBLANES, w))

    carry_sc[...] = lax.fori_loop(0, tm // SUBLANES, group, carry_sc[...])
    o_ref[...] = (h_sc[...] * _gelu_tanh(ry_ref[...])).astype(o_ref.dtype)


def _recurrent_branch(uf, conv_w, conv_b, w_r, b_r, w_i, b_i, lam, tm=512):
    s = uf.shape[0]
    w = BRANCH_WIDTH
    kern = functools.partial(_lru_kernel, tm=tm)
    row = lambda v: v.reshape(1, w)
    full2 = lambda shape: pl.BlockSpec(shape, lambda i: (0, 0))
    full3 = lambda shape: pl.BlockSpec(shape, lambda i: (0, 0, 0))
    return pl.pallas_call(
        kern,
        grid=(s // tm,),
        in_specs=[pl.BlockSpec((tm, w), lambda i: (i, 0)),
                  pl.BlockSpec((tm, w), lambda i: (i, 1)),
                  full2((CONV_WIDTH, w)), full2((1, w)),
                  full3((N_HEADS, HEAD_DIM, HEAD_DIM)), full2((1, w)),
                  full3((N_HEADS, HEAD_DIM, HEAD_DIM)), full2((1, w)),
                  full2((1, w))],
        out_specs=pl.BlockSpec((tm, w), lambda i: (i, 0)),
        out_shape=jax.ShapeDtypeStruct((s, w), BF16),
        scratch_shapes=[pltpu.VMEM((tm + 2 * SUBLANES, w), F32),
                        pltpu.VMEM((tm, w), F32),
                        pltpu.VMEM((tm, w), F32),
                        pltpu.VMEM((tm, w), F32),
                        pltpu.VMEM((SUBLANES, w), F32)],
        compiler_params=_cparams(("arbitrary",)),
        name="recurrent_branch",
    )(uf, uf, conv_w, row(conv_b), w_r.astype(BF16), row(b_r), w_i.astype(BF16), row(b_i), row(lam))


def _merge_kernel(x_ref, o0_ref, o1_ref, o2_ref, o3_ref, wg_ref, bg_ref, wb_ref, out_ref):
    x = x_ref[...]
    merged = None
    for g, o_ref in enumerate((o0_ref, o1_ref, o2_ref, o3_ref)):
        gate = jax.nn.sigmoid(jnp.dot(x, wg_ref[g].astype(BF16), preferred_element_type=F32)
                              + bg_ref[g:g + 1, :])
        term = gate * jnp.dot(o_ref[...], wb_ref[g].astype(BF16), preferred_element_type=F32)
        merged = term if merged is None else merged + term
    out_ref[...] = merged.astype(out_ref.dtype)


def _merge(xb, branches, wg_all, bg_all, wb_all, layer, tm=1024, tn=256):
    s, d = xb.shape
    w = BRANCH_WIDTH
    o_spec = pl.BlockSpec((tm, w), lambda i, j: (i, 0))
    return pl.pallas_call(
        _merge_kernel,
        grid=(s // tm, d // tn),
        in_specs=[pl.BlockSpec((tm, d), lambda i, j: (i, 0)),
                  o_spec, o_spec, o_spec, o_spec,
                  pl.BlockSpec((None, N_BRANCH, d, tn), lambda i, j: (layer, 0, 0, j)),
                  pl.BlockSpec((None, N_BRANCH, tn), lambda i, j: (layer, 0, j)),
                  pl.BlockSpec((None, N_BRANCH, w, tn), lambda i, j: (layer, 0, 0, j))],
        out_specs=pl.BlockSpec((tm, tn), lambda i, j: (i, j)),
        out_shape=jax.ShapeDtypeStruct((s, d), BF16),
        compiler_params=_cparams(("parallel", "arbitrary")),
        name="gated_merge",
    )(xb, *branches, wg_all, bg_all, wb_all)


def _outproj_kernel(m_ref, w_ref, x_ref, g_ref, b_ref, of_ref, ob_ref):
    half = m_ref.shape[0] // 2
    rows = (slice(0, half), slice(half, 2 * half))
    proj = [jnp.dot(m_ref[r, :], w_ref[...], preferred_element_type=F32) for r in rows]
    for r, p in zip(rows, proj):
        y = _layer_norm_rows(ALPHA * x_ref[r, :] + p, g_ref[...], b_ref[...])
        of_ref[r, :] = y
        ob_ref[r, :] = y.astype(BF16)


def _outproj_ln(merged, w_out_all, layer, x, g, b, tm=512):
    s, d = x.shape
    row_spec = pl.BlockSpec((tm, d), lambda i: (i, 0))
    vec_spec = pl.BlockSpec((1, d), lambda i: (0, 0))
    return pl.pallas_call(
        _outproj_kernel,
        grid=(s // tm,),
        in_specs=[row_spec, pl.BlockSpec((None, d, d), lambda i: (layer, 0, 0)), row_spec,
                  vec_spec, vec_spec],
        out_specs=[row_spec, row_spec],
        out_shape=[jax.ShapeDtypeStruct((s, d), F32), jax.ShapeDtypeStruct((s, d), BF16)],
        compiler_params=_cparams(("parallel",)),
        name="outproj_ln",
    )(merged, w_out_all, x, g.reshape(1, d), b.reshape(1, d))


def _ffn_kernel(xb_ref, xf_ref, w1_ref, w2_ref, g_ref, b_ref, of_ref, ob_ref, acc_sc):
    f = pl.program_id(1)

    @pl.when(f == 0)
    def _init():
        acc_sc[...] = jnp.zeros(acc_sc.shape, F32)

    hid = jnp.maximum(jnp.dot(xb_ref[...], w1_ref[...], preferred_element_type=F32), 0.0)
    hid = (hid * hid).astype(BF16)
    acc_sc[...] += jnp.dot(hid, w2_ref[...], preferred_element_type=F32)

    @pl.when(f == pl.num_programs(1) - 1)
    def _finish():
        y = _layer_norm_rows(ALPHA * xf_ref[...] + acc_sc[...], g_ref[...], b_ref[...])
        of_ref[...] = y
        ob_ref[...] = y.astype(BF16)


def _ffn_ln(xb, xf, w1_all, w2_all, layer, g, b, tm=512, tf=1024):
    s, d = xf.shape
    dff = w1_all.shape[2]
    row_spec = pl.BlockSpec((tm, d), lambda i, f: (i, 0))
    vec_spec = pl.BlockSpec((1, d), lambda i, f: (0, 0))
    return pl.pallas_call(
        _ffn_kernel,
        grid=(s // tm, dff // tf),
        in_specs=[row_spec, row_spec,
                  pl.BlockSpec((None, d, tf), lambda i, f: (layer, 0, f)),
                  pl.BlockSpec((None, tf, d), lambda i, f: (layer, f, 0)),
                  vec_spec, vec_spec],
        out_specs=[row_spec, row_spec],
        out_shape=[jax.ShapeDtypeStruct((s, d), F32), jax.ShapeDtypeStruct((s, d), BF16)],
        scratch_shapes=[pltpu.VMEM((tm, d), F32)],
        compiler_params=_cparams(("parallel", "arbitrary")),
        name="ffn_ln",
    )(xb, xf, w1_all, w2_all, g.reshape(1, d), b.reshape(1, d))


def _split_in_proj(w_in):
    qkv = jnp.concatenate([w_in[:, :, _OFF_FQ:_OFF_FF], w_in[:, :, _OFF_SQ:_OFF_END]], axis=2)
    pad = jnp.zeros(w_in.shape[:2] + (LANES - N_HEADS,), w_in.dtype)
    rest = jnp.concatenate([w_in[:, :, _OFF_RX:_OFF_SQ], w_in[:, :, _OFF_FF:_OFF_RX], pad], axis=2)
    return qkv.astype(BF16), rest.astype(BF16)


def kernel(x, ln_in_g, ln_in_b, w_in, b_forget, conv_w, conv_b, w_r, b_r, w_i, b_i, lru_lambda,
           rel_bias, w_branch, w_gate, b_gate, w_out, ln1_g, ln1_b, w_ff1, w_ff2, ln2_g, ln2_b):
    batch, s, d = x.shape
    assert (batch, s, d) == (1, SEQ, D_MODEL)
    w = BRANCH_WIDTH
    band_tq = 256

    col_scale = np.ones((1, 9 * w), np.float32)
    for q_block in (0, 3, 6):
        col_scale[:, q_block * w:(q_block + 1) * w] = QK_SCALE * LOG2_E
    qkv_scale = jnp.asarray(col_scale)
    rest_scale = jnp.ones((1, 2 * w + LANES), F32)

    w_qkv, w_rest = _split_in_proj(w_in)
    wo_b = w_out.astype(BF16)
    w1_b, w2_b = w_ff1.astype(BF16), w_ff2.astype(BF16)

    xf, xb = _entry_ln(x.reshape(s, d), ln_in_g, ln_in_b)
    for l in range(DEPTH):
        qkv = _project(xb, w_qkv, l, qkv_scale, BF16, 1024, 1536, "in_proj_qkv")
        uf = _project(xb, w_rest, l, rest_scale, F32, 1024, 1152, "in_proj_rest")

        f_rows = uf[:, 2 * w:2 * w + N_HEADS].T.reshape(N_HEADS * (s // LANES), LANES)
        b_rows = jnp.repeat(b_forget[l].astype(F32), s // LANES).reshape(-1, 1)
        cf = _forget_cumsum(f_rows, b_rows).reshape(N_HEADS, s)
        cfk = jnp.pad(cf, ((0, SUBLANES - N_HEADS), (0, 0)))

        o_fox = _fox_attention(qkv, cfk, 0)
        o_sb = _sb_attention(qkv, 3)
        o_ch = _band_attention(qkv, _band_bias_vector(rel_bias[l], band_tq), 6, band_tq)
        o_lru = _recurrent_branch(uf, conv_w[l], conv_b[l], w_r[l], b_r[l], w_i[l], b_i[l],
                                  lru_lambda[l])

        merged = _merge(xb, (o_fox, o_lru, o_sb, o_ch), w_gate, b_gate, w_branch, l)
        xf, xb = _outproj_ln(merged, wo_b, l, xf, ln1_g[l], ln1_b[l])
        xf, xb = _ffn_ln(xb, xf, w1_b, w2_b, l, ln2_g[l], ln2_b[l])
    return xf.reshape(batch, s, d)
```

```python
import functools
import math

import jax
import jax.numpy as jnp
import numpy as np
from jax import lax
from jax.experimental import pallas as pl
from jax.experimental.pallas import tpu as pltpu

F32 = jnp.float32
BF16 = jnp.bfloat16

D_MODEL = 2048
SEQ = 8192
DEPTH = 2
CHUNK = 64
HEAD_DIM = 128
N_BRANCH = 4
BRANCH_WIDTH = D_MODEL // N_BRANCH
N_HEADS = BRANCH_WIDTH // HEAD_DIM
CONV_WIDTH = 4
LRU_C = 8.0
LOOKBACK_CHUNKS = 8
REL_CLIP = 256
D_FF = 4 * D_MODEL
ALPHA = (2.0 * DEPTH) ** 0.25
LN_EPS = 1e-5
QK_SCALE = HEAD_DIM ** -0.5
LOG2_E = math.log2(math.e)

_OFF_FQ = 0
_OFF_FK = _OFF_FQ + BRANCH_WIDTH
_OFF_FV = _OFF_FK + BRANCH_WIDTH
_OFF_FF = _OFF_FV + BRANCH_WIDTH
_OFF_RX = _OFF_FF + N_HEADS
_OFF_RY = _OFF_RX + BRANCH_WIDTH
_OFF_SQ = _OFF_RY + BRANCH_WIDTH
_OFF_CQ = _OFF_SQ + 3 * BRANCH_WIDTH
_OFF_END = _OFF_CQ + 3 * BRANCH_WIDTH

LANES = 128
SUBLANES = 8
NEG_BIG = -1e30
SB_DEAD_LOG2 = -180.0
FOX_DEAD_LOG2 = -170.0

VMEM_LIMIT = 56 * 1024 * 1024


def _cparams(sem, vmem=VMEM_LIMIT):
    return pltpu.CompilerParams(dimension_semantics=sem, vmem_limit_bytes=vmem)


def _log_sigmoid(x):
    return jnp.minimum(x, 0.0) - jnp.log1p(jnp.exp(-jnp.abs(x)))


def _layer_norm_rows(y, g, b):
    mu = jnp.mean(y, axis=-1, keepdims=True)
    d = y - mu
    var = jnp.mean(d * d, axis=-1, keepdims=True)
    return d * lax.rsqrt(var + LN_EPS) * g + b


def _ln_kernel(x_ref, g_ref, b_ref, of_ref, ob_ref):
    y = _layer_norm_rows(x_ref[...], g_ref[...], b_ref[...])
    of_ref[...] = y
    ob_ref[...] = y.astype(BF16)


def _entry_ln(x, g, b, tm=512):
    s, d = x.shape
    return pl.pallas_call(
        _ln_kernel,
        grid=(s // tm,),
        in_specs=[pl.BlockSpec((tm, d), lambda i: (i, 0)),
                  pl.BlockSpec((1, d), lambda i: (0, 0)),
                  pl.BlockSpec((1, d), lambda i: (0, 0))],
        out_specs=[pl.BlockSpec((tm, d), lambda i: (i, 0)),
                   pl.BlockSpec((tm, d), lambda i: (i, 0))],
        out_shape=[jax.ShapeDtypeStruct((s, d), F32),
                   jax.ShapeDtypeStruct((s, d), BF16)],
        compiler_params=_cparams(("parallel",)),
        name="entry_ln",
    )(x, g.reshape(1, d), b.reshape(1, d))


def _proj_kernel(x_ref, w_ref, s_ref, o_ref):
    acc = jnp.dot(x_ref[...], w_ref[...], preferred_element_type=F32)
    o_ref[...] = (acc * s_ref[...]).astype(o_ref.dtype)


def _project(xb, w_all, layer, colscale, out_dtype, tm, tn, name):
    m, k = xb.shape
    n = w_all.shape[2]
    return pl.pallas_call(
        _proj_kernel,
        grid=(m // tm, n // tn),
        in_specs=[pl.BlockSpec((tm, k), lambda i, j: (i, 0)),
                  pl.BlockSpec((None, k, tn), lambda i, j: (layer, 0, j)),
                  pl.BlockSpec((1, tn), lambda i, j: (0, j))],
        out_specs=pl.BlockSpec((tm, tn), lambda i, j: (i, j)),
        out_shape=jax.ShapeDtypeStruct((m, n), out_dtype),
        compiler_params=_cparams(("parallel", "arbitrary")),
        name=name,
    )(xb, w_all, colscale)


def _forget_cumsum_kernel(f_ref, b_ref, o_ref):
    rows = f_ref.shape[0]
    per_head = rows // N_HEADS
    ls = _log_sigmoid(f_ref[...] + b_ref[...])
    r = lax.broadcasted_iota(jnp.int32, (LANES, LANES), 0)
    c = lax.broadcasted_iota(jnp.int32, (LANES, LANES), 1)
    upper = (r <= c).astype(F32)
    within = jnp.dot(ls, upper, preferred_element_type=F32,
                     precision=lax.Precision.HIGHEST)
    total = within[:, LANES - 1:LANES]
    rr = lax.broadcasted_iota(jnp.int32, (rows, rows), 0)
    cc = lax.broadcasted_iota(jnp.int32, (rows, rows), 1)
    head_start = rr - (rr & (per_head - 1))
    before = ((cc >= head_start) & (cc < rr)).astype(F32)
    offs = jnp.dot(before, jnp.broadcast_to(total, (rows, LANES)),
                   preferred_element_type=F32, precision=lax.Precision.HIGHEST)
    o_ref[...] = (within + offs) * LOG2_E


def _forget_cumsum(f_rows, b_rows):
    rows = f_rows.shape[0]
    return pl.pallas_call(
        _forget_cumsum_kernel,
        out_shape=jax.ShapeDtypeStruct((rows, LANES), F32),
        name="forget_cumsum",
    )(f_rows, b_rows)


def _fox_kernel(q_ref, k_ref, v_ref, cfk_ref, o_ref, m_sc, acc_sc, vaug_sc, qn_sc, kn_sc, *, tq, tk):
    i = pl.program_id(0)
    heads = [slice(h * HEAD_DIM, (h + 1) * HEAD_DIM) for h in range(N_HEADS)]
    norm_rows = 1024

    @pl.when(i == 0)
    def _largest_key_norm():
        for h, hs in enumerate(heads):
            def chunk(c, best, hs=hs):
                rows = k_ref[pl.ds(pl.multiple_of(c * norm_rows, norm_rows), norm_rows), hs].astype(F32)
                return jnp.maximum(best, jnp.max(jnp.sum(rows * rows, axis=-1, keepdims=True)))
            best = lax.fori_loop(0, k_ref.shape[0] // norm_rows, chunk, jnp.zeros((SUBLANES, LANES), F32))
            kn_sc[h] = jnp.sqrt(best)

    m_sc[...] = jnp.full(m_sc.shape, NEG_BIG, F32)
    acc_sc[...] = jnp.zeros(acc_sc.shape, F32)
    vaug_sc[:, :, HEAD_DIM:] = jnp.ones((N_HEADS, tk, HEAD_DIM), BF16)
    for h, hs in enumerate(heads):
        q = q_ref[:, hs].astype(F32)
        qn_sc[h] = jnp.broadcast_to(jnp.sqrt(jnp.sum(q * q, axis=-1, keepdims=True)), (tq, LANES))
    rep = tk // LANES

    def key_tile(j, masked, hlist):
        k0 = pl.multiple_of(j * tk, tk)
        if masked:
            keep = (lax.broadcasted_iota(jnp.int32, (tq, tk), 1)
                    <= lax.broadcasted_iota(jnp.int32, (tq, tk), 0))
        scores = [lax.dot_general(q_ref[:, heads[h]], k_ref[pl.ds(k0, tk), heads[h]],
                                  (((1,), (1,)), ((), ())), preferred_element_type=F32) for h in hlist]
        probs, alphas = [], []
        for h, s in zip(hlist, scores):
            s = s - cfk_ref[h:h + 1, pl.ds(k0, tk)]
            if masked:
                s = jnp.where(keep, s, NEG_BIG)
            m_old = m_sc[h]
            m_new = jnp.maximum(m_old, jnp.max(s, axis=-1, keepdims=True))
            alphas.append(jnp.exp2(m_old - m_new))
            probs.append(jnp.exp2(s - jnp.concatenate([m_new] * rep, axis=1)).astype(BF16))
            m_sc[h] = m_new
            vaug_sc[h, :, :HEAD_DIM] = v_ref[pl.ds(k0, tk), heads[h]]
        for h, p, alpha in zip(hlist, probs, alphas):
            pv = jnp.dot(p, vaug_sc[h], preferred_element_type=F32)
            acc_sc[h] = jnp.concatenate([alpha, alpha], axis=1) * acc_sc[h] + pv

    def alive(j, hlist):
        newest = pl.multiple_of(jnp.maximum(j, 0) * tk + tk - LANES, LANES)
        reach = None
        for h in hlist:
            decay = -cfk_ref[h:h + 1, pl.ds(newest, LANES)][:, LANES - 1:]
            bound = jnp.max(qn_sc[h] * kn_sc[h, 0:1, :] + decay - m_sc[h], axis=0, keepdims=True)
            reach = bound if reach is None else jnp.minimum(reach, bound)
        return (jnp.max(reach) > FOX_DEAD_LOG2).astype(jnp.int32)

    def walk_back(j_start, hlist):
        def earlier(state):
            j, _ = state
            key_tile(j, False, hlist)
            return j - 1, alive(j - 1, hlist)
        return lax.while_loop(lambda st: jnp.logical_and(st[0] >= 0, st[1] > 0), earlier,
                              (j_start, alive(j_start, hlist)))[0]

    all_heads = list(range(N_HEADS))
    key_tile(i, True, all_heads)
    j_split = walk_back(i - 1, all_heads)
    for h in all_heads:
        walk_back(j_split, [h])
    for h in range(N_HEADS):
        hs = slice(h * HEAD_DIM, (h + 1) * HEAD_DIM)
        o_ref[:, hs] = (acc_sc[h, :, :HEAD_DIM] / acc_sc[h, :, HEAD_DIM:]).astype(o_ref.dtype)


def _fox_attention(qkv, cfk, col0, tq=512):
    s = qkv.shape[0]
    tk = tq
    w = BRANCH_WIDTH
    kern = functools.partial(_fox_kernel, tq=tq, tk=tk)
    resident = pl.Buffered(1)
    return pl.pallas_call(
        kern,
        grid=(s // tq,),
        in_specs=[pl.BlockSpec((tq, w), lambda i: (i, col0)),
                  pl.BlockSpec((s, w), lambda i: (0, col0 + 1), pipeline_mode=resident),
                  pl.BlockSpec((s, w), lambda i: (0, col0 + 2), pipeline_mode=resident),
                  pl.BlockSpec((SUBLANES, s), lambda i: (0, 0), pipeline_mode=resident)],
        out_specs=pl.BlockSpec((tq, w), lambda i: (i, 0)),
        out_shape=jax.ShapeDtypeStruct((s, w), BF16),
        scratch_shapes=[pltpu.VMEM((N_HEADS, tq, LANES), F32),
                        pltpu.VMEM((N_HEADS, tq, 2 * HEAD_DIM), F32),
                        pltpu.VMEM((N_HEADS, tk, 2 * HEAD_DIM), BF16),
                        pltpu.VMEM((N_HEADS, tq, LANES), F32),
                        pltpu.VMEM((N_HEADS, SUBLANES, LANES), F32)],
        compiler_params=_cparams(("arbitrary",)),
        name="fox_attention",
    )(qkv, qkv, qkv, cfk)


def _sb_kernel(q_ref, k_ref, v_ref, o_ref, run_sc, acc_sc, *, tq):
    i = pl.program_id(0)
    nsub = tq // LANES
    r = lax.broadcasted_iota(jnp.int32, (2 * LANES, 2 * LANES), 0) & (LANES - 1)
    c = lax.broadcasted_iota(jnp.int32, (2 * LANES, 2 * LANES), 1)
    tri_aug = ((c >= LANES) | (r > c)).astype(BF16)

    def sub_blocks(items, masked):
        scores = []
        for h, k0, r0 in items:
            hs = slice(h * HEAD_DIM, (h + 1) * HEAD_DIM)
            scores.append(lax.dot_general(q_ref[r0:, hs], k_ref[pl.ds(k0, LANES), hs],
                                          (((1,), (1,)), ((), ())), preferred_element_type=F32))
        log_beta, sums, keeps = [], [], []
        for (h, k0, r0), z in zip(items, scores):
            rows = tq - r0
            lp = jnp.minimum(z, 0.0) - jnp.log2(1.0 + jnp.exp2(-jnp.abs(z)))
            ln = lp - z
            keep = None
            if masked:
                keep = (lax.broadcasted_iota(jnp.int32, (rows, LANES), 1)
                        < lax.broadcasted_iota(jnp.int32, (rows, LANES), 0))
                ln = jnp.where(keep, ln, 0.0)
            ln_hi = ln.astype(BF16)
            ln_lo = (ln - ln_hi.astype(F32)).astype(BF16)
            log_beta.append(lp)
            keeps.append(keep)
            sums.append(jnp.dot(jnp.concatenate([ln_hi, ln_lo], axis=1), tri_aug,
                                preferred_element_type=F32))
        weights = []
        for (h, k0, r0), lp, la, keep in zip(items, log_beta, sums, keeps):
            run = run_sc[h, r0:, :]
            a = jnp.exp2(lp + la[:, :LANES] + run)
            if masked:
                a = jnp.where(keep, a, 0.0)
            run_sc[h, r0:, :] = run + la[:, LANES:]
            weights.append(a.astype(BF16))
        for (h, k0, r0), a in zip(items, weights):
            hs = slice(h * HEAD_DIM, (h + 1) * HEAD_DIM)
            acc_sc[h, r0:, :] += jnp.dot(a, v_ref[pl.ds(k0, LANES), hs], preferred_element_type=F32)

    run_sc[...] = jnp.zeros(run_sc.shape, F32)
    acc_sc[...] = jnp.zeros(acc_sc.shape, F32)
    for cc in range(nsub - 1, -1, -1):
        k0 = pl.multiple_of(i * tq + cc * LANES, LANES)
        sub_blocks([(h, k0, cc * LANES) for h in range(N_HEADS)], True)

    def alive():
        return (jnp.max(run_sc[...]) > SB_DEAD_LOG2).astype(jnp.int32)

    def earlier(state):
        cb, _ = state
        sub_blocks([(h, pl.multiple_of((cb - back) * LANES, LANES), 0)
                    for back in range(2) for h in range(N_HEADS)], False)
        return cb - 2, alive()

    lax.while_loop(lambda st: jnp.logical_and(st[0] >= 0, st[1] > 0), earlier,
                   (i * nsub - 1, alive()))
    for h in range(N_HEADS):
        o_ref[:, h * HEAD_DIM:(h + 1) * HEAD_DIM] = acc_sc[h].astype(o_ref.dtype)


def _sb_attention(qkv, col0, tq=512):
    s = qkv.shape[0]
    w = BRANCH_WIDTH
    assert (tq // LANES) % 2 == 0
    kern = functools.partial(_sb_kernel, tq=tq)
    resident = pl.Buffered(1)
    return pl.pallas_call(
        kern,
        grid=(s // tq,),
        in_specs=[pl.BlockSpec((tq, w), lambda i: (i, col0)),
                  pl.BlockSpec((s, w), lambda i: (0, col0 + 1), pipeline_mode=resident),
                  pl.BlockSpec((s, w), lambda i: (0, col0 + 2), pipeline_mode=resident)],
        out_specs=pl.BlockSpec((tq, w), lambda i: (i, 0)),
        out_shape=jax.ShapeDtypeStruct((s, w), BF16),
        scratch_shapes=[pltpu.VMEM((N_HEADS, tq, LANES), F32),
                        pltpu.VMEM((N_HEADS, tq, HEAD_DIM), F32)],
        compiler_params=_cparams(("arbitrary",)),
        name="sb_attention",
    )(qkv, qkv, qkv)


def _band_kernel(q_ref, k2_ref, k1_ref, k0_ref, v2_ref, v1_ref, v0_ref, ext_ref, o_ref,
                 bias_sc, vaug_sc, *, tq):
    i = pl.program_id(0)
    width = 4 * tq

    @pl.when(i == 0)
    def _build_tables():
        trow = lax.broadcasted_iota(jnp.int32, (tq, width), 0)
        t = lax.broadcasted_iota(jnp.int32, (tq, 3 * tq), 0)
        s = lax.broadcasted_iota(jnp.int32, (tq, 3 * tq), 1)
        shift = CHUNK.bit_length() - 1
        t_chunk = t >> shift
        s_chunk = (s >> shift) - (2 * tq) // CHUNK
        in_band = (t_chunk - s_chunk <= LOOKBACK_CHUNKS) & (s_chunk <= t_chunk)
        for h in range(N_HEADS):
            x = jnp.broadcast_to(ext_ref[h:h + 1, :], (tq, width))
            for b in range(tq.bit_length() - 1):
                x = jnp.where(((trow >> b) & 1) == 1, pltpu.roll(x, 1 << b, axis=1), x)
            bias_sc[h] = jnp.where(in_band, x[:, :3 * tq] * LOG2_E, NEG_BIG)
        vaug_sc[:, :, HEAD_DIM:] = jnp.ones((N_HEADS, 3 * tq, HEAD_DIM), BF16)

    k_refs = (k2_ref, k1_ref, k0_ref)
    v_refs = (v2_ref, v1_ref, v0_ref)
    heads = [slice(h * HEAD_DIM, (h + 1) * HEAD_DIM) for h in range(N_HEADS)]

    def tile(first_tiles):
        scores = [[lax.dot_general(q_ref[:, hs], k_refs[p][:, hs], (((1,), (1,)), ((), ())),
                                   preferred_element_type=F32) for p in range(3)] for hs in heads]
        probs = []
        for h, pieces in enumerate(scores):
            pieces = [s + bias_sc[h, :, p * tq:(p + 1) * tq] for p, s in enumerate(pieces)]
            if first_tiles:
                pieces = [jnp.where(i - 2 + p >= 0, s, NEG_BIG) for p, s in enumerate(pieces)]
            m = jnp.max(jnp.maximum(jnp.maximum(pieces[0], pieces[1]), pieces[2]),
                        axis=-1, keepdims=True)
            probs.append(jnp.concatenate([jnp.exp2(s - m).astype(BF16) for s in pieces], axis=1))
            for p in range(3):
                vaug_sc[h, p * tq:(p + 1) * tq, :HEAD_DIM] = v_refs[p][:, heads[h]]
        for h, p in enumerate(probs):
            acc = jnp.dot(p, vaug_sc[h], preferred_element_type=F32)
            o_ref[:, heads[h]] = (acc[:, :HEAD_DIM] / acc[:, HEAD_DIM:]).astype(o_ref.dtype)

    @pl.when(i < 2)
    def _first_tiles():
        tile(True)

    @pl.when(i >= 2)
    def _other_tiles():
        tile(False)


def _band_bias_vector(rel_bias_l, tq):
    n = np.arange(4 * tq)
    dist = np.where(n < 3 * tq, 2 * tq - n, 6 * tq - n)
    ridx = np.clip(dist, -(CHUNK - 1), REL_CLIP) + (CHUNK - 1)
    return rel_bias_l.astype(F32)[:, ridx]


def _band_attention(qkv, bias_ext, col0, tq=256):
    s = qkv.shape[0]
    w = BRANCH_WIDTH
    assert 2 * tq >= LOOKBACK_CHUNKS * CHUNK and tq % CHUNK == 0 and tq & (tq - 1) == 0
    kern = functools.partial(_band_kernel, tq=tq)

    def kv_spec(back, col):
        return pl.BlockSpec((tq, w), lambda i: (jnp.maximum(i - back, 0), col))

    return pl.pallas_call(
        kern,
        grid=(s // tq,),
        in_specs=[pl.BlockSpec((tq, w), lambda i: (i, col0)),
                  kv_spec(2, col0 + 1), kv_spec(1, col0 + 1), kv_spec(0, col0 + 1),
                  kv_spec(2, col0 + 2), kv_spec(1, col0 + 2), kv_spec(0, col0 + 2),
                  pl.BlockSpec((N_HEADS, 4 * tq), lambda i: (0, 0))],
        out_specs=pl.BlockSpec((tq, w), lambda i: (i, 0)),
        out_shape=jax.ShapeDtypeStruct((s, w), BF16),
        scratch_shapes=[pltpu.VMEM((N_HEADS, tq, 3 * tq), F32),
                        pltpu.VMEM((N_HEADS, 3 * tq, 2 * HEAD_DIM), BF16)],
        compiler_params=_cparams(("arbitrary",)),
        name="band_attention",
    )(qkv, qkv, qkv, qkv, qkv, qkv, qkv, bias_ext)


def _gelu_tanh(x):
    c = math.sqrt(2.0 / math.pi)
    return 0.5 * x * (1.0 + jnp.tanh(c * (x + 0.044715 * (x * x * x))))


def _lru_kernel(rx_ref, ry_ref, cw_ref, cb_ref, wr_ref, br_ref, wi_ref, bi_ref, lam_ref,
                o_ref, xext_sc, a_sc, b_sc, h_sc, carry_sc, *, tm):
    i = pl.program_id(0)
    w = BRANCH_WIDTH
    halo = SUBLANES

    @pl.when(i == 0)
    def _first():
        xext_sc[0:halo, :] = jnp.zeros((halo, w), F32)
        carry_sc[...] = jnp.zeros(carry_sc.shape, F32)

    @pl.when(i > 0)
    def _shift_halo():
        xext_sc[0:halo, :] = xext_sc[tm:tm + halo, :]

    xext_sc[halo:halo + tm, :] = rx_ref[...]
    xext = xext_sc[0:halo + tm, :]
    xc = cb_ref[...] + rx_ref[...] * cw_ref[CONV_WIDTH - 1:CONV_WIDTH, :]
    for back in range(1, CONV_WIDTH):
        t = CONV_WIDTH - 1 - back
        xc = xc + pltpu.roll(xext, back, axis=0)[halo:, :] * cw_ref[t:t + 1, :]

    xcb = xc.astype(BF16)
    r_parts, i_parts = [], []
    for n in range(N_HEADS):
        ns = slice(n * HEAD_DIM, (n + 1) * HEAD_DIM)
        r_parts.append(jnp.dot(xcb[:, ns], wr_ref[n], preferred_element_type=F32))
        i_parts.append(jnp.dot(xcb[:, ns], wi_ref[n], preferred_element_type=F32))
    r = jax.nn.sigmoid(jnp.concatenate(r_parts, axis=1) + br_ref[...])
    gi = jax.nn.sigmoid(jnp.concatenate(i_parts, axis=1) + bi_ref[...])
    log_a = LRU_C * r * _log_sigmoid(lam_ref[...])
    a = jnp.exp(log_a)
    a_sc[...] = a
    b_sc[...] = jnp.sqrt(-jnp.tanh(log_a) * (a * a + 1.0)) * (gi * xc)

    row = lax.broadcasted_iota(jnp.int32, (SUBLANES, w), 0)

    def group(g, carry):
        r0 = pl.multiple_of(g * SUBLANES, SUBLANES)
        a = a_sc[pl.ds(r0, SUBLANES), :]
        b = b_sc[pl.ds(r0, SUBLANES), :]
        for k in (1, 2, 4):
            a_prev = pltpu.roll(a, k, axis=0)
            b_prev = pltpu.roll(b, k, axis=0)
            ok = row >= k
            b = jnp.where(ok, a * b_prev + b, b)
            a = jnp.where(ok, a * a_prev, a)
        hgrp = a * carry + b
        h_sc[pl.ds(r0, SUBLANES), :] = hgrp
        return jnp.broadcast_to(hgrp[SUBLANES - 1:SUBLANES, :], (SUBLANES, w))

    carry_sc[...] = lax.fori_loop(0, tm // SUBLANES, group, carry_sc[...])
    o_ref[...] = (h_sc[...] * _gelu_tanh(ry_ref[...])).astype(o_ref.dtype)


def _recurrent_branch(uf, conv_w, conv_b, w_r, b_r, w_i, b_i, lam, tm=512):
    s = uf.shape[0]
    w = BRANCH_WIDTH
    kern = functools.partial(_lru_kernel, tm=tm)
    row = lambda v: v.reshape(1, w)
    full2 = lambda shape: pl.BlockSpec(shape, lambda i: (0, 0))
    full3 = lambda shape: pl.BlockSpec(shape, lambda i: (0, 0, 0))
    return pl.pallas_call(
        kern,
        grid=(s // tm,),
        in_specs=[pl.BlockSpec((tm, w), lambda i: (i, 0)),
                  pl.BlockSpec((tm, w), lambda i: (i, 1)),
                  full2((CONV_WIDTH, w)), full2((1, w)),
                  full3((N_HEADS, HEAD_DIM, HEAD_DIM)), full2((1, w)),
                  full3((N_HEADS, HEAD_DIM, HEAD_DIM)), full2((1, w)),
                  full2((1, w))],
        out_specs=pl.BlockSpec((tm, w), lambda i: (i, 0)),
        out_shape=jax.ShapeDtypeStruct((s, w), BF16),
        scratch_shapes=[pltpu.VMEM((tm + 2 * SUBLANES, w), F32),
                        pltpu.VMEM((tm, w), F32),
                        pltpu.VMEM((tm, w), F32),
                        pltpu.VMEM((tm, w), F32),
                        pltpu.VMEM((SUBLANES, w), F32)],
        compiler_params=_cparams(("arbitrary",)),
        name="recurrent_branch",
    )(uf, uf, conv_w, row(conv_b), w_r.astype(BF16), row(b_r), w_i.astype(BF16), row(b_i), row(lam))


def _merge_kernel(x_ref, o0_ref, o1_ref, o2_ref, o3_ref, wg_ref, bg_ref, wb_ref, wo_ref, w1_ref, w2_ref,
                  out_ref, wo_bf_ref, w1_bf_ref, w2_bf_ref):
    wo_bf_ref[...] = wo_ref[...].astype(BF16)
    w1_bf_ref[...] = w1_ref[...].astype(BF16)
    w2_bf_ref[...] = w2_ref[...].astype(BF16)
    x = x_ref[...]
    merged = None
    for g, o_ref in enumerate((o0_ref, o1_ref, o2_ref, o3_ref)):
        gate = jax.nn.sigmoid(jnp.dot(x, wg_ref[g].astype(BF16), preferred_element_type=F32)
                              + bg_ref[g:g + 1, :])
        term = gate * jnp.dot(o_ref[...], wb_ref[g].astype(BF16), preferred_element_type=F32)
        merged = term if merged is None else merged + term
    out_ref[...] = merged.astype(out_ref.dtype)


def _merge(xb, branches, wg_all, bg_all, wb_all, layer, later_weights, tm=1024, tn=256):
    s, d = xb.shape
    w = BRANCH_WIDTH
    ni, nj = s // tm, d // tn
    o_spec = pl.BlockSpec((tm, w), lambda i, j: (i, 0))
    cast_in, cast_out, cast_shapes = [], [], []
    for arr in later_weights:
        _, r, c = arr.shape
        cast_in.append(pl.BlockSpec((None, r // ni, c // nj), lambda i, j: (layer, i, j)))
        cast_out.append(pl.BlockSpec((r // ni, c // nj), lambda i, j: (i, j)))
        cast_shapes.append(jax.ShapeDtypeStruct((r, c), BF16))
    return pl.pallas_call(
        _merge_kernel,
        grid=(ni, nj),
        in_specs=[pl.BlockSpec((tm, d), lambda i, j: (i, 0)),
                  o_spec, o_spec, o_spec, o_spec,
                  pl.BlockSpec((None, N_BRANCH, d, tn), lambda i, j: (layer, 0, 0, j)),
                  pl.BlockSpec((None, N_BRANCH, tn), lambda i, j: (layer, 0, j)),
                  pl.BlockSpec((None, N_BRANCH, w, tn), lambda i, j: (layer, 0, 0, j))] + cast_in,
        out_specs=[pl.BlockSpec((tm, tn), lambda i, j: (i, j))] + cast_out,
        out_shape=[jax.ShapeDtypeStruct((s, d), BF16)] + cast_shapes,
        compiler_params=_cparams(("parallel", "arbitrary")),
        name="gated_merge",
    )(xb, *branches, wg_all, bg_all, wb_all, *later_weights)


def _outproj_kernel(m_ref, w_ref, x_ref, g_ref, b_ref, of_ref, ob_ref):
    half = m_ref.shape[0] // 2
    rows = (slice(0, half), slice(half, 2 * half))
    proj = [jnp.dot(m_ref[r, :], w_ref[...], preferred_element_type=F32) for r in rows]
    for r, p in zip(rows, proj):
        y = _layer_norm_rows(ALPHA * x_ref[r, :] + p, g_ref[...], b_ref[...])
        of_ref[r, :] = y
        ob_ref[r, :] = y.astype(BF16)


def _outproj_ln(merged, w_out, x, g, b, tm=512):
    s, d = x.shape
    row_spec = pl.BlockSpec((tm, d), lambda i: (i, 0))
    vec_spec = pl.BlockSpec((1, d), lambda i: (0, 0))
    return pl.pallas_call(
        _outproj_kernel,
        grid=(s // tm,),
        in_specs=[row_spec, pl.BlockSpec((d, d), lambda i: (0, 0)), row_spec, vec_spec, vec_spec],
        out_specs=[row_spec, row_spec],
        out_shape=[jax.ShapeDtypeStruct((s, d), F32), jax.ShapeDtypeStruct((s, d), BF16)],
        compiler_params=_cparams(("parallel",)),
        name="outproj_ln",
    )(merged, w_out, x, g.reshape(1, d), b.reshape(1, d))


def _ffn_kernel(xb_ref, xf_ref, w1_ref, w2_ref, g_ref, b_ref, of_ref, ob_ref, acc_sc):
    f = pl.program_id(1)

    @pl.when(f == 0)
    def _init():
        acc_sc[...] = jnp.zeros(acc_sc.shape, F32)

    hid = jnp.maximum(jnp.dot(xb_ref[...], w1_ref[...], preferred_element_type=F32), 0.0)
    hid = (hid * hid).astype(BF16)
    acc_sc[...] += jnp.dot(hid, w2_ref[...], preferred_element_type=F32)

    @pl.when(f == pl.num_programs(1) - 1)
    def _finish():
        y = _layer_norm_rows(ALPHA * xf_ref[...] + acc_sc[...], g_ref[...], b_ref[...])
        of_ref[...] = y
        ob_ref[...] = y.astype(BF16)


def _ffn_ln(xb, xf, w1, w2, g, b, tm=512, tf=1024):
    s, d = xf.shape
    dff = w1.shape[1]
    row_spec = pl.BlockSpec((tm, d), lambda i, f: (i, 0))
    vec_spec = pl.BlockSpec((1, d), lambda i, f: (0, 0))
    return pl.pallas_call(
        _ffn_kernel,
        grid=(s // tm, dff // tf),
        in_specs=[row_spec, row_spec,
                  pl.BlockSpec((d, tf), lambda i, f: (0, f)),
                  pl.BlockSpec((tf, d), lambda i, f: (f, 0)),
                  vec_spec, vec_spec],
        out_specs=[row_spec, row_spec],
        out_shape=[jax.ShapeDtypeStruct((s, d), F32), jax.ShapeDtypeStruct((s, d), BF16)],
        scratch_shapes=[pltpu.VMEM((tm, d), F32)],
        compiler_params=_cparams(("parallel", "arbitrary")),
        name="ffn_ln",
    )(xb, xf, w1, w2, g.reshape(1, d), b.reshape(1, d))


def _split_in_proj(w_in):
    qkv = jnp.concatenate([w_in[:, :, _OFF_FQ:_OFF_FF], w_in[:, :, _OFF_SQ:_OFF_END]], axis=2)
    pad = jnp.zeros(w_in.shape[:2] + (LANES - N_HEADS,), w_in.dtype)
    rest = jnp.concatenate([w_in[:, :, _OFF_RX:_OFF_SQ], w_in[:, :, _OFF_FF:_OFF_RX], pad], axis=2)
    return qkv.astype(BF16), rest.astype(BF16)


def kernel(x, ln_in_g, ln_in_b, w_in, b_forget, conv_w, conv_b, w_r, b_r, w_i, b_i, lru_lambda,
           rel_bias, w_branch, w_gate, b_gate, w_out, ln1_g, ln1_b, w_ff1, w_ff2, ln2_g, ln2_b):
    batch, s, d = x.shape
    assert (batch, s, d) == (1, SEQ, D_MODEL)
    w = BRANCH_WIDTH
    band_tq = 256

    col_scale = np.ones((1, 9 * w), np.float32)
    for q_block in (0, 3, 6):
        col_scale[:, q_block * w:(q_block + 1) * w] = QK_SCALE * LOG2_E
    qkv_scale = jnp.asarray(col_scale)
    rest_scale = jnp.ones((1, 2 * w + LANES), F32)

    w_qkv, w_rest = _split_in_proj(w_in)

    xf, xb = _entry_ln(x.reshape(s, d), ln_in_g, ln_in_b)
    for l in range(DEPTH):
        qkv = _project(xb, w_qkv, l, qkv_scale, BF16, 1024, 1536, "in_proj_qkv")
        uf = _project(xb, w_rest, l, rest_scale, F32, 1024, 1152, "in_proj_rest")

        f_rows = uf[:, 2 * w:2 * w + N_HEADS].T.reshape(N_HEADS * (s // LANES), LANES)
        b_rows = jnp.repeat(b_forget[l].astype(F32), s // LANES).reshape(-1, 1)
        cf = _forget_cumsum(f_rows, b_rows).reshape(N_HEADS, s)
        cfk = jnp.pad(cf, ((0, SUBLANES - N_HEADS), (0, 0)))

        o_fox = _fox_attention(qkv, cfk, 0)
        o_sb = _sb_attention(qkv, 3)
        o_ch = _band_attention(qkv, _band_bias_vector(rel_bias[l], band_tq), 6, band_tq)
        o_lru = _recurrent_branch(uf, conv_w[l], conv_b[l], w_r[l], b_r[l], w_i[l], b_i[l],
                                  lru_lambda[l])

        merged, wo_b, w1_b, w2_b = _merge(xb, (o_fox, o_lru, o_sb, o_ch), w_gate, b_gate, w_branch, l,
                                          (w_out, w_ff1, w_ff2))
        xf, xb = _outproj_ln(merged, wo_b, xf, ln1_g[l], ln1_b[l])
        xf, xb = _ffn_ln(xb, xf, w1_b, w2_b, ln2_g[l], ln2_b[l])
    return xf.reshape(batch, s, d)
```

```python
import functools
import math

import jax
import jax.numpy as jnp
import numpy as np
from jax import lax
from jax.experimental import pallas as pl
from jax.experimental.pallas import tpu as pltpu

F32 = jnp.float32
BF16 = jnp.bfloat16

D_MODEL = 2048
SEQ = 8192
DEPTH = 2
CHUNK = 64
HEAD_DIM = 128
N_BRANCH = 4
BRANCH_WIDTH = D_MODEL // N_BRANCH
N_HEADS = BRANCH_WIDTH // HEAD_DIM
CONV_WIDTH = 4
LRU_C = 8.0
LOOKBACK_CHUNKS = 8
REL_CLIP = 256
D_FF = 4 * D_MODEL
ALPHA = (2.0 * DEPTH) ** 0.25
LN_EPS = 1e-5
QK_SCALE = HEAD_DIM ** -0.5
LOG2_E = math.log2(math.e)

_OFF_FQ = 0
_OFF_FK = _OFF_FQ + BRANCH_WIDTH
_OFF_FV = _OFF_FK + BRANCH_WIDTH
_OFF_FF = _OFF_FV + BRANCH_WIDTH
_OFF_RX = _OFF_FF + N_HEADS
_OFF_RY = _OFF_RX + BRANCH_WIDTH
_OFF_SQ = _OFF_RY + BRANCH_WIDTH
_OFF_CQ = _OFF_SQ + 3 * BRANCH_WIDTH
_OFF_END = _OFF_CQ + 3 * BRANCH_WIDTH

LANES = 128
SUBLANES = 8
NEG_BIG = -1e30
SB_DEAD_LOG2 = -180.0
FOX_DEAD_LOG2 = -170.0

VMEM_LIMIT = 56 * 1024 * 1024


def _cparams(sem, vmem=VMEM_LIMIT):
    return pltpu.CompilerParams(dimension_semantics=sem, vmem_limit_bytes=vmem)


def _log_sigmoid(x):
    return jnp.minimum(x, 0.0) - jnp.log1p(jnp.exp(-jnp.abs(x)))


def _layer_norm_rows(y, g, b):
    mu = jnp.mean(y, axis=-1, keepdims=True)
    d = y - mu
    var = jnp.mean(d * d, axis=-1, keepdims=True)
    return d * lax.rsqrt(var + LN_EPS) * g + b


def _ln_kernel(x_ref, g_ref, b_ref, of_ref, ob_ref):
    y = _layer_norm_rows(x_ref[...], g_ref[...], b_ref[...])
    of_ref[...] = y
    ob_ref[...] = y.astype(BF16)


def _entry_ln(x, g, b, tm=512):
    s, d = x.shape
    return pl.pallas_call(
        _ln_kernel,
        grid=(s // tm,),
        in_specs=[pl.BlockSpec((tm, d), lambda i: (i, 0)),
                  pl.BlockSpec((1, d), lambda i: (0, 0)),
                  pl.BlockSpec((1, d), lambda i: (0, 0))],
        out_specs=[pl.BlockSpec((tm, d), lambda i: (i, 0)),
                   pl.BlockSpec((tm, d), lambda i: (i, 0))],
        out_shape=[jax.ShapeDtypeStruct((s, d), F32),
                   jax.ShapeDtypeStruct((s, d), BF16)],
        compiler_params=_cparams(("parallel",)),
        name="entry_ln",
    )(x, g.reshape(1, d), b.reshape(1, d))


def _proj_kernel(x_ref, w_ref, s_ref, o_ref):
    acc = jnp.dot(x_ref[...], w_ref[...], preferred_element_type=F32)
    o_ref[...] = (acc * s_ref[...]).astype(o_ref.dtype)


def _project(xb, w_all, layer, colscale, out_dtype, tm, tn, name):
    m, k = xb.shape
    n = w_all.shape[2]
    return pl.pallas_call(
        _proj_kernel,
        grid=(m // tm, n // tn),
        in_specs=[pl.BlockSpec((tm, k), lambda i, j: (i, 0)),
                  pl.BlockSpec((None, k, tn), lambda i, j: (layer, 0, j)),
                  pl.BlockSpec((1, tn), lambda i, j: (0, j))],
        out_specs=pl.BlockSpec((tm, tn), lambda i, j: (i, j)),
        out_shape=jax.ShapeDtypeStruct((m, n), out_dtype),
        compiler_params=_cparams(("parallel", "arbitrary")),
        name=name,
    )(xb, w_all, colscale)


def _forget_cumsum_kernel(f_ref, b_ref, o_ref):
    rows = f_ref.shape[0]
    per_head = rows // N_HEADS
    ls = _log_sigmoid(f_ref[...] + b_ref[...])
    r = lax.broadcasted_iota(jnp.int32, (LANES, LANES), 0)
    c = lax.broadcasted_iota(jnp.int32, (LANES, LANES), 1)
    upper = (r <= c).astype(F32)
    within = jnp.dot(ls, upper, preferred_element_type=F32,
                     precision=lax.Precision.HIGHEST)
    total = within[:, LANES - 1:LANES]
    rr = lax.broadcasted_iota(jnp.int32, (rows, rows), 0)
    cc = lax.broadcasted_iota(jnp.int32, (rows, rows), 1)
    head_start = rr - (rr & (per_head - 1))
    before = ((cc >= head_start) & (cc < rr)).astype(F32)
    offs = jnp.dot(before, jnp.broadcast_to(total, (rows, LANES)),
                   preferred_element_type=F32, precision=lax.Precision.HIGHEST)
    o_ref[...] = (within + offs) * LOG2_E


def _forget_cumsum(f_rows, b_rows):
    rows = f_rows.shape[0]
    return pl.pallas_call(
        _forget_cumsum_kernel,
        out_shape=jax.ShapeDtypeStruct((rows, LANES), F32),
        name="forget_cumsum",
    )(f_rows, b_rows)


def _fox_kernel(q_ref, k_ref, v_ref, cfk_ref, o_ref, m_sc, acc_sc, vaug_sc, qn_sc, kn_sc, *, tq, tk):
    i = pl.program_id(0)
    heads = [slice(h * HEAD_DIM, (h + 1) * HEAD_DIM) for h in range(N_HEADS)]
    norm_rows = 1024

    @pl.when(i == 0)
    def _largest_key_norm():
        for h, hs in enumerate(heads):
            def chunk(c, best, hs=hs):
                rows = k_ref[pl.ds(pl.multiple_of(c * norm_rows, norm_rows), norm_rows), hs].astype(F32)
                return jnp.maximum(best, jnp.max(jnp.sum(rows * rows, axis=-1, keepdims=True)))
            best = lax.fori_loop(0, k_ref.shape[0] // norm_rows, chunk, jnp.zeros((SUBLANES, LANES), F32))
            kn_sc[h] = jnp.sqrt(best)

    m_sc[...] = jnp.full(m_sc.shape, NEG_BIG, F32)
    acc_sc[...] = jnp.zeros(acc_sc.shape, F32)
    vaug_sc[:, :, HEAD_DIM:] = jnp.ones((N_HEADS, tk, HEAD_DIM), BF16)
    for h, hs in enumerate(heads):
        q = q_ref[:, hs].astype(F32)
        qn_sc[h] = jnp.broadcast_to(jnp.sqrt(jnp.sum(q * q, axis=-1, keepdims=True)), (tq, LANES))
    rep = tk // LANES

    def key_tile(j, masked, hlist):
        k0 = pl.multiple_of(j * tk, tk)
        if masked:
            keep = (lax.broadcasted_iota(jnp.int32, (tq, tk), 1)
                    <= lax.broadcasted_iota(jnp.int32, (tq, tk), 0))
        scores = [lax.dot_general(q_ref[:, heads[h]], k_ref[pl.ds(k0, tk), heads[h]],
                                  (((1,), (1,)), ((), ())), preferred_element_type=F32) for h in hlist]
        probs, alphas = [], []
        for h, s in zip(hlist, scores):
            s = s - cfk_ref[h:h + 1, pl.ds(k0, tk)]
            if masked:
                s = jnp.where(keep, s, NEG_BIG)
            m_old = m_sc[h]
            m_new = jnp.maximum(m_old, jnp.max(s, axis=-1, keepdims=True))
            alphas.append(jnp.exp2(m_old - m_new))
            probs.append(jnp.exp2(s - jnp.concatenate([m_new] * rep, axis=1)).astype(BF16))
            m_sc[h] = m_new
            vaug_sc[h, :, :HEAD_DIM] = v_ref[pl.ds(k0, tk), heads[h]]
        for h, p, alpha in zip(hlist, probs, alphas):
            pv = jnp.dot(p, vaug_sc[h], preferred_element_type=F32)
            acc_sc[h] = jnp.concatenate([alpha, alpha], axis=1) * acc_sc[h] + pv

    def alive(j, hlist):
        newest = pl.multiple_of(jnp.maximum(j, 0) * tk + tk - LANES, LANES)
        reach = None
        for h in hlist:
            decay = -cfk_ref[h:h + 1, pl.ds(newest, LANES)][:, LANES - 1:]
            bound = jnp.max(qn_sc[h] * kn_sc[h, 0:1, :] + decay - m_sc[h], axis=0, keepdims=True)
            reach = bound if reach is None else jnp.minimum(reach, bound)
        return (jnp.max(reach) > FOX_DEAD_LOG2).astype(jnp.int32)

    def walk_back(j_start, hlist):
        def earlier(state):
            j, _ = state
            key_tile(j, False, hlist)
            return j - 1, alive(j - 1, hlist)
        return lax.while_loop(lambda st: jnp.logical_and(st[0] >= 0, st[1] > 0), earlier,
                              (j_start, alive(j_start, hlist)))[0]

    all_heads = list(range(N_HEADS))
    key_tile(i, True, all_heads)
    j_split = walk_back(i - 1, all_heads)
    for h in all_heads:
        walk_back(j_split, [h])
    for h in range(N_HEADS):
        hs = slice(h * HEAD_DIM, (h + 1) * HEAD_DIM)
        o_ref[:, hs] = (acc_sc[h, :, :HEAD_DIM] / acc_sc[h, :, HEAD_DIM:]).astype(o_ref.dtype)


def _fox_attention(qkv, cfk, col0, tq=512):
    s = qkv.shape[0]
    tk = tq
    w = BRANCH_WIDTH
    kern = functools.partial(_fox_kernel, tq=tq, tk=tk)
    resident = pl.Buffered(1)
    return pl.pallas_call(
        kern,
        grid=(s // tq,),
        in_specs=[pl.BlockSpec((tq, w), lambda i: (i, col0)),
                  pl.BlockSpec((s, w), lambda i: (0, col0 + 1), pipeline_mode=resident),
                  pl.BlockSpec((s, w), lambda i: (0, col0 + 2), pipeline_mode=resident),
                  pl.BlockSpec((SUBLANES, s), lambda i: (0, 0), pipeline_mode=resident)],
        out_specs=pl.BlockSpec((tq, w), lambda i: (i, 0)),
        out_shape=jax.ShapeDtypeStruct((s, w), BF16),
        scratch_shapes=[pltpu.VMEM((N_HEADS, tq, LANES), F32),
                        pltpu.VMEM((N_HEADS, tq, 2 * HEAD_DIM), F32),
                        pltpu.VMEM((N_HEADS, tk, 2 * HEAD_DIM), BF16),
                        pltpu.VMEM((N_HEADS, tq, LANES), F32),
                        pltpu.VMEM((N_HEADS, SUBLANES, LANES), F32)],
        compiler_params=_cparams(("arbitrary",)),
        name="fox_attention",
    )(qkv, qkv, qkv, cfk)


def _sb_kernel(q_ref, k_ref, v_ref, o_ref, run_sc, acc_sc, *, tq):
    i = pl.program_id(0)
    nsub = tq // LANES
    r = lax.broadcasted_iota(jnp.int32, (2 * LANES, 2 * LANES), 0) & (LANES - 1)
    c = lax.broadcasted_iota(jnp.int32, (2 * LANES, 2 * LANES), 1)
    tri_aug = ((c >= LANES) | (r > c)).astype(BF16)

    def sub_blocks(items, masked):
        scores = []
        for h, k0, r0 in items:
            hs = slice(h * HEAD_DIM, (h + 1) * HEAD_DIM)
            scores.append(lax.dot_general(q_ref[r0:, hs], k_ref[pl.ds(k0, LANES), hs],
                                          (((1,), (1,)), ((), ())), preferred_element_type=F32))
        log_beta, sums, keeps = [], [], []
        for (h, k0, r0), z in zip(items, scores):
            rows = tq - r0
            lp = jnp.minimum(z, 0.0) - jnp.log2(1.0 + jnp.exp2(-jnp.abs(z)))
            ln = lp - z
            keep = None
            if masked:
                keep = (lax.broadcasted_iota(jnp.int32, (rows, LANES), 1)
                        < lax.broadcasted_iota(jnp.int32, (rows, LANES), 0))
                ln = jnp.where(keep, ln, 0.0)
            ln_hi = ln.astype(BF16)
            ln_lo = (ln - ln_hi.astype(F32)).astype(BF16)
            log_beta.append(lp)
            keeps.append(keep)
            sums.append(jnp.dot(jnp.concatenate([ln_hi, ln_lo], axis=1), tri_aug,
                                preferred_element_type=F32))
        weights = []
        for (h, k0, r0), lp, la, keep in zip(items, log_beta, sums, keeps):
            run = run_sc[h, r0:, :]
            a = jnp.exp2(lp + la[:, :LANES] + run)
            if masked:
                a = jnp.where(keep, a, 0.0)
            run_sc[h, r0:, :] = run + la[:, LANES:]
            weights.append(a.astype(BF16))
        for (h, k0, r0), a in zip(items, weights):
            hs = slice(h * HEAD_DIM, (h + 1) * HEAD_DIM)
            acc_sc[h, r0:, :] += jnp.dot(a, v_ref[pl.ds(k0, LANES), hs], preferred_element_type=F32)

    run_sc[...] = jnp.zeros(run_sc.shape, F32)
    acc_sc[...] = jnp.zeros(acc_sc.shape, F32)
    for cc in range(nsub - 1, -1, -1):
        k0 = pl.multiple_of(i * tq + cc * LANES, LANES)
        sub_blocks([(h, k0, cc * LANES) for h in range(N_HEADS)], True)

    def alive():
        return (jnp.max(run_sc[...]) > SB_DEAD_LOG2).astype(jnp.int32)

    def earlier(state):
        cb, _ = state
        sub_blocks([(h, pl.multiple_of((cb - back) * LANES, LANES), 0)
                    for back in range(2) for h in range(N_HEADS)], False)
        return cb - 2, alive()

    lax.while_loop(lambda st: jnp.logical_and(st[0] >= 0, st[1] > 0), earlier,
                   (i * nsub - 1, alive()))
    for h in range(N_HEADS):
        o_ref[:, h * HEAD_DIM:(h + 1) * HEAD_DIM] = acc_sc[h].astype(o_ref.dtype)


def _sb_attention(qkv, col0, tq=512):
    s = qkv.shape[0]
    w = BRANCH_WIDTH
    assert (tq // LANES) % 2 == 0
    kern = functools.partial(_sb_kernel, tq=tq)
    resident = pl.Buffered(1)
    return pl.pallas_call(
        kern,
        grid=(s // tq,),
        in_specs=[pl.BlockSpec((tq, w), lambda i: (i, col0)),
                  pl.BlockSpec((s, w), lambda i: (0, col0 + 1), pipeline_mode=resident),
                  pl.BlockSpec((s, w), lambda i: (0, col0 + 2), pipeline_mode=resident)],
        out_specs=pl.BlockSpec((tq, w), lambda i: (i, 0)),
        out_shape=jax.ShapeDtypeStruct((s, w), BF16),
        scratch_shapes=[pltpu.VMEM((N_HEADS, tq, LANES), F32),
                        pltpu.VMEM((N_HEADS, tq, HEAD_DIM), F32)],
        compiler_params=_cparams(("arbitrary",)),
        name="sb_attention",
    )(qkv, qkv, qkv)


def _band_kernel(q_ref, k2_ref, k1_ref, k0_ref, v2_ref, v1_ref, v0_ref, ext_ref, o_ref,
                 bias_sc, vaug_sc, *, tq):
    i = pl.program_id(0)
    width = 4 * tq

    @pl.when(i == 0)
    def _build_tables():
        trow = lax.broadcasted_iota(jnp.int32, (tq, width), 0)
        t = lax.broadcasted_iota(jnp.int32, (tq, 3 * tq), 0)
        s = lax.broadcasted_iota(jnp.int32, (tq, 3 * tq), 1)
        shift = CHUNK.bit_length() - 1
        t_chunk = t >> shift
        s_chunk = (s >> shift) - (2 * tq) // CHUNK
        in_band = (t_chunk - s_chunk <= LOOKBACK_CHUNKS) & (s_chunk <= t_chunk)
        for h in range(N_HEADS):
            x = jnp.broadcast_to(ext_ref[h:h + 1, :], (tq, width))
            for b in range(tq.bit_length() - 1):
                x = jnp.where(((trow >> b) & 1) == 1, pltpu.roll(x, 1 << b, axis=1), x)
            bias_sc[h] = jnp.where(in_band, x[:, :3 * tq] * LOG2_E, NEG_BIG)
        vaug_sc[:, :, HEAD_DIM:] = jnp.ones((N_HEADS, 3 * tq, HEAD_DIM), BF16)

    k_refs = (k2_ref, k1_ref, k0_ref)
    v_refs = (v2_ref, v1_ref, v0_ref)
    heads = [slice(h * HEAD_DIM, (h + 1) * HEAD_DIM) for h in range(N_HEADS)]

    def tile(first_tiles):
        scores = [[lax.dot_general(q_ref[:, hs], k_refs[p][:, hs], (((1,), (1,)), ((), ())),
                                   preferred_element_type=F32) for p in range(3)] for hs in heads]
        probs = []
        for h, pieces in enumerate(scores):
            pieces = [s + bias_sc[h, :, p * tq:(p + 1) * tq] for p, s in enumerate(pieces)]
            if first_tiles:
                pieces = [jnp.where(i - 2 + p >= 0, s, NEG_BIG) for p, s in enumerate(pieces)]
            m = jnp.max(jnp.maximum(jnp.maximum(pieces[0], pieces[1]), pieces[2]),
                        axis=-1, keepdims=True)
            probs.append(jnp.concatenate([jnp.exp2(s - m).astype(BF16) for s in pieces], axis=1))
            for p in range(3):
                vaug_sc[h, p * tq:(p + 1) * tq, :HEAD_DIM] = v_refs[p][:, heads[h]]
        for h, p in enumerate(probs):
            acc = jnp.dot(p, vaug_sc[h], preferred_element_type=F32)
            o_ref[:, heads[h]] = (acc[:, :HEAD_DIM] / acc[:, HEAD_DIM:]).astype(o_ref.dtype)

    @pl.when(i < 2)
    def _first_tiles():
        tile(True)

    @pl.when(i >= 2)
    def _other_tiles():
        tile(False)


def _band_bias_vector(rel_bias_l, tq):
    n = np.arange(4 * tq)
    dist = np.where(n < 3 * tq, 2 * tq - n, 6 * tq - n)
    ridx = np.clip(dist, -(CHUNK - 1), REL_CLIP) + (CHUNK - 1)
    return rel_bias_l.astype(F32)[:, ridx]


def _band_attention(qkv, bias_ext, col0, tq=256):
    s = qkv.shape[0]
    w = BRANCH_WIDTH
    assert 2 * tq >= LOOKBACK_CHUNKS * CHUNK and tq % CHUNK == 0 and tq & (tq - 1) == 0
    kern = functools.partial(_band_kernel, tq=tq)

    def kv_spec(back, col):
        return pl.BlockSpec((tq, w), lambda i: (jnp.maximum(i - back, 0), col))

    return pl.pallas_call(
        kern,
        grid=(s // tq,),
        in_specs=[pl.BlockSpec((tq, w), lambda i: (i, col0)),
                  kv_spec(2, col0 + 1), kv_spec(1, col0 + 1), kv_spec(0, col0 + 1),
                  kv_spec(2, col0 + 2), kv_spec(1, col0 + 2), kv_spec(0, col0 + 2),
                  pl.BlockSpec((N_HEADS, 4 * tq), lambda i: (0, 0))],
        out_specs=pl.BlockSpec((tq, w), lambda i: (i, 0)),
        out_shape=jax.ShapeDtypeStruct((s, w), BF16),
        scratch_shapes=[pltpu.VMEM((N_HEADS, tq, 3 * tq), F32),
                        pltpu.VMEM((N_HEADS, 3 * tq, 2 * HEAD_DIM), BF16)],
        compiler_params=_cparams(("arbitrary",)),
        name="band_attention",
    )(qkv, qkv, qkv, qkv, qkv, qkv, qkv, bias_ext)


def _gelu_tanh(x):
    c = math.sqrt(2.0 / math.pi)
    return 0.5 * x * (1.0 + jnp.tanh(c * (x + 0.044715 * (x * x * x))))


def _lru_begin_tile(first, xext_sc, carry_sc, *, tm):
    halo = SUBLANES

    @pl.when(first)
    def _first():
        xext_sc[0:halo, :] = jnp.zeros((halo, BRANCH_WIDTH), F32)
        carry_sc[...] = jnp.zeros(carry_sc.shape, F32)

    @pl.when(jnp.logical_not(first))
    def _shift_halo():
        xext_sc[0:halo, :] = xext_sc[tm:tm + halo, :]


def _lru_gates(rx_ref, cw_ref, cb_ref, wr_ref, wi_ref, xext_sc, *, tm):
    halo = SUBLANES
    xext_sc[halo:halo + tm, :] = rx_ref[...]
    xext = xext_sc[0:halo + tm, :]
    xc = cb_ref[...] + rx_ref[...] * cw_ref[CONV_WIDTH - 1:CONV_WIDTH, :]
    for back in range(1, CONV_WIDTH):
        t = CONV_WIDTH - 1 - back
        xc = xc + pltpu.roll(xext, back, axis=0)[halo:, :] * cw_ref[t:t + 1, :]
    xcb = xc.astype(BF16)
    r_parts, i_parts = [], []
    for n in range(N_HEADS):
        ns = slice(n * HEAD_DIM, (n + 1) * HEAD_DIM)
        r_parts.append(jnp.dot(xcb[:, ns], wr_ref[n], preferred_element_type=F32))
        i_parts.append(jnp.dot(xcb[:, ns], wi_ref[n], preferred_element_type=F32))
    return xc, jnp.concatenate(r_parts, axis=1), jnp.concatenate(i_parts, axis=1)


def _lru_coefficients(xc, r_pre, i_pre, br_ref, bi_ref, lam_ref, a_sc, b_sc):
    r = jax.nn.sigmoid(r_pre + br_ref[...])
    gi = jax.nn.sigmoid(i_pre + bi_ref[...])
    log_a = LRU_C * r * _log_sigmoid(lam_ref[...])
    a = jnp.exp(log_a)
    a_sc[...] = a
    b_sc[...] = jnp.sqrt(-jnp.tanh(log_a) * (a * a + 1.0)) * (gi * xc)


def _lru_recurrence(groups, a_sc, b_sc, h_sc, carry_sc):
    w = BRANCH_WIDTH
    row = lax.broadcasted_iota(jnp.int32, (SUBLANES, w), 0)
    carry = carry_sc[...]
    for g in groups:
        rows = slice(g * SUBLANES, (g + 1) * SUBLANES)
        a = a_sc[rows, :]
        b = b_sc[rows, :]
        for k in (1, 2, 4):
            a_prev = pltpu.roll(a, k, axis=0)
            b_prev = pltpu.roll(b, k, axis=0)
            ok = row >= k
            b = jnp.where(ok, a * b_prev + b, b)
            a = jnp.where(ok, a * a_prev, a)
        hgrp = a * carry + b
        h_sc[rows, :] = hgrp
        carry = jnp.broadcast_to(hgrp[SUBLANES - 1:SUBLANES, :], (SUBLANES, w))
    carry_sc[...] = carry


def _proj_lru_kernel(x_ref, w_ref, s_ref, rx_ref, ry_ref, cw_ref, cb_ref, wr_ref, br_ref, wi_ref,
                     bi_ref, lam_ref, qkv_ref, olru_ref, xext_sc, a_sc, b_sc, h_sc, carry_sc, *, tm_lru):
    first = jnp.logical_and(pl.program_id(0) == 0, pl.program_id(1) == 0)
    tn = qkv_ref.shape[1]
    bounds = [0, tn // 3] + [tn // 3 + (k + 1) * (2 * tn // 9) for k in range(3)]
    groups = tm_lru // SUBLANES

    def project(k):
        cols = slice(bounds[k], bounds[k + 1])
        acc = jnp.dot(x_ref[...], w_ref[:, cols], preferred_element_type=F32)
        qkv_ref[:, cols] = (acc * s_ref[:, cols]).astype(qkv_ref.dtype)

    _lru_begin_tile(first, xext_sc, carry_sc, tm=tm_lru)
    project(0)
    xc, r_pre, i_pre = _lru_gates(rx_ref, cw_ref, cb_ref, wr_ref, wi_ref, xext_sc, tm=tm_lru)
    project(1)
    _lru_coefficients(xc, r_pre, i_pre, br_ref, bi_ref, lam_ref, a_sc, b_sc)
    project(2)
    _lru_recurrence(range(groups // 2), a_sc, b_sc, h_sc, carry_sc)
    project(3)
    _lru_recurrence(range(groups // 2, groups), a_sc, b_sc, h_sc, carry_sc)
    olru_ref[...] = (h_sc[...] * _gelu_tanh(ry_ref[...])).astype(olru_ref.dtype)


def _project_qkv_and_recur(xb, w_all, layer, colscale, uf, conv_w, conv_b, w_r, b_r, w_i, b_i, lam,
                           tm=1024, tn=2304, tm_lru=512):
    m, k = xb.shape
    n = w_all.shape[2]
    w = BRANCH_WIDTH
    nj = n // tn
    assert tm == nj * tm_lru
    kern = functools.partial(_proj_lru_kernel, tm_lru=tm_lru)
    row = lambda v: v.reshape(1, w)
    full2 = lambda shape: pl.BlockSpec(shape, lambda i, j: (0, 0))
    full3 = lambda shape: pl.BlockSpec(shape, lambda i, j: (0, 0, 0))
    return pl.pallas_call(
        kern,
        grid=(m // tm, nj),
        in_specs=[pl.BlockSpec((tm, k), lambda i, j: (i, 0)),
                  pl.BlockSpec((None, k, tn), lambda i, j: (layer, 0, j)),
                  pl.BlockSpec((1, tn), lambda i, j: (0, j)),
                  pl.BlockSpec((tm_lru, w), lambda i, j: (i * nj + j, 0)),
                  pl.BlockSpec((tm_lru, w), lambda i, j: (i * nj + j, 1)),
                  full2((CONV_WIDTH, w)), full2((1, w)),
                  full3((N_HEADS, HEAD_DIM, HEAD_DIM)), full2((1, w)),
                  full3((N_HEADS, HEAD_DIM, HEAD_DIM)), full2((1, w)),
                  full2((1, w))],
        out_specs=[pl.BlockSpec((tm, tn), lambda i, j: (i, j)),
                   pl.BlockSpec((tm_lru, w), lambda i, j: (i * nj + j, 0))],
        out_shape=[jax.ShapeDtypeStruct((m, n), BF16), jax.ShapeDtypeStruct((m, w), BF16)],
        scratch_shapes=[pltpu.VMEM((tm_lru + 2 * SUBLANES, w), F32),
                        pltpu.VMEM((tm_lru, w), F32),
                        pltpu.VMEM((tm_lru, w), F32),
                        pltpu.VMEM((tm_lru, w), F32),
                        pltpu.VMEM((SUBLANES, w), F32)],
        compiler_params=_cparams(("arbitrary", "arbitrary")),
        name="in_proj_qkv_recurrent",
    )(xb, w_all, colscale, uf, uf, conv_w, row(conv_b), w_r.astype(BF16), row(b_r),
      w_i.astype(BF16), row(b_i), row(lam))


def _merge_kernel(x_ref, o0_ref, o1_ref, o2_ref, o3_ref, wg_ref, bg_ref, wb_ref, wo_ref, w1_ref, w2_ref,
                  out_ref, wo_bf_ref, w1_bf_ref, w2_bf_ref):
    wo_bf_ref[...] = wo_ref[...].astype(BF16)
    w1_bf_ref[...] = w1_ref[...].astype(BF16)
    w2_bf_ref[...] = w2_ref[...].astype(BF16)
    x = x_ref[...]
    merged = None
    for g, o_ref in enumerate((o0_ref, o1_ref, o2_ref, o3_ref)):
        gate = jax.nn.sigmoid(jnp.dot(x, wg_ref[g].astype(BF16), preferred_element_type=F32)
                              + bg_ref[g:g + 1, :])
        term = gate * jnp.dot(o_ref[...], wb_ref[g].astype(BF16), preferred_element_type=F32)
        merged = term if merged is None else merged + term
    out_ref[...] = merged.astype(out_ref.dtype)


def _merge(xb, branches, wg_all, bg_all, wb_all, layer, later_weights, tm=1024, tn=256):
    s, d = xb.shape
    w = BRANCH_WIDTH
    ni, nj = s // tm, d // tn
    o_spec = pl.BlockSpec((tm, w), lambda i, j: (i, 0))
    cast_in, cast_out, cast_shapes = [], [], []
    for arr in later_weights:
        _, r, c = arr.shape
        cast_in.append(pl.BlockSpec((None, r // ni, c // nj), lambda i, j: (layer, i, j)))
        cast_out.append(pl.BlockSpec((r // ni, c // nj), lambda i, j: (i, j)))
        cast_shapes.append(jax.ShapeDtypeStruct((r, c), BF16))
    return pl.pallas_call(
        _merge_kernel,
        grid=(ni, nj),
        in_specs=[pl.BlockSpec((tm, d), lambda i, j: (i, 0)),
                  o_spec, o_spec, o_spec, o_spec,
                  pl.BlockSpec((None, N_BRANCH, d, tn), lambda i, j: (layer, 0, 0, j)),
                  pl.BlockSpec((None, N_BRANCH, tn), lambda i, j: (layer, 0, j)),
                  pl.BlockSpec((None, N_BRANCH, w, tn), lambda i, j: (layer, 0, 0, j))] + cast_in,
        out_specs=[pl.BlockSpec((tm, tn), lambda i, j: (i, j))] + cast_out,
        out_shape=[jax.ShapeDtypeStruct((s, d), BF16)] + cast_shapes,
        compiler_params=_cparams(("parallel", "arbitrary")),
        name="gated_merge",
    )(xb, *branches, wg_all, bg_all, wb_all, *later_weights)


def _outproj_kernel(m_ref, w_ref, x_ref, g_ref, b_ref, of_ref, ob_ref):
    half = m_ref.shape[0] // 2
    rows = (slice(0, half), slice(half, 2 * half))
    proj = [jnp.dot(m_ref[r, :], w_ref[...], preferred_element_type=F32) for r in rows]
    for r, p in zip(rows, proj):
        y = _layer_norm_rows(ALPHA * x_ref[r, :] + p, g_ref[...], b_ref[...])
        of_ref[r, :] = y
        ob_ref[r, :] = y.astype(BF16)


def _outproj_ln(merged, w_out, x, g, b, tm=512):
    s, d = x.shape
    row_spec = pl.BlockSpec((tm, d), lambda i: (i, 0))
    vec_spec = pl.BlockSpec((1, d), lambda i: (0, 0))
    return pl.pallas_call(
        _outproj_kernel,
        grid=(s // tm,),
        in_specs=[row_spec, pl.BlockSpec((d, d), lambda i: (0, 0)), row_spec, vec_spec, vec_spec],
        out_specs=[row_spec, row_spec],
        out_shape=[jax.ShapeDtypeStruct((s, d), F32), jax.ShapeDtypeStruct((s, d), BF16)],
        compiler_params=_cparams(("parallel",)),
        name="outproj_ln",
    )(merged, w_out, x, g.reshape(1, d), b.reshape(1, d))


def _ffn_kernel(xb_ref, xf_ref, w1_ref, w2_ref, g_ref, b_ref, of_ref, ob_ref, acc_sc):
    f = pl.program_id(1)

    @pl.when(f == 0)
    def _init():
        acc_sc[...] = jnp.zeros(acc_sc.shape, F32)

    hid = jnp.maximum(jnp.dot(xb_ref[...], w1_ref[...], preferred_element_type=F32), 0.0)
    hid = (hid * hid).astype(BF16)
    acc_sc[...] += jnp.dot(hid, w2_ref[...], preferred_element_type=F32)

    @pl.when(f == pl.num_programs(1) - 1)
    def _finish():
        y = _layer_norm_rows(ALPHA * xf_ref[...] + acc_sc[...], g_ref[...], b_ref[...])
        of_ref[...] = y
        ob_ref[...] = y.astype(BF16)


def _ffn_ln(xb, xf, w1, w2, g, b, tm=512, tf=1024):
    s, d = xf.shape
    dff = w1.shape[1]
    row_spec = pl.BlockSpec((tm, d), lambda i, f: (i, 0))
    vec_spec = pl.BlockSpec((1, d), lambda i, f: (0, 0))
    return pl.pallas_call(
        _ffn_kernel,
        grid=(s // tm, dff // tf),
        in_specs=[row_spec, row_spec,
                  pl.BlockSpec((d, tf), lambda i, f: (0, f)),
                  pl.BlockSpec((tf, d), lambda i, f: (f, 0)),
                  vec_spec, vec_spec],
        out_specs=[row_spec, row_spec],
        out_shape=[jax.ShapeDtypeStruct((s, d), F32), jax.ShapeDtypeStruct((s, d), BF16)],
        scratch_shapes=[pltpu.VMEM((tm, d), F32)],
        compiler_params=_cparams(("parallel", "arbitrary")),
        name="ffn_ln",
    )(xb, xf, w1, w2, g.reshape(1, d), b.reshape(1, d))


def _split_in_proj(w_in):
    qkv = jnp.concatenate([w_in[:, :, _OFF_FQ:_OFF_FF], w_in[:, :, _OFF_SQ:_OFF_END]], axis=2)
    pad = jnp.zeros(w_in.shape[:2] + (LANES - N_HEADS,), w_in.dtype)
    rest = jnp.concatenate([w_in[:, :, _OFF_RX:_OFF_SQ], w_in[:, :, _OFF_FF:_OFF_RX], pad], axis=2)
    return qkv.astype(BF16), rest.astype(BF16)


def kernel(x, ln_in_g, ln_in_b, w_in, b_forget, conv_w, conv_b, w_r, b_r, w_i, b_i, lru_lambda,
           rel_bias, w_branch, w_gate, b_gate, w_out, ln1_g, ln1_b, w_ff1, w_ff2, ln2_g, ln2_b):
    batch, s, d = x.shape
    assert (batch, s, d) == (1, SEQ, D_MODEL)
    w = BRANCH_WIDTH
    band_tq = 256

    col_scale = np.ones((1, 9 * w), np.float32)
    for q_block in (0, 3, 6):
        col_scale[:, q_block * w:(q_block + 1) * w] = QK_SCALE * LOG2_E
    qkv_scale = jnp.asarray(col_scale)
    rest_scale = jnp.ones((1, 2 * w + LANES), F32)

    w_qkv, w_rest = _split_in_proj(w_in)

    xf, xb = _entry_ln(x.reshape(s, d), ln_in_g, ln_in_b)
    for l in range(DEPTH):
        uf = _project(xb, w_rest, l, rest_scale, F32, 1024, 1152, "in_proj_rest")
        qkv, o_lru = _project_qkv_and_recur(xb, w_qkv, l, qkv_scale, uf, conv_w[l], conv_b[l], w_r[l],
                                            b_r[l], w_i[l], b_i[l], lru_lambda[l])

        f_rows = uf[:, 2 * w:2 * w + N_HEADS].T.reshape(N_HEADS * (s // LANES), LANES)
        b_rows = jnp.repeat(b_forget[l].astype(F32), s // LANES).reshape(-1, 1)
        cf = _forget_cumsum(f_rows, b_rows).reshape(N_HEADS, s)
        cfk = jnp.pad(cf, ((0, SUBLANES - N_HEADS), (0, 0)))

        o_fox = _fox_attention(qkv, cfk, 0)
        o_sb = _sb_attention(qkv, 3)
        o_ch = _band_attention(qkv, _band_bias_vector(rel_bias[l], band_tq), 6, band_tq)

        merged, wo_b, w1_b, w2_b = _merge(xb, (o_fox, o_lru, o_sb, o_ch), w_gate, b_gate, w_branch, l,
                                          (w_out, w_ff1, w_ff2))
        xf, xb = _outproj_ln(merged, wo_b, xf, ln1_g[l], ln1_b[l])
        xf, xb = _ffn_ln(xb, xf, w1_b, w2_b, ln2_g[l], ln2_b[l])
    return xf.reshape(batch, s, d)
```

```python
import functools
import math

import jax
import jax.numpy as jnp
import numpy as np
from jax import lax
from jax.experimental import pallas as pl
from jax.experimental.pallas import tpu as pltpu

F32 = jnp.float32
BF16 = jnp.bfloat16

D_MODEL = 2048
SEQ = 8192
DEPTH = 2
CHUNK = 64
HEAD_DIM = 128
N_BRANCH = 4
BRANCH_WIDTH = D_MODEL // N_BRANCH
N_HEADS = BRANCH_WIDTH // HEAD_DIM
CONV_WIDTH = 4
LRU_C = 8.0
LOOKBACK_CHUNKS = 8
REL_CLIP = 256
D_FF = 4 * D_MODEL
ALPHA = (2.0 * DEPTH) ** 0.25
LN_EPS = 1e-5
QK_SCALE = HEAD_DIM ** -0.5
LOG2_E = math.log2(math.e)

_OFF_FQ = 0
_OFF_FK = _OFF_FQ + BRANCH_WIDTH
_OFF_FV = _OFF_FK + BRANCH_WIDTH
_OFF_FF = _OFF_FV + BRANCH_WIDTH
_OFF_RX = _OFF_FF + N_HEADS
_OFF_RY = _OFF_RX + BRANCH_WIDTH
_OFF_SQ = _OFF_RY + BRANCH_WIDTH
_OFF_CQ = _OFF_SQ + 3 * BRANCH_WIDTH
_OFF_END = _OFF_CQ + 3 * BRANCH_WIDTH

LANES = 128
SUBLANES = 8
NEG_BIG = -1e30
SB_DEAD_LOG2 = -180.0
FOX_DEAD_LOG2 = -170.0

VMEM_LIMIT = 56 * 1024 * 1024


def _cparams(sem, vmem=VMEM_LIMIT):
    return pltpu.CompilerParams(dimension_semantics=sem, vmem_limit_bytes=vmem)


def _log_sigmoid(x):
    return jnp.minimum(x, 0.0) - jnp.log1p(jnp.exp(-jnp.abs(x)))


def _layer_norm_rows(y, g, b):
    mu = jnp.mean(y, axis=-1, keepdims=True)
    d = y - mu
    var = jnp.mean(d * d, axis=-1, keepdims=True)
    return d * lax.rsqrt(var + LN_EPS) * g + b


def _ln_kernel(x_ref, g_ref, b_ref, of_ref, ob_ref):
    y = _layer_norm_rows(x_ref[...], g_ref[...], b_ref[...])
    of_ref[...] = y
    ob_ref[...] = y.astype(BF16)


def _entry_ln(x, g, b, tm=512):
    s, d = x.shape
    return pl.pallas_call(
        _ln_kernel,
        grid=(s // tm,),
        in_specs=[pl.BlockSpec((tm, d), lambda i: (i, 0)),
                  pl.BlockSpec((1, d), lambda i: (0, 0)),
                  pl.BlockSpec((1, d), lambda i: (0, 0))],
        out_specs=[pl.BlockSpec((tm, d), lambda i: (i, 0)),
                   pl.BlockSpec((tm, d), lambda i: (i, 0))],
        out_shape=[jax.ShapeDtypeStruct((s, d), F32),
                   jax.ShapeDtypeStruct((s, d), BF16)],
        compiler_params=_cparams(("parallel",)),
        name="entry_ln",
    )(x, g.reshape(1, d), b.reshape(1, d))


def _proj_kernel(x_ref, w_ref, s_ref, o_ref):
    acc = jnp.dot(x_ref[...], w_ref[...], preferred_element_type=F32)
    o_ref[...] = (acc * s_ref[...]).astype(o_ref.dtype)


def _project(xb, w_all, layer, colscale, out_dtype, tm, tn, name):
    m, k = xb.shape
    n = w_all.shape[2]
    return pl.pallas_call(
        _proj_kernel,
        grid=(m // tm, n // tn),
        in_specs=[pl.BlockSpec((tm, k), lambda i, j: (i, 0)),
                  pl.BlockSpec((None, k, tn), lambda i, j: (layer, 0, j)),
                  pl.BlockSpec((1, tn), lambda i, j: (0, j))],
        out_specs=pl.BlockSpec((tm, tn), lambda i, j: (i, j)),
        out_shape=jax.ShapeDtypeStruct((m, n), out_dtype),
        compiler_params=_cparams(("parallel", "arbitrary")),
        name=name,
    )(xb, w_all, colscale)


def _forget_cumsum_kernel(f_ref, b_ref, o_ref):
    rows = f_ref.shape[0]
    per_head = rows // N_HEADS
    ls = _log_sigmoid(f_ref[...] + b_ref[...])
    r = lax.broadcasted_iota(jnp.int32, (LANES, LANES), 0)
    c = lax.broadcasted_iota(jnp.int32, (LANES, LANES), 1)
    upper = (r <= c).astype(F32)
    within = jnp.dot(ls, upper, preferred_element_type=F32,
                     precision=lax.Precision.HIGHEST)
    total = within[:, LANES - 1:LANES]
    rr = lax.broadcasted_iota(jnp.int32, (rows, rows), 0)
    cc = lax.broadcasted_iota(jnp.int32, (rows, rows), 1)
    head_start = rr - (rr & (per_head - 1))
    before = ((cc >= head_start) & (cc < rr)).astype(F32)
    offs = jnp.dot(before, jnp.broadcast_to(total, (rows, LANES)),
                   preferred_element_type=F32, precision=lax.Precision.HIGHEST)
    o_ref[...] = (within + offs) * LOG2_E


def _forget_cumsum(f_rows, b_rows):
    rows = f_rows.shape[0]
    return pl.pallas_call(
        _forget_cumsum_kernel,
        out_shape=jax.ShapeDtypeStruct((rows, LANES), F32),
        name="forget_cumsum",
    )(f_rows, b_rows)


def _fox_kernel(q_ref, k_ref, v_ref, cfk_ref, wg_ref, wb_ref, o_ref, wg_bf_ref, wb_bf_ref,
                m_sc, acc_sc, vaug_sc, qn_sc, kn_sc, *, tq, tk):
    i = pl.program_id(0)
    wg_bf_ref[...] = wg_ref[...].astype(BF16)
    wb_bf_ref[...] = wb_ref[...].astype(BF16)
    heads = [slice(h * HEAD_DIM, (h + 1) * HEAD_DIM) for h in range(N_HEADS)]
    norm_rows = 1024

    @pl.when(i == 0)
    def _largest_key_norm():
        for h, hs in enumerate(heads):
            def chunk(c, best, hs=hs):
                rows = k_ref[pl.ds(pl.multiple_of(c * norm_rows, norm_rows), norm_rows), hs].astype(F32)
                return jnp.maximum(best, jnp.max(jnp.sum(rows * rows, axis=-1, keepdims=True)))
            best = lax.fori_loop(0, k_ref.shape[0] // norm_rows, chunk, jnp.zeros((SUBLANES, LANES), F32))
            kn_sc[h] = jnp.sqrt(best)

    m_sc[...] = jnp.full(m_sc.shape, NEG_BIG, F32)
    acc_sc[...] = jnp.zeros(acc_sc.shape, F32)
    vaug_sc[:, :, HEAD_DIM:] = jnp.ones((N_HEADS, tk, HEAD_DIM), BF16)
    for h, hs in enumerate(heads):
        q = q_ref[:, hs].astype(F32)
        qn_sc[h] = jnp.broadcast_to(jnp.sqrt(jnp.sum(q * q, axis=-1, keepdims=True)), (tq, LANES))
    rep = tk // LANES

    def key_tile(j, masked, hlist):
        k0 = pl.multiple_of(j * tk, tk)
        if masked:
            keep = (lax.broadcasted_iota(jnp.int32, (tq, tk), 1)
                    <= lax.broadcasted_iota(jnp.int32, (tq, tk), 0))
        scores = [lax.dot_general(q_ref[:, heads[h]], k_ref[pl.ds(k0, tk), heads[h]],
                                  (((1,), (1,)), ((), ())), preferred_element_type=F32) for h in hlist]
        probs, alphas = [], []
        for h, s in zip(hlist, scores):
            s = s - cfk_ref[h:h + 1, pl.ds(k0, tk)]
            if masked:
                s = jnp.where(keep, s, NEG_BIG)
            m_old = m_sc[h]
            m_new = jnp.maximum(m_old, jnp.max(s, axis=-1, keepdims=True))
            alphas.append(jnp.exp2(m_old - m_new))
            probs.append(jnp.exp2(s - jnp.concatenate([m_new] * rep, axis=1)).astype(BF16))
            m_sc[h] = m_new
            vaug_sc[h, :, :HEAD_DIM] = v_ref[pl.ds(k0, tk), heads[h]]
        for h, p, alpha in zip(hlist, probs, alphas):
            pv = jnp.dot(p, vaug_sc[h], preferred_element_type=F32)
            acc_sc[h] = jnp.concatenate([alpha, alpha], axis=1) * acc_sc[h] + pv

    def alive(j, hlist):
        newest = pl.multiple_of(jnp.maximum(j, 0) * tk + tk - LANES, LANES)
        reach = None
        for h in hlist:
            decay = -cfk_ref[h:h + 1, pl.ds(newest, LANES)][:, LANES - 1:]
            bound = jnp.max(qn_sc[h] * kn_sc[h, 0:1, :] + decay - m_sc[h], axis=0, keepdims=True)
            reach = bound if reach is None else jnp.minimum(reach, bound)
        return (jnp.max(reach) > FOX_DEAD_LOG2).astype(jnp.int32)

    def walk_back(j_start, hlist):
        def earlier(state):
            j, _ = state
            key_tile(j, False, hlist)
            return j - 1, alive(j - 1, hlist)
        return lax.while_loop(lambda st: jnp.logical_and(st[0] >= 0, st[1] > 0), earlier,
                              (j_start, alive(j_start, hlist)))[0]

    all_heads = list(range(N_HEADS))
    key_tile(i, True, all_heads)
    j_split = walk_back(i - 1, all_heads)
    for h in all_heads:
        walk_back(j_split, [h])
    for h in range(N_HEADS):
        hs = slice(h * HEAD_DIM, (h + 1) * HEAD_DIM)
        o_ref[:, hs] = (acc_sc[h, :, :HEAD_DIM] / acc_sc[h, :, HEAD_DIM:]).astype(o_ref.dtype)


def _fox_attention(qkv, cfk, col0, layer, later_weights, tq=512):
    s = qkv.shape[0]
    tk = tq
    w = BRANCH_WIDTH
    steps = s // tq
    kern = functools.partial(_fox_kernel, tq=tq, tk=tk)
    resident = pl.Buffered(1)
    cast_in, cast_out, cast_shapes = [], [], []
    for arr in later_weights:
        _, r, c = arr.shape
        cast_in.append(pl.BlockSpec((None, r // steps, c), lambda i: (layer, i, 0)))
        cast_out.append(pl.BlockSpec((r // steps, c), lambda i: (i, 0)))
        cast_shapes.append(jax.ShapeDtypeStruct((r, c), BF16))
    return pl.pallas_call(
        kern,
        grid=(steps,),
        in_specs=[pl.BlockSpec((tq, w), lambda i: (i, col0)),
                  pl.BlockSpec((s, w), lambda i: (0, col0 + 1), pipeline_mode=resident),
                  pl.BlockSpec((s, w), lambda i: (0, col0 + 2), pipeline_mode=resident),
                  pl.BlockSpec((SUBLANES, s), lambda i: (0, 0), pipeline_mode=resident)] + cast_in,
        out_specs=[pl.BlockSpec((tq, w), lambda i: (i, 0))] + cast_out,
        out_shape=[jax.ShapeDtypeStruct((s, w), BF16)] + cast_shapes,
        scratch_shapes=[pltpu.VMEM((N_HEADS, tq, LANES), F32),
                        pltpu.VMEM((N_HEADS, tq, 2 * HEAD_DIM), F32),
                        pltpu.VMEM((N_HEADS, tk, 2 * HEAD_DIM), BF16),
                        pltpu.VMEM((N_HEADS, tq, LANES), F32),
                        pltpu.VMEM((N_HEADS, SUBLANES, LANES), F32)],
        compiler_params=_cparams(("arbitrary",)),
        name="fox_attention",
    )(qkv, qkv, qkv, cfk, *later_weights)


def _sb_kernel(q_ref, k_ref, v_ref, o_ref, run_sc, acc_sc, *, tq):
    i = pl.program_id(0)
    nsub = tq // LANES
    r = lax.broadcasted_iota(jnp.int32, (2 * LANES, 2 * LANES), 0) & (LANES - 1)
    c = lax.broadcasted_iota(jnp.int32, (2 * LANES, 2 * LANES), 1)
    tri_aug = ((c >= LANES) | (r > c)).astype(BF16)

    def sub_blocks(items, masked):
        scores = []
        for h, k0, r0 in items:
            hs = slice(h * HEAD_DIM, (h + 1) * HEAD_DIM)
            scores.append(lax.dot_general(q_ref[r0:, hs], k_ref[pl.ds(k0, LANES), hs],
                                          (((1,), (1,)), ((), ())), preferred_element_type=F32))
        log_beta, sums, keeps = [], [], []
        for (h, k0, r0), z in zip(items, scores):
            rows = tq - r0
            lp = jnp.minimum(z, 0.0) - jnp.log2(1.0 + jnp.exp2(-jnp.abs(z)))
            ln = lp - z
            keep = None
            if masked:
                keep = (lax.broadcasted_iota(jnp.int32, (rows, LANES), 1)
                        < lax.broadcasted_iota(jnp.int32, (rows, LANES), 0))
                ln = jnp.where(keep, ln, 0.0)
            ln_hi = ln.astype(BF16)
            ln_lo = (ln - ln_hi.astype(F32)).astype(BF16)
            log_beta.append(lp)
            keeps.append(keep)
            sums.append(jnp.dot(jnp.concatenate([ln_hi, ln_lo], axis=1), tri_aug,
                                preferred_element_type=F32))
        weights = []
        for (h, k0, r0), lp, la, keep in zip(items, log_beta, sums, keeps):
            run = run_sc[h, r0:, :]
            a = jnp.exp2(lp + la[:, :LANES] + run)
            if masked:
                a = jnp.where(keep, a, 0.0)
            run_sc[h, r0:, :] = run + la[:, LANES:]
            weights.append(a.astype(BF16))
        for (h, k0, r0), a in zip(items, weights):
            hs = slice(h * HEAD_DIM, (h + 1) * HEAD_DIM)
            acc_sc[h, r0:, :] += jnp.dot(a, v_ref[pl.ds(k0, LANES), hs], preferred_element_type=F32)

    run_sc[...] = jnp.zeros(run_sc.shape, F32)
    acc_sc[...] = jnp.zeros(acc_sc.shape, F32)
    for cc in range(nsub - 1, -1, -1):
        k0 = pl.multiple_of(i * tq + cc * LANES, LANES)
        sub_blocks([(h, k0, cc * LANES) for h in range(N_HEADS)], True)

    def alive():
        return (jnp.max(run_sc[...]) > SB_DEAD_LOG2).astype(jnp.int32)

    def earlier(state):
        cb, _ = state
        sub_blocks([(h, pl.multiple_of((cb - back) * LANES, LANES), 0)
                    for back in range(2) for h in range(N_HEADS)], False)
        return cb - 2, alive()

    lax.while_loop(lambda st: jnp.logical_and(st[0] >= 0, st[1] > 0), earlier,
                   (i * nsub - 1, alive()))
    for h in range(N_HEADS):
        o_ref[:, h * HEAD_DIM:(h + 1) * HEAD_DIM] = acc_sc[h].astype(o_ref.dtype)


def _sb_attention(qkv, col0, tq=512):
    s = qkv.shape[0]
    w = BRANCH_WIDTH
    assert (tq // LANES) % 2 == 0
    kern = functools.partial(_sb_kernel, tq=tq)
    resident = pl.Buffered(1)
    return pl.pallas_call(
        kern,
        grid=(s // tq,),
        in_specs=[pl.BlockSpec((tq, w), lambda i: (i, col0)),
                  pl.BlockSpec((s, w), lambda i: (0, col0 + 1), pipeline_mode=resident),
                  pl.BlockSpec((s, w), lambda i: (0, col0 + 2), pipeline_mode=resident)],
        out_specs=pl.BlockSpec((tq, w), lambda i: (i, 0)),
        out_shape=jax.ShapeDtypeStruct((s, w), BF16),
        scratch_shapes=[pltpu.VMEM((N_HEADS, tq, LANES), F32),
                        pltpu.VMEM((N_HEADS, tq, HEAD_DIM), F32)],
        compiler_params=_cparams(("arbitrary",)),
        name="sb_attention",
    )(qkv, qkv, qkv)


def _band_kernel(q_ref, k2_ref, k1_ref, k0_ref, v2_ref, v1_ref, v0_ref, ext_ref, o_ref,
                 bias_sc, vaug_sc, *, tq):
    i = pl.program_id(0)
    width = 4 * tq

    @pl.when(i == 0)
    def _build_tables():
        trow = lax.broadcasted_iota(jnp.int32, (tq, width), 0)
        t = lax.broadcasted_iota(jnp.int32, (tq, 3 * tq), 0)
        s = lax.broadcasted_iota(jnp.int32, (tq, 3 * tq), 1)
        shift = CHUNK.bit_length() - 1
        t_chunk = t >> shift
        s_chunk = (s >> shift) - (2 * tq) // CHUNK
        in_band = (t_chunk - s_chunk <= LOOKBACK_CHUNKS) & (s_chunk <= t_chunk)
        for h in range(N_HEADS):
            x = jnp.broadcast_to(ext_ref[h:h + 1, :], (tq, width))
            for b in range(tq.bit_length() - 1):
                x = jnp.where(((trow >> b) & 1) == 1, pltpu.roll(x, 1 << b, axis=1), x)
            bias_sc[h] = jnp.where(in_band, x[:, :3 * tq] * LOG2_E, NEG_BIG)
        vaug_sc[:, :, HEAD_DIM:] = jnp.ones((N_HEADS, 3 * tq, HEAD_DIM), BF16)

    k_refs = (k2_ref, k1_ref, k0_ref)
    v_refs = (v2_ref, v1_ref, v0_ref)
    heads = [slice(h * HEAD_DIM, (h + 1) * HEAD_DIM) for h in range(N_HEADS)]

    def tile(first_tiles):
        scores = [[lax.dot_general(q_ref[:, hs], k_refs[p][:, hs], (((1,), (1,)), ((), ())),
                                   preferred_element_type=F32) for p in range(3)] for hs in heads]
        probs = []
        for h, pieces in enumerate(scores):
            pieces = [s + bias_sc[h, :, p * tq:(p + 1) * tq] for p, s in enumerate(pieces)]
            if first_tiles:
                pieces = [jnp.where(i - 2 + p >= 0, s, NEG_BIG) for p, s in enumerate(pieces)]
            m = jnp.max(jnp.maximum(jnp.maximum(pieces[0], pieces[1]), pieces[2]),
                        axis=-1, keepdims=True)
            probs.append(jnp.concatenate([jnp.exp2(s - m).astype(BF16) for s in pieces], axis=1))
            for p in range(3):
                vaug_sc[h, p * tq:(p + 1) * tq, :HEAD_DIM] = v_refs[p][:, heads[h]]
        for h, p in enumerate(probs):
            acc = jnp.dot(p, vaug_sc[h], preferred_element_type=F32)
            o_ref[:, heads[h]] = (acc[:, :HEAD_DIM] / acc[:, HEAD_DIM:]).astype(o_ref.dtype)

    @pl.when(i < 2)
    def _first_tiles():
        tile(True)

    @pl.when(i >= 2)
    def _other_tiles():
        tile(False)


def _band_bias_vector(rel_bias_l, tq):
    n = np.arange(4 * tq)
    dist = np.where(n < 3 * tq, 2 * tq - n, 6 * tq - n)
    ridx = np.clip(dist, -(CHUNK - 1), REL_CLIP) + (CHUNK - 1)
    return rel_bias_l.astype(F32)[:, ridx]


def _band_attention(qkv, bias_ext, col0, tq=256):
    s = qkv.shape[0]
    w = BRANCH_WIDTH
    assert 2 * tq >= LOOKBACK_CHUNKS * CHUNK and tq % CHUNK == 0 and tq & (tq - 1) == 0
    kern = functools.partial(_band_kernel, tq=tq)

    def kv_spec(back, col):
        return pl.BlockSpec((tq, w), lambda i: (jnp.maximum(i - back, 0), col))

    return pl.pallas_call(
        kern,
        grid=(s // tq,),
        in_specs=[pl.BlockSpec((tq, w), lambda i: (i, col0)),
                  kv_spec(2, col0 + 1), kv_spec(1, col0 + 1), kv_spec(0, col0 + 1),
                  kv_spec(2, col0 + 2), kv_spec(1, col0 + 2), kv_spec(0, col0 + 2),
                  pl.BlockSpec((N_HEADS, 4 * tq), lambda i: (0, 0))],
        out_specs=pl.BlockSpec((tq, w), lambda i: (i, 0)),
        out_shape=jax.ShapeDtypeStruct((s, w), BF16),
        scratch_shapes=[pltpu.VMEM((N_HEADS, tq, 3 * tq), F32),
                        pltpu.VMEM((N_HEADS, 3 * tq, 2 * HEAD_DIM), BF16)],
        compiler_params=_cparams(("arbitrary",)),
        name="band_attention",
    )(qkv, qkv, qkv, qkv, qkv, qkv, qkv, bias_ext)


def _gelu_tanh(x):
    c = math.sqrt(2.0 / math.pi)
    return 0.5 * x * (1.0 + jnp.tanh(c * (x + 0.044715 * (x * x * x))))


def _lru_begin_tile(first, xext_sc, carry_sc, *, tm):
    halo = SUBLANES

    @pl.when(first)
    def _first():
        xext_sc[0:halo, :] = jnp.zeros((halo, BRANCH_WIDTH), F32)
        carry_sc[...] = jnp.zeros(carry_sc.shape, F32)

    @pl.when(jnp.logical_not(first))
    def _shift_halo():
        xext_sc[0:halo, :] = xext_sc[tm:tm + halo, :]


def _lru_gates(rx_ref, cw_ref, cb_ref, wr_ref, wi_ref, xext_sc, *, tm):
    halo = SUBLANES
    xext_sc[halo:halo + tm, :] = rx_ref[...]
    xext = xext_sc[0:halo + tm, :]
    xc = cb_ref[...] + rx_ref[...] * cw_ref[CONV_WIDTH - 1:CONV_WIDTH, :]
    for back in range(1, CONV_WIDTH):
        t = CONV_WIDTH - 1 - back
        xc = xc + pltpu.roll(xext, back, axis=0)[halo:, :] * cw_ref[t:t + 1, :]
    xcb = xc.astype(BF16)
    r_parts, i_parts = [], []
    for n in range(N_HEADS):
        ns = slice(n * HEAD_DIM, (n + 1) * HEAD_DIM)
        r_parts.append(jnp.dot(xcb[:, ns], wr_ref[n], preferred_element_type=F32))
        i_parts.append(jnp.dot(xcb[:, ns], wi_ref[n], preferred_element_type=F32))
    return xc, jnp.concatenate(r_parts, axis=1), jnp.concatenate(i_parts, axis=1)


def _lru_coefficients(xc, r_pre, i_pre, br_ref, bi_ref, lam_ref, a_sc, b_sc):
    r = jax.nn.sigmoid(r_pre + br_ref[...])
    gi = jax.nn.sigmoid(i_pre + bi_ref[...])
    log_a = LRU_C * r * _log_sigmoid(lam_ref[...])
    a = jnp.exp(log_a)
    a_sc[...] = a
    b_sc[...] = jnp.sqrt(-jnp.tanh(log_a) * (a * a + 1.0)) * (gi * xc)


def _lru_recurrence(groups, a_sc, b_sc, h_sc, carry_sc):
    w = BRANCH_WIDTH
    row = lax.broadcasted_iota(jnp.int32, (SUBLANES, w), 0)
    carry = carry_sc[...]
    for g in groups:
        rows = slice(g * SUBLANES, (g + 1) * SUBLANES)
        a = a_sc[rows, :]
        b = b_sc[rows, :]
        for k in (1, 2, 4):
            a_prev = pltpu.roll(a, k, axis=0)
            b_prev = pltpu.roll(b, k, axis=0)
            ok = row >= k
            b = jnp.where(ok, a * b_prev + b, b)
            a = jnp.where(ok, a * a_prev, a)
        hgrp = a * carry + b
        h_sc[rows, :] = hgrp
        carry = jnp.broadcast_to(hgrp[SUBLANES - 1:SUBLANES, :], (SUBLANES, w))
    carry_sc[...] = carry


def _proj_lru_kernel(x_ref, w_ref, s_ref, rx_ref, ry_ref, cw_ref, cb_ref, wr_ref, br_ref, wi_ref,
                     bi_ref, lam_ref, qkv_ref, olru_ref, xext_sc, a_sc, b_sc, h_sc, carry_sc, *, tm_lru):
    first = jnp.logical_and(pl.program_id(0) == 0, pl.program_id(1) == 0)
    tn = qkv_ref.shape[1]
    bounds = [0, tn // 3] + [tn // 3 + (k + 1) * (2 * tn // 9) for k in range(3)]
    groups = tm_lru // SUBLANES

    def project(k):
        cols = slice(bounds[k], bounds[k + 1])
        acc = jnp.dot(x_ref[...], w_ref[:, cols], preferred_element_type=F32)
        qkv_ref[:, cols] = (acc * s_ref[:, cols]).astype(qkv_ref.dtype)

    _lru_begin_tile(first, xext_sc, carry_sc, tm=tm_lru)
    project(0)
    xc, r_pre, i_pre = _lru_gates(rx_ref, cw_ref, cb_ref, wr_ref, wi_ref, xext_sc, tm=tm_lru)
    project(1)
    _lru_coefficients(xc, r_pre, i_pre, br_ref, bi_ref, lam_ref, a_sc, b_sc)
    project(2)
    _lru_recurrence(range(groups // 2), a_sc, b_sc, h_sc, carry_sc)
    project(3)
    _lru_recurrence(range(groups // 2, groups), a_sc, b_sc, h_sc, carry_sc)
    olru_ref[...] = (h_sc[...] * _gelu_tanh(ry_ref[...])).astype(olru_ref.dtype)


def _project_qkv_and_recur(xb, w_all, layer, colscale, uf, conv_w, conv_b, w_r, b_r, w_i, b_i, lam,
                           tm=1024, tn=2304, tm_lru=512):
    m, k = xb.shape
    n = w_all.shape[2]
    w = BRANCH_WIDTH
    nj = n // tn
    assert tm == nj * tm_lru
    kern = functools.partial(_proj_lru_kernel, tm_lru=tm_lru)
    row = lambda v: v.reshape(1, w)
    full2 = lambda shape: pl.BlockSpec(shape, lambda i, j: (0, 0))
    full3 = lambda shape: pl.BlockSpec(shape, lambda i, j: (0, 0, 0))
    return pl.pallas_call(
        kern,
        grid=(m // tm, nj),
        in_specs=[pl.BlockSpec((tm, k), lambda i, j: (i, 0)),
                  pl.BlockSpec((None, k, tn), lambda i, j: (layer, 0, j)),
                  pl.BlockSpec((1, tn), lambda i, j: (0, j)),
                  pl.BlockSpec((tm_lru, w), lambda i, j: (i * nj + j, 0)),
                  pl.BlockSpec((tm_lru, w), lambda i, j: (i * nj + j, 1)),
                  full2((CONV_WIDTH, w)), full2((1, w)),
                  full3((N_HEADS, HEAD_DIM, HEAD_DIM)), full2((1, w)),
                  full3((N_HEADS, HEAD_DIM, HEAD_DIM)), full2((1, w)),
                  full2((1, w))],
        out_specs=[pl.BlockSpec((tm, tn), lambda i, j: (i, j)),
                   pl.BlockSpec((tm_lru, w), lambda i, j: (i * nj + j, 0))],
        out_shape=[jax.ShapeDtypeStruct((m, n), BF16), jax.ShapeDtypeStruct((m, w), BF16)],
        scratch_shapes=[pltpu.VMEM((tm_lru + 2 * SUBLANES, w), F32),
                        pltpu.VMEM((tm_lru, w), F32),
                        pltpu.VMEM((tm_lru, w), F32),
                        pltpu.VMEM((tm_lru, w), F32),
                        pltpu.VMEM((SUBLANES, w), F32)],
        compiler_params=_cparams(("arbitrary", "arbitrary")),
        name="in_proj_qkv_recurrent",
    )(xb, w_all, colscale, uf, uf, conv_w, row(conv_b), w_r.astype(BF16), row(b_r),
      w_i.astype(BF16), row(b_i), row(lam))


def _merge_kernel(x_ref, o0_ref, o1_ref, o2_ref, o3_ref, wg_ref, bg_ref, wb_ref, wo_ref, w1_ref, w2_ref,
                  out_ref, wo_bf_ref, w1_bf_ref, w2_bf_ref):
    wo_bf_ref[...] = wo_ref[...].astype(BF16)
    w1_bf_ref[...] = w1_ref[...].astype(BF16)
    w2_bf_ref[...] = w2_ref[...].astype(BF16)
    x = x_ref[...]
    merged = None
    for g, o_ref in enumerate((o0_ref, o1_ref, o2_ref, o3_ref)):
        gate = jax.nn.sigmoid(jnp.dot(x, wg_ref[g], preferred_element_type=F32) + bg_ref[g:g + 1, :])
        term = gate * jnp.dot(o_ref[...], wb_ref[g], preferred_element_type=F32)
        merged = term if merged is None else merged + term
    out_ref[...] = merged.astype(out_ref.dtype)


def _merge(xb, branches, wg, bg_all, wb, layer, later_weights, tm=1024, tn=256):
    s, d = xb.shape
    w = BRANCH_WIDTH
    ni, nj = s // tm, d // tn
    o_spec = pl.BlockSpec((tm, w), lambda i, j: (i, 0))
    cast_in, cast_out, cast_shapes = [], [], []
    for arr in later_weights:
        _, r, c = arr.shape
        cast_in.append(pl.BlockSpec((None, r // ni, c // nj), lambda i, j: (layer, i, j)))
        cast_out.append(pl.BlockSpec((r // ni, c // nj), lambda i, j: (i, j)))
        cast_shapes.append(jax.ShapeDtypeStruct((r, c), BF16))
    return pl.pallas_call(
        _merge_kernel,
        grid=(ni, nj),
        in_specs=[pl.BlockSpec((tm, d), lambda i, j: (i, 0)),
                  o_spec, o_spec, o_spec, o_spec,
                  pl.BlockSpec((N_BRANCH, d, tn), lambda i, j: (0, 0, j)),
                  pl.BlockSpec((None, N_BRANCH, tn), lambda i, j: (layer, 0, j)),
                  pl.BlockSpec((N_BRANCH, w, tn), lambda i, j: (0, 0, j))] + cast_in,
        out_specs=[pl.BlockSpec((tm, tn), lambda i, j: (i, j))] + cast_out,
        out_shape=[jax.ShapeDtypeStruct((s, d), BF16)] + cast_shapes,
        compiler_params=_cparams(("parallel", "arbitrary")),
        name="gated_merge",
    )(xb, *branches, wg, bg_all, wb, *later_weights)


def _outproj_kernel(m_ref, w_ref, x_ref, g_ref, b_ref, of_ref, ob_ref):
    half = m_ref.shape[0] // 2
    rows = (slice(0, half), slice(half, 2 * half))
    proj = [jnp.dot(m_ref[r, :], w_ref[...], preferred_element_type=F32) for r in rows]
    for r, p in zip(rows, proj):
        y = _layer_norm_rows(ALPHA * x_ref[r, :] + p, g_ref[...], b_ref[...])
        of_ref[r, :] = y
        ob_ref[r, :] = y.astype(BF16)


def _outproj_ln(merged, w_out, x, g, b, tm=512):
    s, d = x.shape
    row_spec = pl.BlockSpec((tm, d), lambda i: (i, 0))
    vec_spec = pl.BlockSpec((1, d), lambda i: (0, 0))
    return pl.pallas_call(
        _outproj_kernel,
        grid=(s // tm,),
        in_specs=[row_spec, pl.BlockSpec((d, d), lambda i: (0, 0)), row_spec, vec_spec, vec_spec],
        out_specs=[row_spec, row_spec],
        out_shape=[jax.ShapeDtypeStruct((s, d), F32), jax.ShapeDtypeStruct((s, d), BF16)],
        compiler_params=_cparams(("parallel",)),
        name="outproj_ln",
    )(merged, w_out, x, g.reshape(1, d), b.reshape(1, d))


def _ffn_kernel(xb_ref, xf_ref, w1_ref, w2_ref, g_ref, b_ref, of_ref, ob_ref, acc_sc):
    f = pl.program_id(1)

    @pl.when(f == 0)
    def _init():
        acc_sc[...] = jnp.zeros(acc_sc.shape, F32)

    hid = jnp.maximum(jnp.dot(xb_ref[...], w1_ref[...], preferred_element_type=F32), 0.0)
    hid = (hid * hid).astype(BF16)
    acc_sc[...] += jnp.dot(hid, w2_ref[...], preferred_element_type=F32)

    @pl.when(f == pl.num_programs(1) - 1)
    def _finish():
        y = _layer_norm_rows(ALPHA * xf_ref[...] + acc_sc[...], g_ref[...], b_ref[...])
        of_ref[...] = y
        ob_ref[...] = y.astype(BF16)


def _ffn_ln(xb, xf, w1, w2, g, b, tm=512, tf=1024):
    s, d = xf.shape
    dff = w1.shape[1]
    row_spec = pl.BlockSpec((tm, d), lambda i, f: (i, 0))
    vec_spec = pl.BlockSpec((1, d), lambda i, f: (0, 0))
    return pl.pallas_call(
        _ffn_kernel,
        grid=(s // tm, dff // tf),
        in_specs=[row_spec, row_spec,
                  pl.BlockSpec((d, tf), lambda i, f: (0, f)),
                  pl.BlockSpec((tf, d), lambda i, f: (f, 0)),
                  vec_spec, vec_spec],
        out_specs=[row_spec, row_spec],
        out_shape=[jax.ShapeDtypeStruct((s, d), F32), jax.ShapeDtypeStruct((s, d), BF16)],
        scratch_shapes=[pltpu.VMEM((tm, d), F32)],
        compiler_params=_cparams(("parallel", "arbitrary")),
        name="ffn_ln",
    )(xb, xf, w1, w2, g.reshape(1, d), b.reshape(1, d))


def _split_in_proj(w_in):
    qkv = jnp.concatenate([w_in[:, :, _OFF_FQ:_OFF_FF], w_in[:, :, _OFF_SQ:_OFF_END]], axis=2)
    pad = jnp.zeros(w_in.shape[:2] + (LANES - N_HEADS,), w_in.dtype)
    rest = jnp.concatenate([w_in[:, :, _OFF_RX:_OFF_SQ], w_in[:, :, _OFF_FF:_OFF_RX], pad], axis=2)
    return qkv.astype(BF16), rest.astype(BF16)


def kernel(x, ln_in_g, ln_in_b, w_in, b_forget, conv_w, conv_b, w_r, b_r, w_i, b_i, lru_lambda,
           rel_bias, w_branch, w_gate, b_gate, w_out, ln1_g, ln1_b, w_ff1, w_ff2, ln2_g, ln2_b):
    batch, s, d = x.shape
    assert (batch, s, d) == (1, SEQ, D_MODEL)
    w = BRANCH_WIDTH
    band_tq = 256

    col_scale = np.ones((1, 9 * w), np.float32)
    for q_block in (0, 3, 6):
        col_scale[:, q_block * w:(q_block + 1) * w] = QK_SCALE * LOG2_E
    qkv_scale = jnp.asarray(col_scale)
    rest_scale = jnp.ones((1, 2 * w + LANES), F32)

    w_qkv, w_rest = _split_in_proj(w_in)
    w_gate_rows = w_gate.reshape(DEPTH, N_BRANCH * d, d)
    w_branch_rows = w_branch.reshape(DEPTH, N_BRANCH * w, d)

    xf, xb = _entry_ln(x.reshape(s, d), ln_in_g, ln_in_b)
    for l in range(DEPTH):
        uf = _project(xb, w_rest, l, rest_scale, F32, 1024, 1152, "in_proj_rest")
        qkv, o_lru = _project_qkv_and_recur(xb, w_qkv, l, qkv_scale, uf, conv_w[l], conv_b[l], w_r[l],
                                            b_r[l], w_i[l], b_i[l], lru_lambda[l])

        f_rows = uf[:, 2 * w:2 * w + N_HEADS].T.reshape(N_HEADS * (s // LANES), LANES)
        b_rows = jnp.repeat(b_forget[l].astype(F32), s // LANES).reshape(-1, 1)
        cf = _forget_cumsum(f_rows, b_rows).reshape(N_HEADS, s)
        cfk = jnp.pad(cf, ((0, SUBLANES - N_HEADS), (0, 0)))

        o_fox, wg_b, wb_b = _fox_attention(qkv, cfk, 0, l, (w_gate_rows, w_branch_rows))
        o_sb = _sb_attention(qkv, 3)
        o_ch = _band_attention(qkv, _band_bias_vector(rel_bias[l], band_tq), 6, band_tq)

        merged, wo_b, w1_b, w2_b = _merge(xb, (o_fox, o_lru, o_sb, o_ch),
                                          wg_b.reshape(N_BRANCH, d, d), b_gate,
                                          wb_b.reshape(N_BRANCH, w, d), l, (w_out, w_ff1, w_ff2))
        xf, xb = _outproj_ln(merged, wo_b, xf, ln1_g[l], ln1_b[l])
        xf, xb = _ffn_ln(xb, xf, w1_b, w2_b, ln2_g[l], ln2_b[l])
    return xf.reshape(batch, s, d)
```

```python
import functools
import math

import jax
import jax.numpy as jnp
import numpy as np
from jax import lax
from jax.experimental import pallas as pl
from jax.experimental.pallas import tpu as pltpu

F32 = jnp.float32
BF16 = jnp.bfloat16

D_MODEL = 2048
SEQ = 8192
DEPTH = 2
CHUNK = 64
HEAD_DIM = 128
N_BRANCH = 4
BRANCH_WIDTH = D_MODEL // N_BRANCH
N_HEADS = BRANCH_WIDTH // HEAD_DIM
CONV_WIDTH = 4
LRU_C = 8.0
LOOKBACK_CHUNKS = 8
REL_CLIP = 256
D_FF = 4 * D_MODEL
ALPHA = (2.0 * DEPTH) ** 0.25
LN_EPS = 1e-5
QK_SCALE = HEAD_DIM ** -0.5
LOG2_E = math.log2(math.e)

_OFF_FQ = 0
_OFF_FK = _OFF_FQ + BRANCH_WIDTH
_OFF_FV = _OFF_FK + BRANCH_WIDTH
_OFF_FF = _OFF_FV + BRANCH_WIDTH
_OFF_RX = _OFF_FF + N_HEADS
_OFF_RY = _OFF_RX + BRANCH_WIDTH
_OFF_SQ = _OFF_RY + BRANCH_WIDTH
_OFF_CQ = _OFF_SQ + 3 * BRANCH_WIDTH
_OFF_END = _OFF_CQ + 3 * BRANCH_WIDTH

LANES = 128
SUBLANES = 8
NEG_BIG = -1e30
SB_DEAD_LOG2 = -180.0
FOX_DEAD_LOG2 = -170.0

VMEM_LIMIT = 56 * 1024 * 1024


def _cparams(sem, vmem=VMEM_LIMIT):
    return pltpu.CompilerParams(dimension_semantics=sem, vmem_limit_bytes=vmem)


def _log_sigmoid(x):
    return jnp.minimum(x, 0.0) - jnp.log1p(jnp.exp(-jnp.abs(x)))


def _layer_norm_rows(y, g, b):
    mu = jnp.mean(y, axis=-1, keepdims=True)
    d = y - mu
    var = jnp.mean(d * d, axis=-1, keepdims=True)
    return d * lax.rsqrt(var + LN_EPS) * g + b


def _ln_kernel(x_ref, g_ref, b_ref, of_ref, ob_ref):
    y = _layer_norm_rows(x_ref[...], g_ref[...], b_ref[...])
    of_ref[...] = y
    ob_ref[...] = y.astype(BF16)


def _entry_ln(x, g, b, tm=512):
    s, d = x.shape
    return pl.pallas_call(
        _ln_kernel,
        grid=(s // tm,),
        in_specs=[pl.BlockSpec((tm, d), lambda i: (i, 0)),
                  pl.BlockSpec((1, d), lambda i: (0, 0)),
                  pl.BlockSpec((1, d), lambda i: (0, 0))],
        out_specs=[pl.BlockSpec((tm, d), lambda i: (i, 0)),
                   pl.BlockSpec((tm, d), lambda i: (i, 0))],
        out_shape=[jax.ShapeDtypeStruct((s, d), F32),
                   jax.ShapeDtypeStruct((s, d), BF16)],
        compiler_params=_cparams(("parallel",)),
        name="entry_ln",
    )(x, g.reshape(1, d), b.reshape(1, d))


def _proj_kernel(x_ref, w_ref, s_ref, o_ref):
    acc = jnp.dot(x_ref[...], w_ref[...], preferred_element_type=F32)
    o_ref[...] = (acc * s_ref[...]).astype(o_ref.dtype)


def _project(xb, w_all, layer, colscale, out_dtype, tm, tn, name):
    m, k = xb.shape
    n = w_all.shape[2]
    return pl.pallas_call(
        _proj_kernel,
        grid=(m // tm, n // tn),
        in_specs=[pl.BlockSpec((tm, k), lambda i, j: (i, 0)),
                  pl.BlockSpec((None, k, tn), lambda i, j: (layer, 0, j)),
                  pl.BlockSpec((1, tn), lambda i, j: (0, j))],
        out_specs=pl.BlockSpec((tm, tn), lambda i, j: (i, j)),
        out_shape=jax.ShapeDtypeStruct((m, n), out_dtype),
        compiler_params=_cparams(("parallel", "arbitrary")),
        name=name,
    )(xb, w_all, colscale)


def _forget_cumsum_kernel(f_ref, b_ref, o_ref):
    rows = f_ref.shape[0]
    per_head = rows // N_HEADS
    ls = _log_sigmoid(f_ref[...] + b_ref[...])
    r = lax.broadcasted_iota(jnp.int32, (LANES, LANES), 0)
    c = lax.broadcasted_iota(jnp.int32, (LANES, LANES), 1)
    upper = (r <= c).astype(F32)
    within = jnp.dot(ls, upper, preferred_element_type=F32,
                     precision=lax.Precision.HIGHEST)
    total = within[:, LANES - 1:LANES]
    rr = lax.broadcasted_iota(jnp.int32, (rows, rows), 0)
    cc = lax.broadcasted_iota(jnp.int32, (rows, rows), 1)
    head_start = rr - (rr & (per_head - 1))
    before = ((cc >= head_start) & (cc < rr)).astype(F32)
    offs = jnp.dot(before, jnp.broadcast_to(total, (rows, LANES)),
                   preferred_element_type=F32, precision=lax.Precision.HIGHEST)
    o_ref[...] = (within + offs) * LOG2_E


def _forget_cumsum(f_rows, b_rows):
    rows = f_rows.shape[0]
    return pl.pallas_call(
        _forget_cumsum_kernel,
        out_shape=jax.ShapeDtypeStruct((rows, LANES), F32),
        name="forget_cumsum",
    )(f_rows, b_rows)


def _fox_kernel(q_ref, k_ref, v_ref, cfk_ref, wg_ref, wb_ref, o_ref, wg_bf_ref, wb_bf_ref,
                m_sc, acc_sc, vaug_sc, qn_sc, kn_sc, *, tq, tk):
    i = pl.program_id(0)
    wg_bf_ref[...] = wg_ref[...].astype(BF16)
    wb_bf_ref[...] = wb_ref[...].astype(BF16)
    heads = [slice(h * HEAD_DIM, (h + 1) * HEAD_DIM) for h in range(N_HEADS)]
    norm_rows = 1024

    @pl.when(i == 0)
    def _largest_key_norm():
        for h, hs in enumerate(heads):
            def chunk(c, best, hs=hs):
                rows = k_ref[pl.ds(pl.multiple_of(c * norm_rows, norm_rows), norm_rows), hs].astype(F32)
                return jnp.maximum(best, jnp.max(jnp.sum(rows * rows, axis=-1, keepdims=True)))
            best = lax.fori_loop(0, k_ref.shape[0] // norm_rows, chunk, jnp.zeros((SUBLANES, LANES), F32))
            kn_sc[h] = jnp.sqrt(best)

    m_sc[...] = jnp.full(m_sc.shape, NEG_BIG, F32)
    acc_sc[...] = jnp.zeros(acc_sc.shape, F32)
    vaug_sc[:, :, :, HEAD_DIM:] = jnp.ones((2, N_HEADS, tk, HEAD_DIM), BF16)
    for h, hs in enumerate(heads):
        q = q_ref[:, hs].astype(F32)
        qn_sc[h] = jnp.broadcast_to(jnp.sqrt(jnp.sum(q * q, axis=-1, keepdims=True)), (tq, LANES))
    rep = tk // LANES

    def key_tiles(items, masked):
        if masked:
            keep = (lax.broadcasted_iota(jnp.int32, (tq, tk), 1)
                    <= lax.broadcasted_iota(jnp.int32, (tq, tk), 0))
        starts = [pl.multiple_of(j * tk, tk) for j, _ in items]
        slots = [sum(1 for _, h2 in items[:n] if h2 == h) for n, (_, h) in enumerate(items)]
        scores = [lax.dot_general(q_ref[:, heads[h]], k_ref[pl.ds(k0, tk), heads[h]],
                                  (((1,), (1,)), ((), ())), preferred_element_type=F32)
                  for (_, h), k0 in zip(items, starts)]
        probs, alphas = [], []
        for (_, h), k0, slot, s in zip(items, starts, slots, scores):
            s = s - cfk_ref[h:h + 1, pl.ds(k0, tk)]
            if masked:
                s = jnp.where(keep, s, NEG_BIG)
            m_old = m_sc[h]
            m_new = jnp.maximum(m_old, jnp.max(s, axis=-1, keepdims=True))
            alphas.append(jnp.exp2(m_old - m_new))
            probs.append(jnp.exp2(s - jnp.concatenate([m_new] * rep, axis=1)).astype(BF16))
            m_sc[h] = m_new
            vaug_sc[slot, h, :, :HEAD_DIM] = v_ref[pl.ds(k0, tk), heads[h]]
        for (_, h), slot, p, alpha in zip(items, slots, probs, alphas):
            pv = jnp.dot(p, vaug_sc[slot, h], preferred_element_type=F32)
            acc_sc[h] = jnp.concatenate([alpha, alpha], axis=1) * acc_sc[h] + pv

    def alive(j, hlist):
        newest = pl.multiple_of(jnp.maximum(j, 0) * tk + tk - LANES, LANES)
        reach = None
        for h in hlist:
            decay = -cfk_ref[h:h + 1, pl.ds(newest, LANES)][:, LANES - 1:]
            bound = jnp.max(qn_sc[h] * kn_sc[h, 0:1, :] + decay - m_sc[h], axis=0, keepdims=True)
            reach = bound if reach is None else jnp.minimum(reach, bound)
        return (jnp.max(reach) > FOX_DEAD_LOG2).astype(jnp.int32)

    def walk_back(j_start, hlist, span):
        def earlier(state):
            j, _ = state
            key_tiles([(j - back, h) for back in range(span) for h in hlist], False)
            return j - span, alive(j - span, hlist)
        return lax.while_loop(lambda st: jnp.logical_and(st[0] >= span - 1, st[1] > 0), earlier,
                              (j_start, alive(j_start, hlist)))[0]

    all_heads = list(range(N_HEADS))
    key_tiles([(i, h) for h in all_heads], True)
    j_split = walk_back(i - 1, all_heads, 1)
    for h in all_heads:
        walk_back(walk_back(j_split, [h], 2), [h], 1)
    for h in range(N_HEADS):
        hs = slice(h * HEAD_DIM, (h + 1) * HEAD_DIM)
        o_ref[:, hs] = (acc_sc[h, :, :HEAD_DIM] / acc_sc[h, :, HEAD_DIM:]).astype(o_ref.dtype)


def _fox_attention(qkv, cfk, col0, layer, later_weights, tq=512):
    s = qkv.shape[0]
    tk = tq
    w = BRANCH_WIDTH
    steps = s // tq
    kern = functools.partial(_fox_kernel, tq=tq, tk=tk)
    resident = pl.Buffered(1)
    cast_in, cast_out, cast_shapes = [], [], []
    for arr in later_weights:
        _, r, c = arr.shape
        cast_in.append(pl.BlockSpec((None, r // steps, c), lambda i: (layer, i, 0)))
        cast_out.append(pl.BlockSpec((r // steps, c), lambda i: (i, 0)))
        cast_shapes.append(jax.ShapeDtypeStruct((r, c), BF16))
    return pl.pallas_call(
        kern,
        grid=(steps,),
        in_specs=[pl.BlockSpec((tq, w), lambda i: (i, col0)),
                  pl.BlockSpec((s, w), lambda i: (0, col0 + 1), pipeline_mode=resident),
                  pl.BlockSpec((s, w), lambda i: (0, col0 + 2), pipeline_mode=resident),
                  pl.BlockSpec((SUBLANES, s), lambda i: (0, 0), pipeline_mode=resident)] + cast_in,
        out_specs=[pl.BlockSpec((tq, w), lambda i: (i, 0))] + cast_out,
        out_shape=[jax.ShapeDtypeStruct((s, w), BF16)] + cast_shapes,
        scratch_shapes=[pltpu.VMEM((N_HEADS, tq, LANES), F32),
                        pltpu.VMEM((N_HEADS, tq, 2 * HEAD_DIM), F32),
                        pltpu.VMEM((2, N_HEADS, tk, 2 * HEAD_DIM), BF16),
                        pltpu.VMEM((N_HEADS, tq, LANES), F32),
                        pltpu.VMEM((N_HEADS, SUBLANES, LANES), F32)],
        compiler_params=_cparams(("arbitrary",)),
        name="fox_attention",
    )(qkv, qkv, qkv, cfk, *later_weights)


def _sb_kernel(q_ref, k_ref, v_ref, o_ref, run_sc, acc_sc, *, tq):
    i = pl.program_id(0)
    nsub = tq // LANES
    r = lax.broadcasted_iota(jnp.int32, (2 * LANES, 2 * LANES), 0) & (LANES - 1)
    c = lax.broadcasted_iota(jnp.int32, (2 * LANES, 2 * LANES), 1)
    tri_aug = ((c >= LANES) | (r > c)).astype(BF16)

    def sub_blocks(items, masked):
        scores = []
        for h, k0, r0 in items:
            hs = slice(h * HEAD_DIM, (h + 1) * HEAD_DIM)
            scores.append(lax.dot_general(q_ref[r0:, hs], k_ref[pl.ds(k0, LANES), hs],
                                          (((1,), (1,)), ((), ())), preferred_element_type=F32))
        log_beta, sums, keeps = [], [], []
        for (h, k0, r0), z in zip(items, scores):
            rows = tq - r0
            lp = jnp.minimum(z, 0.0) - jnp.log2(1.0 + jnp.exp2(-jnp.abs(z)))
            ln = lp - z
            keep = None
            if masked:
                keep = (lax.broadcasted_iota(jnp.int32, (rows, LANES), 1)
                        < lax.broadcasted_iota(jnp.int32, (rows, LANES), 0))
                ln = jnp.where(keep, ln, 0.0)
            ln_hi = ln.astype(BF16)
            ln_lo = (ln - ln_hi.astype(F32)).astype(BF16)
            log_beta.append(lp)
            keeps.append(keep)
            sums.append(jnp.dot(jnp.concatenate([ln_hi, ln_lo], axis=1), tri_aug,
                                preferred_element_type=F32))
        weights = []
        for (h, k0, r0), lp, la, keep in zip(items, log_beta, sums, keeps):
            run = run_sc[h, r0:, :]
            a = jnp.exp2(lp + la[:, :LANES] + run)
            if masked:
                a = jnp.where(keep, a, 0.0)
            run_sc[h, r0:, :] = run + la[:, LANES:]
            weights.append(a.astype(BF16))
        for (h, k0, r0), a in zip(items, weights):
            hs = slice(h * HEAD_DIM, (h + 1) * HEAD_DIM)
            acc_sc[h, r0:, :] += jnp.dot(a, v_ref[pl.ds(k0, LANES), hs], preferred_element_type=F32)

    run_sc[...] = jnp.zeros(run_sc.shape, F32)
    acc_sc[...] = jnp.zeros(acc_sc.shape, F32)
    for cc in range(nsub - 1, -1, -1):
        k0 = pl.multiple_of(i * tq + cc * LANES, LANES)
        sub_blocks([(h, k0, cc * LANES) for h in range(N_HEADS)], True)

    def alive():
        return (jnp.max(run_sc[...]) > SB_DEAD_LOG2).astype(jnp.int32)

    def earlier(state):
        cb, _ = state
        sub_blocks([(h, pl.multiple_of((cb - back) * LANES, LANES), 0)
                    for back in range(2) for h in range(N_HEADS)], False)
        return cb - 2, alive()

    lax.while_loop(lambda st: jnp.logical_and(st[0] >= 0, st[1] > 0), earlier,
                   (i * nsub - 1, alive()))
    for h in range(N_HEADS):
        o_ref[:, h * HEAD_DIM:(h + 1) * HEAD_DIM] = acc_sc[h].astype(o_ref.dtype)


def _sb_attention(qkv, col0, tq=512):
    s = qkv.shape[0]
    w = BRANCH_WIDTH
    assert (tq // LANES) % 2 == 0
    kern = functools.partial(_sb_kernel, tq=tq)
    resident = pl.Buffered(1)
    return pl.pallas_call(
        kern,
        grid=(s // tq,),
        in_specs=[pl.BlockSpec((tq, w), lambda i: (i, col0)),
                  pl.BlockSpec((s, w), lambda i: (0, col0 + 1), pipeline_mode=resident),
                  pl.BlockSpec((s, w), lambda i: (0, col0 + 2), pipeline_mode=resident)],
        out_specs=pl.BlockSpec((tq, w), lambda i: (i, 0)),
        out_shape=jax.ShapeDtypeStruct((s, w), BF16),
        scratch_shapes=[pltpu.VMEM((N_HEADS, tq, LANES), F32),
                        pltpu.VMEM((N_HEADS, tq, HEAD_DIM), F32)],
        compiler_params=_cparams(("arbitrary",)),
        name="sb_attention",
    )(qkv, qkv, qkv)


def _band_kernel(q_ref, k2_ref, k1_ref, k0_ref, v2_ref, v1_ref, v0_ref, ext_ref, o_ref,
                 bias_sc, vaug_sc, *, tq):
    i = pl.program_id(0)
    width = 4 * tq

    @pl.when(i == 0)
    def _build_tables():
        trow = lax.broadcasted_iota(jnp.int32, (tq, width), 0)
        t = lax.broadcasted_iota(jnp.int32, (tq, 3 * tq), 0)
        s = lax.broadcasted_iota(jnp.int32, (tq, 3 * tq), 1)
        shift = CHUNK.bit_length() - 1
        t_chunk = t >> shift
        s_chunk = (s >> shift) - (2 * tq) // CHUNK
        in_band = (t_chunk - s_chunk <= LOOKBACK_CHUNKS) & (s_chunk <= t_chunk)
        for h in range(N_HEADS):
            x = jnp.broadcast_to(ext_ref[h:h + 1, :], (tq, width))
            for b in range(tq.bit_length() - 1):
                x = jnp.where(((trow >> b) & 1) == 1, pltpu.roll(x, 1 << b, axis=1), x)
            bias_sc[h] = jnp.where(in_band, x[:, :3 * tq] * LOG2_E, NEG_BIG)
        vaug_sc[:, :, HEAD_DIM:] = jnp.ones((N_HEADS, 3 * tq, HEAD_DIM), BF16)

    k_refs = (k2_ref, k1_ref, k0_ref)
    v_refs = (v2_ref, v1_ref, v0_ref)
    heads = [slice(h * HEAD_DIM, (h + 1) * HEAD_DIM) for h in range(N_HEADS)]

    def tile(first_tiles):
        scores = [[lax.dot_general(q_ref[:, hs], k_refs[p][:, hs], (((1,), (1,)), ((), ())),
                                   preferred_element_type=F32) for p in range(3)] for hs in heads]
        probs = []
        for h, pieces in enumerate(scores):
            pieces = [s + bias_sc[h, :, p * tq:(p + 1) * tq] for p, s in enumerate(pieces)]
            if first_tiles:
                pieces = [jnp.where(i - 2 + p >= 0, s, NEG_BIG) for p, s in enumerate(pieces)]
            m = jnp.max(jnp.maximum(jnp.maximum(pieces[0], pieces[1]), pieces[2]),
                        axis=-1, keepdims=True)
            probs.append(jnp.concatenate([jnp.exp2(s - m).astype(BF16) for s in pieces], axis=1))
            for p in range(3):
                vaug_sc[h, p * tq:(p + 1) * tq, :HEAD_DIM] = v_refs[p][:, heads[h]]
        for h, p in enumerate(probs):
            acc = jnp.dot(p, vaug_sc[h], preferred_element_type=F32)
            o_ref[:, heads[h]] = (acc[:, :HEAD_DIM] / acc[:, HEAD_DIM:]).astype(o_ref.dtype)

    @pl.when(i < 2)
    def _first_tiles():
        tile(True)

    @pl.when(i >= 2)
    def _other_tiles():
        tile(False)


def _band_bias_vector(rel_bias_l, tq):
    n = np.arange(4 * tq)
    dist = np.where(n < 3 * tq, 2 * tq - n, 6 * tq - n)
    ridx = np.clip(dist, -(CHUNK - 1), REL_CLIP) + (CHUNK - 1)
    return rel_bias_l.astype(F32)[:, ridx]


def _band_attention(qkv, bias_ext, col0, tq=256):
    s = qkv.shape[0]
    w = BRANCH_WIDTH
    assert 2 * tq >= LOOKBACK_CHUNKS * CHUNK and tq % CHUNK == 0 and tq & (tq - 1) == 0
    kern = functools.partial(_band_kernel, tq=tq)

    def kv_spec(back, col):
        return pl.BlockSpec((tq, w), lambda i: (jnp.maximum(i - back, 0), col))

    return pl.pallas_call(
        kern,
        grid=(s // tq,),
        in_specs=[pl.BlockSpec((tq, w), lambda i: (i, col0)),
                  kv_spec(2, col0 + 1), kv_spec(1, col0 + 1), kv_spec(0, col0 + 1),
                  kv_spec(2, col0 + 2), kv_spec(1, col0 + 2), kv_spec(0, col0 + 2),
                  pl.BlockSpec((N_HEADS, 4 * tq), lambda i: (0, 0))],
        out_specs=pl.BlockSpec((tq, w), lambda i: (i, 0)),
        out_shape=jax.ShapeDtypeStruct((s, w), BF16),
        scratch_shapes=[pltpu.VMEM((N_HEADS, tq, 3 * tq), F32),
                        pltpu.VMEM((N_HEADS, 3 * tq, 2 * HEAD_DIM), BF16)],
        compiler_params=_cparams(("arbitrary",)),
        name="band_attention",
    )(qkv, qkv, qkv, qkv, qkv, qkv, qkv, bias_ext)


def _gelu_tanh(x):
    c = math.sqrt(2.0 / math.pi)
    return 0.5 * x * (1.0 + jnp.tanh(c * (x + 0.044715 * (x * x * x))))


def _lru_begin_tile(first, xext_sc, carry_sc, *, tm):
    halo = SUBLANES

    @pl.when(first)
    def _first():
        xext_sc[0:halo, :] = jnp.zeros((halo, BRANCH_WIDTH), F32)
        carry_sc[...] = jnp.zeros(carry_sc.shape, F32)

    @pl.when(jnp.logical_not(first))
    def _shift_halo():
        xext_sc[0:halo, :] = xext_sc[tm:tm + halo, :]


def _lru_gates(rx_ref, cw_ref, cb_ref, wr_ref, wi_ref, xext_sc, *, tm):
    halo = SUBLANES
    xext_sc[halo:halo + tm, :] = rx_ref[...]
    xext = xext_sc[0:halo + tm, :]
    xc = cb_ref[...] + rx_ref[...] * cw_ref[CONV_WIDTH - 1:CONV_WIDTH, :]
    for back in range(1, CONV_WIDTH):
        t = CONV_WIDTH - 1 - back
        xc = xc + pltpu.roll(xext, back, axis=0)[halo:, :] * cw_ref[t:t + 1, :]
    xcb = xc.astype(BF16)
    r_parts, i_parts = [], []
    for n in range(N_HEADS):
        ns = slice(n * HEAD_DIM, (n + 1) * HEAD_DIM)
        r_parts.append(jnp.dot(xcb[:, ns], wr_ref[n], preferred_element_type=F32))
        i_parts.append(jnp.dot(xcb[:, ns], wi_ref[n], preferred_element_type=F32))
    return xc, jnp.concatenate(r_parts, axis=1), jnp.concatenate(i_parts, axis=1)


def _lru_coefficients(xc, r_pre, i_pre, br_ref, bi_ref, lam_ref, a_sc, b_sc):
    r = jax.nn.sigmoid(r_pre + br_ref[...])
    gi = jax.nn.sigmoid(i_pre + bi_ref[...])
    log_a = LRU_C * r * _log_sigmoid(lam_ref[...])
    a = jnp.exp(log_a)
    a_sc[...] = a
    b_sc[...] = jnp.sqrt(-jnp.tanh(log_a) * (a * a + 1.0)) * (gi * xc)


def _lru_recurrence(groups, a_sc, b_sc, h_sc, carry_sc):
    w = BRANCH_WIDTH
    row = lax.broadcasted_iota(jnp.int32, (SUBLANES, w), 0)
    carry = carry_sc[...]
    for g in groups:
        rows = slice(g * SUBLANES, (g + 1) * SUBLANES)
        a = a_sc[rows, :]
        b = b_sc[rows, :]
        for k in (1, 2, 4):
            a_prev = pltpu.roll(a, k, axis=0)
            b_prev = pltpu.roll(b, k, axis=0)
            ok = row >= k
            b = jnp.where(ok, a * b_prev + b, b)
            a = jnp.where(ok, a * a_prev, a)
        hgrp = a * carry + b
        h_sc[rows, :] = hgrp
        carry = jnp.broadcast_to(hgrp[SUBLANES - 1:SUBLANES, :], (SUBLANES, w))
    carry_sc[...] = carry


def _proj_lru_kernel(x_ref, w_ref, s_ref, rx_ref, ry_ref, cw_ref, cb_ref, wr_ref, br_ref, wi_ref,
                     bi_ref, lam_ref, qkv_ref, olru_ref, xext_sc, a_sc, b_sc, h_sc, carry_sc, *, tm_lru):
    first = jnp.logical_and(pl.program_id(0) == 0, pl.program_id(1) == 0)
    tn = qkv_ref.shape[1]
    bounds = [0, tn // 3] + [tn // 3 + (k + 1) * (2 * tn // 9) for k in range(3)]
    groups = tm_lru // SUBLANES

    def project(k):
        cols = slice(bounds[k], bounds[k + 1])
        acc = jnp.dot(x_ref[...], w_ref[:, cols], preferred_element_type=F32)
        qkv_ref[:, cols] = (acc * s_ref[:, cols]).astype(qkv_ref.dtype)

    _lru_begin_tile(first, xext_sc, carry_sc, tm=tm_lru)
    project(0)
    xc, r_pre, i_pre = _lru_gates(rx_ref, cw_ref, cb_ref, wr_ref, wi_ref, xext_sc, tm=tm_lru)
    project(1)
    _lru_coefficients(xc, r_pre, i_pre, br_ref, bi_ref, lam_ref, a_sc, b_sc)
    project(2)
    _lru_recurrence(range(groups // 2), a_sc, b_sc, h_sc, carry_sc)
    project(3)
    _lru_recurrence(range(groups // 2, groups), a_sc, b_sc, h_sc, carry_sc)
    olru_ref[...] = (h_sc[...] * _gelu_tanh(ry_ref[...])).astype(olru_ref.dtype)


def _project_qkv_and_recur(xb, w_all, layer, colscale, uf, conv_w, conv_b, w_r, b_r, w_i, b_i, lam,
                           tm=1024, tn=2304, tm_lru=512):
    m, k = xb.shape
    n = w_all.shape[2]
    w = BRANCH_WIDTH
    nj = n // tn
    assert tm == nj * tm_lru
    kern = functools.partial(_proj_lru_kernel, tm_lru=tm_lru)
    row = lambda v: v.reshape(1, w)
    full2 = lambda shape: pl.BlockSpec(shape, lambda i, j: (0, 0))
    full3 = lambda shape: pl.BlockSpec(shape, lambda i, j: (0, 0, 0))
    return pl.pallas_call(
        kern,
        grid=(m // tm, nj),
        in_specs=[pl.BlockSpec((tm, k), lambda i, j: (i, 0)),
                  pl.BlockSpec((None, k, tn), lambda i, j: (layer, 0, j)),
                  pl.BlockSpec((1, tn), lambda i, j: (0, j)),
                  pl.BlockSpec((tm_lru, w), lambda i, j: (i * nj + j, 0)),
                  pl.BlockSpec((tm_lru, w), lambda i, j: (i * nj + j, 1)),
                  full2((CONV_WIDTH, w)), full2((1, w)),
                  full3((N_HEADS, HEAD_DIM, HEAD_DIM)), full2((1, w)),
                  full3((N_HEADS, HEAD_DIM, HEAD_DIM)), full2((1, w)),
                  full2((1, w))],
        out_specs=[pl.BlockSpec((tm, tn), lambda i, j: (i, j)),
                   pl.BlockSpec((tm_lru, w), lambda i, j: (i * nj + j, 0))],
        out_shape=[jax.ShapeDtypeStruct((m, n), BF16), jax.ShapeDtypeStruct((m, w), BF16)],
        scratch_shapes=[pltpu.VMEM((tm_lru + 2 * SUBLANES, w), F32),
                        pltpu.VMEM((tm_lru, w), F32),
                        pltpu.VMEM((tm_lru, w), F32),
                        pltpu.VMEM((tm_lru, w), F32),
                        pltpu.VMEM((SUBLANES, w), F32)],
        compiler_params=_cparams(("arbitrary", "arbitrary")),
        name="in_proj_qkv_recurrent",
    )(xb, w_all, colscale, uf, uf, conv_w, row(conv_b), w_r.astype(BF16), row(b_r),
      w_i.astype(BF16), row(b_i), row(lam))


def _merge_kernel(x_ref, o0_ref, o1_ref, o2_ref, o3_ref, wg_ref, bg_ref, wb_ref, wo_ref, w1_ref, w2_ref,
                  out_ref, wo_bf_ref, w1_bf_ref, w2_bf_ref):
    wo_bf_ref[...] = wo_ref[...].astype(BF16)
    w1_bf_ref[...] = w1_ref[...].astype(BF16)
    w2_bf_ref[...] = w2_ref[...].astype(BF16)
    x = x_ref[...]
    merged = None
    for g, o_ref in enumerate((o0_ref, o1_ref, o2_ref, o3_ref)):
        gate = jax.nn.sigmoid(jnp.dot(x, wg_ref[g], preferred_element_type=F32) + bg_ref[g:g + 1, :])
        term = gate * jnp.dot(o_ref[...], wb_ref[g], preferred_element_type=F32)
        merged = term if merged is None else merged + term
    out_ref[...] = merged.astype(out_ref.dtype)


def _merge(xb, branches, wg, bg_all, wb, layer, later_weights, tm=1024, tn=256):
    s, d = xb.shape
    w = BRANCH_WIDTH
    ni, nj = s // tm, d // tn
    o_spec = pl.BlockSpec((tm, w), lambda i, j: (i, 0))
    cast_in, cast_out, cast_shapes = [], [], []
    for arr in later_weights:
        _, r, c = arr.shape
        cast_in.append(pl.BlockSpec((None, r // ni, c // nj), lambda i, j: (layer, i, j)))
        cast_out.append(pl.BlockSpec((r // ni, c // nj), lambda i, j: (i, j)))
        cast_shapes.append(jax.ShapeDtypeStruct((r, c), BF16))
    return pl.pallas_call(
        _merge_kernel,
        grid=(ni, nj),
        in_specs=[pl.BlockSpec((tm, d), lambda i, j: (i, 0)),
                  o_spec, o_spec, o_spec, o_spec,
                  pl.BlockSpec((N_BRANCH, d, tn), lambda i, j: (0, 0, j)),
                  pl.BlockSpec((None, N_BRANCH, tn), lambda i, j: (layer, 0, j)),
                  pl.BlockSpec((N_BRANCH, w, tn), lambda i, j: (0, 0, j))] + cast_in,
        out_specs=[pl.BlockSpec((tm, tn), lambda i, j: (i, j))] + cast_out,
        out_shape=[jax.ShapeDtypeStruct((s, d), BF16)] + cast_shapes,
        compiler_params=_cparams(("parallel", "arbitrary")),
        name="gated_merge",
    )(xb, *branches, wg, bg_all, wb, *later_weights)


def _outproj_kernel(m_ref, w_ref, x_ref, g_ref, b_ref, of_ref, ob_ref):
    half = m_ref.shape[0] // 2
    rows = (slice(0, half), slice(half, 2 * half))
    proj = [jnp.dot(m_ref[r, :], w_ref[...], preferred_element_type=F32) for r in rows]
    for r, p in zip(rows, proj):
        y = _layer_norm_rows(ALPHA * x_ref[r, :] + p, g_ref[...], b_ref[...])
        of_ref[r, :] = y
        ob_ref[r, :] = y.astype(BF16)


def _outproj_ln(merged, w_out, x, g, b, tm=512):
    s, d = x.shape
    row_spec = pl.BlockSpec((tm, d), lambda i: (i, 0))
    vec_spec = pl.BlockSpec((1, d), lambda i: (0, 0))
    return pl.pallas_call(
        _outproj_kernel,
        grid=(s // tm,),
        in_specs=[row_spec, pl.BlockSpec((d, d), lambda i: (0, 0)), row_spec, vec_spec, vec_spec],
        out_specs=[row_spec, row_spec],
        out_shape=[jax.ShapeDtypeStruct((s, d), F32), jax.ShapeDtypeStruct((s, d), BF16)],
        compiler_params=_cparams(("parallel",)),
        name="outproj_ln",
    )(merged, w_out, x, g.reshape(1, d), b.reshape(1, d))


def _ffn_kernel(xb_ref, xf_ref, w1_ref, w2_ref, g_ref, b_ref, of_ref, ob_ref, acc_sc):
    f = pl.program_id(1)

    @pl.when(f == 0)
    def _init():
        acc_sc[...] = jnp.zeros(acc_sc.shape, F32)

    hid = jnp.maximum(jnp.dot(xb_ref[...], w1_ref[...], preferred_element_type=F32), 0.0)
    hid = (hid * hid).astype(BF16)
    acc_sc[...] += jnp.dot(hid, w2_ref[...], preferred_element_type=F32)

    @pl.when(f == pl.num_programs(1) - 1)
    def _finish():
        y = _layer_norm_rows(ALPHA * xf_ref[...] + acc_sc[...], g_ref[...], b_ref[...])
        of_ref[...] = y
        ob_ref[...] = y.astype(BF16)


def _ffn_ln(xb, xf, w1, w2, g, b, tm=512, tf=1024):
    s, d = xf.shape
    dff = w1.shape[1]
    row_spec = pl.BlockSpec((tm, d), lambda i, f: (i, 0))
    vec_spec = pl.BlockSpec((1, d), lambda i, f: (0, 0))
    return pl.pallas_call(
        _ffn_kernel,
        grid=(s // tm, dff // tf),
        in_specs=[row_spec, row_spec,
                  pl.BlockSpec((d, tf), lambda i, f: (0, f)),
                  pl.BlockSpec((tf, d), lambda i, f: (f, 0)),
                  vec_spec, vec_spec],
        out_specs=[row_spec, row_spec],
        out_shape=[jax.ShapeDtypeStruct((s, d), F32), jax.ShapeDtypeStruct((s, d), BF16)],
        scratch_shapes=[pltpu.VMEM((tm, d), F32)],
        compiler_params=_cparams(("parallel", "arbitrary")),
        name="ffn_ln",
    )(xb, xf, w1, w2, g.reshape(1, d), b.reshape(1, d))


def _split_in_proj(w_in):
    qkv = jnp.concatenate([w_in[:, :, _OFF_FQ:_OFF_FF], w_in[:, :, _OFF_SQ:_OFF_END]], axis=2)
    pad = jnp.zeros(w_in.shape[:2] + (LANES - N_HEADS,), w_in.dtype)
    rest = jnp.concatenate([w_in[:, :, _OFF_RX:_OFF_SQ], w_in[:, :, _OFF_FF:_OFF_RX], pad], axis=2)
    return qkv.astype(BF16), rest.astype(BF16)


def kernel(x, ln_in_g, ln_in_b, w_in, b_forget, conv_w, conv_b, w_r, b_r, w_i, b_i, lru_lambda,
           rel_bias, w_branch, w_gate, b_gate, w_out, ln1_g, ln1_b, w_ff1, w_ff2, ln2_g, ln2_b):
    batch, s, d = x.shape
    assert (batch, s, d) == (1, SEQ, D_MODEL)
    w = BRANCH_WIDTH
    band_tq = 256

    col_scale = np.ones((1, 9 * w), np.float32)
    for q_block in (0, 3, 6):
        col_scale[:, q_block * w:(q_block + 1) * w] = QK_SCALE * LOG2_E
    qkv_scale = jnp.asarray(col_scale)
    rest_scale = jnp.ones((1, 2 * w + LANES), F32)

    w_qkv, w_rest = _split_in_proj(w_in)
    w_gate_rows = w_gate.reshape(DEPTH, N_BRANCH * d, d)
    w_branch_rows = w_branch.reshape(DEPTH, N_BRANCH * w, d)

    xf, xb = _entry_ln(x.reshape(s, d), ln_in_g, ln_in_b)
    for l in range(DEPTH):
        uf = _project(xb, w_rest, l, rest_scale, F32, 1024, 1152, "in_proj_rest")
        qkv, o_lru = _project_qkv_and_recur(xb, w_qkv, l, qkv_scale, uf, conv_w[l], conv_b[l], w_r[l],
                                            b_r[l], w_i[l], b_i[l], lru_lambda[l])

        f_rows = uf[:, 2 * w:2 * w + N_HEADS].T.reshape(N_HEADS * (s // LANES), LANES)
        b_rows = jnp.repeat(b_forget[l].astype(F32), s // LANES).reshape(-1, 1)
        cf = _forget_cumsum(f_rows, b_rows).reshape(N_HEADS, s)
        cfk = jnp.pad(cf, ((0, SUBLANES - N_HEADS), (0, 0)))

        o_fox, wg_b, wb_b = _fox_attention(qkv, cfk, 0, l, (w_gate_rows, w_branch_rows))
        o_sb = _sb_attention(qkv, 3)
        o_ch = _band_attention(qkv, _band_bias_vector(rel_bias[l], band_tq), 6, band_tq)

        merged, wo_b, w1_b, w2_b = _merge(xb, (o_fox, o_lru, o_sb, o_ch),
                                          wg_b.reshape(N_BRANCH, d, d), b_gate,
                                          wb_b.reshape(N_BRANCH, w, d), l, (w_out, w_ff1, w_ff2))
        xf, xb = _outproj_ln(merged, wo_b, xf, ln1_g[l], ln1_b[l])
        xf, xb = _ffn_ln(xb, xf, w1_b, w2_b, ln2_g[l], ln2_b[l])
    return xf.reshape(batch, s, d)
```

```python
import functools
import math

import jax
import jax.numpy as jnp
import numpy as np
from jax import lax
from jax.experimental import pallas as pl
from jax.experimental.pallas import tpu as pltpu

F32 = jnp.float32
BF16 = jnp.bfloat16

D_MODEL = 2048
SEQ = 8192
DEPTH = 2
CHUNK = 64
HEAD_DIM = 128
N_BRANCH = 4
BRANCH_WIDTH = D_MODEL // N_BRANCH
N_HEADS = BRANCH_WIDTH // HEAD_DIM
CONV_WIDTH = 4
LRU_C = 8.0
LOOKBACK_CHUNKS = 8
REL_CLIP = 256
D_FF = 4 * D_MODEL
ALPHA = (2.0 * DEPTH) ** 0.25
LN_EPS = 1e-5
QK_SCALE = HEAD_DIM ** -0.5
LOG2_E = math.log2(math.e)

_OFF_FQ = 0
_OFF_FK = _OFF_FQ + BRANCH_WIDTH
_OFF_FV = _OFF_FK + BRANCH_WIDTH
_OFF_FF = _OFF_FV + BRANCH_WIDTH
_OFF_RX = _OFF_FF + N_HEADS
_OFF_RY = _OFF_RX + BRANCH_WIDTH
_OFF_SQ = _OFF_RY + BRANCH_WIDTH
_OFF_CQ = _OFF_SQ + 3 * BRANCH_WIDTH
_OFF_END = _OFF_CQ + 3 * BRANCH_WIDTH

LANES = 128
SUBLANES = 8
NEG_BIG = -1e30
SB_DEAD_LOG2 = -180.0
FOX_DEAD_LOG2 = -170.0

VMEM_LIMIT = 56 * 1024 * 1024


def _cparams(sem, vmem=VMEM_LIMIT):
    return pltpu.CompilerParams(dimension_semantics=sem, vmem_limit_bytes=vmem)


def _log_sigmoid(x):
    return jnp.minimum(x, 0.0) - jnp.log1p(jnp.exp(-jnp.abs(x)))


def _layer_norm_rows(y, g, b):
    mu = jnp.mean(y, axis=-1, keepdims=True)
    d = y - mu
    var = jnp.mean(d * d, axis=-1, keepdims=True)
    return d * lax.rsqrt(var + LN_EPS) * g + b


def _ln_kernel(x_ref, g_ref, b_ref, ob_ref):
    ob_ref[...] = _layer_norm_rows(x_ref[...], g_ref[...], b_ref[...]).astype(BF16)


def _entry_ln(x, g, b, tm=512):
    s, d = x.shape
    return pl.pallas_call(
        _ln_kernel,
        grid=(s // tm,),
        in_specs=[pl.BlockSpec((tm, d), lambda i: (i, 0)),
                  pl.BlockSpec((1, d), lambda i: (0, 0)),
                  pl.BlockSpec((1, d), lambda i: (0, 0))],
        out_specs=pl.BlockSpec((tm, d), lambda i: (i, 0)),
        out_shape=jax.ShapeDtypeStruct((s, d), BF16),
        compiler_params=_cparams(("parallel",)),
        name="entry_ln",
    )(x, g.reshape(1, d), b.reshape(1, d))


def _proj_kernel(x_ref, w_ref, s_ref, o_ref):
    acc = jnp.dot(x_ref[...], w_ref[...], preferred_element_type=F32)
    o_ref[...] = (acc * s_ref[...]).astype(o_ref.dtype)


def _project(xb, w_all, layer, colscale, out_dtype, tm, tn, name):
    m, k = xb.shape
    n = w_all.shape[2]
    return pl.pallas_call(
        _proj_kernel,
        grid=(m // tm, n // tn),
        in_specs=[pl.BlockSpec((tm, k), lambda i, j: (i, 0)),
                  pl.BlockSpec((None, k, tn), lambda i, j: (layer, 0, j)),
                  pl.BlockSpec((1, tn), lambda i, j: (0, j))],
        out_specs=pl.BlockSpec((tm, tn), lambda i, j: (i, j)),
        out_shape=jax.ShapeDtypeStruct((m, n), out_dtype),
        compiler_params=_cparams(("parallel", "arbitrary")),
        name=name,
    )(xb, w_all, colscale)


def _forget_cumsum_kernel(f_ref, b_ref, o_ref):
    rows = f_ref.shape[0]
    per_head = rows // N_HEADS
    ls = _log_sigmoid(f_ref[...] + b_ref[...])
    r = lax.broadcasted_iota(jnp.int32, (LANES, LANES), 0)
    c = lax.broadcasted_iota(jnp.int32, (LANES, LANES), 1)
    upper = (r <= c).astype(F32)
    within = jnp.dot(ls, upper, preferred_element_type=F32,
                     precision=lax.Precision.HIGHEST)
    total = within[:, LANES - 1:LANES]
    rr = lax.broadcasted_iota(jnp.int32, (rows, rows), 0)
    cc = lax.broadcasted_iota(jnp.int32, (rows, rows), 1)
    head_start = rr - (rr & (per_head - 1))
    before = ((cc >= head_start) & (cc < rr)).astype(F32)
    offs = jnp.dot(before, jnp.broadcast_to(total, (rows, LANES)),
                   preferred_element_type=F32, precision=lax.Precision.HIGHEST)
    o_ref[...] = (within + offs) * LOG2_E


def _forget_cumsum(f_rows, b_rows):
    rows = f_rows.shape[0]
    return pl.pallas_call(
        _forget_cumsum_kernel,
        out_shape=jax.ShapeDtypeStruct((rows, LANES), F32),
        name="forget_cumsum",
    )(f_rows, b_rows)


def _fox_kernel(q_ref, k_ref, v_ref, cfk_ref, wg_ref, wb_ref, o_ref, wg_bf_ref, wb_bf_ref,
                m_sc, acc_sc, vaug_sc, qn_sc, kn_sc, *, tq, tk):
    i = pl.program_id(0)
    wg_bf_ref[...] = wg_ref[...].astype(BF16)
    wb_bf_ref[...] = wb_ref[...].astype(BF16)
    heads = [slice(h * HEAD_DIM, (h + 1) * HEAD_DIM) for h in range(N_HEADS)]
    norm_rows = 1024

    @pl.when(i == 0)
    def _largest_key_norm():
        for h, hs in enumerate(heads):
            def chunk(c, best, hs=hs):
                rows = k_ref[pl.ds(pl.multiple_of(c * norm_rows, norm_rows), norm_rows), hs].astype(F32)
                return jnp.maximum(best, jnp.max(jnp.sum(rows * rows, axis=-1, keepdims=True)))
            best = lax.fori_loop(0, k_ref.shape[0] // norm_rows, chunk, jnp.zeros((SUBLANES, LANES), F32))
            kn_sc[h] = jnp.sqrt(best)

    m_sc[...] = jnp.full(m_sc.shape, NEG_BIG, F32)
    acc_sc[...] = jnp.zeros(acc_sc.shape, F32)
    vaug_sc[:, :, HEAD_DIM:] = jnp.ones((N_HEADS, tk, HEAD_DIM), BF16)
    for h, hs in enumerate(heads):
        q = q_ref[:, hs].astype(F32)
        qn_sc[h] = jnp.broadcast_to(jnp.sqrt(jnp.sum(q * q, axis=-1, keepdims=True)), (tq, LANES))
    rep = tk // LANES

    def key_tile(j, masked, hlist):
        k0 = pl.multiple_of(j * tk, tk)
        if masked:
            keep = (lax.broadcasted_iota(jnp.int32, (tq, tk), 1)
                    <= lax.broadcasted_iota(jnp.int32, (tq, tk), 0))
        scores = [lax.dot_general(q_ref[:, heads[h]], k_ref[pl.ds(k0, tk), heads[h]],
                                  (((1,), (1,)), ((), ())), preferred_element_type=F32) for h in hlist]
        probs, alphas = [], []
        for h, s in zip(hlist, scores):
            s = s - cfk_ref[h:h + 1, pl.ds(k0, tk)]
            if masked:
                s = jnp.where(keep, s, NEG_BIG)
            m_old = m_sc[h]
            m_new = jnp.maximum(m_old, jnp.max(s, axis=-1, keepdims=True))
            alphas.append(jnp.exp2(m_old - m_new))
            probs.append(jnp.exp2(s - jnp.concatenate([m_new] * rep, axis=1)).astype(BF16))
            m_sc[h] = m_new
            vaug_sc[h, :, :HEAD_DIM] = v_ref[pl.ds(k0, tk), heads[h]]
        for h, p, alpha in zip(hlist, probs, alphas):
            pv = jnp.dot(p, vaug_sc[h], preferred_element_type=F32)
            acc_sc[h] = jnp.concatenate([alpha, alpha], axis=1) * acc_sc[h] + pv

    def alive(j, hlist):
        newest = pl.multiple_of(jnp.maximum(j, 0) * tk + tk - LANES, LANES)
        reach = None
        for h in hlist:
            decay = -cfk_ref[h:h + 1, pl.ds(newest, LANES)][:, LANES - 1:]
            bound = jnp.max(qn_sc[h] * kn_sc[h, 0:1, :] + decay - m_sc[h], axis=0, keepdims=True)
            reach = bound if reach is None else jnp.minimum(reach, bound)
        return (jnp.max(reach) > FOX_DEAD_LOG2).astype(jnp.int32)

    def walk_back(j_start, hlist):
        def earlier(state):
            j, _ = state
            key_tile(j, False, hlist)
            return j - 1, alive(j - 1, hlist)
        return lax.while_loop(lambda st: jnp.logical_and(st[0] >= 0, st[1] > 0), earlier,
                              (j_start, alive(j_start, hlist)))[0]

    all_heads = list(range(N_HEADS))
    key_tile(i, True, all_heads)
    j_split = walk_back(i - 1, all_heads)
    for h in all_heads:
        walk_back(j_split, [h])
    for h in range(N_HEADS):
        hs = slice(h * HEAD_DIM, (h + 1) * HEAD_DIM)
        o_ref[:, hs] = (acc_sc[h, :, :HEAD_DIM] / acc_sc[h, :, HEAD_DIM:]).astype(o_ref.dtype)


def _fox_attention(qkv, cfk, col0, layer, later_weights, tq=512):
    s = qkv.shape[0]
    tk = tq
    w = BRANCH_WIDTH
    steps = s // tq
    kern = functools.partial(_fox_kernel, tq=tq, tk=tk)
    resident = pl.Buffered(1)
    cast_in, cast_out, cast_shapes = [], [], []
    for arr in later_weights:
        _, r, c = arr.shape
        cast_in.append(pl.BlockSpec((None, r // steps, c), lambda i: (layer, i, 0)))
        cast_out.append(pl.BlockSpec((r // steps, c), lambda i: (i, 0)))
        cast_shapes.append(jax.ShapeDtypeStruct((r, c), BF16))
    return pl.pallas_call(
        kern,
        grid=(steps,),
        in_specs=[pl.BlockSpec((tq, w), lambda i: (i, col0)),
                  pl.BlockSpec((s, w), lambda i: (0, col0 + 1), pipeline_mode=resident),
                  pl.BlockSpec((s, w), lambda i: (0, col0 + 2), pipeline_mode=resident),
                  pl.BlockSpec((SUBLANES, s), lambda i: (0, 0), pipeline_mode=resident)] + cast_in,
        out_specs=[pl.BlockSpec((tq, w), lambda i: (i, 0))] + cast_out,
        out_shape=[jax.ShapeDtypeStruct((s, w), BF16)] + cast_shapes,
        scratch_shapes=[pltpu.VMEM((N_HEADS, tq, LANES), F32),
                        pltpu.VMEM((N_HEADS, tq, 2 * HEAD_DIM), F32),
                        pltpu.VMEM((N_HEADS, tk, 2 * HEAD_DIM), BF16),
                        pltpu.VMEM((N_HEADS, tq, LANES), F32),
                        pltpu.VMEM((N_HEADS, SUBLANES, LANES), F32)],
        compiler_params=_cparams(("arbitrary",)),
        name="fox_attention",
    )(qkv, qkv, qkv, cfk, *later_weights)


def _sb_kernel(q_ref, k_ref, v_ref, o_ref, run_sc, acc_sc, *, tq):
    i = pl.program_id(0)
    nsub = tq // LANES
    r = lax.broadcasted_iota(jnp.int32, (2 * LANES, 2 * LANES), 0) & (LANES - 1)
    c = lax.broadcasted_iota(jnp.int32, (2 * LANES, 2 * LANES), 1)
    tri_aug = ((c >= LANES) | (r > c)).astype(BF16)

    def sub_blocks(items, masked):
        scores = []
        for h, k0, r0 in items:
            hs = slice(h * HEAD_DIM, (h + 1) * HEAD_DIM)
            scores.append(lax.dot_general(q_ref[r0:, hs], k_ref[pl.ds(k0, LANES), hs],
                                          (((1,), (1,)), ((), ())), preferred_element_type=F32))
        log_beta, sums, keeps = [], [], []
        for (h, k0, r0), z in zip(items, scores):
            rows = tq - r0
            lp = jnp.minimum(z, 0.0) - jnp.log2(1.0 + jnp.exp2(-jnp.abs(z)))
            ln = lp - z
            keep = None
            if masked:
                keep = (lax.broadcasted_iota(jnp.int32, (rows, LANES), 1)
                        < lax.broadcasted_iota(jnp.int32, (rows, LANES), 0))
                ln = jnp.where(keep, ln, 0.0)
            ln_hi = ln.astype(BF16)
            ln_lo = (ln - ln_hi.astype(F32)).astype(BF16)
            log_beta.append(lp)
            keeps.append(keep)
            sums.append(jnp.dot(jnp.concatenate([ln_hi, ln_lo], axis=1), tri_aug,
                                preferred_element_type=F32))
        weights = []
        for (h, k0, r0), lp, la, keep in zip(items, log_beta, sums, keeps):
            run = run_sc[h, r0:, :]
            a = jnp.exp2(lp + la[:, :LANES] + run)
            if masked:
                a = jnp.where(keep, a, 0.0)
            run_sc[h, r0:, :] = run + la[:, LANES:]
            weights.append(a.astype(BF16))
        for (h, k0, r0), a in zip(items, weights):
            hs = slice(h * HEAD_DIM, (h + 1) * HEAD_DIM)
            acc_sc[h, r0:, :] += jnp.dot(a, v_ref[pl.ds(k0, LANES), hs], preferred_element_type=F32)

    run_sc[...] = jnp.zeros(run_sc.shape, F32)
    acc_sc[...] = jnp.zeros(acc_sc.shape, F32)
    sub_blocks([(h, pl.multiple_of(i * tq + cc * LANES, LANES), cc * LANES)
                for cc in range(nsub - 1, -1, -1) for h in range(N_HEADS)], True)

    def alive():
        return (jnp.max(run_sc[...]) > SB_DEAD_LOG2).astype(jnp.int32)

    def earlier(state):
        cb, _ = state
        sub_blocks([(h, pl.multiple_of((cb - back) * LANES, LANES), 0)
                    for back in range(2) for h in range(N_HEADS)], False)
        return cb - 2, alive()

    lax.while_loop(lambda st: jnp.logical_and(st[0] >= 0, st[1] > 0), earlier,
                   (i * nsub - 1, alive()))
    for h in range(N_HEADS):
        o_ref[:, h * HEAD_DIM:(h + 1) * HEAD_DIM] = acc_sc[h].astype(o_ref.dtype)


def _sb_attention(qkv, col0, tq=512):
    s = qkv.shape[0]
    w = BRANCH_WIDTH
    assert (tq // LANES) % 2 == 0
    kern = functools.partial(_sb_kernel, tq=tq)
    resident = pl.Buffered(1)
    return pl.pallas_call(
        kern,
        grid=(s // tq,),
        in_specs=[pl.BlockSpec((tq, w), lambda i: (i, col0)),
                  pl.BlockSpec((s, w), lambda i: (0, col0 + 1), pipeline_mode=resident),
                  pl.BlockSpec((s, w), lambda i: (0, col0 + 2), pipeline_mode=resident)],
        out_specs=pl.BlockSpec((tq, w), lambda i: (i, 0)),
        out_shape=jax.ShapeDtypeStruct((s, w), BF16),
        scratch_shapes=[pltpu.VMEM((N_HEADS, tq, LANES), F32),
                        pltpu.VMEM((N_HEADS, tq, HEAD_DIM), F32)],
        compiler_params=_cparams(("arbitrary",)),
        name="sb_attention",
    )(qkv, qkv, qkv)


def _band_kernel(q_ref, k2_ref, k1_ref, k0_ref, v2_ref, v1_ref, v0_ref, ext_ref, o_ref,
                 bias_sc, vaug_sc, *, tq):
    i = pl.program_id(0)
    width = 4 * tq

    @pl.when(i == 0)
    def _build_tables():
        trow = lax.broadcasted_iota(jnp.int32, (tq, width), 0)
        t = lax.broadcasted_iota(jnp.int32, (tq, 3 * tq), 0)
        s = lax.broadcasted_iota(jnp.int32, (tq, 3 * tq), 1)
        shift = CHUNK.bit_length() - 1
        t_chunk = t >> shift
        s_chunk = (s >> shift) - (2 * tq) // CHUNK
        in_band = (t_chunk - s_chunk <= LOOKBACK_CHUNKS) & (s_chunk <= t_chunk)
        for h in range(N_HEADS):
            x = jnp.broadcast_to(ext_ref[h:h + 1, :], (tq, width))
            for b in range(tq.bit_length() - 1):
                x = jnp.where(((trow >> b) & 1) == 1, pltpu.roll(x, 1 << b, axis=1), x)
            bias_sc[h] = jnp.where(in_band, x[:, :3 * tq] * LOG2_E, NEG_BIG)
        vaug_sc[:, :, HEAD_DIM:] = jnp.ones((N_HEADS, 3 * tq, HEAD_DIM), BF16)

    k_refs = (k2_ref, k1_ref, k0_ref)
    v_refs = (v2_ref, v1_ref, v0_ref)
    heads = [slice(h * HEAD_DIM, (h + 1) * HEAD_DIM) for h in range(N_HEADS)]

    def tile(first_tiles):
        scores = [[lax.dot_general(q_ref[:, hs], k_refs[p][:, hs], (((1,), (1,)), ((), ())),
                                   preferred_element_type=F32) for p in range(3)] for hs in heads]
        probs = []
        for h, pieces in enumerate(scores):
            pieces = [s + bias_sc[h, :, p * tq:(p + 1) * tq] for p, s in enumerate(pieces)]
            if first_tiles:
                pieces = [jnp.where(i - 2 + p >= 0, s, NEG_BIG) for p, s in enumerate(pieces)]
            m = jnp.max(jnp.maximum(jnp.maximum(pieces[0], pieces[1]), pieces[2]),
                        axis=-1, keepdims=True)
            probs.append(jnp.concatenate([jnp.exp2(s - m).astype(BF16) for s in pieces], axis=1))
            for p in range(3):
                vaug_sc[h, p * tq:(p + 1) * tq, :HEAD_DIM] = v_refs[p][:, heads[h]]
        for h, p in enumerate(probs):
            acc = jnp.dot(p, vaug_sc[h], preferred_element_type=F32)
            o_ref[:, heads[h]] = (acc[:, :HEAD_DIM] / acc[:, HEAD_DIM:]).astype(o_ref.dtype)

    @pl.when(i < 2)
    def _first_tiles():
        tile(True)

    @pl.when(i >= 2)
    def _other_tiles():
        tile(False)


def _band_bias_vector(rel_bias_l, tq):
    n = np.arange(4 * tq)
    dist = np.where(n < 3 * tq, 2 * tq - n, 6 * tq - n)
    ridx = np.clip(dist, -(CHUNK - 1), REL_CLIP) + (CHUNK - 1)
    return rel_bias_l.astype(F32)[:, ridx]


def _band_attention(qkv, bias_ext, col0, tq=256):
    s = qkv.shape[0]
    w = BRANCH_WIDTH
    assert 2 * tq >= LOOKBACK_CHUNKS * CHUNK and tq % CHUNK == 0 and tq & (tq - 1) == 0
    kern = functools.partial(_band_kernel, tq=tq)

    def kv_spec(back, col):
        return pl.BlockSpec((tq, w), lambda i: (jnp.maximum(i - back, 0), col))

    return pl.pallas_call(
        kern,
        grid=(s // tq,),
        in_specs=[pl.BlockSpec((tq, w), lambda i: (i, col0)),
                  kv_spec(2, col0 + 1), kv_spec(1, col0 + 1), kv_spec(0, col0 + 1),
                  kv_spec(2, col0 + 2), kv_spec(1, col0 + 2), kv_spec(0, col0 + 2),
                  pl.BlockSpec((N_HEADS, 4 * tq), lambda i: (0, 0))],
        out_specs=pl.BlockSpec((tq, w), lambda i: (i, 0)),
        out_shape=jax.ShapeDtypeStruct((s, w), BF16),
        scratch_shapes=[pltpu.VMEM((N_HEADS, tq, 3 * tq), F32),
                        pltpu.VMEM((N_HEADS, 3 * tq, 2 * HEAD_DIM), BF16)],
        compiler_params=_cparams(("arbitrary",)),
        name="band_attention",
    )(qkv, qkv, qkv, qkv, qkv, qkv, qkv, bias_ext)


def _gelu_tanh(x):
    c = math.sqrt(2.0 / math.pi)
    return 0.5 * x * (1.0 + jnp.tanh(c * (x + 0.044715 * (x * x * x))))


def _lru_begin_tile(first, xext_sc, carry_sc, *, tm):
    halo = SUBLANES

    @pl.when(first)
    def _first():
        xext_sc[0:halo, :] = jnp.zeros((halo, BRANCH_WIDTH), F32)
        carry_sc[...] = jnp.zeros(carry_sc.shape, F32)

    @pl.when(jnp.logical_not(first))
    def _shift_halo():
        xext_sc[0:halo, :] = xext_sc[tm:tm + halo, :]


def _lru_gates(rx_ref, cw_ref, cb_ref, wr_ref, wi_ref, xext_sc, *, tm):
    halo = SUBLANES
    xext_sc[halo:halo + tm, :] = rx_ref[...]
    xext = xext_sc[0:halo + tm, :]
    xc = cb_ref[...] + rx_ref[...] * cw_ref[CONV_WIDTH - 1:CONV_WIDTH, :]
    for back in range(1, CONV_WIDTH):
        t = CONV_WIDTH - 1 - back
        xc = xc + pltpu.roll(xext, back, axis=0)[halo:, :] * cw_ref[t:t + 1, :]
    xcb = xc.astype(BF16)
    r_parts, i_parts = [], []
    for n in range(N_HEADS):
        ns = slice(n * HEAD_DIM, (n + 1) * HEAD_DIM)
        r_parts.append(jnp.dot(xcb[:, ns], wr_ref[n], preferred_element_type=F32))
        i_parts.append(jnp.dot(xcb[:, ns], wi_ref[n], preferred_element_type=F32))
    return xc, jnp.concatenate(r_parts, axis=1), jnp.concatenate(i_parts, axis=1)


def _lru_coefficients(xc, r_pre, i_pre, br_ref, bi_ref, lam_ref, a_sc, b_sc):
    r = jax.nn.sigmoid(r_pre + br_ref[...])
    gi = jax.nn.sigmoid(i_pre + bi_ref[...])
    log_a = LRU_C * r * _log_sigmoid(lam_ref[...])
    a = jnp.exp(log_a)
    a_sc[...] = a
    b_sc[...] = jnp.sqrt(-jnp.tanh(log_a) * (a * a + 1.0)) * (gi * xc)


def _lru_recurrence(groups, a_sc, b_sc, h_sc, carry_sc):
    w = BRANCH_WIDTH
    row = lax.broadcasted_iota(jnp.int32, (SUBLANES, w), 0)
    carry = carry_sc[...]
    for g in groups:
        rows = slice(g * SUBLANES, (g + 1) * SUBLANES)
        a = a_sc[rows, :]
        b = b_sc[rows, :]
        for k in (1, 2, 4):
            a_prev = pltpu.roll(a, k, axis=0)
            b_prev = pltpu.roll(b, k, axis=0)
            ok = row >= k
            b = jnp.where(ok, a * b_prev + b, b)
            a = jnp.where(ok, a * a_prev, a)
        hgrp = a * carry + b
        h_sc[rows, :] = hgrp
        carry = jnp.broadcast_to(hgrp[SUBLANES - 1:SUBLANES, :], (SUBLANES, w))
    carry_sc[...] = carry


def _proj_lru_kernel(x_ref, w_ref, s_ref, rx_ref, ry_ref, cw_ref, cb_ref, wr_ref, br_ref, wi_ref,
                     bi_ref, lam_ref, qkv_ref, olru_ref, xext_sc, a_sc, b_sc, h_sc, carry_sc, *, tm_lru):
    first = jnp.logical_and(pl.program_id(0) == 0, pl.program_id(1) == 0)
    tn = qkv_ref.shape[1]
    bounds = [0, tn // 3] + [tn // 3 + (k + 1) * (2 * tn // 9) for k in range(3)]
    groups = tm_lru // SUBLANES

    def project(k):
        cols = slice(bounds[k], bounds[k + 1])
        acc = jnp.dot(x_ref[...], w_ref[:, cols], preferred_element_type=F32)
        qkv_ref[:, cols] = (acc * s_ref[:, cols]).astype(qkv_ref.dtype)

    _lru_begin_tile(first, xext_sc, carry_sc, tm=tm_lru)
    project(0)
    xc, r_pre, i_pre = _lru_gates(rx_ref, cw_ref, cb_ref, wr_ref, wi_ref, xext_sc, tm=tm_lru)
    project(1)
    _lru_coefficients(xc, r_pre, i_pre, br_ref, bi_ref, lam_ref, a_sc, b_sc)
    project(2)
    _lru_recurrence(range(groups // 2), a_sc, b_sc, h_sc, carry_sc)
    project(3)
    _lru_recurrence(range(groups // 2, groups), a_sc, b_sc, h_sc, carry_sc)
    olru_ref[...] = (h_sc[...] * _gelu_tanh(ry_ref[...])).astype(olru_ref.dtype)


def _project_qkv_and_recur(xb, w_all, layer, colscale, uf, conv_w, conv_b, w_r, b_r, w_i, b_i, lam,
                           tm=1024, tn=2304, tm_lru=512):
    m, k = xb.shape
    n = w_all.shape[2]
    w = BRANCH_WIDTH
    nj = n // tn
    assert tm == nj * tm_lru
    kern = functools.partial(_proj_lru_kernel, tm_lru=tm_lru)
    row = lambda v: v.reshape(1, w)
    full2 = lambda shape: pl.BlockSpec(shape, lambda i, j: (0, 0))
    full3 = lambda shape: pl.BlockSpec(shape, lambda i, j: (0, 0, 0))
    return pl.pallas_call(
        kern,
        grid=(m // tm, nj),
        in_specs=[pl.BlockSpec((tm, k), lambda i, j: (i, 0)),
                  pl.BlockSpec((None, k, tn), lambda i, j: (layer, 0, j)),
                  pl.BlockSpec((1, tn), lambda i, j: (0, j)),
                  pl.BlockSpec((tm_lru, w), lambda i, j: (i * nj + j, 0)),
                  pl.BlockSpec((tm_lru, w), lambda i, j: (i * nj + j, 1)),
                  full2((CONV_WIDTH, w)), full2((1, w)),
                  full3((N_HEADS, HEAD_DIM, HEAD_DIM)), full2((1, w)),
                  full3((N_HEADS, HEAD_DIM, HEAD_DIM)), full2((1, w)),
                  full2((1, w))],
        out_specs=[pl.BlockSpec((tm, tn), lambda i, j: (i, j)),
                   pl.BlockSpec((tm_lru, w), lambda i, j: (i * nj + j, 0))],
        out_shape=[jax.ShapeDtypeStruct((m, n), BF16), jax.ShapeDtypeStruct((m, w), BF16)],
        scratch_shapes=[pltpu.VMEM((tm_lru + 2 * SUBLANES, w), F32),
                        pltpu.VMEM((tm_lru, w), F32),
                        pltpu.VMEM((tm_lru, w), F32),
                        pltpu.VMEM((tm_lru, w), F32),
                        pltpu.VMEM((SUBLANES, w), F32)],
        compiler_params=_cparams(("arbitrary", "arbitrary")),
        name="in_proj_qkv_recurrent",
    )(xb, w_all, colscale, uf, uf, conv_w, row(conv_b), w_r.astype(BF16), row(b_r),
      w_i.astype(BF16), row(b_i), row(lam))


def _merge_kernel(x_ref, o0_ref, o1_ref, o2_ref, o3_ref, wg_ref, bg_ref, wb_ref, wo_ref, w1_ref, w2_ref,
                  out_ref, wo_bf_ref, w1_bf_ref, w2_bf_ref):
    wo_bf_ref[...] = wo_ref[...].astype(BF16)
    w1_bf_ref[...] = w1_ref[...].astype(BF16)
    w2_bf_ref[...] = w2_ref[...].astype(BF16)
    x = x_ref[...]
    merged = None
    for g, o_ref in enumerate((o0_ref, o1_ref, o2_ref, o3_ref)):
        gate = jax.nn.sigmoid(jnp.dot(x, wg_ref[g], preferred_element_type=F32) + bg_ref[g:g + 1, :])
        term = gate * jnp.dot(o_ref[...], wb_ref[g], preferred_element_type=F32)
        merged = term if merged is None else merged + term
    out_ref[...] = merged.astype(out_ref.dtype)


def _merge(xb, branches, wg, bg_all, wb, layer, later_weights, tm=1024, tn=256):
    s, d = xb.shape
    w = BRANCH_WIDTH
    ni, nj = s // tm, d // tn
    o_spec = pl.BlockSpec((tm, w), lambda i, j: (i, 0))
    cast_in, cast_out, cast_shapes = [], [], []
    for arr in later_weights:
        _, r, c = arr.shape
        cast_in.append(pl.BlockSpec((None, r // ni, c // nj), lambda i, j: (layer, i, j)))
        cast_out.append(pl.BlockSpec((r // ni, c // nj), lambda i, j: (i, j)))
        cast_shapes.append(jax.ShapeDtypeStruct((r, c), BF16))
    return pl.pallas_call(
        _merge_kernel,
        grid=(ni, nj),
        in_specs=[pl.BlockSpec((tm, d), lambda i, j: (i, 0)),
                  o_spec, o_spec, o_spec, o_spec,
                  pl.BlockSpec((N_BRANCH, d, tn), lambda i, j: (0, 0, j)),
                  pl.BlockSpec((None, N_BRANCH, tn), lambda i, j: (layer, 0, j)),
                  pl.BlockSpec((N_BRANCH, w, tn), lambda i, j: (0, 0, j))] + cast_in,
        out_specs=[pl.BlockSpec((tm, tn), lambda i, j: (i, j))] + cast_out,
        out_shape=[jax.ShapeDtypeStruct((s, d), BF16)] + cast_shapes,
        compiler_params=_cparams(("parallel", "arbitrary")),
        name="gated_merge",
    )(xb, *branches, wg, bg_all, wb, *later_weights)


def _outproj_kernel(m_ref, w_ref, x_ref, pre_g_ref, pre_b_ref, g_ref, b_ref, of_ref, ob_ref, *, prenorm):
    half = m_ref.shape[0] // 2
    rows = (slice(0, half), slice(half, 2 * half))
    proj = [jnp.dot(m_ref[r, :], w_ref[...], preferred_element_type=F32) for r in rows]
    for r, p in zip(rows, proj):
        x = x_ref[r, :]
        if prenorm:
            x = _layer_norm_rows(x, pre_g_ref[...], pre_b_ref[...])
        y = _layer_norm_rows(ALPHA * x + p, g_ref[...], b_ref[...])
        of_ref[r, :] = y
        ob_ref[r, :] = y.astype(BF16)


def _outproj_ln(merged, w_out, x, pre_g, pre_b, prenorm, g, b, tm=512):
    s, d = x.shape
    row_spec = pl.BlockSpec((tm, d), lambda i: (i, 0))
    vec_spec = pl.BlockSpec((1, d), lambda i: (0, 0))
    return pl.pallas_call(
        functools.partial(_outproj_kernel, prenorm=prenorm),
        grid=(s // tm,),
        in_specs=[row_spec, pl.BlockSpec((d, d), lambda i: (0, 0)), row_spec, vec_spec, vec_spec,
                  vec_spec, vec_spec],
        out_specs=[row_spec, row_spec],
        out_shape=[jax.ShapeDtypeStruct((s, d), F32), jax.ShapeDtypeStruct((s, d), BF16)],
        compiler_params=_cparams(("parallel",)),
        name="outproj_ln",
    )(merged, w_out, x, pre_g.reshape(1, d), pre_b.reshape(1, d), g.reshape(1, d), b.reshape(1, d))


def _ffn_kernel(xb_ref, xf_ref, w1_ref, w2_ref, g_ref, b_ref, of_ref, ob_ref, acc_sc):
    f = pl.program_id(1)
    last = pl.num_programs(1) - 1

    def hidden():
        hid = jnp.maximum(jnp.dot(xb_ref[...], w1_ref[...], preferred_element_type=F32), 0.0)
        return (hid * hid).astype(BF16)

    @pl.when(f == 0)
    def _first():
        acc_sc[...] = jnp.dot(hidden(), w2_ref[...], preferred_element_type=F32)

    @pl.when(jnp.logical_and(f > 0, f < last))
    def _middle():
        acc_sc[...] += jnp.dot(hidden(), w2_ref[...], preferred_element_type=F32)

    @pl.when(f == last)
    def _last():
        hid = hidden()
        half = hid.shape[0] // 2
        rows = (slice(0, half), slice(half, 2 * half))
        down = [jnp.dot(hid[r, :], w2_ref[...], preferred_element_type=F32) for r in rows]
        for r, dn in zip(rows, down):
            y = _layer_norm_rows(ALPHA * xf_ref[r, :] + (acc_sc[r, :] + dn), g_ref[...], b_ref[...])
            of_ref[r, :] = y
            ob_ref[r, :] = y.astype(BF16)


def _ffn_ln(xb, xf, w1, w2, g, b, tm=512, tf=1024):
    s, d = xf.shape
    dff = w1.shape[1]
    row_spec = pl.BlockSpec((tm, d), lambda i, f: (i, 0))
    vec_spec = pl.BlockSpec((1, d), lambda i, f: (0, 0))
    return pl.pallas_call(
        _ffn_kernel,
        grid=(s // tm, dff // tf),
        in_specs=[row_spec, row_spec,
                  pl.BlockSpec((d, tf), lambda i, f: (0, f)),
                  pl.BlockSpec((tf, d), lambda i, f: (f, 0)),
                  vec_spec, vec_spec],
        out_specs=[row_spec, row_spec],
        out_shape=[jax.ShapeDtypeStruct((s, d), F32), jax.ShapeDtypeStruct((s, d), BF16)],
        scratch_shapes=[pltpu.VMEM((tm, d), F32)],
        compiler_params=_cparams(("parallel", "arbitrary")),
        name="ffn_ln",
    )(xb, xf, w1, w2, g.reshape(1, d), b.reshape(1, d))


def _split_in_proj(w_in):
    qkv = jnp.concatenate([w_in[:, :, _OFF_FQ:_OFF_FF], w_in[:, :, _OFF_SQ:_OFF_END]], axis=2)
    pad = jnp.zeros(w_in.shape[:2] + (LANES - N_HEADS,), w_in.dtype)
    rest = jnp.concatenate([w_in[:, :, _OFF_RX:_OFF_SQ], w_in[:, :, _OFF_FF:_OFF_RX], pad], axis=2)
    return qkv.astype(BF16), rest.astype(BF16)


def kernel(x, ln_in_g, ln_in_b, w_in, b_forget, conv_w, conv_b, w_r, b_r, w_i, b_i, lru_lambda,
           rel_bias, w_branch, w_gate, b_gate, w_out, ln1_g, ln1_b, w_ff1, w_ff2, ln2_g, ln2_b):
    batch, s, d = x.shape
    assert (batch, s, d) == (1, SEQ, D_MODEL)
    w = BRANCH_WIDTH
    band_tq = 256

    col_scale = np.ones((1, 9 * w), np.float32)
    for q_block in (0, 3, 6):
        col_scale[:, q_block * w:(q_block + 1) * w] = QK_SCALE * LOG2_E
    qkv_scale = jnp.asarray(col_scale)
    rest_scale = jnp.ones((1, 2 * w + LANES), F32)

    w_qkv, w_rest = _split_in_proj(w_in)
    w_gate_rows = w_gate.reshape(DEPTH, N_BRANCH * d, d)
    w_branch_rows = w_branch.reshape(DEPTH, N_BRANCH * w, d)

    xf = x.reshape(s, d)
    xb = _entry_ln(xf, ln_in_g, ln_in_b)
    for l in range(DEPTH):
        uf = _project(xb, w_rest, l, rest_scale, F32, 1024, 1152, "in_proj_rest")
        qkv, o_lru = _project_qkv_and_recur(xb, w_qkv, l, qkv_scale, uf, conv_w[l], conv_b[l], w_r[l],
                                            b_r[l], w_i[l], b_i[l], lru_lambda[l])

        f_rows = uf[:, 2 * w:2 * w + N_HEADS].T.reshape(N_HEADS * (s // LANES), LANES)
        b_rows = jnp.repeat(b_forget[l].astype(F32), s // LANES).reshape(-1, 1)
        cf = _forget_cumsum(f_rows, b_rows).reshape(N_HEADS, s)
        cfk = jnp.pad(cf, ((0, SUBLANES - N_HEADS), (0, 0)))

        o_fox, wg_b, wb_b = _fox_attention(qkv, cfk, 0, l, (w_gate_rows, w_branch_rows))
        o_sb = _sb_attention(qkv, 3)
        o_ch = _band_attention(qkv, _band_bias_vector(rel_bias[l], band_tq), 6, band_tq)

        merged, wo_b, w1_b, w2_b = _merge(xb, (o_fox, o_lru, o_sb, o_ch),
                                          wg_b.reshape(N_BRANCH, d, d), b_gate,
                                          wb_b.reshape(N_BRANCH, w, d), l, (w_out, w_ff1, w_ff2))
        xf, xb = _outproj_ln(merged, wo_b, xf, ln_in_g, ln_in_b, l == 0, ln1_g[l], ln1_b[l])
        xf, xb = _ffn_ln(xb, xf, w1_b, w2_b, ln2_g[l], ln2_b[l])
    return xf.reshape(batch, s, d)
```

```python
import functools
import math

import jax
import jax.numpy as jnp
import numpy as np
from jax import lax
from jax.experimental import pallas as pl
from jax.experimental.pallas import tpu as pltpu

F32 = jnp.float32
BF16 = jnp.bfloat16

D_MODEL = 2048
SEQ = 8192
DEPTH = 2
CHUNK = 64
HEAD_DIM = 128
N_BRANCH = 4
BRANCH_WIDTH = D_MODEL // N_BRANCH
N_HEADS = BRANCH_WIDTH // HEAD_DIM
CONV_WIDTH = 4
LRU_C = 8.0
LOOKBACK_CHUNKS = 8
REL_CLIP = 256
D_FF = 4 * D_MODEL
ALPHA = (2.0 * DEPTH) ** 0.25
LN_EPS = 1e-5
QK_SCALE = HEAD_DIM ** -0.5
LOG2_E = math.log2(math.e)

_OFF_FQ = 0
_OFF_FK = _OFF_FQ + BRANCH_WIDTH
_OFF_FV = _OFF_FK + BRANCH_WIDTH
_OFF_FF = _OFF_FV + BRANCH_WIDTH
_OFF_RX = _OFF_FF + N_HEADS
_OFF_RY = _OFF_RX + BRANCH_WIDTH
_OFF_SQ = _OFF_RY + BRANCH_WIDTH
_OFF_CQ = _OFF_SQ + 3 * BRANCH_WIDTH
_OFF_END = _OFF_CQ + 3 * BRANCH_WIDTH

LANES = 128
SUBLANES = 8
NEG_BIG = -1e30
SB_DEAD_LOG2 = -180.0
FOX_DEAD_LOG2 = -170.0

VMEM_LIMIT = 56 * 1024 * 1024


def _cparams(sem, vmem=VMEM_LIMIT):
    return pltpu.CompilerParams(dimension_semantics=sem, vmem_limit_bytes=vmem)


def _log_sigmoid(x):
    return jnp.minimum(x, 0.0) - jnp.log1p(jnp.exp(-jnp.abs(x)))


def _layer_norm_rows(y, g, b):
    mu = jnp.mean(y, axis=-1, keepdims=True)
    d = y - mu
    var = jnp.mean(d * d, axis=-1, keepdims=True)
    return d * lax.rsqrt(var + LN_EPS) * g + b


def _ln_kernel(x_ref, g_ref, b_ref, ob_ref):
    ob_ref[...] = _layer_norm_rows(x_ref[...], g_ref[...], b_ref[...]).astype(BF16)


def _entry_ln(x, g, b, tm=512):
    s, d = x.shape
    return pl.pallas_call(
        _ln_kernel,
        grid=(s // tm,),
        in_specs=[pl.BlockSpec((tm, d), lambda i: (i, 0)),
                  pl.BlockSpec((1, d), lambda i: (0, 0)),
                  pl.BlockSpec((1, d), lambda i: (0, 0))],
        out_specs=pl.BlockSpec((tm, d), lambda i: (i, 0)),
        out_shape=jax.ShapeDtypeStruct((s, d), BF16),
        compiler_params=_cparams(("parallel",)),
        name="entry_ln",
    )(x, g.reshape(1, d), b.reshape(1, d))


def _proj_kernel(x_ref, w_ref, s_ref, o_ref):
    acc = jnp.dot(x_ref[...], w_ref[...], preferred_element_type=F32)
    o_ref[...] = (acc * s_ref[...]).astype(o_ref.dtype)


def _project(xb, w_all, layer, colscale, out_dtype, tm, tn, name):
    m, k = xb.shape
    n = w_all.shape[2]
    return pl.pallas_call(
        _proj_kernel,
        grid=(m // tm, n // tn),
        in_specs=[pl.BlockSpec((tm, k), lambda i, j: (i, 0)),
                  pl.BlockSpec((None, k, tn), lambda i, j: (layer, 0, j)),
                  pl.BlockSpec((1, tn), lambda i, j: (0, j))],
        out_specs=pl.BlockSpec((tm, tn), lambda i, j: (i, j)),
        out_shape=jax.ShapeDtypeStruct((m, n), out_dtype),
        compiler_params=_cparams(("parallel", "arbitrary")),
        name=name,
    )(xb, w_all, colscale)


def _forget_cumsum_kernel(f_ref, b_ref, o_ref):
    rows = f_ref.shape[0]
    per_head = rows // N_HEADS
    ls = _log_sigmoid(f_ref[...] + b_ref[...])
    r = lax.broadcasted_iota(jnp.int32, (LANES, LANES), 0)
    c = lax.broadcasted_iota(jnp.int32, (LANES, LANES), 1)
    upper = (r <= c).astype(F32)
    within = jnp.dot(ls, upper, preferred_element_type=F32,
                     precision=lax.Precision.HIGHEST)
    total = within[:, LANES - 1:LANES]
    rr = lax.broadcasted_iota(jnp.int32, (rows, rows), 0)
    cc = lax.broadcasted_iota(jnp.int32, (rows, rows), 1)
    head_start = rr - (rr & (per_head - 1))
    before = ((cc >= head_start) & (cc < rr)).astype(F32)
    offs = jnp.dot(before, jnp.broadcast_to(total, (rows, LANES)),
                   preferred_element_type=F32, precision=lax.Precision.HIGHEST)
    o_ref[...] = (within + offs) * LOG2_E


def _forget_cumsum(f_rows, b_rows):
    rows = f_rows.shape[0]
    return pl.pallas_call(
        _forget_cumsum_kernel,
        out_shape=jax.ShapeDtypeStruct((rows, LANES), F32),
        name="forget_cumsum",
    )(f_rows, b_rows)


def _fox_kernel(q_ref, k_ref, v_ref, cfk_ref, wg_ref, wb_ref, o_ref, wg_bf_ref, wb_bf_ref,
                m_sc, acc_sc, vaug_sc, qn_sc, kn_sc, *, tq, tk):
    i = pl.program_id(0)
    for src, dst in ((wg_ref, wg_bf_ref), (wb_ref, wb_bf_ref)):
        width = dst.shape[2]
        for c in range(dst.shape[0]):
            dst[c] = src[:, c * width:(c + 1) * width].astype(BF16)
    heads = [slice(h * HEAD_DIM, (h + 1) * HEAD_DIM) for h in range(N_HEADS)]
    norm_rows = 1024

    @pl.when(i == 0)
    def _largest_key_norm():
        for h, hs in enumerate(heads):
            def chunk(c, best, hs=hs):
                rows = k_ref[pl.ds(pl.multiple_of(c * norm_rows, norm_rows), norm_rows), hs].astype(F32)
                return jnp.maximum(best, jnp.max(jnp.sum(rows * rows, axis=-1, keepdims=True)))
            best = lax.fori_loop(0, k_ref.shape[0] // norm_rows, chunk, jnp.zeros((SUBLANES, LANES), F32))
            kn_sc[h] = jnp.sqrt(best)

    m_sc[...] = jnp.full(m_sc.shape, NEG_BIG, F32)
    acc_sc[...] = jnp.zeros(acc_sc.shape, F32)
    vaug_sc[:, :, HEAD_DIM:] = jnp.ones((N_HEADS, tk, HEAD_DIM), BF16)
    for h, hs in enumerate(heads):
        q = q_ref[:, hs].astype(F32)
        qn_sc[h] = jnp.broadcast_to(jnp.sqrt(jnp.sum(q * q, axis=-1, keepdims=True)), (tq, LANES))
    rep = tk // LANES

    def key_tile(j, masked, hlist):
        k0 = pl.multiple_of(j * tk, tk)
        if masked:
            keep = (lax.broadcasted_iota(jnp.int32, (tq, tk), 1)
                    <= lax.broadcasted_iota(jnp.int32, (tq, tk), 0))
        scores = [lax.dot_general(q_ref[:, heads[h]], k_ref[pl.ds(k0, tk), heads[h]],
                                  (((1,), (1,)), ((), ())), preferred_element_type=F32) for h in hlist]
        probs, alphas = [], []
        for h, s in zip(hlist, scores):
            s = s - cfk_ref[h:h + 1, pl.ds(k0, tk)]
            if masked:
                s = jnp.where(keep, s, NEG_BIG)
            m_old = m_sc[h]
            m_new = jnp.maximum(m_old, jnp.max(s, axis=-1, keepdims=True))
            alphas.append(jnp.exp2(m_old - m_new))
            probs.append(jnp.exp2(s - jnp.concatenate([m_new] * rep, axis=1)).astype(BF16))
            m_sc[h] = m_new
            vaug_sc[h, :, :HEAD_DIM] = v_ref[pl.ds(k0, tk), heads[h]]
        for h, p, alpha in zip(hlist, probs, alphas):
            pv = jnp.dot(p, vaug_sc[h], preferred_element_type=F32)
            acc_sc[h] = jnp.concatenate([alpha, alpha], axis=1) * acc_sc[h] + pv

    def alive(j, hlist):
        newest = pl.multiple_of(jnp.maximum(j, 0) * tk + tk - LANES, LANES)
        reach = None
        for h in hlist:
            decay = -cfk_ref[h:h + 1, pl.ds(newest, LANES)][:, LANES - 1:]
            bound = jnp.max(qn_sc[h] * kn_sc[h, 0:1, :] + decay - m_sc[h], axis=0, keepdims=True)
            reach = bound if reach is None else jnp.minimum(reach, bound)
        return (jnp.max(reach) > FOX_DEAD_LOG2).astype(jnp.int32)

    def walk_back(j_start, hlist):
        def earlier(state):
            j, _ = state
            key_tile(j, False, hlist)
            return j - 1, alive(j - 1, hlist)
        return lax.while_loop(lambda st: jnp.logical_and(st[0] >= 0, st[1] > 0), earlier,
                              (j_start, alive(j_start, hlist)))[0]

    all_heads = list(range(N_HEADS))
    key_tile(i, True, all_heads)
    j_split = walk_back(i - 1, all_heads)
    for h in all_heads:
        walk_back(j_split, [h])
    for h in range(N_HEADS):
        hs = slice(h * HEAD_DIM, (h + 1) * HEAD_DIM)
        o_ref[:, hs] = (acc_sc[h, :, :HEAD_DIM] / acc_sc[h, :, HEAD_DIM:]).astype(o_ref.dtype)


def _fox_attention(qkv, cfk, col0, layer, later_weights, col_blocks, tq=512):
    s = qkv.shape[0]
    tk = tq
    w = BRANCH_WIDTH
    steps = s // tq
    kern = functools.partial(_fox_kernel, tq=tq, tk=tk)
    resident = pl.Buffered(1)
    cast_in, cast_out, cast_shapes = [], [], []
    for arr in later_weights:
        _, r, c = arr.shape
        cast_in.append(pl.BlockSpec((None, r // steps, c), lambda i: (layer, i, 0)))
        cast_out.append(pl.BlockSpec((col_blocks, r // steps, c // col_blocks), lambda i: (0, i, 0)))
        cast_shapes.append(jax.ShapeDtypeStruct((col_blocks, r, c // col_blocks), BF16))
    return pl.pallas_call(
        kern,
        grid=(steps,),
        in_specs=[pl.BlockSpec((tq, w), lambda i: (i, col0)),
                  pl.BlockSpec((s, w), lambda i: (0, col0 + 1), pipeline_mode=resident),
                  pl.BlockSpec((s, w), lambda i: (0, col0 + 2), pipeline_mode=resident),
                  pl.BlockSpec((SUBLANES, s), lambda i: (0, 0), pipeline_mode=resident)] + cast_in,
        out_specs=[pl.BlockSpec((tq, w), lambda i: (i, 0))] + cast_out,
        out_shape=[jax.ShapeDtypeStruct((s, w), BF16)] + cast_shapes,
        scratch_shapes=[pltpu.VMEM((N_HEADS, tq, LANES), F32),
                        pltpu.VMEM((N_HEADS, tq, 2 * HEAD_DIM), F32),
                        pltpu.VMEM((N_HEADS, tk, 2 * HEAD_DIM), BF16),
                        pltpu.VMEM((N_HEADS, tq, LANES), F32),
                        pltpu.VMEM((N_HEADS, SUBLANES, LANES), F32)],
        compiler_params=_cparams(("arbitrary",)),
        name="fox_attention",
    )(qkv, qkv, qkv, cfk, *later_weights)


def _sb_kernel(q_ref, k_ref, v_ref, o_ref, run_sc, acc_sc, *, tq):
    i = pl.program_id(0)
    nsub = tq // LANES
    r = lax.broadcasted_iota(jnp.int32, (2 * LANES, 2 * LANES), 0) & (LANES - 1)
    c = lax.broadcasted_iota(jnp.int32, (2 * LANES, 2 * LANES), 1)
    tri_aug = ((c >= LANES) | (r > c)).astype(BF16)

    def sub_blocks(items, masked):
        scores = []
        for h, k0, r0 in items:
            hs = slice(h * HEAD_DIM, (h + 1) * HEAD_DIM)
            scores.append(lax.dot_general(q_ref[r0:, hs], k_ref[pl.ds(k0, LANES), hs],
                                          (((1,), (1,)), ((), ())), preferred_element_type=F32))
        log_beta, sums, keeps = [], [], []
        for (h, k0, r0), z in zip(items, scores):
            rows = tq - r0
            lp = jnp.minimum(z, 0.0) - jnp.log2(1.0 + jnp.exp2(-jnp.abs(z)))
            ln = lp - z
            keep = None
            if masked:
                keep = (lax.broadcasted_iota(jnp.int32, (rows, LANES), 1)
                        < lax.broadcasted_iota(jnp.int32, (rows, LANES), 0))
                ln = jnp.where(keep, ln, 0.0)
            ln_hi = ln.astype(BF16)
            ln_lo = (ln - ln_hi.astype(F32)).astype(BF16)
            log_beta.append(lp)
            keeps.append(keep)
            sums.append(jnp.dot(jnp.concatenate([ln_hi, ln_lo], axis=1), tri_aug,
                                preferred_element_type=F32))
        weights = []
        for (h, k0, r0), lp, la, keep in zip(items, log_beta, sums, keeps):
            run = run_sc[h, r0:, :]
            a = jnp.exp2(lp + la[:, :LANES] + run)
            if masked:
                a = jnp.where(keep, a, 0.0)
            run_sc[h, r0:, :] = run + la[:, LANES:]
            weights.append(a.astype(BF16))
        for (h, k0, r0), a in zip(items, weights):
            hs = slice(h * HEAD_DIM, (h + 1) * HEAD_DIM)
            acc_sc[h, r0:, :] += jnp.dot(a, v_ref[pl.ds(k0, LANES), hs], preferred_element_type=F32)

    run_sc[...] = jnp.zeros(run_sc.shape, F32)
    acc_sc[...] = jnp.zeros(acc_sc.shape, F32)
    sub_blocks([(h, pl.multiple_of(i * tq + cc * LANES, LANES), cc * LANES)
                for cc in range(nsub - 1, -1, -1) for h in range(N_HEADS)], True)

    def alive():
        return (jnp.max(run_sc[...]) > SB_DEAD_LOG2).astype(jnp.int32)

    def earlier(state):
        cb, _ = state
        sub_blocks([(h, pl.multiple_of((cb - back) * LANES, LANES), 0)
                    for back in range(2) for h in range(N_HEADS)], False)
        return cb - 2, alive()

    lax.while_loop(lambda st: jnp.logical_and(st[0] >= 0, st[1] > 0), earlier,
                   (i * nsub - 1, alive()))
    for h in range(N_HEADS):
        o_ref[:, h * HEAD_DIM:(h + 1) * HEAD_DIM] = acc_sc[h].astype(o_ref.dtype)


def _sb_attention(qkv, col0, tq=512):
    s = qkv.shape[0]
    w = BRANCH_WIDTH
    assert (tq // LANES) % 2 == 0
    kern = functools.partial(_sb_kernel, tq=tq)
    resident = pl.Buffered(1)
    return pl.pallas_call(
        kern,
        grid=(s // tq,),
        in_specs=[pl.BlockSpec((tq, w), lambda i: (i, col0)),
                  pl.BlockSpec((s, w), lambda i: (0, col0 + 1), pipeline_mode=resident),
                  pl.BlockSpec((s, w), lambda i: (0, col0 + 2), pipeline_mode=resident)],
        out_specs=pl.BlockSpec((tq, w), lambda i: (i, 0)),
        out_shape=jax.ShapeDtypeStruct((s, w), BF16),
        scratch_shapes=[pltpu.VMEM((N_HEADS, tq, LANES), F32),
                        pltpu.VMEM((N_HEADS, tq, HEAD_DIM), F32)],
        compiler_params=_cparams(("arbitrary",)),
        name="sb_attention",
    )(qkv, qkv, qkv)


def _band_kernel(q_ref, k2_ref, k1_ref, k0_ref, v2_ref, v1_ref, v0_ref, ext_ref, o_ref,
                 bias_sc, vaug_sc, *, tq):
    i = pl.program_id(0)
    width = 4 * tq

    @pl.when(i == 0)
    def _build_tables():
        trow = lax.broadcasted_iota(jnp.int32, (tq, width), 0)
        t = lax.broadcasted_iota(jnp.int32, (tq, 3 * tq), 0)
        s = lax.broadcasted_iota(jnp.int32, (tq, 3 * tq), 1)
        shift = CHUNK.bit_length() - 1
        t_chunk = t >> shift
        s_chunk = (s >> shift) - (2 * tq) // CHUNK
        in_band = (t_chunk - s_chunk <= LOOKBACK_CHUNKS) & (s_chunk <= t_chunk)
        for h in range(N_HEADS):
            x = jnp.broadcast_to(ext_ref[h:h + 1, :], (tq, width))
            for b in range(tq.bit_length() - 1):
                x = jnp.where(((trow >> b) & 1) == 1, pltpu.roll(x, 1 << b, axis=1), x)
            bias_sc[h] = jnp.where(in_band, x[:, :3 * tq] * LOG2_E, NEG_BIG)
        vaug_sc[:, :, HEAD_DIM:] = jnp.ones((N_HEADS, 3 * tq, HEAD_DIM), BF16)

    k_refs = (k2_ref, k1_ref, k0_ref)
    v_refs = (v2_ref, v1_ref, v0_ref)
    heads = [slice(h * HEAD_DIM, (h + 1) * HEAD_DIM) for h in range(N_HEADS)]

    def tile(first_tiles):
        scores = [[lax.dot_general(q_ref[:, hs], k_refs[p][:, hs], (((1,), (1,)), ((), ())),
                                   preferred_element_type=F32) for p in range(3)] for hs in heads]
        probs = []
        for h, pieces in enumerate(scores):
            pieces = [s + bias_sc[h, :, p * tq:(p + 1) * tq] for p, s in enumerate(pieces)]
            if first_tiles:
                pieces = [jnp.where(i - 2 + p >= 0, s, NEG_BIG) for p, s in enumerate(pieces)]
            m = jnp.max(jnp.maximum(jnp.maximum(pieces[0], pieces[1]), pieces[2]),
                        axis=-1, keepdims=True)
            probs.append(jnp.concatenate([jnp.exp2(s - m).astype(BF16) for s in pieces], axis=1))
            for p in range(3):
                vaug_sc[h, p * tq:(p + 1) * tq, :HEAD_DIM] = v_refs[p][:, heads[h]]
        for h, p in enumerate(probs):
            acc = jnp.dot(p, vaug_sc[h], preferred_element_type=F32)
            o_ref[:, heads[h]] = (acc[:, :HEAD_DIM] / acc[:, HEAD_DIM:]).astype(o_ref.dtype)

    @pl.when(i < 2)
    def _first_tiles():
        tile(True)

    @pl.when(i >= 2)
    def _other_tiles():
        tile(False)


def _band_bias_vector(rel_bias_l, tq):
    n = np.arange(4 * tq)
    dist = np.where(n < 3 * tq, 2 * tq - n, 6 * tq - n)
    ridx = np.clip(dist, -(CHUNK - 1), REL_CLIP) + (CHUNK - 1)
    return rel_bias_l.astype(F32)[:, ridx]


def _band_attention(qkv, bias_ext, col0, tq=256):
    s = qkv.shape[0]
    w = BRANCH_WIDTH
    assert 2 * tq >= LOOKBACK_CHUNKS * CHUNK and tq % CHUNK == 0 and tq & (tq - 1) == 0
    kern = functools.partial(_band_kernel, tq=tq)

    def kv_spec(back, col):
        return pl.BlockSpec((tq, w), lambda i: (jnp.maximum(i - back, 0), col))

    return pl.pallas_call(
        kern,
        grid=(s // tq,),
        in_specs=[pl.BlockSpec((tq, w), lambda i: (i, col0)),
                  kv_spec(2, col0 + 1), kv_spec(1, col0 + 1), kv_spec(0, col0 + 1),
                  kv_spec(2, col0 + 2), kv_spec(1, col0 + 2), kv_spec(0, col0 + 2),
                  pl.BlockSpec((N_HEADS, 4 * tq), lambda i: (0, 0))],
        out_specs=pl.BlockSpec((tq, w), lambda i: (i, 0)),
        out_shape=jax.ShapeDtypeStruct((s, w), BF16),
        scratch_shapes=[pltpu.VMEM((N_HEADS, tq, 3 * tq), F32),
                        pltpu.VMEM((N_HEADS, 3 * tq, 2 * HEAD_DIM), BF16)],
        compiler_params=_cparams(("arbitrary",)),
        name="band_attention",
    )(qkv, qkv, qkv, qkv, qkv, qkv, qkv, bias_ext)


def _gelu_tanh(x):
    c = math.sqrt(2.0 / math.pi)
    return 0.5 * x * (1.0 + jnp.tanh(c * (x + 0.044715 * (x * x * x))))


def _lru_begin_tile(first, xext_sc, carry_sc, *, tm):
    halo = SUBLANES

    @pl.when(first)
    def _first():
        xext_sc[0:halo, :] = jnp.zeros((halo, BRANCH_WIDTH), F32)
        carry_sc[...] = jnp.zeros(carry_sc.shape, F32)

    @pl.when(jnp.logical_not(first))
    def _shift_halo():
        xext_sc[0:halo, :] = xext_sc[tm:tm + halo, :]


def _lru_gates(rx_ref, cw_ref, cb_ref, wr_ref, wi_ref, xext_sc, *, tm):
    halo = SUBLANES
    xext_sc[halo:halo + tm, :] = rx_ref[...]
    xext = xext_sc[0:halo + tm, :]
    xc = cb_ref[...] + rx_ref[...] * cw_ref[CONV_WIDTH - 1:CONV_WIDTH, :]
    for back in range(1, CONV_WIDTH):
        t = CONV_WIDTH - 1 - back
        xc = xc + pltpu.roll(xext, back, axis=0)[halo:, :] * cw_ref[t:t + 1, :]
    xcb = xc.astype(BF16)
    r_parts, i_parts = [], []
    for n in range(N_HEADS):
        ns = slice(n * HEAD_DIM, (n + 1) * HEAD_DIM)
        r_parts.append(jnp.dot(xcb[:, ns], wr_ref[n], preferred_element_type=F32))
        i_parts.append(jnp.dot(xcb[:, ns], wi_ref[n], preferred_element_type=F32))
    return xc, jnp.concatenate(r_parts, axis=1), jnp.concatenate(i_parts, axis=1)


def _lru_coefficients(xc, r_pre, i_pre, br_ref, bi_ref, lam_ref, a_sc, b_sc):
    r = jax.nn.sigmoid(r_pre + br_ref[...])
    gi = jax.nn.sigmoid(i_pre + bi_ref[...])
    log_a = LRU_C * r * _log_sigmoid(lam_ref[...])
    a = jnp.exp(log_a)
    a_sc[...] = a
    b_sc[...] = jnp.sqrt(-jnp.tanh(log_a) * (a * a + 1.0)) * (gi * xc)


def _lru_recurrence(groups, a_sc, b_sc, h_sc, carry_sc):
    w = BRANCH_WIDTH
    row = lax.broadcasted_iota(jnp.int32, (SUBLANES, w), 0)
    carry = carry_sc[...]
    for g in groups:
        rows = slice(g * SUBLANES, (g + 1) * SUBLANES)
        a = a_sc[rows, :]
        b = b_sc[rows, :]
        for k in (1, 2, 4):
            a_prev = pltpu.roll(a, k, axis=0)
            b_prev = pltpu.roll(b, k, axis=0)
            ok = row >= k
            b = jnp.where(ok, a * b_prev + b, b)
            a = jnp.where(ok, a * a_prev, a)
        hgrp = a * carry + b
        h_sc[rows, :] = hgrp
        carry = jnp.broadcast_to(hgrp[SUBLANES - 1:SUBLANES, :], (SUBLANES, w))
    carry_sc[...] = carry


def _proj_lru_kernel(x_ref, w_ref, s_ref, rx_ref, ry_ref, cw_ref, cb_ref, wr_ref, br_ref, wi_ref,
                     bi_ref, lam_ref, qkv_ref, olru_ref, xext_sc, a_sc, b_sc, h_sc, carry_sc, *, tm_lru):
    first = jnp.logical_and(pl.program_id(0) == 0, pl.program_id(1) == 0)
    tn = qkv_ref.shape[1]
    bounds = [0, tn // 3] + [tn // 3 + (k + 1) * (2 * tn // 9) for k in range(3)]
    groups = tm_lru // SUBLANES

    def project(k):
        cols = slice(bounds[k], bounds[k + 1])
        acc = jnp.dot(x_ref[...], w_ref[:, cols], preferred_element_type=F32)
        qkv_ref[:, cols] = (acc * s_ref[:, cols]).astype(qkv_ref.dtype)

    _lru_begin_tile(first, xext_sc, carry_sc, tm=tm_lru)
    project(0)
    xc, r_pre, i_pre = _lru_gates(rx_ref, cw_ref, cb_ref, wr_ref, wi_ref, xext_sc, tm=tm_lru)
    project(1)
    _lru_coefficients(xc, r_pre, i_pre, br_ref, bi_ref, lam_ref, a_sc, b_sc)
    project(2)
    _lru_recurrence(range(groups // 2), a_sc, b_sc, h_sc, carry_sc)
    project(3)
    _lru_recurrence(range(groups // 2, groups), a_sc, b_sc, h_sc, carry_sc)
    olru_ref[...] = (h_sc[...] * _gelu_tanh(ry_ref[...])).astype(olru_ref.dtype)


def _project_qkv_and_recur(xb, w_all, layer, colscale, uf, conv_w, conv_b, w_r, b_r, w_i, b_i, lam,
                           tm=1024, tn=2304, tm_lru=512):
    m, k = xb.shape
    n = w_all.shape[2]
    w = BRANCH_WIDTH
    nj = n // tn
    assert tm == nj * tm_lru
    kern = functools.partial(_proj_lru_kernel, tm_lru=tm_lru)
    row = lambda v: v.reshape(1, w)
    full2 = lambda shape: pl.BlockSpec(shape, lambda i, j: (0, 0))
    full3 = lambda shape: pl.BlockSpec(shape, lambda i, j: (0, 0, 0))
    return pl.pallas_call(
        kern,
        grid=(m // tm, nj),
        in_specs=[pl.BlockSpec((tm, k), lambda i, j: (i, 0)),
                  pl.BlockSpec((None, k, tn), lambda i, j: (layer, 0, j)),
                  pl.BlockSpec((1, tn), lambda i, j: (0, j)),
                  pl.BlockSpec((tm_lru, w), lambda i, j: (i * nj + j, 0)),
                  pl.BlockSpec((tm_lru, w), lambda i, j: (i * nj + j, 1)),
                  full2((CONV_WIDTH, w)), full2((1, w)),
                  full3((N_HEADS, HEAD_DIM, HEAD_DIM)), full2((1, w)),
                  full3((N_HEADS, HEAD_DIM, HEAD_DIM)), full2((1, w)),
                  full2((1, w))],
        out_specs=[pl.BlockSpec((tm, tn), lambda i, j: (i, j)),
                   pl.BlockSpec((tm_lru, w), lambda i, j: (i * nj + j, 0))],
        out_shape=[jax.ShapeDtypeStruct((m, n), BF16), jax.ShapeDtypeStruct((m, w), BF16)],
        scratch_shapes=[pltpu.VMEM((tm_lru + 2 * SUBLANES, w), F32),
                        pltpu.VMEM((tm_lru, w), F32),
                        pltpu.VMEM((tm_lru, w), F32),
                        pltpu.VMEM((tm_lru, w), F32),
                        pltpu.VMEM((SUBLANES, w), F32)],
        compiler_params=_cparams(("arbitrary", "arbitrary")),
        name="in_proj_qkv_recurrent",
    )(xb, w_all, colscale, uf, uf, conv_w, row(conv_b), w_r.astype(BF16), row(b_r),
      w_i.astype(BF16), row(b_i), row(lam))


def _merge_kernel(x_ref, o0_ref, o1_ref, o2_ref, o3_ref, wg_ref, bg_ref, wb_ref, wo_ref, w1_ref, w2_ref,
                  out_ref, wo_bf_ref, w1_bf_ref, w2_bf_ref):
    wo_bf_ref[...] = wo_ref[...].astype(BF16)
    w1_bf_ref[...] = w1_ref[...].astype(BF16)
    w2_bf_ref[...] = w2_ref[...].astype(BF16)
    x = x_ref[...]
    merged = None
    for g, o_ref in enumerate((o0_ref, o1_ref, o2_ref, o3_ref)):
        gate = jax.nn.sigmoid(jnp.dot(x, wg_ref[g], preferred_element_type=F32) + bg_ref[g:g + 1, :])
        term = gate * jnp.dot(o_ref[...], wb_ref[g], preferred_element_type=F32)
        merged = term if merged is None else merged + term
    out_ref[...] = merged.astype(out_ref.dtype)


def _merge(xb, branches, wg, bg_all, wb, layer, later_weights, tm=1024):
    s, d = xb.shape
    w = BRANCH_WIDTH
    nj, _, _, tn = wg.shape
    ni = s // tm
    o_spec = pl.BlockSpec((tm, w), lambda i, j: (i, 0))
    cast_in, cast_out, cast_shapes = [], [], []
    for arr, column_major in later_weights:
        _, r, c = arr.shape
        cast_in.append(pl.BlockSpec((None, r // ni, c // nj), lambda i, j: (layer, i, j)))
        if column_major:
            cast_out.append(pl.BlockSpec((None, r // ni, c // nj), lambda i, j: (j, i, 0)))
            cast_shapes.append(jax.ShapeDtypeStruct((nj, r, c // nj), BF16))
        else:
            cast_out.append(pl.BlockSpec((r // ni, c // nj), lambda i, j: (i, j)))
            cast_shapes.append(jax.ShapeDtypeStruct((r, c), BF16))
    return pl.pallas_call(
        _merge_kernel,
        grid=(ni, nj),
        in_specs=[pl.BlockSpec((tm, d), lambda i, j: (i, 0)),
                  o_spec, o_spec, o_spec, o_spec,
                  pl.BlockSpec((None, N_BRANCH, d, tn), lambda i, j: (j, 0, 0, 0)),
                  pl.BlockSpec((None, N_BRANCH, tn), lambda i, j: (layer, 0, j)),
                  pl.BlockSpec((None, N_BRANCH, w, tn), lambda i, j: (j, 0, 0, 0))] + cast_in,
        out_specs=[pl.BlockSpec((tm, tn), lambda i, j: (i, j))] + cast_out,
        out_shape=[jax.ShapeDtypeStruct((s, d), BF16)] + cast_shapes,
        compiler_params=_cparams(("parallel", "arbitrary")),
        name="gated_merge",
    )(xb, *branches, wg, bg_all, wb, *[arr for arr, _ in later_weights])


def _outproj_kernel(m_ref, w_ref, x_ref, pre_g_ref, pre_b_ref, g_ref, b_ref, of_ref, ob_ref, *, prenorm):
    half = m_ref.shape[0] // 2
    rows = (slice(0, half), slice(half, 2 * half))
    proj = [jnp.dot(m_ref[r, :], w_ref[...], preferred_element_type=F32) for r in rows]
    for r, p in zip(rows, proj):
        x = x_ref[r, :]
        if prenorm:
            x = _layer_norm_rows(x, pre_g_ref[...], pre_b_ref[...])
        y = _layer_norm_rows(ALPHA * x + p, g_ref[...], b_ref[...])
        of_ref[r, :] = y
        ob_ref[r, :] = y.astype(BF16)


def _outproj_ln(merged, w_out, x, pre_g, pre_b, prenorm, g, b, tm=512):
    s, d = x.shape
    row_spec = pl.BlockSpec((tm, d), lambda i: (i, 0))
    vec_spec = pl.BlockSpec((1, d), lambda i: (0, 0))
    return pl.pallas_call(
        functools.partial(_outproj_kernel, prenorm=prenorm),
        grid=(s // tm,),
        in_specs=[row_spec, pl.BlockSpec((d, d), lambda i: (0, 0)), row_spec, vec_spec, vec_spec,
                  vec_spec, vec_spec],
        out_specs=[row_spec, row_spec],
        out_shape=[jax.ShapeDtypeStruct((s, d), F32), jax.ShapeDtypeStruct((s, d), BF16)],
        compiler_params=_cparams(("parallel",)),
        name="outproj_ln",
    )(merged, w_out, x, pre_g.reshape(1, d), pre_b.reshape(1, d), g.reshape(1, d), b.reshape(1, d))


def _ffn_kernel(xb_ref, xf_ref, w1_ref, w2_ref, g_ref, b_ref, of_ref, ob_ref, acc_sc):
    f = pl.program_id(1)
    last = pl.num_programs(1) - 1

    def hidden():
        hid = jnp.maximum(jnp.dot(xb_ref[...], w1_ref[...], preferred_element_type=F32), 0.0)
        return (hid * hid).astype(BF16)

    @pl.when(f == 0)
    def _first():
        acc_sc[...] = jnp.dot(hidden(), w2_ref[...], preferred_element_type=F32)

    @pl.when(jnp.logical_and(f > 0, f < last))
    def _middle():
        acc_sc[...] += jnp.dot(hidden(), w2_ref[...], preferred_element_type=F32)

    @pl.when(f == last)
    def _last():
        hid = hidden()
        half = hid.shape[0] // 2
        rows = (slice(0, half), slice(half, 2 * half))
        down = [jnp.dot(hid[r, :], w2_ref[...], preferred_element_type=F32) for r in rows]
        for r, dn in zip(rows, down):
            y = _layer_norm_rows(ALPHA * xf_ref[r, :] + (acc_sc[r, :] + dn), g_ref[...], b_ref[...])
            of_ref[r, :] = y
            ob_ref[r, :] = y.astype(BF16)


def _ffn_ln(xb, xf, w1, w2, g, b, tm=512):
    s, d = xf.shape
    nf, _, tf = w1.shape
    row_spec = pl.BlockSpec((tm, d), lambda i, f: (i, 0))
    vec_spec = pl.BlockSpec((1, d), lambda i, f: (0, 0))
    return pl.pallas_call(
        _ffn_kernel,
        grid=(s // tm, nf),
        in_specs=[row_spec, row_spec,
                  pl.BlockSpec((None, d, tf), lambda i, f: (f, 0, 0)),
                  pl.BlockSpec((tf, d), lambda i, f: (f, 0)),
                  vec_spec, vec_spec],
        out_specs=[row_spec, row_spec],
        out_shape=[jax.ShapeDtypeStruct((s, d), F32), jax.ShapeDtypeStruct((s, d), BF16)],
        scratch_shapes=[pltpu.VMEM((tm, d), F32)],
        compiler_params=_cparams(("parallel", "arbitrary")),
        name="ffn_ln",
    )(xb, xf, w1, w2, g.reshape(1, d), b.reshape(1, d))


def _split_in_proj(w_in):
    qkv = jnp.concatenate([w_in[:, :, _OFF_FQ:_OFF_FF], w_in[:, :, _OFF_SQ:_OFF_END]], axis=2)
    pad = jnp.zeros(w_in.shape[:2] + (LANES - N_HEADS,), w_in.dtype)
    rest = jnp.concatenate([w_in[:, :, _OFF_RX:_OFF_SQ], w_in[:, :, _OFF_FF:_OFF_RX], pad], axis=2)
    return qkv.astype(BF16), rest.astype(BF16)


def kernel(x, ln_in_g, ln_in_b, w_in, b_forget, conv_w, conv_b, w_r, b_r, w_i, b_i, lru_lambda,
           rel_bias, w_branch, w_gate, b_gate, w_out, ln1_g, ln1_b, w_ff1, w_ff2, ln2_g, ln2_b):
    batch, s, d = x.shape
    assert (batch, s, d) == (1, SEQ, D_MODEL)
    w = BRANCH_WIDTH
    band_tq = 256
    merge_blocks = 8

    col_scale = np.ones((1, 9 * w), np.float32)
    for q_block in (0, 3, 6):
        col_scale[:, q_block * w:(q_block + 1) * w] = QK_SCALE * LOG2_E
    qkv_scale = jnp.asarray(col_scale)
    rest_scale = jnp.ones((1, 2 * w + LANES), F32)

    w_qkv, w_rest = _split_in_proj(w_in)
    w_gate_rows = w_gate.reshape(DEPTH, N_BRANCH * d, d)
    w_branch_rows = w_branch.reshape(DEPTH, N_BRANCH * w, d)

    xf = x.reshape(s, d)
    xb = _entry_ln(xf, ln_in_g, ln_in_b)
    for l in range(DEPTH):
        uf = _project(xb, w_rest, l, rest_scale, F32, 1024, 1152, "in_proj_rest")
        qkv, o_lru = _project_qkv_and_recur(xb, w_qkv, l, qkv_scale, uf, conv_w[l], conv_b[l], w_r[l],
                                            b_r[l], w_i[l], b_i[l], lru_lambda[l])

        f_rows = uf[:, 2 * w:2 * w + N_HEADS].T.reshape(N_HEADS * (s // LANES), LANES)
        b_rows = jnp.repeat(b_forget[l].astype(F32), s // LANES).reshape(-1, 1)
        cf = _forget_cumsum(f_rows, b_rows).reshape(N_HEADS, s)
        cfk = jnp.pad(cf, ((0, SUBLANES - N_HEADS), (0, 0)))

        o_fox, wg_b, wb_b = _fox_attention(qkv, cfk, 0, l, (w_gate_rows, w_branch_rows), merge_blocks)
        o_sb = _sb_attention(qkv, 3)
        o_ch = _band_attention(qkv, _band_bias_vector(rel_bias[l], band_tq), 6, band_tq)

        merged, wo_b, w1_b, w2_b = _merge(
            xb, (o_fox, o_lru, o_sb, o_ch),
            wg_b.reshape(merge_blocks, N_BRANCH, d, d // merge_blocks), b_gate,
            wb_b.reshape(merge_blocks, N_BRANCH, w, d // merge_blocks), l,
            ((w_out, False), (w_ff1, True), (w_ff2, False)))
        xf, xb = _outproj_ln(merged, wo_b, xf, ln_in_g, ln_in_b, l == 0, ln1_g[l], ln1_b[l])
        xf, xb = _ffn_ln(xb, xf, w1_b, w2_b, ln2_g[l], ln2_b[l])
    return xf.reshape(batch, s, d)
```

```python
import functools
import math

import jax
import jax.numpy as jnp
import numpy as np
from jax import lax
from jax.experimental import pallas as pl
from jax.experimental.pallas import tpu as pltpu

F32 = jnp.float32
BF16 = jnp.bfloat16

D_MODEL = 2048
SEQ = 8192
DEPTH = 2
CHUNK = 64
HEAD_DIM = 128
N_BRANCH = 4
BRANCH_WIDTH = D_MODEL // N_BRANCH
N_HEADS = BRANCH_WIDTH // HEAD_DIM
CONV_WIDTH = 4
LRU_C = 8.0
LOOKBACK_CHUNKS = 8
REL_CLIP = 256
D_FF = 4 * D_MODEL
ALPHA = (2.0 * DEPTH) ** 0.25
LN_EPS = 1e-5
QK_SCALE = HEAD_DIM ** -0.5
LOG2_E = math.log2(math.e)

_OFF_FQ = 0
_OFF_FK = _OFF_FQ + BRANCH_WIDTH
_OFF_FV = _OFF_FK + BRANCH_WIDTH
_OFF_FF = _OFF_FV + BRANCH_WIDTH
_OFF_RX = _OFF_FF + N_HEADS
_OFF_RY = _OFF_RX + BRANCH_WIDTH
_OFF_SQ = _OFF_RY + BRANCH_WIDTH
_OFF_CQ = _OFF_SQ + 3 * BRANCH_WIDTH
_OFF_END = _OFF_CQ + 3 * BRANCH_WIDTH

LANES = 128
SUBLANES = 8
NEG_BIG = -1e30
SB_DEAD_LOG2 = -180.0
FOX_DEAD_LOG2 = -170.0

VMEM_LIMIT = 56 * 1024 * 1024


def _cparams(sem, vmem=VMEM_LIMIT):
    return pltpu.CompilerParams(dimension_semantics=sem, vmem_limit_bytes=vmem)


def _log_sigmoid(x):
    return jnp.minimum(x, 0.0) - jnp.log1p(jnp.exp(-jnp.abs(x)))


def _layer_norm_rows(y, g, b):
    mu = jnp.mean(y, axis=-1, keepdims=True)
    d = y - mu
    var = jnp.mean(d * d, axis=-1, keepdims=True)
    return d * lax.rsqrt(var + LN_EPS) * g + b


def _ln_kernel(x_ref, g_ref, b_ref, ob_ref):
    ob_ref[...] = _layer_norm_rows(x_ref[...], g_ref[...], b_ref[...]).astype(BF16)


def _entry_ln(x, g, b, tm=512):
    s, d = x.shape
    return pl.pallas_call(
        _ln_kernel,
        grid=(s // tm,),
        in_specs=[pl.BlockSpec((tm, d), lambda i: (i, 0)),
                  pl.BlockSpec((1, d), lambda i: (0, 0)),
                  pl.BlockSpec((1, d), lambda i: (0, 0))],
        out_specs=pl.BlockSpec((tm, d), lambda i: (i, 0)),
        out_shape=jax.ShapeDtypeStruct((s, d), BF16),
        compiler_params=_cparams(("parallel",)),
        name="entry_ln",
    )(x, g.reshape(1, d), b.reshape(1, d))


def _proj_kernel(x_ref, w_ref, s_ref, o_ref):
    acc = jnp.dot(x_ref[...], w_ref[...], preferred_element_type=F32)
    o_ref[...] = (acc * s_ref[...]).astype(o_ref.dtype)


def _project(xb, w_all, layer, colscale, out_dtype, tm, tn, name):
    m, k = xb.shape
    n = w_all.shape[2]
    return pl.pallas_call(
        _proj_kernel,
        grid=(m // tm, n // tn),
        in_specs=[pl.BlockSpec((tm, k), lambda i, j: (i, 0)),
                  pl.BlockSpec((None, k, tn), lambda i, j: (layer, 0, j)),
                  pl.BlockSpec((1, tn), lambda i, j: (0, j))],
        out_specs=pl.BlockSpec((tm, tn), lambda i, j: (i, j)),
        out_shape=jax.ShapeDtypeStruct((m, n), out_dtype),
        compiler_params=_cparams(("parallel", "arbitrary")),
        name=name,
    )(xb, w_all, colscale)


def _forget_cumsum_kernel(f_ref, b_ref, o_ref):
    rows = f_ref.shape[0]
    per_head = rows // N_HEADS
    ls = _log_sigmoid(f_ref[...] + b_ref[...])
    r = lax.broadcasted_iota(jnp.int32, (LANES, LANES), 0)
    c = lax.broadcasted_iota(jnp.int32, (LANES, LANES), 1)
    upper = (r <= c).astype(F32)
    within = jnp.dot(ls, upper, preferred_element_type=F32,
                     precision=lax.Precision.HIGHEST)
    total = within[:, LANES - 1:LANES]
    rr = lax.broadcasted_iota(jnp.int32, (rows, rows), 0)
    cc = lax.broadcasted_iota(jnp.int32, (rows, rows), 1)
    head_start = rr - (rr & (per_head - 1))
    before = ((cc >= head_start) & (cc < rr)).astype(F32)
    offs = jnp.dot(before, jnp.broadcast_to(total, (rows, LANES)),
                   preferred_element_type=F32, precision=lax.Precision.HIGHEST)
    o_ref[...] = (within + offs) * LOG2_E


def _forget_cumsum(f_rows, b_rows):
    rows = f_rows.shape[0]
    return pl.pallas_call(
        _forget_cumsum_kernel,
        out_shape=jax.ShapeDtypeStruct((rows, LANES), F32),
        name="forget_cumsum",
    )(f_rows, b_rows)


def _fox_kernel(q_ref, k_ref, v_ref, cfk_ref, wg_ref, wb_ref, o_ref, wg_bf_ref, wb_bf_ref,
                m_sc, acc_sc, vaug_sc, qn_sc, kn_sc, *, tq, tk):
    i = pl.program_id(0)
    for src, dst in ((wg_ref, wg_bf_ref), (wb_ref, wb_bf_ref)):
        width = dst.shape[2]
        for c in range(dst.shape[0]):
            dst[c] = src[:, c * width:(c + 1) * width].astype(BF16)
    heads = [slice(h * HEAD_DIM, (h + 1) * HEAD_DIM) for h in range(N_HEADS)]
    norm_rows = 1024

    @pl.when(i == 0)
    def _largest_key_norm():
        for h, hs in enumerate(heads):
            def chunk(c, best, hs=hs):
                rows = k_ref[pl.ds(pl.multiple_of(c * norm_rows, norm_rows), norm_rows), hs].astype(F32)
                return jnp.maximum(best, jnp.max(jnp.sum(rows * rows, axis=-1, keepdims=True)))
            best = lax.fori_loop(0, k_ref.shape[0] // norm_rows, chunk, jnp.zeros((SUBLANES, LANES), F32))
            kn_sc[h] = jnp.sqrt(best)

    m_sc[...] = jnp.full(m_sc.shape, NEG_BIG, F32)
    acc_sc[...] = jnp.zeros(acc_sc.shape, F32)
    vaug_sc[:, :, HEAD_DIM:] = jnp.ones((N_HEADS, tk, HEAD_DIM), BF16)
    for h, hs in enumerate(heads):
        q = q_ref[:, hs].astype(F32)
        qn_sc[h] = jnp.broadcast_to(jnp.sqrt(jnp.sum(q * q, axis=-1, keepdims=True)), (tq, LANES))
    rep = tk // LANES

    def key_tile(j, masked, hlist):
        k0 = pl.multiple_of(j * tk, tk)
        if masked:
            keep = (lax.broadcasted_iota(jnp.int32, (tq, tk), 1)
                    <= lax.broadcasted_iota(jnp.int32, (tq, tk), 0))
        scores = [lax.dot_general(q_ref[:, heads[h]], k_ref[pl.ds(k0, tk), heads[h]],
                                  (((1,), (1,)), ((), ())), preferred_element_type=F32) for h in hlist]
        probs, alphas = [], []
        for h, s in zip(hlist, scores):
            s = s - cfk_ref[h:h + 1, pl.ds(k0, tk)]
            if masked:
                s = jnp.where(keep, s, NEG_BIG)
            m_old = m_sc[h]
            m_new = jnp.maximum(m_old, jnp.max(s, axis=-1, keepdims=True))
            alphas.append(jnp.exp2(m_old - m_new))
            probs.append(jnp.exp2(s - jnp.concatenate([m_new] * rep, axis=1)).astype(BF16))
            m_sc[h] = m_new
            vaug_sc[h, :, :HEAD_DIM] = v_ref[pl.ds(k0, tk), heads[h]]
        for h, p, alpha in zip(hlist, probs, alphas):
            pv = jnp.dot(p, vaug_sc[h], preferred_element_type=F32)
            acc_sc[h] = jnp.concatenate([alpha, alpha], axis=1) * acc_sc[h] + pv

    def alive(j, hlist):
        newest = pl.multiple_of(jnp.maximum(j, 0) * tk + tk - LANES, LANES)
        reach = None
        for h in hlist:
            decay = -cfk_ref[h:h + 1, pl.ds(newest, LANES)][:, LANES - 1:]
            bound = jnp.max(qn_sc[h] * kn_sc[h, 0:1, :] + decay - m_sc[h], axis=0, keepdims=True)
            reach = bound if reach is None else jnp.minimum(reach, bound)
        return (jnp.max(reach) > FOX_DEAD_LOG2).astype(jnp.int32)

    def walk_back(j_start, hlist):
        def earlier(state):
            j, _ = state
            key_tile(j, False, hlist)
            return j - 1, alive(j - 1, hlist)
        return lax.while_loop(lambda st: jnp.logical_and(st[0] >= 0, st[1] > 0), earlier,
                              (j_start, alive(j_start, hlist)))[0]

    all_heads = list(range(N_HEADS))
    key_tile(i, True, all_heads)
    j_split = walk_back(i - 1, all_heads)
    for h in all_heads:
        walk_back(j_split, [h])
    for h in range(N_HEADS):
        hs = slice(h * HEAD_DIM, (h + 1) * HEAD_DIM)
        o_ref[:, hs] = (acc_sc[h, :, :HEAD_DIM] / acc_sc[h, :, HEAD_DIM:]).astype(o_ref.dtype)


def _fox_attention(qkv, cfk, col0, layer, later_weights, col_blocks, tq=512):
    s = qkv.shape[0]
    tk = tq
    w = BRANCH_WIDTH
    steps = s // tq
    kern = functools.partial(_fox_kernel, tq=tq, tk=tk)
    resident = pl.Buffered(1)
    cast_in, cast_out, cast_shapes = [], [], []
    for arr in later_weights:
        _, r, c = arr.shape
        cast_in.append(pl.BlockSpec((None, r // steps, c), lambda i: (layer, i, 0)))
        cast_out.append(pl.BlockSpec((col_blocks, r // steps, c // col_blocks), lambda i: (0, i, 0)))
        cast_shapes.append(jax.ShapeDtypeStruct((col_blocks, r, c // col_blocks), BF16))
    return pl.pallas_call(
        kern,
        grid=(steps,),
        in_specs=[pl.BlockSpec((tq, w), lambda i: (i, col0)),
                  pl.BlockSpec((s, w), lambda i: (0, col0 + 1), pipeline_mode=resident),
                  pl.BlockSpec((s, w), lambda i: (0, col0 + 2), pipeline_mode=resident),
                  pl.BlockSpec((SUBLANES, s), lambda i: (0, 0), pipeline_mode=resident)] + cast_in,
        out_specs=[pl.BlockSpec((tq, w), lambda i: (i, 0))] + cast_out,
        out_shape=[jax.ShapeDtypeStruct((s, w), BF16)] + cast_shapes,
        scratch_shapes=[pltpu.VMEM((N_HEADS, tq, LANES), F32),
                        pltpu.VMEM((N_HEADS, tq, 2 * HEAD_DIM), F32),
                        pltpu.VMEM((N_HEADS, tk, 2 * HEAD_DIM), BF16),
                        pltpu.VMEM((N_HEADS, tq, LANES), F32),
                        pltpu.VMEM((N_HEADS, SUBLANES, LANES), F32)],
        compiler_params=_cparams(("arbitrary",)),
        name="fox_attention",
    )(qkv, qkv, qkv, cfk, *later_weights)


def _sb_kernel(q_ref, k_ref, v_ref, o_ref, run_sc, acc_sc, *, tq):
    i = pl.program_id(0)
    nsub = tq // LANES
    r = lax.broadcasted_iota(jnp.int32, (2 * LANES, 2 * LANES), 0) & (LANES - 1)
    c = lax.broadcasted_iota(jnp.int32, (2 * LANES, 2 * LANES), 1)
    tri_aug = ((c >= LANES) | (r > c)).astype(BF16)

    def sub_blocks(items, masked):
        scores = []
        for h, k0, r0 in items:
            hs = slice(h * HEAD_DIM, (h + 1) * HEAD_DIM)
            scores.append(lax.dot_general(q_ref[r0:, hs], k_ref[pl.ds(k0, LANES), hs],
                                          (((1,), (1,)), ((), ())), preferred_element_type=F32))
        log_beta, sums, keeps = [], [], []
        for (h, k0, r0), z in zip(items, scores):
            rows = tq - r0
            lp = jnp.minimum(z, 0.0) - jnp.log2(1.0 + jnp.exp2(-jnp.abs(z)))
            ln = lp - z
            keep = None
            if masked:
                keep = (lax.broadcasted_iota(jnp.int32, (rows, LANES), 1)
                        < lax.broadcasted_iota(jnp.int32, (rows, LANES), 0))
                ln = jnp.where(keep, ln, 0.0)
            ln_hi = ln.astype(BF16)
            ln_lo = (ln - ln_hi.astype(F32)).astype(BF16)
            log_beta.append(lp)
            keeps.append(keep)
            sums.append(jnp.dot(jnp.concatenate([ln_hi, ln_lo], axis=1), tri_aug,
                                preferred_element_type=F32))
        weights = []
        for (h, k0, r0), lp, la, keep in zip(items, log_beta, sums, keeps):
            run = run_sc[h, r0:, :]
            a = jnp.exp2(lp + la[:, :LANES] + run)
            if masked:
                a = jnp.where(keep, a, 0.0)
            run_sc[h, r0:, :] = run + la[:, LANES:]
            weights.append(a.astype(BF16))
        for (h, k0, r0), a in zip(items, weights):
            hs = slice(h * HEAD_DIM, (h + 1) * HEAD_DIM)
            acc_sc[h, r0:, :] += jnp.dot(a, v_ref[pl.ds(k0, LANES), hs], preferred_element_type=F32)

    run_sc[...] = jnp.zeros(run_sc.shape, F32)
    acc_sc[...] = jnp.zeros(acc_sc.shape, F32)
    sub_blocks([(h, pl.multiple_of(i * tq + cc * LANES, LANES), cc * LANES)
                for cc in range(nsub - 1, -1, -1) for h in range(N_HEADS)], True)

    def alive():
        return (jnp.max(run_sc[...]) > SB_DEAD_LOG2).astype(jnp.int32)

    def earlier(state):
        cb, _ = state
        sub_blocks([(h, pl.multiple_of((cb - back) * LANES, LANES), 0)
                    for back in range(2) for h in range(N_HEADS)], False)
        return cb - 2, alive()

    lax.while_loop(lambda st: jnp.logical_and(st[0] >= 0, st[1] > 0), earlier,
                   (i * nsub - 1, alive()))
    for h in range(N_HEADS):
        o_ref[:, h * HEAD_DIM:(h + 1) * HEAD_DIM] = acc_sc[h].astype(o_ref.dtype)


def _sb_attention(qkv, col0, tq=512):
    s = qkv.shape[0]
    w = BRANCH_WIDTH
    assert (tq // LANES) % 2 == 0
    kern = functools.partial(_sb_kernel, tq=tq)
    resident = pl.Buffered(1)
    return pl.pallas_call(
        kern,
        grid=(s // tq,),
        in_specs=[pl.BlockSpec((tq, w), lambda i: (i, col0)),
                  pl.BlockSpec((s, w), lambda i: (0, col0 + 1), pipeline_mode=resident),
                  pl.BlockSpec((s, w), lambda i: (0, col0 + 2), pipeline_mode=resident)],
        out_specs=pl.BlockSpec((tq, w), lambda i: (i, 0)),
        out_shape=jax.ShapeDtypeStruct((s, w), BF16),
        scratch_shapes=[pltpu.VMEM((N_HEADS, tq, LANES), F32),
                        pltpu.VMEM((N_HEADS, tq, HEAD_DIM), F32)],
        compiler_params=_cparams(("arbitrary",)),
        name="sb_attention",
    )(qkv, qkv, qkv)


def _band_kernel(q_ref, k2_ref, k1_ref, k0_ref, v2_ref, v1_ref, v0_ref, ext_ref, o_ref,
                 bias_sc, vaug_sc, *, tq):
    i = pl.program_id(0)
    width = 4 * tq

    @pl.when(i == 0)
    def _build_tables():
        trow = lax.broadcasted_iota(jnp.int32, (tq, width), 0)
        t = lax.broadcasted_iota(jnp.int32, (tq, 3 * tq), 0)
        s = lax.broadcasted_iota(jnp.int32, (tq, 3 * tq), 1)
        shift = CHUNK.bit_length() - 1
        t_chunk = t >> shift
        s_chunk = (s >> shift) - (2 * tq) // CHUNK
        in_band = (t_chunk - s_chunk <= LOOKBACK_CHUNKS) & (s_chunk <= t_chunk)
        for h in range(N_HEADS):
            x = jnp.broadcast_to(ext_ref[h:h + 1, :], (tq, width))
            for b in range(tq.bit_length() - 1):
                x = jnp.where(((trow >> b) & 1) == 1, pltpu.roll(x, 1 << b, axis=1), x)
            bias_sc[h] = jnp.where(in_band, x[:, :3 * tq] * LOG2_E, NEG_BIG)
        vaug_sc[:, :, HEAD_DIM:] = jnp.ones((N_HEADS, 3 * tq, HEAD_DIM), BF16)

    k_refs = (k2_ref, k1_ref, k0_ref)
    v_refs = (v2_ref, v1_ref, v0_ref)
    heads = [slice(h * HEAD_DIM, (h + 1) * HEAD_DIM) for h in range(N_HEADS)]

    def tile(first_tiles):
        scores = [[lax.dot_general(q_ref[:, hs], k_refs[p][:, hs], (((1,), (1,)), ((), ())),
                                   preferred_element_type=F32) for p in range(3)] for hs in heads]
        probs = []
        for h, pieces in enumerate(scores):
            pieces = [s + bias_sc[h, :, p * tq:(p + 1) * tq] for p, s in enumerate(pieces)]
            if first_tiles:
                pieces = [jnp.where(i - 2 + p >= 0, s, NEG_BIG) for p, s in enumerate(pieces)]
            m = jnp.max(jnp.maximum(jnp.maximum(pieces[0], pieces[1]), pieces[2]),
                        axis=-1, keepdims=True)
            probs.append(jnp.concatenate([jnp.exp2(s - m).astype(BF16) for s in pieces], axis=1))
            for p in range(3):
                vaug_sc[h, p * tq:(p + 1) * tq, :HEAD_DIM] = v_refs[p][:, heads[h]]
        for h, p in enumerate(probs):
            acc = jnp.dot(p, vaug_sc[h], preferred_element_type=F32)
            o_ref[:, heads[h]] = (acc[:, :HEAD_DIM] / acc[:, HEAD_DIM:]).astype(o_ref.dtype)

    @pl.when(i < 2)
    def _first_tiles():
        tile(True)

    @pl.when(i >= 2)
    def _other_tiles():
        tile(False)


def _band_bias_vector(rel_bias_l, tq):
    n = np.arange(4 * tq)
    dist = np.where(n < 3 * tq, 2 * tq - n, 6 * tq - n)
    ridx = np.clip(dist, -(CHUNK - 1), REL_CLIP) + (CHUNK - 1)
    return rel_bias_l.astype(F32)[:, ridx]


def _band_attention(qkv, bias_ext, col0, tq=256):
    s = qkv.shape[0]
    w = BRANCH_WIDTH
    assert 2 * tq >= LOOKBACK_CHUNKS * CHUNK and tq % CHUNK == 0 and tq & (tq - 1) == 0
    kern = functools.partial(_band_kernel, tq=tq)

    def kv_spec(back, col):
        return pl.BlockSpec((tq, w), lambda i: (jnp.maximum(i - back, 0), col))

    return pl.pallas_call(
        kern,
        grid=(s // tq,),
        in_specs=[pl.BlockSpec((tq, w), lambda i: (i, col0)),
                  kv_spec(2, col0 + 1), kv_spec(1, col0 + 1), kv_spec(0, col0 + 1),
                  kv_spec(2, col0 + 2), kv_spec(1, col0 + 2), kv_spec(0, col0 + 2),
                  pl.BlockSpec((N_HEADS, 4 * tq), lambda i: (0, 0))],
        out_specs=pl.BlockSpec((tq, w), lambda i: (i, 0)),
        out_shape=jax.ShapeDtypeStruct((s, w), BF16),
        scratch_shapes=[pltpu.VMEM((N_HEADS, tq, 3 * tq), F32),
                        pltpu.VMEM((N_HEADS, 3 * tq, 2 * HEAD_DIM), BF16)],
        compiler_params=_cparams(("arbitrary",)),
        name="band_attention",
    )(qkv, qkv, qkv, qkv, qkv, qkv, qkv, bias_ext)


def _gelu_tanh(x):
    c = math.sqrt(2.0 / math.pi)
    return 0.5 * x * (1.0 + jnp.tanh(c * (x + 0.044715 * (x * x * x))))


def _lru_begin_tile(first, xext_sc, carry_sc, *, tm):
    halo = SUBLANES

    @pl.when(first)
    def _first():
        xext_sc[0:halo, :] = jnp.zeros((halo, BRANCH_WIDTH), F32)
        carry_sc[...] = jnp.zeros(carry_sc.shape, F32)

    @pl.when(jnp.logical_not(first))
    def _shift_halo():
        xext_sc[0:halo, :] = xext_sc[tm:tm + halo, :]


def _lru_gates(rx_ref, cw_ref, cb_ref, wr_ref, wi_ref, xext_sc, *, tm):
    halo = SUBLANES
    xext_sc[halo:halo + tm, :] = rx_ref[...]
    xext = xext_sc[0:halo + tm, :]
    xc = cb_ref[...] + rx_ref[...] * cw_ref[CONV_WIDTH - 1:CONV_WIDTH, :]
    for back in range(1, CONV_WIDTH):
        t = CONV_WIDTH - 1 - back
        xc = xc + pltpu.roll(xext, back, axis=0)[halo:, :] * cw_ref[t:t + 1, :]
    xcb = xc.astype(BF16)
    r_parts, i_parts = [], []
    for n in range(N_HEADS):
        ns = slice(n * HEAD_DIM, (n + 1) * HEAD_DIM)
        r_parts.append(jnp.dot(xcb[:, ns], wr_ref[n], preferred_element_type=F32))
        i_parts.append(jnp.dot(xcb[:, ns], wi_ref[n], preferred_element_type=F32))
    return xc, jnp.concatenate(r_parts, axis=1), jnp.concatenate(i_parts, axis=1)


def _lru_coefficients(xc, r_pre, i_pre, br_ref, bi_ref, lam_ref, a_sc, b_sc):
    r = jax.nn.sigmoid(r_pre + br_ref[...])
    gi = jax.nn.sigmoid(i_pre + bi_ref[...])
    log_a = LRU_C * r * _log_sigmoid(lam_ref[...])
    a = jnp.exp(log_a)
    a_sc[...] = a
    b_sc[...] = jnp.sqrt(-jnp.tanh(log_a) * (a * a + 1.0)) * (gi * xc)


def _lru_recurrence(groups, a_sc, b_sc, h_sc, carry_sc):
    w = BRANCH_WIDTH
    row = lax.broadcasted_iota(jnp.int32, (SUBLANES, w), 0)
    carry = carry_sc[...]
    for g in groups:
        rows = slice(g * SUBLANES, (g + 1) * SUBLANES)
        a = a_sc[rows, :]
        b = b_sc[rows, :]
        for k in (1, 2, 4):
            a_prev = pltpu.roll(a, k, axis=0)
            b_prev = pltpu.roll(b, k, axis=0)
            ok = row >= k
            b = jnp.where(ok, a * b_prev + b, b)
            a = jnp.where(ok, a * a_prev, a)
        hgrp = a * carry + b
        h_sc[rows, :] = hgrp
        carry = jnp.broadcast_to(hgrp[SUBLANES - 1:SUBLANES, :], (SUBLANES, w))
    carry_sc[...] = carry


def _proj_lru_kernel(x_ref, w_ref, s_ref, rx_ref, ry_ref, cw_ref, cb_ref, wr_ref, br_ref, wi_ref,
                     bi_ref, lam_ref, qkv_ref, olru_ref, xext_sc, a_sc, b_sc, h_sc, carry_sc, *, tm_lru):
    first = jnp.logical_and(pl.program_id(0) == 0, pl.program_id(1) == 0)
    tn = qkv_ref.shape[1]
    bounds = [0, tn // 3] + [tn // 3 + (k + 1) * (2 * tn // 9) for k in range(3)]
    groups = tm_lru // SUBLANES

    def project(k):
        cols = slice(bounds[k], bounds[k + 1])
        acc = jnp.dot(x_ref[...], w_ref[:, cols], preferred_element_type=F32)
        qkv_ref[:, cols] = (acc * s_ref[:, cols]).astype(qkv_ref.dtype)

    _lru_begin_tile(first, xext_sc, carry_sc, tm=tm_lru)
    project(0)
    xc, r_pre, i_pre = _lru_gates(rx_ref, cw_ref, cb_ref, wr_ref, wi_ref, xext_sc, tm=tm_lru)
    project(1)
    _lru_coefficients(xc, r_pre, i_pre, br_ref, bi_ref, lam_ref, a_sc, b_sc)
    project(2)
    _lru_recurrence(range(groups // 2), a_sc, b_sc, h_sc, carry_sc)
    project(3)
    _lru_recurrence(range(groups // 2, groups), a_sc, b_sc, h_sc, carry_sc)
    olru_ref[...] = (h_sc[...] * _gelu_tanh(ry_ref[...])).astype(olru_ref.dtype)


def _project_qkv_and_recur(xb, w_all, layer, colscale, uf, conv_w, conv_b, w_r, b_r, w_i, b_i, lam,
                           tm=1024, tn=2304, tm_lru=512):
    m, k = xb.shape
    n = w_all.shape[2]
    w = BRANCH_WIDTH
    nj = n // tn
    assert tm == nj * tm_lru
    kern = functools.partial(_proj_lru_kernel, tm_lru=tm_lru)
    row = lambda v: v.reshape(1, w)
    full2 = lambda shape: pl.BlockSpec(shape, lambda i, j: (0, 0))
    full3 = lambda shape: pl.BlockSpec(shape, lambda i, j: (0, 0, 0))
    return pl.pallas_call(
        kern,
        grid=(m // tm, nj),
        in_specs=[pl.BlockSpec((tm, k), lambda i, j: (i, 0)),
                  pl.BlockSpec((None, k, tn), lambda i, j: (layer, 0, j)),
                  pl.BlockSpec((1, tn), lambda i, j: (0, j)),
                  pl.BlockSpec((tm_lru, w), lambda i, j: (i * nj + j, 0)),
                  pl.BlockSpec((tm_lru, w), lambda i, j: (i * nj + j, 1)),
                  full2((CONV_WIDTH, w)), full2((1, w)),
                  full3((N_HEADS, HEAD_DIM, HEAD_DIM)), full2((1, w)),
                  full3((N_HEADS, HEAD_DIM, HEAD_DIM)), full2((1, w)),
                  full2((1, w))],
        out_specs=[pl.BlockSpec((tm, tn), lambda i, j: (i, j)),
                   pl.BlockSpec((tm_lru, w), lambda i, j: (i * nj + j, 0))],
        out_shape=[jax.ShapeDtypeStruct((m, n), BF16), jax.ShapeDtypeStruct((m, w), BF16)],
        scratch_shapes=[pltpu.VMEM((tm_lru + 2 * SUBLANES, w), F32),
                        pltpu.VMEM((tm_lru, w), F32),
                        pltpu.VMEM((tm_lru, w), F32),
                        pltpu.VMEM((tm_lru, w), F32),
                        pltpu.VMEM((SUBLANES, w), F32)],
        compiler_params=_cparams(("arbitrary", "arbitrary")),
        name="in_proj_qkv_recurrent",
    )(xb, w_all, colscale, uf, uf, conv_w, row(conv_b), w_r.astype(BF16), row(b_r),
      w_i.astype(BF16), row(b_i), row(lam))


def _merge_kernel(x_ref, o0_ref, o1_ref, o2_ref, o3_ref, wg_ref, bg_ref, wb_ref, wo_ref, w1_ref, w2_ref,
                  out_ref, wo_bf_ref, w1_bf_ref, w2_bf_ref):
    wo_bf_ref[...] = wo_ref[...].astype(BF16)
    w1_bf_ref[...] = w1_ref[...].astype(BF16)
    w2_bf_ref[...] = w2_ref[...].astype(BF16)
    x = x_ref[...]
    merged = None
    for g, o_ref in enumerate((o0_ref, o1_ref, o2_ref, o3_ref)):
        gate = jax.nn.sigmoid(jnp.dot(x, wg_ref[g], preferred_element_type=F32) + bg_ref[g:g + 1, :])
        term = gate * jnp.dot(o_ref[...], wb_ref[g], preferred_element_type=F32)
        merged = term if merged is None else merged + term
    out_ref[...] = merged.astype(out_ref.dtype)


def _merge(xb, branches, wg, bg_all, wb, layer, later_weights, tm=1024):
    s, d = xb.shape
    w = BRANCH_WIDTH
    nj, _, _, tn = wg.shape
    ni = s // tm
    o_spec = pl.BlockSpec((tm, w), lambda i, j: (i, 0))
    cast_in, cast_out, cast_shapes = [], [], []
    for arr, column_major in later_weights:
        _, r, c = arr.shape
        cast_in.append(pl.BlockSpec((None, r // ni, c // nj), lambda i, j: (layer, i, j)))
        if column_major:
            cast_out.append(pl.BlockSpec((None, r // ni, c // nj), lambda i, j: (j, i, 0)))
            cast_shapes.append(jax.ShapeDtypeStruct((nj, r, c // nj), BF16))
        else:
            cast_out.append(pl.BlockSpec((r // ni, c // nj), lambda i, j: (i, j)))
            cast_shapes.append(jax.ShapeDtypeStruct((r, c), BF16))
    return pl.pallas_call(
        _merge_kernel,
        grid=(ni, nj),
        in_specs=[pl.BlockSpec((tm, d), lambda i, j: (i, 0)),
                  o_spec, o_spec, o_spec, o_spec,
                  pl.BlockSpec((None, N_BRANCH, d, tn), lambda i, j: (j, 0, 0, 0)),
                  pl.BlockSpec((None, N_BRANCH, tn), lambda i, j: (layer, 0, j)),
                  pl.BlockSpec((None, N_BRANCH, w, tn), lambda i, j: (j, 0, 0, 0))] + cast_in,
        out_specs=[pl.BlockSpec((tm, tn), lambda i, j: (i, j))] + cast_out,
        out_shape=[jax.ShapeDtypeStruct((s, d), BF16)] + cast_shapes,
        compiler_params=_cparams(("parallel", "arbitrary")),
        name="gated_merge",
    )(xb, *branches, wg, bg_all, wb, *[arr for arr, _ in later_weights])


def _outproj_kernel(m_ref, w_ref, x_ref, pre_g_ref, pre_b_ref, g_ref, b_ref, of_ref, ob_ref, *, prenorm):
    half = m_ref.shape[0] // 2
    rows = (slice(0, half), slice(half, 2 * half))
    proj = [jnp.dot(m_ref[r, :], w_ref[...], preferred_element_type=F32) for r in rows]
    for r, p in zip(rows, proj):
        x = x_ref[r, :]
        if prenorm:
            x = _layer_norm_rows(x, pre_g_ref[...], pre_b_ref[...])
        y = _layer_norm_rows(ALPHA * x + p, g_ref[...], b_ref[...])
        of_ref[r, :] = y
        ob_ref[r, :] = y.astype(BF16)


def _outproj_ln(merged, w_out, x, pre_g, pre_b, prenorm, g, b, tm=512):
    s, d = x.shape
    row_spec = pl.BlockSpec((tm, d), lambda i: (i, 0))
    vec_spec = pl.BlockSpec((1, d), lambda i: (0, 0))
    return pl.pallas_call(
        functools.partial(_outproj_kernel, prenorm=prenorm),
        grid=(s // tm,),
        in_specs=[row_spec, pl.BlockSpec((d, d), lambda i: (0, 0)), row_spec, vec_spec, vec_spec,
                  vec_spec, vec_spec],
        out_specs=[row_spec, row_spec],
        out_shape=[jax.ShapeDtypeStruct((s, d), F32), jax.ShapeDtypeStruct((s, d), BF16)],
        compiler_params=_cparams(("parallel",)),
        name="outproj_ln",
    )(merged, w_out, x, pre_g.reshape(1, d), pre_b.reshape(1, d), g.reshape(1, d), b.reshape(1, d))


def _ffn_kernel(xb_ref, xf_ref, w1_ref, w2_ref, g_ref, b_ref, of_ref, ob_ref, acc_sc):
    f = pl.program_id(1)
    last = pl.num_programs(1) - 1

    def hidden():
        hid = jnp.maximum(jnp.dot(xb_ref[...], w1_ref[...], preferred_element_type=F32), 0.0)
        return (hid * hid).astype(BF16)

    @pl.when(f == 0)
    def _first():
        acc_sc[...] = jnp.dot(hidden(), w2_ref[...], preferred_element_type=F32)

    @pl.when(jnp.logical_and(f > 0, f < last))
    def _middle():
        acc_sc[...] += jnp.dot(hidden(), w2_ref[...], preferred_element_type=F32)

    @pl.when(f == last)
    def _last():
        hid = hidden()
        half = hid.shape[0] // 2
        rows = (slice(0, half), slice(half, 2 * half))
        down = [jnp.dot(hid[r, :], w2_ref[...], preferred_element_type=F32) for r in rows]
        for r, dn in zip(rows, down):
            y = _layer_norm_rows(ALPHA * xf_ref[r, :] + (acc_sc[r, :] + dn), g_ref[...], b_ref[...])
            of_ref[r, :] = y
            ob_ref[r, :] = y.astype(BF16)


def _ffn_ln(xb, xf, w1, w2, g, b, tm=512):
    s, d = xf.shape
    nf, _, tf = w1.shape
    row_spec = pl.BlockSpec((tm, d), lambda i, f: (i, 0))
    vec_spec = pl.BlockSpec((1, d), lambda i, f: (0, 0))
    return pl.pallas_call(
        _ffn_kernel,
        grid=(s // tm, nf),
        in_specs=[row_spec,
                  pl.BlockSpec((tm, d), lambda i, f: (jnp.where(f >= nf // 2, i, jnp.maximum(i - 1, 0)), 0)),
                  pl.BlockSpec((None, d, tf), lambda i, f: (f, 0, 0)),
                  pl.BlockSpec((tf, d), lambda i, f: (f, 0)),
                  vec_spec, vec_spec],
        out_specs=[row_spec, row_spec],
        out_shape=[jax.ShapeDtypeStruct((s, d), F32), jax.ShapeDtypeStruct((s, d), BF16)],
        scratch_shapes=[pltpu.VMEM((tm, d), F32)],
        compiler_params=_cparams(("parallel", "arbitrary")),
        name="ffn_ln",
    )(xb, xf, w1, w2, g.reshape(1, d), b.reshape(1, d))


def _split_in_proj(w_in):
    qkv = jnp.concatenate([w_in[:, :, _OFF_FQ:_OFF_FF], w_in[:, :, _OFF_SQ:_OFF_END]], axis=2)
    pad = jnp.zeros(w_in.shape[:2] + (LANES - N_HEADS,), w_in.dtype)
    rest = jnp.concatenate([w_in[:, :, _OFF_RX:_OFF_SQ], w_in[:, :, _OFF_FF:_OFF_RX], pad], axis=2)
    return qkv.astype(BF16), rest.astype(BF16)


def kernel(x, ln_in_g, ln_in_b, w_in, b_forget, conv_w, conv_b, w_r, b_r, w_i, b_i, lru_lambda,
           rel_bias, w_branch, w_gate, b_gate, w_out, ln1_g, ln1_b, w_ff1, w_ff2, ln2_g, ln2_b):
    batch, s, d = x.shape
    assert (batch, s, d) == (1, SEQ, D_MODEL)
    w = BRANCH_WIDTH
    band_tq = 256
    merge_blocks = 8

    col_scale = np.ones((1, 9 * w), np.float32)
    for q_block in (0, 3, 6):
        col_scale[:, q_block * w:(q_block + 1) * w] = QK_SCALE * LOG2_E
    qkv_scale = jnp.asarray(col_scale)
    rest_scale = jnp.ones((1, 2 * w + LANES), F32)

    w_qkv, w_rest = _split_in_proj(w_in)
    w_gate_rows = w_gate.reshape(DEPTH, N_BRANCH * d, d)
    w_branch_rows = w_branch.reshape(DEPTH, N_BRANCH * w, d)

    xf = x.reshape(s, d)
    xb = _entry_ln(xf, ln_in_g, ln_in_b)
    for l in range(DEPTH):
        uf = _project(xb, w_rest, l, rest_scale, F32, 1024, 1152, "in_proj_rest")
        qkv, o_lru = _project_qkv_and_recur(xb, w_qkv, l, qkv_scale, uf, conv_w[l], conv_b[l], w_r[l],
                                            b_r[l], w_i[l], b_i[l], lru_lambda[l])

        f_rows = uf[:, 2 * w:2 * w + N_HEADS].T.reshape(N_HEADS * (s // LANES), LANES)
        b_rows = jnp.repeat(b_forget[l].astype(F32), s // LANES).reshape(-1, 1)
        cf = _forget_cumsum(f_rows, b_rows).reshape(N_HEADS, s)
        cfk = jnp.pad(cf, ((0, SUBLANES - N_HEADS), (0, 0)))

        o_fox, wg_b, wb_b = _fox_attention(qkv, cfk, 0, l, (w_gate_rows, w_branch_rows), merge_blocks)
        o_sb = _sb_attention(qkv, 3)
        o_ch = _band_attention(qkv, _band_bias_vector(rel_bias[l], band_tq), 6, band_tq)

        merged, wo_b, w1_b, w2_b = _merge(
            xb, (o_fox, o_lru, o_sb, o_ch),
            wg_b.reshape(merge_blocks, N_BRANCH, d, d // merge_blocks), b_gate,
            wb_b.reshape(merge_blocks, N_BRANCH, w, d // merge_blocks), l,
            ((w_out, False), (w_ff1, True), (w_ff2, False)))
        xf, xb = _outproj_ln(merged, wo_b, xf, ln_in_g, ln_in_b, l == 0, ln1_g[l], ln1_b[l])
        xf, xb = _ffn_ln(xb, xf, w1_b, w2_b, ln2_g[l], ln2_b[l])
    return xf.reshape(batch, s, d)
```

```python
import functools
import math

import jax
import jax.numpy as jnp
import numpy as np
from jax import lax
from jax.experimental import pallas as pl
from jax.experimental.pallas import tpu as pltpu

F32 = jnp.float32
BF16 = jnp.bfloat16

D_MODEL = 2048
SEQ = 8192
DEPTH = 2
CHUNK = 64
HEAD_DIM = 128
N_BRANCH = 4
BRANCH_WIDTH = D_MODEL // N_BRANCH
N_HEADS = BRANCH_WIDTH // HEAD_DIM
CONV_WIDTH = 4
LRU_C = 8.0
LOOKBACK_CHUNKS = 8
REL_CLIP = 256
D_FF = 4 * D_MODEL
ALPHA = (2.0 * DEPTH) ** 0.25
LN_EPS = 1e-5
QK_SCALE = HEAD_DIM ** -0.5
LOG2_E = math.log2(math.e)

_OFF_FQ = 0
_OFF_FK = _OFF_FQ + BRANCH_WIDTH
_OFF_FV = _OFF_FK + BRANCH_WIDTH
_OFF_FF = _OFF_FV + BRANCH_WIDTH
_OFF_RX = _OFF_FF + N_HEADS
_OFF_RY = _OFF_RX + BRANCH_WIDTH
_OFF_SQ = _OFF_RY + BRANCH_WIDTH
_OFF_CQ = _OFF_SQ + 3 * BRANCH_WIDTH
_OFF_END = _OFF_CQ + 3 * BRANCH_WIDTH

LANES = 128
SUBLANES = 8
NEG_BIG = -1e30
SB_DEAD_LOG2 = -180.0
FOX_DEAD_LOG2 = -170.0

VMEM_LIMIT = 56 * 1024 * 1024


def _cparams(sem, vmem=VMEM_LIMIT):
    return pltpu.CompilerParams(dimension_semantics=sem, vmem_limit_bytes=vmem)


def _log_sigmoid(x):
    return jnp.minimum(x, 0.0) - jnp.log1p(jnp.exp(-jnp.abs(x)))


def _layer_norm_rows(y, g, b):
    mu = jnp.mean(y, axis=-1, keepdims=True)
    d = y - mu
    var = jnp.mean(d * d, axis=-1, keepdims=True)
    return d * lax.rsqrt(var + LN_EPS) * g + b


def _ln_kernel(x_ref, g_ref, b_ref, ob_ref):
    ob_ref[...] = _layer_norm_rows(x_ref[...], g_ref[...], b_ref[...]).astype(BF16)


def _entry_ln(x, g, b, tm=512):
    s, d = x.shape
    return pl.pallas_call(
        _ln_kernel,
        grid=(s // tm,),
        in_specs=[pl.BlockSpec((tm, d), lambda i: (i, 0)),
                  pl.BlockSpec((1, d), lambda i: (0, 0)),
                  pl.BlockSpec((1, d), lambda i: (0, 0))],
        out_specs=pl.BlockSpec((tm, d), lambda i: (i, 0)),
        out_shape=jax.ShapeDtypeStruct((s, d), BF16),
        compiler_params=_cparams(("parallel",)),
        name="entry_ln",
    )(x, g.reshape(1, d), b.reshape(1, d))


def _proj_kernel(x_ref, w_ref, s_ref, o_ref):
    acc = jnp.dot(x_ref[...], w_ref[...], preferred_element_type=F32)
    o_ref[...] = (acc * s_ref[...]).astype(o_ref.dtype)


def _project(xb, w_all, layer, colscale, out_dtype, tm, tn, name):
    m, k = xb.shape
    n = w_all.shape[2]
    return pl.pallas_call(
        _proj_kernel,
        grid=(m // tm, n // tn),
        in_specs=[pl.BlockSpec((tm, k), lambda i, j: (i, 0)),
                  pl.BlockSpec((None, k, tn), lambda i, j: (layer, 0, j)),
                  pl.BlockSpec((1, tn), lambda i, j: (0, j))],
        out_specs=pl.BlockSpec((tm, tn), lambda i, j: (i, j)),
        out_shape=jax.ShapeDtypeStruct((m, n), out_dtype),
        compiler_params=_cparams(("parallel", "arbitrary")),
        name=name,
    )(xb, w_all, colscale)


def _forget_cumsum_kernel(f_ref, b_ref, o_ref):
    rows = f_ref.shape[0]
    per_head = rows // N_HEADS
    ls = _log_sigmoid(f_ref[...] + b_ref[...])
    r = lax.broadcasted_iota(jnp.int32, (LANES, LANES), 0)
    c = lax.broadcasted_iota(jnp.int32, (LANES, LANES), 1)
    upper = (r <= c).astype(F32)
    within = jnp.dot(ls, upper, preferred_element_type=F32,
                     precision=lax.Precision.HIGHEST)
    total = within[:, LANES - 1:LANES]
    rr = lax.broadcasted_iota(jnp.int32, (rows, rows), 0)
    cc = lax.broadcasted_iota(jnp.int32, (rows, rows), 1)
    head_start = rr - (rr & (per_head - 1))
    before = ((cc >= head_start) & (cc < rr)).astype(F32)
    offs = jnp.dot(before, jnp.broadcast_to(total, (rows, LANES)),
                   preferred_element_type=F32, precision=lax.Precision.HIGHEST)
    o_ref[...] = (within + offs) * LOG2_E


def _forget_cumsum(f_rows, b_rows):
    rows = f_rows.shape[0]
    return pl.pallas_call(
        _forget_cumsum_kernel,
        out_shape=jax.ShapeDtypeStruct((rows, LANES), F32),
        name="forget_cumsum",
    )(f_rows, b_rows)


def _fox_kernel(q_ref, k_ref, v_ref, cfk_ref, wg_ref, wb_ref, o_ref, wg_bf_ref, wb_bf_ref,
                m_sc, acc_sc, vaug_sc, qn_sc, kn_sc, *, tq, tk):
    i = pl.program_id(0)
    for src, dst in ((wg_ref, wg_bf_ref), (wb_ref, wb_bf_ref)):
        width = dst.shape[2]
        for c in range(dst.shape[0]):
            dst[c] = src[:, c * width:(c + 1) * width].astype(BF16)
    heads = [slice(h * HEAD_DIM, (h + 1) * HEAD_DIM) for h in range(N_HEADS)]
    norm_rows = 1024

    @pl.when(i == 0)
    def _largest_key_norm():
        for h, hs in enumerate(heads):
            def chunk(c, best, hs=hs):
                rows = k_ref[pl.ds(pl.multiple_of(c * norm_rows, norm_rows), norm_rows), hs].astype(F32)
                return jnp.maximum(best, jnp.max(jnp.sum(rows * rows, axis=-1, keepdims=True)))
            best = lax.fori_loop(0, k_ref.shape[0] // norm_rows, chunk, jnp.zeros((SUBLANES, LANES), F32))
            kn_sc[h] = jnp.sqrt(best)

    m_sc[...] = jnp.full(m_sc.shape, NEG_BIG, F32)
    acc_sc[...] = jnp.zeros(acc_sc.shape, F32)
    vaug_sc[:, :, HEAD_DIM:] = jnp.ones((N_HEADS, tk, HEAD_DIM), BF16)
    for h, hs in enumerate(heads):
        q = q_ref[:, hs].astype(F32)
        qn_sc[h] = jnp.broadcast_to(jnp.sqrt(jnp.sum(q * q, axis=-1, keepdims=True)), (tq, LANES))
    rep = tk // LANES

    def key_tile(j, masked, hlist):
        k0 = pl.multiple_of(j * tk, tk)
        if masked:
            keep = (lax.broadcasted_iota(jnp.int32, (tq, tk), 1)
                    <= lax.broadcasted_iota(jnp.int32, (tq, tk), 0))
        scores = [lax.dot_general(q_ref[:, heads[h]], k_ref[pl.ds(k0, tk), heads[h]],
                                  (((1,), (1,)), ((), ())), preferred_element_type=F32) for h in hlist]
        probs, alphas = [], []
        for h, s in zip(hlist, scores):
            s = s - cfk_ref[h:h + 1, pl.ds(k0, tk)]
            if masked:
                s = jnp.where(keep, s, NEG_BIG)
            m_old = m_sc[h]
            m_new = jnp.maximum(m_old, jnp.max(s, axis=-1, keepdims=True))
            alphas.append(jnp.exp2(m_old - m_new))
            probs.append(jnp.exp2(s - jnp.concatenate([m_new] * rep, axis=1)).astype(BF16))
            m_sc[h] = m_new
            vaug_sc[h, :, :HEAD_DIM] = v_ref[pl.ds(k0, tk), heads[h]]
        for h, p, alpha in zip(hlist, probs, alphas):
            pv = jnp.dot(p, vaug_sc[h], preferred_element_type=F32)
            acc_sc[h] = jnp.concatenate([alpha, alpha], axis=1) * acc_sc[h] + pv

    def alive(j, hlist):
        newest = pl.multiple_of(jnp.maximum(j, 0) * tk + tk - LANES, LANES)
        reach = None
        for h in hlist:
            decay = -cfk_ref[h:h + 1, pl.ds(newest, LANES)][:, LANES - 1:]
            bound = jnp.max(qn_sc[h] * kn_sc[h, 0:1, :] + decay - m_sc[h], axis=0, keepdims=True)
            reach = bound if reach is None else jnp.minimum(reach, bound)
        return (jnp.max(reach) > FOX_DEAD_LOG2).astype(jnp.int32)

    def walk_back(j_start, hlist):
        def earlier(state):
            j, _ = state
            key_tile(j, False, hlist)
            return j - 1, alive(j - 1, hlist)
        return lax.while_loop(lambda st: jnp.logical_and(st[0] >= 0, st[1] > 0), earlier,
                              (j_start, alive(j_start, hlist)))[0]

    all_heads = list(range(N_HEADS))
    key_tile(i, True, all_heads)
    j_split = walk_back(i - 1, all_heads)
    for h in all_heads:
        walk_back(j_split, [h])
    for h in range(N_HEADS):
        hs = slice(h * HEAD_DIM, (h + 1) * HEAD_DIM)
        o_ref[:, hs] = (acc_sc[h, :, :HEAD_DIM] / acc_sc[h, :, HEAD_DIM:]).astype(o_ref.dtype)


def _fox_attention(qkv, cfk, col0, layer, later_weights, col_blocks, tq=512):
    s = qkv.shape[0]
    tk = tq
    w = BRANCH_WIDTH
    steps = s // tq
    kern = functools.partial(_fox_kernel, tq=tq, tk=tk)
    resident = pl.Buffered(1)
    cast_in, cast_out, cast_shapes = [], [], []
    for arr in later_weights:
        _, r, c = arr.shape
        cast_in.append(pl.BlockSpec((None, r // steps, c), lambda i: (layer, i, 0)))
        cast_out.append(pl.BlockSpec((col_blocks, r // steps, c // col_blocks), lambda i: (0, i, 0)))
        cast_shapes.append(jax.ShapeDtypeStruct((col_blocks, r, c // col_blocks), BF16))
    return pl.pallas_call(
        kern,
        grid=(steps,),
        in_specs=[pl.BlockSpec((tq, w), lambda i: (i, col0)),
                  pl.BlockSpec((s, w), lambda i: (0, col0 + 1), pipeline_mode=resident),
                  pl.BlockSpec((s, w), lambda i: (0, col0 + 2), pipeline_mode=resident),
                  pl.BlockSpec((SUBLANES, s), lambda i: (0, 0), pipeline_mode=resident)] + cast_in,
        out_specs=[pl.BlockSpec((tq, w), lambda i: (i, 0))] + cast_out,
        out_shape=[jax.ShapeDtypeStruct((s, w), BF16)] + cast_shapes,
        scratch_shapes=[pltpu.VMEM((N_HEADS, tq, LANES), F32),
                        pltpu.VMEM((N_HEADS, tq, 2 * HEAD_DIM), F32),
                        pltpu.VMEM((N_HEADS, tk, 2 * HEAD_DIM), BF16),
                        pltpu.VMEM((N_HEADS, tq, LANES), F32),
                        pltpu.VMEM((N_HEADS, SUBLANES, LANES), F32)],
        compiler_params=_cparams(("arbitrary",)),
        name="fox_attention",
    )(qkv, qkv, qkv, cfk, *later_weights)


def _sb_kernel(q_ref, k_ref, v_ref, o_ref, run_sc, acc_sc, *, tq):
    i = pl.program_id(0)
    nsub = tq // LANES
    r = lax.broadcasted_iota(jnp.int32, (2 * LANES, 2 * LANES), 0) & (LANES - 1)
    c = lax.broadcasted_iota(jnp.int32, (2 * LANES, 2 * LANES), 1)
    tri_aug = ((c >= LANES) | (r > c)).astype(BF16)

    def sub_blocks(items, masked):
        scores = []
        for h, k0, r0 in items:
            hs = slice(h * HEAD_DIM, (h + 1) * HEAD_DIM)
            scores.append(lax.dot_general(q_ref[r0:, hs], k_ref[pl.ds(k0, LANES), hs],
                                          (((1,), (1,)), ((), ())), preferred_element_type=F32))
        log_beta, sums, keeps = [], [], []
        for (h, k0, r0), z in zip(items, scores):
            rows = tq - r0
            lp = jnp.minimum(z, 0.0) - jnp.log2(1.0 + jnp.exp2(-jnp.abs(z)))
            ln = lp - z
            keep = None
            if masked:
                keep = (lax.broadcasted_iota(jnp.int32, (rows, LANES), 1)
                        < lax.broadcasted_iota(jnp.int32, (rows, LANES), 0))
                ln = jnp.where(keep, ln, 0.0)
            ln_hi = ln.astype(BF16)
            ln_lo = (ln - ln_hi.astype(F32)).astype(BF16)
            log_beta.append(lp)
            keeps.append(keep)
            sums.append(jnp.dot(jnp.concatenate([ln_hi, ln_lo], axis=1), tri_aug,
                                preferred_element_type=F32))
        weights = []
        for (h, k0, r0), lp, la, keep in zip(items, log_beta, sums, keeps):
            run = run_sc[h, r0:, :]
            a = jnp.exp2(lp + la[:, :LANES] + run)
            if masked:
                a = jnp.where(keep, a, 0.0)
            run_sc[h, r0:, :] = run + la[:, LANES:]
            weights.append(a.astype(BF16))
        for (h, k0, r0), a in zip(items, weights):
            hs = slice(h * HEAD_DIM, (h + 1) * HEAD_DIM)
            acc_sc[h, r0:, :] += jnp.dot(a, v_ref[pl.ds(k0, LANES), hs], preferred_element_type=F32)

    run_sc[...] = jnp.zeros(run_sc.shape, F32)
    acc_sc[...] = jnp.zeros(acc_sc.shape, F32)
    sub_blocks([(h, pl.multiple_of(i * tq + cc * LANES, LANES), cc * LANES)
                for cc in range(nsub - 1, -1, -1) for h in range(N_HEADS)], True)

    def alive():
        return (jnp.max(run_sc[...]) > SB_DEAD_LOG2).astype(jnp.int32)

    def earlier(state):
        cb, _ = state
        sub_blocks([(h, pl.multiple_of((cb - back) * LANES, LANES), 0)
                    for back in range(2) for h in range(N_HEADS)], False)
        return cb - 2, alive()

    lax.while_loop(lambda st: jnp.logical_and(st[0] >= 0, st[1] > 0), earlier,
                   (i * nsub - 1, alive()))
    for h in range(N_HEADS):
        o_ref[:, h * HEAD_DIM:(h + 1) * HEAD_DIM] = acc_sc[h].astype(o_ref.dtype)


def _sb_attention(qkv, col0, tq=256):
    s = qkv.shape[0]
    w = BRANCH_WIDTH
    assert (tq // LANES) % 2 == 0
    kern = functools.partial(_sb_kernel, tq=tq)
    resident = pl.Buffered(1)
    return pl.pallas_call(
        kern,
        grid=(s // tq,),
        in_specs=[pl.BlockSpec((tq, w), lambda i: (i, col0)),
                  pl.BlockSpec((s, w), lambda i: (0, col0 + 1), pipeline_mode=resident),
                  pl.BlockSpec((s, w), lambda i: (0, col0 + 2), pipeline_mode=resident)],
        out_specs=pl.BlockSpec((tq, w), lambda i: (i, 0)),
        out_shape=jax.ShapeDtypeStruct((s, w), BF16),
        scratch_shapes=[pltpu.VMEM((N_HEADS, tq, LANES), F32),
                        pltpu.VMEM((N_HEADS, tq, HEAD_DIM), F32)],
        compiler_params=_cparams(("arbitrary",)),
        name="sb_attention",
    )(qkv, qkv, qkv)


def _band_kernel(q_ref, k2_ref, k1_ref, k0_ref, v2_ref, v1_ref, v0_ref, ext_ref, o_ref,
                 bias_sc, vaug_sc, *, tq):
    i = pl.program_id(0)
    width = 4 * tq

    @pl.when(i == 0)
    def _build_tables():
        trow = lax.broadcasted_iota(jnp.int32, (tq, width), 0)
        t = lax.broadcasted_iota(jnp.int32, (tq, 3 * tq), 0)
        s = lax.broadcasted_iota(jnp.int32, (tq, 3 * tq), 1)
        shift = CHUNK.bit_length() - 1
        t_chunk = t >> shift
        s_chunk = (s >> shift) - (2 * tq) // CHUNK
        in_band = (t_chunk - s_chunk <= LOOKBACK_CHUNKS) & (s_chunk <= t_chunk)
        for h in range(N_HEADS):
            x = jnp.broadcast_to(ext_ref[h:h + 1, :], (tq, width))
            for b in range(tq.bit_length() - 1):
                x = jnp.where(((trow >> b) & 1) == 1, pltpu.roll(x, 1 << b, axis=1), x)
            bias_sc[h] = jnp.where(in_band, x[:, :3 * tq] * LOG2_E, NEG_BIG)
        vaug_sc[:, :, HEAD_DIM:] = jnp.ones((N_HEADS, 3 * tq, HEAD_DIM), BF16)

    k_refs = (k2_ref, k1_ref, k0_ref)
    v_refs = (v2_ref, v1_ref, v0_ref)
    heads = [slice(h * HEAD_DIM, (h + 1) * HEAD_DIM) for h in range(N_HEADS)]

    def tile(first_tiles):
        scores = [[lax.dot_general(q_ref[:, hs], k_refs[p][:, hs], (((1,), (1,)), ((), ())),
                                   preferred_element_type=F32) for p in range(3)] for hs in heads]
        probs = []
        for h, pieces in enumerate(scores):
            pieces = [s + bias_sc[h, :, p * tq:(p + 1) * tq] for p, s in enumerate(pieces)]
            if first_tiles:
                pieces = [jnp.where(i - 2 + p >= 0, s, NEG_BIG) for p, s in enumerate(pieces)]
            m = jnp.max(jnp.maximum(jnp.maximum(pieces[0], pieces[1]), pieces[2]),
                        axis=-1, keepdims=True)
            probs.append(jnp.concatenate([jnp.exp2(s - m).astype(BF16) for s in pieces], axis=1))
            for p in range(3):
                vaug_sc[h, p * tq:(p + 1) * tq, :HEAD_DIM] = v_refs[p][:, heads[h]]
        for h, p in enumerate(probs):
            acc = jnp.dot(p, vaug_sc[h], preferred_element_type=F32)
            o_ref[:, heads[h]] = (acc[:, :HEAD_DIM] / acc[:, HEAD_DIM:]).astype(o_ref.dtype)

    @pl.when(i < 2)
    def _first_tiles():
        tile(True)

    @pl.when(i >= 2)
    def _other_tiles():
        tile(False)


def _band_bias_vector(rel_bias_l, tq):
    n = np.arange(4 * tq)
    dist = np.where(n < 3 * tq, 2 * tq - n, 6 * tq - n)
    ridx = np.clip(dist, -(CHUNK - 1), REL_CLIP) + (CHUNK - 1)
    return rel_bias_l.astype(F32)[:, ridx]


def _band_attention(qkv, bias_ext, col0, tq=256):
    s = qkv.shape[0]
    w = BRANCH_WIDTH
    assert 2 * tq >= LOOKBACK_CHUNKS * CHUNK and tq % CHUNK == 0 and tq & (tq - 1) == 0
    kern = functools.partial(_band_kernel, tq=tq)

    def kv_spec(back, col):
        return pl.BlockSpec((tq, w), lambda i: (jnp.maximum(i - back, 0), col))

    return pl.pallas_call(
        kern,
        grid=(s // tq,),
        in_specs=[pl.BlockSpec((tq, w), lambda i: (i, col0)),
                  kv_spec(2, col0 + 1), kv_spec(1, col0 + 1), kv_spec(0, col0 + 1),
                  kv_spec(2, col0 + 2), kv_spec(1, col0 + 2), kv_spec(0, col0 + 2),
                  pl.BlockSpec((N_HEADS, 4 * tq), lambda i: (0, 0))],
        out_specs=pl.BlockSpec((tq, w), lambda i: (i, 0)),
        out_shape=jax.ShapeDtypeStruct((s, w), BF16),
        scratch_shapes=[pltpu.VMEM((N_HEADS, tq, 3 * tq), F32),
                        pltpu.VMEM((N_HEADS, 3 * tq, 2 * HEAD_DIM), BF16)],
        compiler_params=_cparams(("arbitrary",)),
        name="band_attention",
    )(qkv, qkv, qkv, qkv, qkv, qkv, qkv, bias_ext)


def _gelu_tanh(x):
    c = math.sqrt(2.0 / math.pi)
    return 0.5 * x * (1.0 + jnp.tanh(c * (x + 0.044715 * (x * x * x))))


def _lru_begin_tile(first, xext_sc, carry_sc, *, tm):
    halo = SUBLANES

    @pl.when(first)
    def _first():
        xext_sc[0:halo, :] = jnp.zeros((halo, BRANCH_WIDTH), F32)
        carry_sc[...] = jnp.zeros(carry_sc.shape, F32)

    @pl.when(jnp.logical_not(first))
    def _shift_halo():
        xext_sc[0:halo, :] = xext_sc[tm:tm + halo, :]


def _lru_gates(rx_ref, cw_ref, cb_ref, wr_ref, wi_ref, xext_sc, *, tm):
    halo = SUBLANES
    xext_sc[halo:halo + tm, :] = rx_ref[...]
    xext = xext_sc[0:halo + tm, :]
    xc = cb_ref[...] + rx_ref[...] * cw_ref[CONV_WIDTH - 1:CONV_WIDTH, :]
    for back in range(1, CONV_WIDTH):
        t = CONV_WIDTH - 1 - back
        xc = xc + pltpu.roll(xext, back, axis=0)[halo:, :] * cw_ref[t:t + 1, :]
    xcb = xc.astype(BF16)
    r_parts, i_parts = [], []
    for n in range(N_HEADS):
        ns = slice(n * HEAD_DIM, (n + 1) * HEAD_DIM)
        r_parts.append(jnp.dot(xcb[:, ns], wr_ref[n], preferred_element_type=F32))
        i_parts.append(jnp.dot(xcb[:, ns], wi_ref[n], preferred_element_type=F32))
    return xc, jnp.concatenate(r_parts, axis=1), jnp.concatenate(i_parts, axis=1)


def _lru_coefficients(xc, r_pre, i_pre, br_ref, bi_ref, lam_ref, a_sc, b_sc):
    r = jax.nn.sigmoid(r_pre + br_ref[...])
    gi = jax.nn.sigmoid(i_pre + bi_ref[...])
    log_a = LRU_C * r * _log_sigmoid(lam_ref[...])
    a = jnp.exp(log_a)
    a_sc[...] = a
    b_sc[...] = jnp.sqrt(-jnp.tanh(log_a) * (a * a + 1.0)) * (gi * xc)


def _lru_recurrence(groups, a_sc, b_sc, h_sc, carry_sc):
    w = BRANCH_WIDTH
    row = lax.broadcasted_iota(jnp.int32, (SUBLANES, w), 0)
    carry = carry_sc[...]
    for g in groups:
        rows = slice(g * SUBLANES, (g + 1) * SUBLANES)
        a = a_sc[rows, :]
        b = b_sc[rows, :]
        for k in (1, 2, 4):
            a_prev = pltpu.roll(a, k, axis=0)
            b_prev = pltpu.roll(b, k, axis=0)
            ok = row >= k
            b = jnp.where(ok, a * b_prev + b, b)
            a = jnp.where(ok, a * a_prev, a)
        hgrp = a * carry + b
        h_sc[rows, :] = hgrp
        carry = jnp.broadcast_to(hgrp[SUBLANES - 1:SUBLANES, :], (SUBLANES, w))
    carry_sc[...] = carry


def _proj_lru_kernel(x_ref, w_ref, s_ref, rx_ref, ry_ref, cw_ref, cb_ref, wr_ref, br_ref, wi_ref,
                     bi_ref, lam_ref, qkv_ref, olru_ref, xext_sc, a_sc, b_sc, h_sc, carry_sc, *, tm_lru):
    first = jnp.logical_and(pl.program_id(0) == 0, pl.program_id(1) == 0)
    tn = qkv_ref.shape[1]
    bounds = [0, tn // 3] + [tn // 3 + (k + 1) * (2 * tn // 9) for k in range(3)]
    groups = tm_lru // SUBLANES

    def project(k):
        cols = slice(bounds[k], bounds[k + 1])
        acc = jnp.dot(x_ref[...], w_ref[:, cols], preferred_element_type=F32)
        qkv_ref[:, cols] = (acc * s_ref[:, cols]).astype(qkv_ref.dtype)

    _lru_begin_tile(first, xext_sc, carry_sc, tm=tm_lru)
    project(0)
    xc, r_pre, i_pre = _lru_gates(rx_ref, cw_ref, cb_ref, wr_ref, wi_ref, xext_sc, tm=tm_lru)
    project(1)
    _lru_coefficients(xc, r_pre, i_pre, br_ref, bi_ref, lam_ref, a_sc, b_sc)
    project(2)
    _lru_recurrence(range(groups // 2), a_sc, b_sc, h_sc, carry_sc)
    project(3)
    _lru_recurrence(range(groups // 2, groups), a_sc, b_sc, h_sc, carry_sc)
    olru_ref[...] = (h_sc[...] * _gelu_tanh(ry_ref[...])).astype(olru_ref.dtype)


def _project_qkv_and_recur(xb, w_all, layer, colscale, uf, conv_w, conv_b, w_r, b_r, w_i, b_i, lam,
                           tm=1024, tn=2304, tm_lru=512):
    m, k = xb.shape
    n = w_all.shape[2]
    w = BRANCH_WIDTH
    nj = n // tn
    assert tm == nj * tm_lru
    kern = functools.partial(_proj_lru_kernel, tm_lru=tm_lru)
    row = lambda v: v.reshape(1, w)
    full2 = lambda shape: pl.BlockSpec(shape, lambda i, j: (0, 0))
    full3 = lambda shape: pl.BlockSpec(shape, lambda i, j: (0, 0, 0))
    return pl.pallas_call(
        kern,
        grid=(m // tm, nj),
        in_specs=[pl.BlockSpec((tm, k), lambda i, j: (i, 0)),
                  pl.BlockSpec((None, k, tn), lambda i, j: (layer, 0, j)),
                  pl.BlockSpec((1, tn), lambda i, j: (0, j)),
                  pl.BlockSpec((tm_lru, w), lambda i, j: (i * nj + j, 0)),
                  pl.BlockSpec((tm_lru, w), lambda i, j: (i * nj + j, 1)),
                  full2((CONV_WIDTH, w)), full2((1, w)),
                  full3((N_HEADS, HEAD_DIM, HEAD_DIM)), full2((1, w)),
                  full3((N_HEADS, HEAD_DIM, HEAD_DIM)), full2((1, w)),
                  full2((1, w))],
        out_specs=[pl.BlockSpec((tm, tn), lambda i, j: (i, j)),
                   pl.BlockSpec((tm_lru, w), lambda i, j: (i * nj + j, 0))],
        out_shape=[jax.ShapeDtypeStruct((m, n), BF16), jax.ShapeDtypeStruct((m, w), BF16)],
        scratch_shapes=[pltpu.VMEM((tm_lru + 2 * SUBLANES, w), F32),
                        pltpu.VMEM((tm_lru, w), F32),
                        pltpu.VMEM((tm_lru, w), F32),
                        pltpu.VMEM((tm_lru, w), F32),
                        pltpu.VMEM((SUBLANES, w), F32)],
        compiler_params=_cparams(("arbitrary", "arbitrary")),
        name="in_proj_qkv_recurrent",
    )(xb, w_all, colscale, uf, uf, conv_w, row(conv_b), w_r.astype(BF16), row(b_r),
      w_i.astype(BF16), row(b_i), row(lam))


def _merge_kernel(x_ref, o0_ref, o1_ref, o2_ref, o3_ref, wg_ref, bg_ref, wb_ref, wo_ref, w1_ref, w2_ref,
                  out_ref, wo_bf_ref, w1_bf_ref, w2_bf_ref):
    wo_bf_ref[...] = wo_ref[...].astype(BF16)
    w1_bf_ref[...] = w1_ref[...].astype(BF16)
    w2_bf_ref[...] = w2_ref[...].astype(BF16)
    x = x_ref[...]
    merged = None
    for g, o_ref in enumerate((o0_ref, o1_ref, o2_ref, o3_ref)):
        gate = jax.nn.sigmoid(jnp.dot(x, wg_ref[g], preferred_element_type=F32) + bg_ref[g:g + 1, :])
        term = gate * jnp.dot(o_ref[...], wb_ref[g], preferred_element_type=F32)
        merged = term if merged is None else merged + term
    out_ref[...] = merged.astype(out_ref.dtype)


def _merge(xb, branches, wg, bg_all, wb, layer, later_weights, tm=1024):
    s, d = xb.shape
    w = BRANCH_WIDTH
    nj, _, _, tn = wg.shape
    ni = s // tm
    o_spec = pl.BlockSpec((tm, w), lambda i, j: (i, 0))
    cast_in, cast_out, cast_shapes = [], [], []
    for arr, column_major in later_weights:
        _, r, c = arr.shape
        cast_in.append(pl.BlockSpec((None, r // ni, c // nj), lambda i, j: (layer, i, j)))
        if column_major:
            cast_out.append(pl.BlockSpec((None, r // ni, c // nj), lambda i, j: (j, i, 0)))
            cast_shapes.append(jax.ShapeDtypeStruct((nj, r, c // nj), BF16))
        else:
            cast_out.append(pl.BlockSpec((r // ni, c // nj), lambda i, j: (i, j)))
            cast_shapes.append(jax.ShapeDtypeStruct((r, c), BF16))
    return pl.pallas_call(
        _merge_kernel,
        grid=(ni, nj),
        in_specs=[pl.BlockSpec((tm, d), lambda i, j: (i, 0)),
                  o_spec, o_spec, o_spec, o_spec,
                  pl.BlockSpec((None, N_BRANCH, d, tn), lambda i, j: (j, 0, 0, 0)),
                  pl.BlockSpec((None, N_BRANCH, tn), lambda i, j: (layer, 0, j)),
                  pl.BlockSpec((None, N_BRANCH, w, tn), lambda i, j: (j, 0, 0, 0))] + cast_in,
        out_specs=[pl.BlockSpec((tm, tn), lambda i, j: (i, j))] + cast_out,
        out_shape=[jax.ShapeDtypeStruct((s, d), BF16)] + cast_shapes,
        compiler_params=_cparams(("parallel", "arbitrary")),
        name="gated_merge",
    )(xb, *branches, wg, bg_all, wb, *[arr for arr, _ in later_weights])


def _outproj_kernel(m_ref, w_ref, x_ref, pre_g_ref, pre_b_ref, g_ref, b_ref, of_ref, ob_ref, *, prenorm):
    half = m_ref.shape[0] // 2
    rows = (slice(0, half), slice(half, 2 * half))
    proj = [jnp.dot(m_ref[r, :], w_ref[...], preferred_element_type=F32) for r in rows]
    for r, p in zip(rows, proj):
        x = x_ref[r, :]
        if prenorm:
            x = _layer_norm_rows(x, pre_g_ref[...], pre_b_ref[...])
        y = _layer_norm_rows(ALPHA * x + p, g_ref[...], b_ref[...])
        of_ref[r, :] = y
        ob_ref[r, :] = y.astype(BF16)


def _outproj_ln(merged, w_out, x, pre_g, pre_b, prenorm, g, b, tm=512):
    s, d = x.shape
    row_spec = pl.BlockSpec((tm, d), lambda i: (i, 0))
    vec_spec = pl.BlockSpec((1, d), lambda i: (0, 0))
    return pl.pallas_call(
        functools.partial(_outproj_kernel, prenorm=prenorm),
        grid=(s // tm,),
        in_specs=[row_spec, pl.BlockSpec((d, d), lambda i: (0, 0)), row_spec, vec_spec, vec_spec,
                  vec_spec, vec_spec],
        out_specs=[row_spec, row_spec],
        out_shape=[jax.ShapeDtypeStruct((s, d), F32), jax.ShapeDtypeStruct((s, d), BF16)],
        compiler_params=_cparams(("parallel",)),
        name="outproj_ln",
    )(merged, w_out, x, pre_g.reshape(1, d), pre_b.reshape(1, d), g.reshape(1, d), b.reshape(1, d))


def _ffn_kernel(xb_ref, xf_ref, w1_ref, w2_ref, g_ref, b_ref, of_ref, ob_ref, acc_sc):
    f = pl.program_id(1)
    last = pl.num_programs(1) - 1

    def hidden():
        hid = jnp.maximum(jnp.dot(xb_ref[...], w1_ref[...], preferred_element_type=F32), 0.0)
        return (hid * hid).astype(BF16)

    @pl.when(f == 0)
    def _first():
        acc_sc[...] = jnp.dot(hidden(), w2_ref[...], preferred_element_type=F32)

    @pl.when(jnp.logical_and(f > 0, f < last))
    def _middle():
        acc_sc[...] += jnp.dot(hidden(), w2_ref[...], preferred_element_type=F32)

    @pl.when(f == last)
    def _last():
        hid = hidden()
        half = hid.shape[0] // 2
        rows = (slice(0, half), slice(half, 2 * half))
        down = [jnp.dot(hid[r, :], w2_ref[...], preferred_element_type=F32) for r in rows]
        for r, dn in zip(rows, down):
            y = _layer_norm_rows(ALPHA * xf_ref[r, :] + (acc_sc[r, :] + dn), g_ref[...], b_ref[...])
            of_ref[r, :] = y
            ob_ref[r, :] = y.astype(BF16)


def _ffn_ln(xb, xf, w1, w2, g, b, tm=512):
    s, d = xf.shape
    nf, _, tf = w1.shape
    row_spec = pl.BlockSpec((tm, d), lambda i, f: (i, 0))
    vec_spec = pl.BlockSpec((1, d), lambda i, f: (0, 0))
    return pl.pallas_call(
        _ffn_kernel,
        grid=(s // tm, nf),
        in_specs=[row_spec,
                  pl.BlockSpec((tm, d), lambda i, f: (jnp.where(f >= nf // 2, i, jnp.maximum(i - 1, 0)), 0)),
                  pl.BlockSpec((None, d, tf), lambda i, f: (f, 0, 0)),
                  pl.BlockSpec((tf, d), lambda i, f: (f, 0)),
                  vec_spec, vec_spec],
        out_specs=[row_spec, row_spec],
        out_shape=[jax.ShapeDtypeStruct((s, d), F32), jax.ShapeDtypeStruct((s, d), BF16)],
        scratch_shapes=[pltpu.VMEM((tm, d), F32)],
        compiler_params=_cparams(("parallel", "arbitrary")),
        name="ffn_ln",
    )(xb, xf, w1, w2, g.reshape(1, d), b.reshape(1, d))


def _split_in_proj(w_in):
    qkv = jnp.concatenate([w_in[:, :, _OFF_FQ:_OFF_FF], w_in[:, :, _OFF_SQ:_OFF_END]], axis=2)
    pad = jnp.zeros(w_in.shape[:2] + (LANES - N_HEADS,), w_in.dtype)
    rest = jnp.concatenate([w_in[:, :, _OFF_RX:_OFF_SQ], w_in[:, :, _OFF_FF:_OFF_RX], pad], axis=2)
    return qkv.astype(BF16), rest.astype(BF16)


def kernel(x, ln_in_g, ln_in_b, w_in, b_forget, conv_w, conv_b, w_r, b_r, w_i, b_i, lru_lambda,
           rel_bias, w_branch, w_gate, b_gate, w_out, ln1_g, ln1_b, w_ff1, w_ff2, ln2_g, ln2_b):
    batch, s, d = x.shape
    assert (batch, s, d) == (1, SEQ, D_MODEL)
    w = BRANCH_WIDTH
    band_tq = 256
    merge_blocks = 8

    col_scale = np.ones((1, 9 * w), np.float32)
    for q_block in (0, 3, 6):
        col_scale[:, q_block * w:(q_block + 1) * w] = QK_SCALE * LOG2_E
    qkv_scale = jnp.asarray(col_scale)
    rest_scale = jnp.ones((1, 2 * w + LANES), F32)

    w_qkv, w_rest = _split_in_proj(w_in)
    w_gate_rows = w_gate.reshape(DEPTH, N_BRANCH * d, d)
    w_branch_rows = w_branch.reshape(DEPTH, N_BRANCH * w, d)

    xf = x.reshape(s, d)
    xb = _entry_ln(xf, ln_in_g, ln_in_b)
    for l in range(DEPTH):
        uf = _project(xb, w_rest, l, rest_scale, F32, 1024, 1152, "in_proj_rest")
        qkv, o_lru = _project_qkv_and_recur(xb, w_qkv, l, qkv_scale, uf, conv_w[l], conv_b[l], w_r[l],
                                            b_r[l], w_i[l], b_i[l], lru_lambda[l])

        f_rows = uf[:, 2 * w:2 * w + N_HEADS].T.reshape(N_HEADS * (s // LANES), LANES)
        b_rows = jnp.repeat(b_forget[l].astype(F32), s // LANES).reshape(-1, 1)
        cf = _forget_cumsum(f_rows, b_rows).reshape(N_HEADS, s)
        cfk = jnp.pad(cf, ((0, SUBLANES - N_HEADS), (0, 0)))

        o_fox, wg_b, wb_b = _fox_attention(qkv, cfk, 0, l, (w_gate_rows, w_branch_rows), merge_blocks)
        o_sb = _sb_attention(qkv, 3)
        o_ch = _band_attention(qkv, _band_bias_vector(rel_bias[l], band_tq), 6, band_tq)

        merged, wo_b, w1_b, w2_b = _merge(
            xb, (o_fox, o_lru, o_sb, o_ch),
            wg_b.reshape(merge_blocks, N_BRANCH, d, d // merge_blocks), b_gate,
            wb_b.reshape(merge_blocks, N_BRANCH, w, d // merge_blocks), l,
            ((w_out, False), (w_ff1, True), (w_ff2, False)))
        xf, xb = _outproj_ln(merged, wo_b, xf, ln_in_g, ln_in_b, l == 0, ln1_g[l], ln1_b[l])
        xf, xb = _ffn_ln(xb, xf, w1_b, w2_b, ln2_g[l], ln2_b[l])
    return xf.reshape(batch, s, d)
```

```python
import functools
import math

import jax
import jax.numpy as jnp
import numpy as np
from jax import lax
from jax.experimental import pallas as pl
from jax.experimental.pallas import tpu as pltpu

F32 = jnp.float32
BF16 = jnp.bfloat16

D_MODEL = 2048
SEQ = 8192
DEPTH = 2
CHUNK = 64
HEAD_DIM = 128
N_BRANCH = 4
BRANCH_WIDTH = D_MODEL // N_BRANCH
N_HEADS = BRANCH_WIDTH // HEAD_DIM
CONV_WIDTH = 4
LRU_C = 8.0
LOOKBACK_CHUNKS = 8
REL_CLIP = 256
D_FF = 4 * D_MODEL
ALPHA = (2.0 * DEPTH) ** 0.25
LN_EPS = 1e-5
QK_SCALE = HEAD_DIM ** -0.5
LOG2_E = math.log2(math.e)

_OFF_FQ = 0
_OFF_FK = _OFF_FQ + BRANCH_WIDTH
_OFF_FV = _OFF_FK + BRANCH_WIDTH
_OFF_FF = _OFF_FV + BRANCH_WIDTH
_OFF_RX = _OFF_FF + N_HEADS
_OFF_RY = _OFF_RX + BRANCH_WIDTH
_OFF_SQ = _OFF_RY + BRANCH_WIDTH
_OFF_CQ = _OFF_SQ + 3 * BRANCH_WIDTH
_OFF_END = _OFF_CQ + 3 * BRANCH_WIDTH

LANES = 128
SUBLANES = 8
NEG_BIG = -1e30
SB_DEAD_LOG2 = -180.0
FOX_DEAD_LOG2 = -170.0

VMEM_LIMIT = 56 * 1024 * 1024


def _cparams(sem, vmem=VMEM_LIMIT):
    return pltpu.CompilerParams(dimension_semantics=sem, vmem_limit_bytes=vmem)


def _log_sigmoid(x):
    return jnp.minimum(x, 0.0) - jnp.log1p(jnp.exp(-jnp.abs(x)))


def _layer_norm_rows(y, g, b):
    mu = jnp.mean(y, axis=-1, keepdims=True)
    d = y - mu
    var = jnp.mean(d * d, axis=-1, keepdims=True)
    return d * lax.rsqrt(var + LN_EPS) * g + b


def _ln_kernel(x_ref, g_ref, b_ref, ob_ref):
    ob_ref[...] = _layer_norm_rows(x_ref[...], g_ref[...], b_ref[...]).astype(BF16)


def _entry_ln(x, g, b, tm=512):
    s, d = x.shape
    return pl.pallas_call(
        _ln_kernel,
        grid=(s // tm,),
        in_specs=[pl.BlockSpec((tm, d), lambda i: (i, 0)),
                  pl.BlockSpec((1, d), lambda i: (0, 0)),
                  pl.BlockSpec((1, d), lambda i: (0, 0))],
        out_specs=pl.BlockSpec((tm, d), lambda i: (i, 0)),
        out_shape=jax.ShapeDtypeStruct((s, d), BF16),
        compiler_params=_cparams(("parallel",)),
        name="entry_ln",
    )(x, g.reshape(1, d), b.reshape(1, d))


def _proj_kernel(x_ref, w_ref, s_ref, o_ref):
    acc = lax.dot_general(x_ref[...], w_ref[...], (((1,), (1,)), ((), ())), preferred_element_type=F32)
    o_ref[...] = (acc * s_ref[...]).astype(o_ref.dtype)


def _project(xb, w_all, layer, colscale, out_dtype, tm, tn, name):
    m, k = xb.shape
    n = w_all.shape[1]
    return pl.pallas_call(
        _proj_kernel,
        grid=(m // tm, n // tn),
        in_specs=[pl.BlockSpec((tm, k), lambda i, j: (i, 0)),
                  pl.BlockSpec((None, tn, k), lambda i, j: (layer, j, 0)),
                  pl.BlockSpec((1, tn), lambda i, j: (0, j))],
        out_specs=pl.BlockSpec((tm, tn), lambda i, j: (i, j)),
        out_shape=jax.ShapeDtypeStruct((m, n), out_dtype),
        compiler_params=_cparams(("parallel", "arbitrary")),
        name=name,
    )(xb, w_all, colscale)


def _forget_cumsum_kernel(f_ref, b_ref, o_ref):
    rows = f_ref.shape[0]
    per_head = rows // N_HEADS
    ls = _log_sigmoid(f_ref[...] + b_ref[...])
    r = lax.broadcasted_iota(jnp.int32, (LANES, LANES), 0)
    c = lax.broadcasted_iota(jnp.int32, (LANES, LANES), 1)
    upper = (r <= c).astype(F32)
    within = jnp.dot(ls, upper, preferred_element_type=F32,
                     precision=lax.Precision.HIGHEST)
    total = within[:, LANES - 1:LANES]
    rr = lax.broadcasted_iota(jnp.int32, (rows, rows), 0)
    cc = lax.broadcasted_iota(jnp.int32, (rows, rows), 1)
    head_start = rr - (rr & (per_head - 1))
    before = ((cc >= head_start) & (cc < rr)).astype(F32)
    offs = jnp.dot(before, jnp.broadcast_to(total, (rows, LANES)),
                   preferred_element_type=F32, precision=lax.Precision.HIGHEST)
    o_ref[...] = (within + offs) * LOG2_E


def _forget_cumsum(f_rows, b_rows):
    rows = f_rows.shape[0]
    return pl.pallas_call(
        _forget_cumsum_kernel,
        out_shape=jax.ShapeDtypeStruct((rows, LANES), F32),
        name="forget_cumsum",
    )(f_rows, b_rows)


def _fox_kernel(q_ref, k_ref, v_ref, cfk_ref, wg_ref, wb_ref, o_ref, wg_bf_ref, wb_bf_ref,
                m_sc, acc_sc, vaug_sc, qn_sc, kn_sc, *, tq, tk):
    i = pl.program_id(0)
    for src, dst in ((wg_ref, wg_bf_ref), (wb_ref, wb_bf_ref)):
        width = dst.shape[2]
        for c in range(dst.shape[0]):
            dst[c] = src[:, c * width:(c + 1) * width].astype(BF16)
    heads = [slice(h * HEAD_DIM, (h + 1) * HEAD_DIM) for h in range(N_HEADS)]
    norm_rows = 1024

    @pl.when(i == 0)
    def _largest_key_norm():
        for h, hs in enumerate(heads):
            def chunk(c, best, hs=hs):
                rows = k_ref[pl.ds(pl.multiple_of(c * norm_rows, norm_rows), norm_rows), hs].astype(F32)
                return jnp.maximum(best, jnp.max(jnp.sum(rows * rows, axis=-1, keepdims=True)))
            best = lax.fori_loop(0, k_ref.shape[0] // norm_rows, chunk, jnp.zeros((SUBLANES, LANES), F32))
            kn_sc[h] = jnp.sqrt(best)

    m_sc[...] = jnp.full(m_sc.shape, NEG_BIG, F32)
    acc_sc[...] = jnp.zeros(acc_sc.shape, F32)
    vaug_sc[:, :, HEAD_DIM:] = jnp.ones((N_HEADS, tk, HEAD_DIM), BF16)
    for h, hs in enumerate(heads):
        q = q_ref[:, hs].astype(F32)
        qn_sc[h] = jnp.broadcast_to(jnp.sqrt(jnp.sum(q * q, axis=-1, keepdims=True)), (tq, LANES))
    rep = tk // LANES

    def key_tile(j, masked, hlist):
        k0 = pl.multiple_of(j * tk, tk)
        if masked:
            keep = (lax.broadcasted_iota(jnp.int32, (tq, tk), 1)
                    <= lax.broadcasted_iota(jnp.int32, (tq, tk), 0))
        scores = [lax.dot_general(q_ref[:, heads[h]], k_ref[pl.ds(k0, tk), heads[h]],
                                  (((1,), (1,)), ((), ())), preferred_element_type=F32) for h in hlist]
        probs, alphas = [], []
        for h, s in zip(hlist, scores):
            s = s - cfk_ref[h:h + 1, pl.ds(k0, tk)]
            if masked:
                s = jnp.where(keep, s, NEG_BIG)
            m_old = m_sc[h]
            m_new = jnp.maximum(m_old, jnp.max(s, axis=-1, keepdims=True))
            alphas.append(jnp.exp2(m_old - m_new))
            probs.append(jnp.exp2(s - jnp.concatenate([m_new] * rep, axis=1)).astype(BF16))
            m_sc[h] = m_new
            vaug_sc[h, :, :HEAD_DIM] = v_ref[pl.ds(k0, tk), heads[h]]
        for h, p, alpha in zip(hlist, probs, alphas):
            pv = jnp.dot(p, vaug_sc[h], preferred_element_type=F32)
            acc_sc[h] = jnp.concatenate([alpha, alpha], axis=1) * acc_sc[h] + pv

    def alive(j, hlist):
        newest = pl.multiple_of(jnp.maximum(j, 0) * tk + tk - LANES, LANES)
        reach = None
        for h in hlist:
            decay = -cfk_ref[h:h + 1, pl.ds(newest, LANES)][:, LANES - 1:]
            bound = jnp.max(qn_sc[h] * kn_sc[h, 0:1, :] + decay - m_sc[h], axis=0, keepdims=True)
            reach = bound if reach is None else jnp.minimum(reach, bound)
        return (jnp.max(reach) > FOX_DEAD_LOG2).astype(jnp.int32)

    def walk_back(j_start, hlist):
        def earlier(state):
            j, _ = state
            key_tile(j, False, hlist)
            return j - 1, alive(j - 1, hlist)
        return lax.while_loop(lambda st: jnp.logical_and(st[0] >= 0, st[1] > 0), earlier,
                              (j_start, alive(j_start, hlist)))[0]

    all_heads = list(range(N_HEADS))
    key_tile(i, True, all_heads)
    j_split = walk_back(i - 1, all_heads)
    for h in all_heads:
        walk_back(j_split, [h])
    for h in range(N_HEADS):
        hs = slice(h * HEAD_DIM, (h + 1) * HEAD_DIM)
        o_ref[:, hs] = (acc_sc[h, :, :HEAD_DIM] / acc_sc[h, :, HEAD_DIM:]).astype(o_ref.dtype)


def _fox_attention(qkv, cfk, col0, layer, later_weights, col_blocks, tq=512):
    s = qkv.shape[0]
    tk = tq
    w = BRANCH_WIDTH
    steps = s // tq
    kern = functools.partial(_fox_kernel, tq=tq, tk=tk)
    resident = pl.Buffered(1)
    cast_in, cast_out, cast_shapes = [], [], []
    for arr in later_weights:
        _, r, c = arr.shape
        cast_in.append(pl.BlockSpec((None, r // steps, c), lambda i: (layer, i, 0)))
        cast_out.append(pl.BlockSpec((col_blocks, r // steps, c // col_blocks), lambda i: (0, i, 0)))
        cast_shapes.append(jax.ShapeDtypeStruct((col_blocks, r, c // col_blocks), BF16))
    return pl.pallas_call(
        kern,
        grid=(steps,),
        in_specs=[pl.BlockSpec((tq, w), lambda i: (i, col0)),
                  pl.BlockSpec((s, w), lambda i: (0, col0 + 1), pipeline_mode=resident),
                  pl.BlockSpec((s, w), lambda i: (0, col0 + 2), pipeline_mode=resident),
                  pl.BlockSpec((SUBLANES, s), lambda i: (0, 0), pipeline_mode=resident)] + cast_in,
        out_specs=[pl.BlockSpec((tq, w), lambda i: (i, 0))] + cast_out,
        out_shape=[jax.ShapeDtypeStruct((s, w), BF16)] + cast_shapes,
        scratch_shapes=[pltpu.VMEM((N_HEADS, tq, LANES), F32),
                        pltpu.VMEM((N_HEADS, tq, 2 * HEAD_DIM), F32),
                        pltpu.VMEM((N_HEADS, tk, 2 * HEAD_DIM), BF16),
                        pltpu.VMEM((N_HEADS, tq, LANES), F32),
                        pltpu.VMEM((N_HEADS, SUBLANES, LANES), F32)],
        compiler_params=_cparams(("arbitrary",)),
        name="fox_attention",
    )(qkv, qkv, qkv, cfk, *later_weights)


def _sb_kernel(q_ref, k_ref, v_ref, o_ref, run_sc, acc_sc, *, tq):
    i = pl.program_id(0)
    nsub = tq // LANES
    r = lax.broadcasted_iota(jnp.int32, (2 * LANES, 2 * LANES), 0) & (LANES - 1)
    c = lax.broadcasted_iota(jnp.int32, (2 * LANES, 2 * LANES), 1)
    tri_aug = ((c >= LANES) | (r > c)).astype(BF16)

    def sub_blocks(items, masked):
        scores = []
        for h, k0, r0 in items:
            hs = slice(h * HEAD_DIM, (h + 1) * HEAD_DIM)
            scores.append(lax.dot_general(q_ref[r0:, hs], k_ref[pl.ds(k0, LANES), hs],
                                          (((1,), (1,)), ((), ())), preferred_element_type=F32))
        log_beta, sums, keeps = [], [], []
        for (h, k0, r0), z in zip(items, scores):
            rows = tq - r0
            lp = jnp.minimum(z, 0.0) - jnp.log2(1.0 + jnp.exp2(-jnp.abs(z)))
            ln = lp - z
            keep = None
            if masked:
                keep = (lax.broadcasted_iota(jnp.int32, (rows, LANES), 1)
                        < lax.broadcasted_iota(jnp.int32, (rows, LANES), 0))
                ln = jnp.where(keep, ln, 0.0)
            ln_hi = ln.astype(BF16)
            ln_lo = (ln - ln_hi.astype(F32)).astype(BF16)
            log_beta.append(lp)
            keeps.append(keep)
            sums.append(jnp.dot(jnp.concatenate([ln_hi, ln_lo], axis=1), tri_aug,
                                preferred_element_type=F32))
        weights = []
        for (h, k0, r0), lp, la, keep in zip(items, log_beta, sums, keeps):
            run = run_sc[h, r0:, :]
            a = jnp.exp2(lp + la[:, :LANES] + run)
            if masked:
                a = jnp.where(keep, a, 0.0)
            run_sc[h, r0:, :] = run + la[:, LANES:]
            weights.append(a.astype(BF16))
        for (h, k0, r0), a in zip(items, weights):
            hs = slice(h * HEAD_DIM, (h + 1) * HEAD_DIM)
            acc_sc[h, r0:, :] += jnp.dot(a, v_ref[pl.ds(k0, LANES), hs], preferred_element_type=F32)

    run_sc[...] = jnp.zeros(run_sc.shape, F32)
    acc_sc[...] = jnp.zeros(acc_sc.shape, F32)
    sub_blocks([(h, pl.multiple_of(i * tq + cc * LANES, LANES), cc * LANES)
                for cc in range(nsub - 1, -1, -1) for h in range(N_HEADS)], True)

    def alive():
        return (jnp.max(run_sc[...]) > SB_DEAD_LOG2).astype(jnp.int32)

    def earlier(state):
        cb, _ = state
        sub_blocks([(h, pl.multiple_of((cb - back) * LANES, LANES), 0)
                    for back in range(2) for h in range(N_HEADS)], False)
        return cb - 2, alive()

    lax.while_loop(lambda st: jnp.logical_and(st[0] >= 0, st[1] > 0), earlier,
                   (i * nsub - 1, alive()))
    for h in range(N_HEADS):
        o_ref[:, h * HEAD_DIM:(h + 1) * HEAD_DIM] = acc_sc[h].astype(o_ref.dtype)


def _sb_attention(qkv, col0, tq=256):
    s = qkv.shape[0]
    w = BRANCH_WIDTH
    assert (tq // LANES) % 2 == 0
    kern = functools.partial(_sb_kernel, tq=tq)
    resident = pl.Buffered(1)
    return pl.pallas_call(
        kern,
        grid=(s // tq,),
        in_specs=[pl.BlockSpec((tq, w), lambda i: (i, col0)),
                  pl.BlockSpec((s, w), lambda i: (0, col0 + 1), pipeline_mode=resident),
                  pl.BlockSpec((s, w), lambda i: (0, col0 + 2), pipeline_mode=resident)],
        out_specs=pl.BlockSpec((tq, w), lambda i: (i, 0)),
        out_shape=jax.ShapeDtypeStruct((s, w), BF16),
        scratch_shapes=[pltpu.VMEM((N_HEADS, tq, LANES), F32),
                        pltpu.VMEM((N_HEADS, tq, HEAD_DIM), F32)],
        compiler_params=_cparams(("arbitrary",)),
        name="sb_attention",
    )(qkv, qkv, qkv)


def _band_kernel(q_ref, k2_ref, k1_ref, k0_ref, v2_ref, v1_ref, v0_ref, ext_ref, o_ref,
                 bias_sc, vaug_sc, *, tq):
    i = pl.program_id(0)
    width = 4 * tq

    @pl.when(i == 0)
    def _build_tables():
        trow = lax.broadcasted_iota(jnp.int32, (tq, width), 0)
        t = lax.broadcasted_iota(jnp.int32, (tq, 3 * tq), 0)
        s = lax.broadcasted_iota(jnp.int32, (tq, 3 * tq), 1)
        shift = CHUNK.bit_length() - 1
        t_chunk = t >> shift
        s_chunk = (s >> shift) - (2 * tq) // CHUNK
        in_band = (t_chunk - s_chunk <= LOOKBACK_CHUNKS) & (s_chunk <= t_chunk)
        for h in range(N_HEADS):
            x = jnp.broadcast_to(ext_ref[h:h + 1, :], (tq, width))
            for b in range(tq.bit_length() - 1):
                x = jnp.where(((trow >> b) & 1) == 1, pltpu.roll(x, 1 << b, axis=1), x)
            bias_sc[h] = jnp.where(in_band, x[:, :3 * tq] * LOG2_E, NEG_BIG)
        vaug_sc[:, :, HEAD_DIM:] = jnp.ones((N_HEADS, 3 * tq, HEAD_DIM), BF16)

    k_refs = (k2_ref, k1_ref, k0_ref)
    v_refs = (v2_ref, v1_ref, v0_ref)
    heads = [slice(h * HEAD_DIM, (h + 1) * HEAD_DIM) for h in range(N_HEADS)]

    def tile(first_tiles):
        scores = [[lax.dot_general(q_ref[:, hs], k_refs[p][:, hs], (((1,), (1,)), ((), ())),
                                   preferred_element_type=F32) for p in range(3)] for hs in heads]
        probs = []
        for h, pieces in enumerate(scores):
            pieces = [s + bias_sc[h, :, p * tq:(p + 1) * tq] for p, s in enumerate(pieces)]
            if first_tiles:
                pieces = [jnp.where(i - 2 + p >= 0, s, NEG_BIG) for p, s in enumerate(pieces)]
            m = jnp.max(jnp.maximum(jnp.maximum(pieces[0], pieces[1]), pieces[2]),
                        axis=-1, keepdims=True)
            probs.append(jnp.concatenate([jnp.exp2(s - m).astype(BF16) for s in pieces], axis=1))
            for p in range(3):
                vaug_sc[h, p * tq:(p + 1) * tq, :HEAD_DIM] = v_refs[p][:, heads[h]]
        for h, p in enumerate(probs):
            acc = jnp.dot(p, vaug_sc[h], preferred_element_type=F32)
            o_ref[:, heads[h]] = (acc[:, :HEAD_DIM] / acc[:, HEAD_DIM:]).astype(o_ref.dtype)

    @pl.when(i < 2)
    def _first_tiles():
        tile(True)

    @pl.when(i >= 2)
    def _other_tiles():
        tile(False)


def _band_bias_vector(rel_bias_l, tq):
    n = np.arange(4 * tq)
    dist = np.where(n < 3 * tq, 2 * tq - n, 6 * tq - n)
    ridx = np.clip(dist, -(CHUNK - 1), REL_CLIP) + (CHUNK - 1)
    return rel_bias_l.astype(F32)[:, ridx]


def _band_attention(qkv, bias_ext, col0, tq=256):
    s = qkv.shape[0]
    w = BRANCH_WIDTH
    assert 2 * tq >= LOOKBACK_CHUNKS * CHUNK and tq % CHUNK == 0 and tq & (tq - 1) == 0
    kern = functools.partial(_band_kernel, tq=tq)

    def kv_spec(back, col):
        return pl.BlockSpec((tq, w), lambda i: (jnp.maximum(i - back, 0), col))

    return pl.pallas_call(
        kern,
        grid=(s // tq,),
        in_specs=[pl.BlockSpec((tq, w), lambda i: (i, col0)),
                  kv_spec(2, col0 + 1), kv_spec(1, col0 + 1), kv_spec(0, col0 + 1),
                  kv_spec(2, col0 + 2), kv_spec(1, col0 + 2), kv_spec(0, col0 + 2),
                  pl.BlockSpec((N_HEADS, 4 * tq), lambda i: (0, 0))],
        out_specs=pl.BlockSpec((tq, w), lambda i: (i, 0)),
        out_shape=jax.ShapeDtypeStruct((s, w), BF16),
        scratch_shapes=[pltpu.VMEM((N_HEADS, tq, 3 * tq), F32),
                        pltpu.VMEM((N_HEADS, 3 * tq, 2 * HEAD_DIM), BF16)],
        compiler_params=_cparams(("arbitrary",)),
        name="band_attention",
    )(qkv, qkv, qkv, qkv, qkv, qkv, qkv, bias_ext)


def _gelu_tanh(x):
    c = math.sqrt(2.0 / math.pi)
    return 0.5 * x * (1.0 + jnp.tanh(c * (x + 0.044715 * (x * x * x))))


def _lru_begin_tile(first, xext_sc, carry_sc, *, tm):
    halo = SUBLANES

    @pl.when(first)
    def _first():
        xext_sc[0:halo, :] = jnp.zeros((halo, BRANCH_WIDTH), F32)
        carry_sc[...] = jnp.zeros(carry_sc.shape, F32)

    @pl.when(jnp.logical_not(first))
    def _shift_halo():
        xext_sc[0:halo, :] = xext_sc[tm:tm + halo, :]


def _lru_gates(rx_ref, cw_ref, cb_ref, wr_ref, wi_ref, xext_sc, *, tm):
    halo = SUBLANES
    xext_sc[halo:halo + tm, :] = rx_ref[...]
    xext = xext_sc[0:halo + tm, :]
    xc = cb_ref[...] + rx_ref[...] * cw_ref[CONV_WIDTH - 1:CONV_WIDTH, :]
    for back in range(1, CONV_WIDTH):
        t = CONV_WIDTH - 1 - back
        xc = xc + pltpu.roll(xext, back, axis=0)[halo:, :] * cw_ref[t:t + 1, :]
    xcb = xc.astype(BF16)
    r_parts, i_parts = [], []
    for n in range(N_HEADS):
        ns = slice(n * HEAD_DIM, (n + 1) * HEAD_DIM)
        r_parts.append(jnp.dot(xcb[:, ns], wr_ref[n], preferred_element_type=F32))
        i_parts.append(jnp.dot(xcb[:, ns], wi_ref[n], preferred_element_type=F32))
    return xc, jnp.concatenate(r_parts, axis=1), jnp.concatenate(i_parts, axis=1)


def _lru_coefficients(xc, r_pre, i_pre, br_ref, bi_ref, lam_ref, a_sc, b_sc):
    r = jax.nn.sigmoid(r_pre + br_ref[...])
    gi = jax.nn.sigmoid(i_pre + bi_ref[...])
    log_a = LRU_C * r * _log_sigmoid(lam_ref[...])
    a = jnp.exp(log_a)
    a_sc[...] = a
    b_sc[...] = jnp.sqrt(-jnp.tanh(log_a) * (a * a + 1.0)) * (gi * xc)


def _lru_recurrence(groups, a_sc, b_sc, h_sc, carry_sc):
    w = BRANCH_WIDTH
    row = lax.broadcasted_iota(jnp.int32, (SUBLANES, w), 0)
    carry = carry_sc[...]
    for g in groups:
        rows = slice(g * SUBLANES, (g + 1) * SUBLANES)
        a = a_sc[rows, :]
        b = b_sc[rows, :]
        for k in (1, 2, 4):
            a_prev = pltpu.roll(a, k, axis=0)
            b_prev = pltpu.roll(b, k, axis=0)
            ok = row >= k
            b = jnp.where(ok, a * b_prev + b, b)
            a = jnp.where(ok, a * a_prev, a)
        hgrp = a * carry + b
        h_sc[rows, :] = hgrp
        carry = jnp.broadcast_to(hgrp[SUBLANES - 1:SUBLANES, :], (SUBLANES, w))
    carry_sc[...] = carry


def _proj_lru_kernel(x_ref, w_ref, s_ref, rx_ref, ry_ref, cw_ref, cb_ref, wr_ref, br_ref, wi_ref,
                     bi_ref, lam_ref, qkv_ref, olru_ref, xext_sc, a_sc, b_sc, h_sc, carry_sc, *, tm_lru):
    first = jnp.logical_and(pl.program_id(0) == 0, pl.program_id(1) == 0)
    tn = qkv_ref.shape[1]
    bounds = [0, tn // 3] + [tn // 3 + (k + 1) * (2 * tn // 9) for k in range(3)]
    groups = tm_lru // SUBLANES

    def project(k):
        cols = slice(bounds[k], bounds[k + 1])
        acc = lax.dot_general(x_ref[...], w_ref[cols, :], (((1,), (1,)), ((), ())),
                              preferred_element_type=F32)
        qkv_ref[:, cols] = (acc * s_ref[:, cols]).astype(qkv_ref.dtype)

    _lru_begin_tile(first, xext_sc, carry_sc, tm=tm_lru)
    project(0)
    xc, r_pre, i_pre = _lru_gates(rx_ref, cw_ref, cb_ref, wr_ref, wi_ref, xext_sc, tm=tm_lru)
    project(1)
    _lru_coefficients(xc, r_pre, i_pre, br_ref, bi_ref, lam_ref, a_sc, b_sc)
    project(2)
    _lru_recurrence(range(groups // 2), a_sc, b_sc, h_sc, carry_sc)
    project(3)
    _lru_recurrence(range(groups // 2, groups), a_sc, b_sc, h_sc, carry_sc)
    olru_ref[...] = (h_sc[...] * _gelu_tanh(ry_ref[...])).astype(olru_ref.dtype)


def _project_qkv_and_recur(xb, w_all, layer, colscale, uf, conv_w, conv_b, w_r, b_r, w_i, b_i, lam,
                           tm=1024, tn=2304, tm_lru=512):
    m, k = xb.shape
    n = w_all.shape[1]
    w = BRANCH_WIDTH
    nj = n // tn
    assert tm == nj * tm_lru
    kern = functools.partial(_proj_lru_kernel, tm_lru=tm_lru)
    row = lambda v: v.reshape(1, w)
    full2 = lambda shape: pl.BlockSpec(shape, lambda i, j: (0, 0))
    full3 = lambda shape: pl.BlockSpec(shape, lambda i, j: (0, 0, 0))
    return pl.pallas_call(
        kern,
        grid=(m // tm, nj),
        in_specs=[pl.BlockSpec((tm, k), lambda i, j: (i, 0)),
                  pl.BlockSpec((None, tn, k), lambda i, j: (layer, j, 0)),
                  pl.BlockSpec((1, tn), lambda i, j: (0, j)),
                  pl.BlockSpec((tm_lru, w), lambda i, j: (i * nj + j, 0)),
                  pl.BlockSpec((tm_lru, w), lambda i, j: (i * nj + j, 1)),
                  full2((CONV_WIDTH, w)), full2((1, w)),
                  full3((N_HEADS, HEAD_DIM, HEAD_DIM)), full2((1, w)),
                  full3((N_HEADS, HEAD_DIM, HEAD_DIM)), full2((1, w)),
                  full2((1, w))],
        out_specs=[pl.BlockSpec((tm, tn), lambda i, j: (i, j)),
                   pl.BlockSpec((tm_lru, w), lambda i, j: (i * nj + j, 0))],
        out_shape=[jax.ShapeDtypeStruct((m, n), BF16), jax.ShapeDtypeStruct((m, w), BF16)],
        scratch_shapes=[pltpu.VMEM((tm_lru + 2 * SUBLANES, w), F32),
                        pltpu.VMEM((tm_lru, w), F32),
                        pltpu.VMEM((tm_lru, w), F32),
                        pltpu.VMEM((tm_lru, w), F32),
                        pltpu.VMEM((SUBLANES, w), F32)],
        compiler_params=_cparams(("arbitrary", "arbitrary")),
        name="in_proj_qkv_recurrent",
    )(xb, w_all, colscale, uf, uf, conv_w, row(conv_b), w_r.astype(BF16), row(b_r),
      w_i.astype(BF16), row(b_i), row(lam))


def _merge_kernel(x_ref, o0_ref, o1_ref, o2_ref, o3_ref, wg_ref, bg_ref, wb_ref, wo_ref, w1_ref, w2_ref,
                  out_ref, wo_bf_ref, w1_bf_ref, w2_bf_ref):
    wo_bf_ref[...] = wo_ref[...].astype(BF16)
    w1_bf_ref[...] = w1_ref[...].astype(BF16)
    w2_bf_ref[...] = w2_ref[...].astype(BF16)
    x = x_ref[...]
    merged = None
    for g, o_ref in enumerate((o0_ref, o1_ref, o2_ref, o3_ref)):
        gate = jax.nn.sigmoid(jnp.dot(x, wg_ref[g], preferred_element_type=F32) + bg_ref[g:g + 1, :])
        term = gate * jnp.dot(o_ref[...], wb_ref[g], preferred_element_type=F32)
        merged = term if merged is None else merged + term
    out_ref[...] = merged.astype(out_ref.dtype)


def _merge(xb, branches, wg, bg_all, wb, layer, later_weights, tm=1024):
    s, d = xb.shape
    w = BRANCH_WIDTH
    nj, _, _, tn = wg.shape
    ni = s // tm
    o_spec = pl.BlockSpec((tm, w), lambda i, j: (i, 0))
    cast_in, cast_out, cast_shapes = [], [], []
    for arr, column_major in later_weights:
        _, r, c = arr.shape
        cast_in.append(pl.BlockSpec((None, r // ni, c // nj), lambda i, j: (layer, i, j)))
        if column_major:
            cast_out.append(pl.BlockSpec((None, r // ni, c // nj), lambda i, j: (j, i, 0)))
            cast_shapes.append(jax.ShapeDtypeStruct((nj, r, c // nj), BF16))
        else:
            cast_out.append(pl.BlockSpec((r // ni, c // nj), lambda i, j: (i, j)))
            cast_shapes.append(jax.ShapeDtypeStruct((r, c), BF16))
    return pl.pallas_call(
        _merge_kernel,
        grid=(ni, nj),
        in_specs=[pl.BlockSpec((tm, d), lambda i, j: (i, 0)),
                  o_spec, o_spec, o_spec, o_spec,
                  pl.BlockSpec((None, N_BRANCH, d, tn), lambda i, j: (j, 0, 0, 0)),
                  pl.BlockSpec((None, N_BRANCH, tn), lambda i, j: (layer, 0, j)),
                  pl.BlockSpec((None, N_BRANCH, w, tn), lambda i, j: (j, 0, 0, 0))] + cast_in,
        out_specs=[pl.BlockSpec((tm, tn), lambda i, j: (i, j))] + cast_out,
        out_shape=[jax.ShapeDtypeStruct((s, d), BF16)] + cast_shapes,
        compiler_params=_cparams(("parallel", "arbitrary")),
        name="gated_merge",
    )(xb, *branches, wg, bg_all, wb, *[arr for arr, _ in later_weights])


def _outproj_kernel(m_ref, w_ref, x_ref, pre_g_ref, pre_b_ref, g_ref, b_ref, of_ref, ob_ref, *, prenorm):
    half = m_ref.shape[0] // 2
    rows = (slice(0, half), slice(half, 2 * half))
    proj = [jnp.dot(m_ref[r, :], w_ref[...], preferred_element_type=F32) for r in rows]
    for r, p in zip(rows, proj):
        x = x_ref[r, :]
        if prenorm:
            x = _layer_norm_rows(x, pre_g_ref[...], pre_b_ref[...])
        y = _layer_norm_rows(ALPHA * x + p, g_ref[...], b_ref[...])
        of_ref[r, :] = y
        ob_ref[r, :] = y.astype(BF16)


def _outproj_ln(merged, w_out, x, pre_g, pre_b, prenorm, g, b, tm=512):
    s, d = x.shape
    row_spec = pl.BlockSpec((tm, d), lambda i: (i, 0))
    vec_spec = pl.BlockSpec((1, d), lambda i: (0, 0))
    return pl.pallas_call(
        functools.partial(_outproj_kernel, prenorm=prenorm),
        grid=(s // tm,),
        in_specs=[row_spec, pl.BlockSpec((d, d), lambda i: (0, 0)), row_spec, vec_spec, vec_spec,
                  vec_spec, vec_spec],
        out_specs=[row_spec, row_spec],
        out_shape=[jax.ShapeDtypeStruct((s, d), F32), jax.ShapeDtypeStruct((s, d), BF16)],
        compiler_params=_cparams(("parallel",)),
        name="outproj_ln",
    )(merged, w_out, x, pre_g.reshape(1, d), pre_b.reshape(1, d), g.reshape(1, d), b.reshape(1, d))


def _ffn_kernel(xb_ref, xf_ref, w1_ref, w2_ref, g_ref, b_ref, of_ref, ob_ref, acc_sc):
    f = pl.program_id(1)
    last = pl.num_programs(1) - 1

    def hidden():
        hid = jnp.maximum(jnp.dot(xb_ref[...], w1_ref[...], preferred_element_type=F32), 0.0)
        return (hid * hid).astype(BF16)

    @pl.when(f == 0)
    def _first():
        acc_sc[...] = jnp.dot(hidden(), w2_ref[...], preferred_element_type=F32)

    @pl.when(jnp.logical_and(f > 0, f < last))
    def _middle():
        acc_sc[...] += jnp.dot(hidden(), w2_ref[...], preferred_element_type=F32)

    @pl.when(f == last)
    def _last():
        hid = hidden()
        half = hid.shape[0] // 2
        rows = (slice(0, half), slice(half, 2 * half))
        down = [jnp.dot(hid[r, :], w2_ref[...], preferred_element_type=F32) for r in rows]
        for r, dn in zip(rows, down):
            y = _layer_norm_rows(ALPHA * xf_ref[r, :] + (acc_sc[r, :] + dn), g_ref[...], b_ref[...])
            of_ref[r, :] = y
            ob_ref[r, :] = y.astype(BF16)


def _ffn_ln(xb, xf, w1, w2, g, b, tm=512):
    s, d = xf.shape
    nf, _, tf = w1.shape
    row_spec = pl.BlockSpec((tm, d), lambda i, f: (i, 0))
    vec_spec = pl.BlockSpec((1, d), lambda i, f: (0, 0))
    return pl.pallas_call(
        _ffn_kernel,
        grid=(s // tm, nf),
        in_specs=[row_spec,
                  pl.BlockSpec((tm, d), lambda i, f: (jnp.where(f >= nf // 2, i, jnp.maximum(i - 1, 0)), 0)),
                  pl.BlockSpec((None, d, tf), lambda i, f: (f, 0, 0)),
                  pl.BlockSpec((tf, d), lambda i, f: (f, 0)),
                  vec_spec, vec_spec],
        out_specs=[row_spec, row_spec],
        out_shape=[jax.ShapeDtypeStruct((s, d), F32), jax.ShapeDtypeStruct((s, d), BF16)],
        scratch_shapes=[pltpu.VMEM((tm, d), F32)],
        compiler_params=_cparams(("parallel", "arbitrary")),
        name="ffn_ln",
    )(xb, xf, w1, w2, g.reshape(1, d), b.reshape(1, d))


def _split_in_proj(w_in):
    w_t = jnp.swapaxes(w_in, 1, 2)
    qkv = jnp.concatenate([w_t[:, _OFF_FQ:_OFF_FF], w_t[:, _OFF_SQ:_OFF_END]], axis=1)
    pad = jnp.zeros((w_t.shape[0], LANES - N_HEADS, w_t.shape[2]), w_in.dtype)
    rest = jnp.concatenate([w_t[:, _OFF_RX:_OFF_SQ], w_t[:, _OFF_FF:_OFF_RX], pad], axis=1)
    return qkv.astype(BF16), rest.astype(BF16)


def kernel(x, ln_in_g, ln_in_b, w_in, b_forget, conv_w, conv_b, w_r, b_r, w_i, b_i, lru_lambda,
           rel_bias, w_branch, w_gate, b_gate, w_out, ln1_g, ln1_b, w_ff1, w_ff2, ln2_g, ln2_b):
    batch, s, d = x.shape
    assert (batch, s, d) == (1, SEQ, D_MODEL)
    w = BRANCH_WIDTH
    band_tq = 256
    merge_blocks = 8

    col_scale = np.ones((1, 9 * w), np.float32)
    for q_block in (0, 3, 6):
        col_scale[:, q_block * w:(q_block + 1) * w] = QK_SCALE * LOG2_E
    qkv_scale = jnp.asarray(col_scale)
    rest_scale = jnp.ones((1, 2 * w + LANES), F32)

    w_qkv, w_rest = _split_in_proj(w_in)
    w_gate_rows = w_gate.reshape(DEPTH, N_BRANCH * d, d)
    w_branch_rows = w_branch.reshape(DEPTH, N_BRANCH * w, d)

    xf = x.reshape(s, d)
    xb = _entry_ln(xf, ln_in_g, ln_in_b)
    for l in range(DEPTH):
        uf = _project(xb, w_rest, l, rest_scale, F32, 1024, 1152, "in_proj_rest")
        qkv, o_lru = _project_qkv_and_recur(xb, w_qkv, l, qkv_scale, uf, conv_w[l], conv_b[l], w_r[l],
                                            b_r[l], w_i[l], b_i[l], lru_lambda[l])

        f_rows = uf[:, 2 * w:2 * w + N_HEADS].T.reshape(N_HEADS * (s // LANES), LANES)
        b_rows = jnp.repeat(b_forget[l].astype(F32), s // LANES).reshape(-1, 1)
        cf = _forget_cumsum(f_rows, b_rows).reshape(N_HEADS, s)
        cfk = jnp.pad(cf, ((0, SUBLANES - N_HEADS), (0, 0)))

        o_fox, wg_b, wb_b = _fox_attention(qkv, cfk, 0, l, (w_gate_rows, w_branch_rows), merge_blocks)
        o_sb = _sb_attention(qkv, 3)
        o_ch = _band_attention(qkv, _band_bias_vector(rel_bias[l], band_tq), 6, band_tq)

        merged, wo_b, w1_b, w2_b = _merge(
            xb, (o_fox, o_lru, o_sb, o_ch),
            wg_b.reshape(merge_blocks, N_BRANCH, d, d // merge_blocks), b_gate,
            wb_b.reshape(merge_blocks, N_BRANCH, w, d // merge_blocks), l,
            ((w_out, False), (w_ff1, True), (w_ff2, False)))
        xf, xb = _outproj_ln(merged, wo_b, xf, ln_in_g, ln_in_b, l == 0, ln1_g[l], ln1_b[l])
        xf, xb = _ffn_ln(xb, xf, w1_b, w2_b, ln2_g[l], ln2_b[l])
    return xf.reshape(batch, s, d)
```

```python
import functools
import math

import jax
import jax.numpy as jnp
import numpy as np
from jax import lax
from jax.experimental import pallas as pl
from jax.experimental.pallas import tpu as pltpu

F32 = jnp.float32
BF16 = jnp.bfloat16

D_MODEL = 2048
SEQ = 8192
DEPTH = 2
CHUNK = 64
HEAD_DIM = 128
N_BRANCH = 4
BRANCH_WIDTH = D_MODEL // N_BRANCH
N_HEADS = BRANCH_WIDTH // HEAD_DIM
CONV_WIDTH = 4
LRU_C = 8.0
LOOKBACK_CHUNKS = 8
REL_CLIP = 256
D_FF = 4 * D_MODEL
ALPHA = (2.0 * DEPTH) ** 0.25
LN_EPS = 1e-5
QK_SCALE = HEAD_DIM ** -0.5
LOG2_E = math.log2(math.e)

_OFF_FQ = 0
_OFF_FK = _OFF_FQ + BRANCH_WIDTH
_OFF_FV = _OFF_FK + BRANCH_WIDTH
_OFF_FF = _OFF_FV + BRANCH_WIDTH
_OFF_RX = _OFF_FF + N_HEADS
_OFF_RY = _OFF_RX + BRANCH_WIDTH
_OFF_SQ = _OFF_RY + BRANCH_WIDTH
_OFF_CQ = _OFF_SQ + 3 * BRANCH_WIDTH
_OFF_END = _OFF_CQ + 3 * BRANCH_WIDTH

LANES = 128
SUBLANES = 8
NEG_BIG = -1e30
SB_DEAD_LOG2 = -180.0
FOX_DEAD_LOG2 = -152.0

VMEM_LIMIT = 56 * 1024 * 1024


def _cparams(sem, vmem=VMEM_LIMIT):
    return pltpu.CompilerParams(dimension_semantics=sem, vmem_limit_bytes=vmem)


def _log_sigmoid(x):
    return jnp.minimum(x, 0.0) - jnp.log1p(jnp.exp(-jnp.abs(x)))


def _layer_norm_rows(y, g, b):
    mu = jnp.mean(y, axis=-1, keepdims=True)
    d = y - mu
    var = jnp.mean(d * d, axis=-1, keepdims=True)
    return d * lax.rsqrt(var + LN_EPS) * g + b


def _ln_kernel(x_ref, g_ref, b_ref, ob_ref):
    ob_ref[...] = _layer_norm_rows(x_ref[...], g_ref[...], b_ref[...]).astype(BF16)


def _entry_ln(x, g, b, tm=512):
    s, d = x.shape
    return pl.pallas_call(
        _ln_kernel,
        grid=(s // tm,),
        in_specs=[pl.BlockSpec((tm, d), lambda i: (i, 0)),
                  pl.BlockSpec((1, d), lambda i: (0, 0)),
                  pl.BlockSpec((1, d), lambda i: (0, 0))],
        out_specs=pl.BlockSpec((tm, d), lambda i: (i, 0)),
        out_shape=jax.ShapeDtypeStruct((s, d), BF16),
        compiler_params=_cparams(("parallel",)),
        name="entry_ln",
    )(x, g.reshape(1, d), b.reshape(1, d))


def _proj_kernel(x_ref, w_ref, s_ref, o_ref):
    acc = jnp.dot(x_ref[...], w_ref[...], preferred_element_type=F32)
    o_ref[...] = (acc * s_ref[...]).astype(o_ref.dtype)


def _project(xb, w_all, layer, colscale, out_dtype, tm, tn, name):
    m, k = xb.shape
    n = w_all.shape[2]
    return pl.pallas_call(
        _proj_kernel,
        grid=(m // tm, n // tn),
        in_specs=[pl.BlockSpec((tm, k), lambda i, j: (i, 0)),
                  pl.BlockSpec((None, k, tn), lambda i, j: (layer, 0, j)),
                  pl.BlockSpec((1, tn), lambda i, j: (0, j))],
        out_specs=pl.BlockSpec((tm, tn), lambda i, j: (i, j)),
        out_shape=jax.ShapeDtypeStruct((m, n), out_dtype),
        compiler_params=_cparams(("parallel", "arbitrary")),
        name=name,
    )(xb, w_all, colscale)


def _forget_cumsum_kernel(f_ref, b_ref, o_ref):
    rows = f_ref.shape[0]
    per_head = rows // N_HEADS
    ls = _log_sigmoid(f_ref[...] + b_ref[...])
    r = lax.broadcasted_iota(jnp.int32, (LANES, LANES), 0)
    c = lax.broadcasted_iota(jnp.int32, (LANES, LANES), 1)
    upper = (r <= c).astype(F32)
    within = jnp.dot(ls, upper, preferred_element_type=F32,
                     precision=lax.Precision.HIGHEST)
    total = within[:, LANES - 1:LANES]
    rr = lax.broadcasted_iota(jnp.int32, (rows, rows), 0)
    cc = lax.broadcasted_iota(jnp.int32, (rows, rows), 1)
    head_start = rr - (rr & (per_head - 1))
    before = ((cc >= head_start) & (cc < rr)).astype(F32)
    offs = jnp.dot(before, jnp.broadcast_to(total, (rows, LANES)),
                   preferred_element_type=F32, precision=lax.Precision.HIGHEST)
    o_ref[...] = (within + offs) * LOG2_E


def _forget_cumsum(f_rows, b_rows):
    rows = f_rows.shape[0]
    return pl.pallas_call(
        _forget_cumsum_kernel,
        out_shape=jax.ShapeDtypeStruct((rows, LANES), F32),
        name="forget_cumsum",
    )(f_rows, b_rows)


def _fox_kernel(q_ref, k_ref, v_ref, cfk_ref, wg_ref, wb_ref, o_ref, wg_bf_ref, wb_bf_ref,
                m_sc, acc_sc, vaug_sc, qn_sc, kn_sc, *, tq, tk):
    i = pl.program_id(0)
    for src, dst in ((wg_ref, wg_bf_ref), (wb_ref, wb_bf_ref)):
        width = dst.shape[2]
        for c in range(dst.shape[0]):
            dst[c] = src[:, c * width:(c + 1) * width].astype(BF16)
    heads = [slice(h * HEAD_DIM, (h + 1) * HEAD_DIM) for h in range(N_HEADS)]
    norm_rows = 1024

    @pl.when(i == 0)
    def _largest_key_norm():
        for h, hs in enumerate(heads):
            def chunk(c, best, hs=hs):
                rows = k_ref[pl.ds(pl.multiple_of(c * norm_rows, norm_rows), norm_rows), hs].astype(F32)
                return jnp.maximum(best, jnp.max(jnp.sum(rows * rows, axis=-1, keepdims=True)))
            best = lax.fori_loop(0, k_ref.shape[0] // norm_rows, chunk, jnp.zeros((SUBLANES, LANES), F32))
            kn_sc[h] = jnp.sqrt(best)

    m_sc[...] = jnp.full(m_sc.shape, NEG_BIG, F32)
    acc_sc[...] = jnp.zeros(acc_sc.shape, F32)
    vaug_sc[:, :, HEAD_DIM:] = jnp.ones((N_HEADS, tk, HEAD_DIM), BF16)
    for h, hs in enumerate(heads):
        q = q_ref[:, hs].astype(F32)
        qn_sc[h] = jnp.broadcast_to(jnp.sqrt(jnp.sum(q * q, axis=-1, keepdims=True)), (tq, LANES))
    rep = tk // LANES

    def key_tile(j, masked, hlist):
        k0 = pl.multiple_of(j * tk, tk)
        if masked:
            keep = (lax.broadcasted_iota(jnp.int32, (tq, tk), 1)
                    <= lax.broadcasted_iota(jnp.int32, (tq, tk), 0))
        scores = [lax.dot_general(q_ref[:, heads[h]], k_ref[pl.ds(k0, tk), heads[h]],
                                  (((1,), (1,)), ((), ())), preferred_element_type=F32) for h in hlist]
        probs, alphas = [], []
        for h, s in zip(hlist, scores):
            s = s - cfk_ref[h:h + 1, pl.ds(k0, tk)]
            if masked:
                s = jnp.where(keep, s, NEG_BIG)
            m_old = m_sc[h]
            m_new = jnp.maximum(m_old, jnp.max(s, axis=-1, keepdims=True))
            alphas.append(jnp.exp2(m_old - m_new))
            probs.append(jnp.exp2(s - jnp.concatenate([m_new] * rep, axis=1)).astype(BF16))
            m_sc[h] = m_new
            vaug_sc[h, :, :HEAD_DIM] = v_ref[pl.ds(k0, tk), heads[h]]
        for h, p, alpha in zip(hlist, probs, alphas):
            pv = jnp.dot(p, vaug_sc[h], preferred_element_type=F32)
            acc_sc[h] = jnp.concatenate([alpha, alpha], axis=1) * acc_sc[h] + pv

    def alive(j, hlist):
        newest = pl.multiple_of(jnp.maximum(j, 0) * tk + tk - LANES, LANES)
        reach = None
        for h in hlist:
            decay = -cfk_ref[h:h + 1, pl.ds(newest, LANES)][:, LANES - 1:]
            bound = jnp.max(qn_sc[h] * kn_sc[h, 0:1, :] + decay - m_sc[h], axis=0, keepdims=True)
            reach = bound if reach is None else jnp.minimum(reach, bound)
        return (jnp.max(reach) > FOX_DEAD_LOG2).astype(jnp.int32)

    def walk_back(j_start, hlist):
        def earlier(state):
            j, _ = state
            key_tile(j, False, hlist)
            return j - 1, alive(j - 1, hlist)
        return lax.while_loop(lambda st: jnp.logical_and(st[0] >= 0, st[1] > 0), earlier,
                              (j_start, alive(j_start, hlist)))[0]

    all_heads = list(range(N_HEADS))
    key_tile(i, True, all_heads)
    j_split = walk_back(i - 1, all_heads)
    for h in all_heads:
        walk_back(j_split, [h])
    for h in range(N_HEADS):
        hs = slice(h * HEAD_DIM, (h + 1) * HEAD_DIM)
        o_ref[:, hs] = (acc_sc[h, :, :HEAD_DIM] / acc_sc[h, :, HEAD_DIM:]).astype(o_ref.dtype)


def _fox_attention(qkv, cfk, col0, layer, later_weights, col_blocks, tq=512):
    s = qkv.shape[0]
    tk = tq
    w = BRANCH_WIDTH
    steps = s // tq
    kern = functools.partial(_fox_kernel, tq=tq, tk=tk)
    resident = pl.Buffered(1)
    cast_in, cast_out, cast_shapes = [], [], []
    for arr in later_weights:
        _, r, c = arr.shape
        cast_in.append(pl.BlockSpec((None, r // steps, c), lambda i: (layer, i, 0)))
        cast_out.append(pl.BlockSpec((col_blocks, r // steps, c // col_blocks), lambda i: (0, i, 0)))
        cast_shapes.append(jax.ShapeDtypeStruct((col_blocks, r, c // col_blocks), BF16))
    return pl.pallas_call(
        kern,
        grid=(steps,),
        in_specs=[pl.BlockSpec((tq, w), lambda i: (i, col0)),
                  pl.BlockSpec((s, w), lambda i: (0, col0 + 1), pipeline_mode=resident),
                  pl.BlockSpec((s, w), lambda i: (0, col0 + 2), pipeline_mode=resident),
                  pl.BlockSpec((SUBLANES, s), lambda i: (0, 0), pipeline_mode=resident)] + cast_in,
        out_specs=[pl.BlockSpec((tq, w), lambda i: (i, 0))] + cast_out,
        out_shape=[jax.ShapeDtypeStruct((s, w), BF16)] + cast_shapes,
        scratch_shapes=[pltpu.VMEM((N_HEADS, tq, LANES), F32),
                        pltpu.VMEM((N_HEADS, tq, 2 * HEAD_DIM), F32),
                        pltpu.VMEM((N_HEADS, tk, 2 * HEAD_DIM), BF16),
                        pltpu.VMEM((N_HEADS, tq, LANES), F32),
                        pltpu.VMEM((N_HEADS, SUBLANES, LANES), F32)],
        compiler_params=_cparams(("arbitrary",)),
        name="fox_attention",
    )(qkv, qkv, qkv, cfk, *later_weights)


def _sb_kernel(q_ref, k_ref, v_ref, o_ref, run_sc, acc_sc, *, tq):
    i = pl.program_id(0)
    nsub = tq // LANES
    r = lax.broadcasted_iota(jnp.int32, (2 * LANES, 2 * LANES), 0) & (LANES - 1)
    c = lax.broadcasted_iota(jnp.int32, (2 * LANES, 2 * LANES), 1)
    tri_aug = ((c >= LANES) | (r > c)).astype(BF16)

    def sub_blocks(items, masked):
        scores = []
        for h, k0, r0 in items:
            hs = slice(h * HEAD_DIM, (h + 1) * HEAD_DIM)
            scores.append(lax.dot_general(q_ref[r0:, hs], k_ref[pl.ds(k0, LANES), hs],
                                          (((1,), (1,)), ((), ())), preferred_element_type=F32))
        log_beta, sums, keeps = [], [], []
        for (h, k0, r0), z in zip(items, scores):
            rows = tq - r0
            lp = jnp.minimum(z, 0.0) - jnp.log2(1.0 + jnp.exp2(-jnp.abs(z)))
            ln = lp - z
            keep = None
            if masked:
                keep = (lax.broadcasted_iota(jnp.int32, (rows, LANES), 1)
                        < lax.broadcasted_iota(jnp.int32, (rows, LANES), 0))
                ln = jnp.where(keep, ln, 0.0)
            ln_hi = ln.astype(BF16)
            ln_lo = (ln - ln_hi.astype(F32)).astype(BF16)
            log_beta.append(lp)
            keeps.append(keep)
            sums.append(jnp.dot(jnp.concatenate([ln_hi, ln_lo], axis=1), tri_aug,
                                preferred_element_type=F32))
        weights = []
        for (h, k0, r0), lp, la, keep in zip(items, log_beta, sums, keeps):
            run = run_sc[h, r0:, :]
            a = jnp.exp2(lp + la[:, :LANES] + run)
            if masked:
                a = jnp.where(keep, a, 0.0)
            run_sc[h, r0:, :] = run + la[:, LANES:]
            weights.append(a.astype(BF16))
        for (h, k0, r0), a in zip(items, weights):
            hs = slice(h * HEAD_DIM, (h + 1) * HEAD_DIM)
            acc_sc[h, r0:, :] += jnp.dot(a, v_ref[pl.ds(k0, LANES), hs], preferred_element_type=F32)

    run_sc[...] = jnp.zeros(run_sc.shape, F32)
    acc_sc[...] = jnp.zeros(acc_sc.shape, F32)
    sub_blocks([(h, pl.multiple_of(i * tq + cc * LANES, LANES), cc * LANES)
                for cc in range(nsub - 1, -1, -1) for h in range(N_HEADS)], True)

    def alive():
        return (jnp.max(run_sc[...]) > SB_DEAD_LOG2).astype(jnp.int32)

    def earlier(state):
        cb, _ = state
        sub_blocks([(h, pl.multiple_of((cb - back) * LANES, LANES), 0)
                    for back in range(2) for h in range(N_HEADS)], False)
        return cb - 2, alive()

    lax.while_loop(lambda st: jnp.logical_and(st[0] >= 0, st[1] > 0), earlier,
                   (i * nsub - 1, alive()))
    for h in range(N_HEADS):
        o_ref[:, h * HEAD_DIM:(h + 1) * HEAD_DIM] = acc_sc[h].astype(o_ref.dtype)


def _sb_attention(qkv, col0, tq=256):
    s = qkv.shape[0]
    w = BRANCH_WIDTH
    assert (tq // LANES) % 2 == 0
    kern = functools.partial(_sb_kernel, tq=tq)
    resident = pl.Buffered(1)
    return pl.pallas_call(
        kern,
        grid=(s // tq,),
        in_specs=[pl.BlockSpec((tq, w), lambda i: (i, col0)),
                  pl.BlockSpec((s, w), lambda i: (0, col0 + 1), pipeline_mode=resident),
                  pl.BlockSpec((s, w), lambda i: (0, col0 + 2), pipeline_mode=resident)],
        out_specs=pl.BlockSpec((tq, w), lambda i: (i, 0)),
        out_shape=jax.ShapeDtypeStruct((s, w), BF16),
        scratch_shapes=[pltpu.VMEM((N_HEADS, tq, LANES), F32),
                        pltpu.VMEM((N_HEADS, tq, HEAD_DIM), F32)],
        compiler_params=_cparams(("arbitrary",)),
        name="sb_attention",
    )(qkv, qkv, qkv)


def _band_kernel(q_ref, k2_ref, k1_ref, k0_ref, v2_ref, v1_ref, v0_ref, ext_ref, o_ref,
                 bias_sc, vaug_sc, *, tq):
    i = pl.program_id(0)
    width = 4 * tq

    @pl.when(i == 0)
    def _build_tables():
        trow = lax.broadcasted_iota(jnp.int32, (tq, width), 0)
        t = lax.broadcasted_iota(jnp.int32, (tq, 3 * tq), 0)
        s = lax.broadcasted_iota(jnp.int32, (tq, 3 * tq), 1)
        shift = CHUNK.bit_length() - 1
        t_chunk = t >> shift
        s_chunk = (s >> shift) - (2 * tq) // CHUNK
        in_band = (t_chunk - s_chunk <= LOOKBACK_CHUNKS) & (s_chunk <= t_chunk)
        for h in range(N_HEADS):
            x = jnp.broadcast_to(ext_ref[h:h + 1, :], (tq, width))
            for b in range(tq.bit_length() - 1):
                x = jnp.where(((trow >> b) & 1) == 1, pltpu.roll(x, 1 << b, axis=1), x)
            bias_sc[h] = jnp.where(in_band, x[:, :3 * tq] * LOG2_E, NEG_BIG)
        vaug_sc[:, :, HEAD_DIM:] = jnp.ones((N_HEADS, 3 * tq, HEAD_DIM), BF16)

    k_refs = (k2_ref, k1_ref, k0_ref)
    v_refs = (v2_ref, v1_ref, v0_ref)
    heads = [slice(h * HEAD_DIM, (h + 1) * HEAD_DIM) for h in range(N_HEADS)]

    def tile(first_tiles):
        scores = [[lax.dot_general(q_ref[:, hs], k_refs[p][:, hs], (((1,), (1,)), ((), ())),
                                   preferred_element_type=F32) for p in range(3)] for hs in heads]
        probs = []
        for h, pieces in enumerate(scores):
            pieces = [s + bias_sc[h, :, p * tq:(p + 1) * tq] for p, s in enumerate(pieces)]
            if first_tiles:
                pieces = [jnp.where(i - 2 + p >= 0, s, NEG_BIG) for p, s in enumerate(pieces)]
            m = jnp.max(jnp.maximum(jnp.maximum(pieces[0], pieces[1]), pieces[2]),
                        axis=-1, keepdims=True)
            probs.append(jnp.concatenate([jnp.exp2(s - m).astype(BF16) for s in pieces], axis=1))
            for p in range(3):
                vaug_sc[h, p * tq:(p + 1) * tq, :HEAD_DIM] = v_refs[p][:, heads[h]]
        for h, p in enumerate(probs):
            acc = jnp.dot(p, vaug_sc[h], preferred_element_type=F32)
            o_ref[:, heads[h]] = (acc[:, :HEAD_DIM] / acc[:, HEAD_DIM:]).astype(o_ref.dtype)

    @pl.when(i < 2)
    def _first_tiles():
        tile(True)

    @pl.when(i >= 2)
    def _other_tiles():
        tile(False)


def _band_bias_vector(rel_bias_l, tq):
    n = np.arange(4 * tq)
    dist = np.where(n < 3 * tq, 2 * tq - n, 6 * tq - n)
    ridx = np.clip(dist, -(CHUNK - 1), REL_CLIP) + (CHUNK - 1)
    return rel_bias_l.astype(F32)[:, ridx]


def _band_attention(qkv, bias_ext, col0, tq=256):
    s = qkv.shape[0]
    w = BRANCH_WIDTH
    assert 2 * tq >= LOOKBACK_CHUNKS * CHUNK and tq % CHUNK == 0 and tq & (tq - 1) == 0
    kern = functools.partial(_band_kernel, tq=tq)

    def kv_spec(back, col):
        return pl.BlockSpec((tq, w), lambda i: (jnp.maximum(i - back, 0), col))

    return pl.pallas_call(
        kern,
        grid=(s // tq,),
        in_specs=[pl.BlockSpec((tq, w), lambda i: (i, col0)),
                  kv_spec(2, col0 + 1), kv_spec(1, col0 + 1), kv_spec(0, col0 + 1),
                  kv_spec(2, col0 + 2), kv_spec(1, col0 + 2), kv_spec(0, col0 + 2),
                  pl.BlockSpec((N_HEADS, 4 * tq), lambda i: (0, 0))],
        out_specs=pl.BlockSpec((tq, w), lambda i: (i, 0)),
        out_shape=jax.ShapeDtypeStruct((s, w), BF16),
        scratch_shapes=[pltpu.VMEM((N_HEADS, tq, 3 * tq), F32),
                        pltpu.VMEM((N_HEADS, 3 * tq, 2 * HEAD_DIM), BF16)],
        compiler_params=_cparams(("arbitrary",)),
        name="band_attention",
    )(qkv, qkv, qkv, qkv, qkv, qkv, qkv, bias_ext)


def _gelu_tanh(x):
    c = math.sqrt(2.0 / math.pi)
    return 0.5 * x * (1.0 + jnp.tanh(c * (x + 0.044715 * (x * x * x))))


def _lru_begin_tile(first, xext_sc, carry_sc, *, tm):
    halo = SUBLANES

    @pl.when(first)
    def _first():
        xext_sc[0:halo, :] = jnp.zeros((halo, BRANCH_WIDTH), F32)
        carry_sc[...] = jnp.zeros(carry_sc.shape, F32)

    @pl.when(jnp.logical_not(first))
    def _shift_halo():
        xext_sc[0:halo, :] = xext_sc[tm:tm + halo, :]


def _lru_gates(rx_ref, cw_ref, cb_ref, wr_ref, wi_ref, xext_sc, *, tm):
    halo = SUBLANES
    xext_sc[halo:halo + tm, :] = rx_ref[...]
    xext = xext_sc[0:halo + tm, :]
    xc = cb_ref[...] + rx_ref[...] * cw_ref[CONV_WIDTH - 1:CONV_WIDTH, :]
    for back in range(1, CONV_WIDTH):
        t = CONV_WIDTH - 1 - back
        xc = xc + pltpu.roll(xext, back, axis=0)[halo:, :] * cw_ref[t:t + 1, :]
    xcb = xc.astype(BF16)
    r_parts, i_parts = [], []
    for n in range(N_HEADS):
        ns = slice(n * HEAD_DIM, (n + 1) * HEAD_DIM)
        r_parts.append(jnp.dot(xcb[:, ns], wr_ref[n], preferred_element_type=F32))
        i_parts.append(jnp.dot(xcb[:, ns], wi_ref[n], preferred_element_type=F32))
    return xc, jnp.concatenate(r_parts, axis=1), jnp.concatenate(i_parts, axis=1)


def _lru_coefficients(xc, r_pre, i_pre, br_ref, bi_ref, lam_ref, a_sc, b_sc):
    r = jax.nn.sigmoid(r_pre + br_ref[...])
    gi = jax.nn.sigmoid(i_pre + bi_ref[...])
    log_a = LRU_C * r * _log_sigmoid(lam_ref[...])
    a = jnp.exp(log_a)
    a_sc[...] = a
    b_sc[...] = jnp.sqrt(-jnp.tanh(log_a) * (a * a + 1.0)) * (gi * xc)


def _lru_recurrence(groups, a_sc, b_sc, h_sc, carry_sc):
    w = BRANCH_WIDTH
    row = lax.broadcasted_iota(jnp.int32, (SUBLANES, w), 0)
    carry = carry_sc[...]
    for g in groups:
        rows = slice(g * SUBLANES, (g + 1) * SUBLANES)
        a = a_sc[rows, :]
        b = b_sc[rows, :]
        for k in (1, 2, 4):
            a_prev = pltpu.roll(a, k, axis=0)
            b_prev = pltpu.roll(b, k, axis=0)
            ok = row >= k
            b = jnp.where(ok, a * b_prev + b, b)
            a = jnp.where(ok, a * a_prev, a)
        hgrp = a * carry + b
        h_sc[rows, :] = hgrp
        carry = jnp.broadcast_to(hgrp[SUBLANES - 1:SUBLANES, :], (SUBLANES, w))
    carry_sc[...] = carry


def _proj_lru_kernel(x_ref, w_ref, s_ref, rx_ref, ry_ref, cw_ref, cb_ref, wr_ref, br_ref, wi_ref,
                     bi_ref, lam_ref, qkv_ref, olru_ref, xext_sc, a_sc, b_sc, h_sc, carry_sc, *, tm_lru):
    first = jnp.logical_and(pl.program_id(0) == 0, pl.program_id(1) == 0)
    tn = qkv_ref.shape[1]
    bounds = [0, tn // 3] + [tn // 3 + (k + 1) * (2 * tn // 9) for k in range(3)]
    groups = tm_lru // SUBLANES

    def project(k):
        cols = slice(bounds[k], bounds[k + 1])
        acc = jnp.dot(x_ref[...], w_ref[:, cols], preferred_element_type=F32)
        qkv_ref[:, cols] = (acc * s_ref[:, cols]).astype(qkv_ref.dtype)

    _lru_begin_tile(first, xext_sc, carry_sc, tm=tm_lru)
    project(0)
    xc, r_pre, i_pre = _lru_gates(rx_ref, cw_ref, cb_ref, wr_ref, wi_ref, xext_sc, tm=tm_lru)
    project(1)
    _lru_coefficients(xc, r_pre, i_pre, br_ref, bi_ref, lam_ref, a_sc, b_sc)
    project(2)
    _lru_recurrence(range(groups // 2), a_sc, b_sc, h_sc, carry_sc)
    project(3)
    _lru_recurrence(range(groups // 2, groups), a_sc, b_sc, h_sc, carry_sc)
    olru_ref[...] = (h_sc[...] * _gelu_tanh(ry_ref[...])).astype(olru_ref.dtype)


def _project_qkv_and_recur(xb, w_all, layer, colscale, uf, conv_w, conv_b, w_r, b_r, w_i, b_i, lam,
                           tm=1024, tn=2304, tm_lru=512):
    m, k = xb.shape
    n = w_all.shape[2]
    w = BRANCH_WIDTH
    nj = n // tn
    assert tm == nj * tm_lru
    kern = functools.partial(_proj_lru_kernel, tm_lru=tm_lru)
    row = lambda v: v.reshape(1, w)
    full2 = lambda shape: pl.BlockSpec(shape, lambda i, j: (0, 0))
    full3 = lambda shape: pl.BlockSpec(shape, lambda i, j: (0, 0, 0))
    return pl.pallas_call(
        kern,
        grid=(m // tm, nj),
        in_specs=[pl.BlockSpec((tm, k), lambda i, j: (i, 0)),
                  pl.BlockSpec((None, k, tn), lambda i, j: (layer, 0, j)),
                  pl.BlockSpec((1, tn), lambda i, j: (0, j)),
                  pl.BlockSpec((tm_lru, w), lambda i, j: (i * nj + j, 0)),
                  pl.BlockSpec((tm_lru, w), lambda i, j: (i * nj + j, 1)),
                  full2((CONV_WIDTH, w)), full2((1, w)),
                  full3((N_HEADS, HEAD_DIM, HEAD_DIM)), full2((1, w)),
                  full3((N_HEADS, HEAD_DIM, HEAD_DIM)), full2((1, w)),
                  full2((1, w))],
        out_specs=[pl.BlockSpec((tm, tn), lambda i, j: (i, j)),
                   pl.BlockSpec((tm_lru, w), lambda i, j: (i * nj + j, 0))],
        out_shape=[jax.ShapeDtypeStruct((m, n), BF16), jax.ShapeDtypeStruct((m, w), BF16)],
        scratch_shapes=[pltpu.VMEM((tm_lru + 2 * SUBLANES, w), F32),
                        pltpu.VMEM((tm_lru, w), F32),
                        pltpu.VMEM((tm_lru, w), F32),
                        pltpu.VMEM((tm_lru, w), F32),
                        pltpu.VMEM((SUBLANES, w), F32)],
        compiler_params=_cparams(("arbitrary", "arbitrary")),
        name="in_proj_qkv_recurrent",
    )(xb, w_all, colscale, uf, uf, conv_w, row(conv_b), w_r.astype(BF16), row(b_r),
      w_i.astype(BF16), row(b_i), row(lam))


def _merge_kernel(x_ref, o0_ref, o1_ref, o2_ref, o3_ref, wg_ref, bg_ref, wb_ref, wo_ref, w1_ref, w2_ref,
                  out_ref, wo_bf_ref, w1_bf_ref, w2_bf_ref):
    wo_bf_ref[...] = wo_ref[...].astype(BF16)
    w1_bf_ref[...] = w1_ref[...].astype(BF16)
    w2_bf_ref[...] = w2_ref[...].astype(BF16)
    x = x_ref[...]
    merged = None
    for g, o_ref in enumerate((o0_ref, o1_ref, o2_ref, o3_ref)):
        gate = jax.nn.sigmoid(jnp.dot(x, wg_ref[g], preferred_element_type=F32) + bg_ref[g:g + 1, :])
        term = gate * jnp.dot(o_ref[...], wb_ref[g], preferred_element_type=F32)
        merged = term if merged is None else merged + term
    out_ref[...] = merged.astype(out_ref.dtype)


def _merge(xb, branches, wg, bg_all, wb, layer, later_weights, tm=1024):
    s, d = xb.shape
    w = BRANCH_WIDTH
    nj, _, _, tn = wg.shape
    ni = s // tm
    o_spec = pl.BlockSpec((tm, w), lambda i, j: (i, 0))
    cast_in, cast_out, cast_shapes = [], [], []
    for arr, column_major in later_weights:
        _, r, c = arr.shape
        cast_in.append(pl.BlockSpec((None, r // ni, c // nj), lambda i, j: (layer, i, j)))
        if column_major:
            cast_out.append(pl.BlockSpec((None, r // ni, c // nj), lambda i, j: (j, i, 0)))
            cast_shapes.append(jax.ShapeDtypeStruct((nj, r, c // nj), BF16))
        else:
            cast_out.append(pl.BlockSpec((r // ni, c // nj), lambda i, j: (i, j)))
            cast_shapes.append(jax.ShapeDtypeStruct((r, c), BF16))
    return pl.pallas_call(
        _merge_kernel,
        grid=(ni, nj),
        in_specs=[pl.BlockSpec((tm, d), lambda i, j: (i, 0)),
                  o_spec, o_spec, o_spec, o_spec,
                  pl.BlockSpec((None, N_BRANCH, d, tn), lambda i, j: (j, 0, 0, 0)),
                  pl.BlockSpec((None, N_BRANCH, tn), lambda i, j: (layer, 0, j)),
                  pl.BlockSpec((None, N_BRANCH, w, tn), lambda i, j: (j, 0, 0, 0))] + cast_in,
        out_specs=[pl.BlockSpec((tm, tn), lambda i, j: (i, j))] + cast_out,
        out_shape=[jax.ShapeDtypeStruct((s, d), BF16)] + cast_shapes,
        compiler_params=_cparams(("parallel", "arbitrary")),
        name="gated_merge",
    )(xb, *branches, wg, bg_all, wb, *[arr for arr, _ in later_weights])


def _outproj_kernel(m_ref, w_ref, x_ref, pre_g_ref, pre_b_ref, g_ref, b_ref, of_ref, ob_ref, *, prenorm):
    half = m_ref.shape[0] // 2
    rows = (slice(0, half), slice(half, 2 * half))
    proj = [jnp.dot(m_ref[r, :], w_ref[...], preferred_element_type=F32) for r in rows]
    for r, p in zip(rows, proj):
        x = x_ref[r, :]
        if prenorm:
            x = _layer_norm_rows(x, pre_g_ref[...], pre_b_ref[...])
        y = _layer_norm_rows(ALPHA * x + p, g_ref[...], b_ref[...])
        of_ref[r, :] = y
        ob_ref[r, :] = y.astype(BF16)


def _outproj_ln(merged, w_out, x, pre_g, pre_b, prenorm, g, b, tm=512):
    s, d = x.shape
    row_spec = pl.BlockSpec((tm, d), lambda i: (i, 0))
    vec_spec = pl.BlockSpec((1, d), lambda i: (0, 0))
    return pl.pallas_call(
        functools.partial(_outproj_kernel, prenorm=prenorm),
        grid=(s // tm,),
        in_specs=[row_spec, pl.BlockSpec((d, d), lambda i: (0, 0)), row_spec, vec_spec, vec_spec,
                  vec_spec, vec_spec],
        out_specs=[row_spec, row_spec],
        out_shape=[jax.ShapeDtypeStruct((s, d), F32), jax.ShapeDtypeStruct((s, d), BF16)],
        compiler_params=_cparams(("parallel",)),
        name="outproj_ln",
    )(merged, w_out, x, pre_g.reshape(1, d), pre_b.reshape(1, d), g.reshape(1, d), b.reshape(1, d))


def _ffn_kernel(xb_ref, xf_ref, w1_ref, w2_ref, g_ref, b_ref, of_ref, ob_ref, acc_sc):
    f = pl.program_id(1)
    last = pl.num_programs(1) - 1

    def hidden():
        hid = jnp.maximum(jnp.dot(xb_ref[...], w1_ref[...], preferred_element_type=F32), 0.0)
        return (hid * hid).astype(BF16)

    @pl.when(f == 0)
    def _first():
        acc_sc[...] = jnp.dot(hidden(), w2_ref[...], preferred_element_type=F32)

    @pl.when(jnp.logical_and(f > 0, f < last))
    def _middle():
        acc_sc[...] += jnp.dot(hidden(), w2_ref[...], preferred_element_type=F32)

    @pl.when(f == last)
    def _last():
        hid = hidden()
        half = hid.shape[0] // 2
        rows = (slice(0, half), slice(half, 2 * half))
        down = [jnp.dot(hid[r, :], w2_ref[...], preferred_element_type=F32) for r in rows]
        for r, dn in zip(rows, down):
            y = _layer_norm_rows(ALPHA * xf_ref[r, :] + (acc_sc[r, :] + dn), g_ref[...], b_ref[...])
            of_ref[r, :] = y
            ob_ref[r, :] = y.astype(BF16)


def _ffn_ln(xb, xf, w1, w2, g, b, tm=512):
    s, d = xf.shape
    nf, _, tf = w1.shape
    row_spec = pl.BlockSpec((tm, d), lambda i, f: (i, 0))
    vec_spec = pl.BlockSpec((1, d), lambda i, f: (0, 0))
    return pl.pallas_call(
        _ffn_kernel,
        grid=(s // tm, nf),
        in_specs=[row_spec,
                  pl.BlockSpec((tm, d), lambda i, f: (jnp.where(f >= nf // 2, i, jnp.maximum(i - 1, 0)), 0)),
                  pl.BlockSpec((None, d, tf), lambda i, f: (f, 0, 0)),
                  pl.BlockSpec((tf, d), lambda i, f: (f, 0)),
                  vec_spec, vec_spec],
        out_specs=[row_spec, row_spec],
        out_shape=[jax.ShapeDtypeStruct((s, d), F32), jax.ShapeDtypeStruct((s, d), BF16)],
        scratch_shapes=[pltpu.VMEM((tm, d), F32)],
        compiler_params=_cparams(("parallel", "arbitrary")),
        name="ffn_ln",
    )(xb, xf, w1, w2, g.reshape(1, d), b.reshape(1, d))


def _split_in_proj(w_in):
    qkv = jnp.concatenate([w_in[:, :, _OFF_FQ:_OFF_FF], w_in[:, :, _OFF_SQ:_OFF_END]], axis=2)
    pad = jnp.zeros(w_in.shape[:2] + (LANES - N_HEADS,), w_in.dtype)
    rest = jnp.concatenate([w_in[:, :, _OFF_RX:_OFF_SQ], w_in[:, :, _OFF_FF:_OFF_RX], pad], axis=2)
    return qkv.astype(BF16), rest.astype(BF16)


def kernel(x, ln_in_g, ln_in_b, w_in, b_forget, conv_w, conv_b, w_r, b_r, w_i, b_i, lru_lambda,
           rel_bias, w_branch, w_gate, b_gate, w_out, ln1_g, ln1_b, w_ff1, w_ff2, ln2_g, ln2_b):
    batch, s, d = x.shape
    assert (batch, s, d) == (1, SEQ, D_MODEL)
    w = BRANCH_WIDTH
    band_tq = 256
    merge_blocks = 8

    col_scale = np.ones((1, 9 * w), np.float32)
    for q_block in (0, 3, 6):
        col_scale[:, q_block * w:(q_block + 1) * w] = QK_SCALE * LOG2_E
    qkv_scale = jnp.asarray(col_scale)
    rest_scale = jnp.ones((1, 2 * w + LANES), F32)

    w_qkv, w_rest = _split_in_proj(w_in)
    w_gate_rows = w_gate.reshape(DEPTH, N_BRANCH * d, d)
    w_branch_rows = w_branch.reshape(DEPTH, N_BRANCH * w, d)

    xf = x.reshape(s, d)
    xb = _entry_ln(xf, ln_in_g, ln_in_b)
    for l in range(DEPTH):
        uf = _project(xb, w_rest, l, rest_scale, F32, 1024, 1152, "in_proj_rest")
        qkv, o_lru = _project_qkv_and_recur(xb, w_qkv, l, qkv_scale, uf, conv_w[l], conv_b[l], w_r[l],
                                            b_r[l], w_i[l], b_i[l], lru_lambda[l])

        f_rows = uf[:, 2 * w:2 * w + N_HEADS].T.reshape(N_HEADS * (s // LANES), LANES)
        b_rows = jnp.repeat(b_forget[l].astype(F32), s // LANES).reshape(-1, 1)
        cf = _forget_cumsum(f_rows, b_rows).reshape(N_HEADS, s)
        cfk = jnp.pad(cf, ((0, SUBLANES - N_HEADS), (0, 0)))

        o_fox, wg_b, wb_b = _fox_attention(qkv, cfk, 0, l, (w_gate_rows, w_branch_rows), merge_blocks)
        o_sb = _sb_attention(qkv, 3)
        o_ch = _band_attention(qkv, _band_bias_vector(rel_bias[l], band_tq), 6, band_tq)

        merged, wo_b, w1_b, w2_b = _merge(
            xb, (o_fox, o_lru, o_sb, o_ch),
            wg_b.reshape(merge_blocks, N_BRANCH, d, d // merge_blocks), b_gate,
            wb_b.reshape(merge_blocks, N_BRANCH, w, d // merge_blocks), l,
            ((w_out, False), (w_ff1, True), (w_ff2, False)))
        xf, xb = _outproj_ln(merged, wo_b, xf, ln_in_g, ln_in_b, l == 0, ln1_g[l], ln1_b[l])
        xf, xb = _ffn_ln(xb, xf, w1_b, w2_b, ln2_g[l], ln2_b[l])
    return xf.reshape(batch, s, d)
```

```python
import functools
import math

import jax
import jax.numpy as jnp
import numpy as np
from jax import lax
from jax.experimental import pallas as pl
from jax.experimental.pallas import tpu as pltpu

F32 = jnp.float32
BF16 = jnp.bfloat16

D_MODEL = 2048
SEQ = 8192
DEPTH = 2
CHUNK = 64
HEAD_DIM = 128
N_BRANCH = 4
BRANCH_WIDTH = D_MODEL // N_BRANCH
N_HEADS = BRANCH_WIDTH // HEAD_DIM
CONV_WIDTH = 4
LRU_C = 8.0
LOOKBACK_CHUNKS = 8
REL_CLIP = 256
D_FF = 4 * D_MODEL
ALPHA = (2.0 * DEPTH) ** 0.25
LN_EPS = 1e-5
QK_SCALE = HEAD_DIM ** -0.5
LOG2_E = math.log2(math.e)

_OFF_FQ = 0
_OFF_FK = _OFF_FQ + BRANCH_WIDTH
_OFF_FV = _OFF_FK + BRANCH_WIDTH
_OFF_FF = _OFF_FV + BRANCH_WIDTH
_OFF_RX = _OFF_FF + N_HEADS
_OFF_RY = _OFF_RX + BRANCH_WIDTH
_OFF_SQ = _OFF_RY + BRANCH_WIDTH
_OFF_CQ = _OFF_SQ + 3 * BRANCH_WIDTH
_OFF_END = _OFF_CQ + 3 * BRANCH_WIDTH

LANES = 128
SUBLANES = 8
NEG_BIG = -1e30
SB_DEAD_LOG2 = -180.0
FOX_DEAD_LOG2 = -152.0

VMEM_LIMIT = 56 * 1024 * 1024


def _cparams(sem, vmem=VMEM_LIMIT):
    return pltpu.CompilerParams(dimension_semantics=sem, vmem_limit_bytes=vmem)


def _log_sigmoid(x):
    return jnp.minimum(x, 0.0) - jnp.log1p(jnp.exp(-jnp.abs(x)))


def _layer_norm_rows(y, g, b):
    mu = jnp.mean(y, axis=-1, keepdims=True)
    d = y - mu
    var = jnp.mean(d * d, axis=-1, keepdims=True)
    return d * lax.rsqrt(var + LN_EPS) * g + b


def _ln_kernel(x_ref, g_ref, b_ref, ob_ref):
    ob_ref[...] = _layer_norm_rows(x_ref[...], g_ref[...], b_ref[...]).astype(BF16)


def _entry_ln(x, g, b, tm=512):
    s, d = x.shape
    return pl.pallas_call(
        _ln_kernel,
        grid=(s // tm,),
        in_specs=[pl.BlockSpec((tm, d), lambda i: (i, 0)),
                  pl.BlockSpec((1, d), lambda i: (0, 0)),
                  pl.BlockSpec((1, d), lambda i: (0, 0))],
        out_specs=pl.BlockSpec((tm, d), lambda i: (i, 0)),
        out_shape=jax.ShapeDtypeStruct((s, d), BF16),
        compiler_params=_cparams(("parallel",)),
        name="entry_ln",
    )(x, g.reshape(1, d), b.reshape(1, d))


def _proj_kernel(x_ref, w_ref, s_ref, o_ref):
    acc = jnp.dot(x_ref[...], w_ref[...], preferred_element_type=F32)
    o_ref[...] = (acc * s_ref[...]).astype(o_ref.dtype)


def _project(xb, w_all, layer, colscale, out_dtype, tm, tn, name):
    m, k = xb.shape
    n = w_all.shape[2]
    return pl.pallas_call(
        _proj_kernel,
        grid=(m // tm, n // tn),
        in_specs=[pl.BlockSpec((tm, k), lambda i, j: (i, 0)),
                  pl.BlockSpec((None, k, tn), lambda i, j: (layer, 0, j)),
                  pl.BlockSpec((1, tn), lambda i, j: (0, j))],
        out_specs=pl.BlockSpec((tm, tn), lambda i, j: (i, j)),
        out_shape=jax.ShapeDtypeStruct((m, n), out_dtype),
        compiler_params=_cparams(("parallel", "arbitrary")),
        name=name,
    )(xb, w_all, colscale)


def _forget_cumsum_kernel(f_ref, b_ref, o_ref):
    rows = f_ref.shape[0]
    per_head = rows // N_HEADS
    ls = _log_sigmoid(f_ref[...] + b_ref[...])
    r = lax.broadcasted_iota(jnp.int32, (LANES, LANES), 0)
    c = lax.broadcasted_iota(jnp.int32, (LANES, LANES), 1)
    upper = (r <= c).astype(F32)
    within = jnp.dot(ls, upper, preferred_element_type=F32,
                     precision=lax.Precision.HIGHEST)
    total = within[:, LANES - 1:LANES]
    rr = lax.broadcasted_iota(jnp.int32, (rows, rows), 0)
    cc = lax.broadcasted_iota(jnp.int32, (rows, rows), 1)
    head_start = rr - (rr & (per_head - 1))
    before = ((cc >= head_start) & (cc < rr)).astype(F32)
    offs = jnp.dot(before, jnp.broadcast_to(total, (rows, LANES)),
                   preferred_element_type=F32, precision=lax.Precision.HIGHEST)
    o_ref[...] = (within + offs) * LOG2_E


def _forget_cumsum(f_rows, b_rows):
    rows = f_rows.shape[0]
    return pl.pallas_call(
        _forget_cumsum_kernel,
        out_shape=jax.ShapeDtypeStruct((rows, LANES), F32),
        name="forget_cumsum",
    )(f_rows, b_rows)


def _fox_kernel(q_ref, k_ref, v_ref, cfk_ref, wg_ref, wb_ref, o_ref, wg_bf_ref, wb_bf_ref,
                m_sc, acc_sc, vaug_sc, qn_sc, kn_sc, *, tq, tk):
    i = pl.program_id(0)
    for src, dst in ((wg_ref, wg_bf_ref), (wb_ref, wb_bf_ref)):
        width = dst.shape[2]
        for c in range(dst.shape[0]):
            dst[c] = src[:, c * width:(c + 1) * width].astype(BF16)
    heads = [slice(h * HEAD_DIM, (h + 1) * HEAD_DIM) for h in range(N_HEADS)]
    norm_rows = 1024

    @pl.when(i == 0)
    def _largest_key_norm():
        for h, hs in enumerate(heads):
            def chunk(c, best, hs=hs):
                rows = k_ref[pl.ds(pl.multiple_of(c * norm_rows, norm_rows), norm_rows), hs].astype(F32)
                return jnp.maximum(best, jnp.max(jnp.sum(rows * rows, axis=-1, keepdims=True)))
            best = lax.fori_loop(0, k_ref.shape[0] // norm_rows, chunk, jnp.zeros((SUBLANES, LANES), F32))
            kn_sc[h] = jnp.sqrt(best)

    m_sc[...] = jnp.full(m_sc.shape, NEG_BIG, F32)
    acc_sc[...] = jnp.zeros(acc_sc.shape, F32)
    vaug_sc[:, :, HEAD_DIM:] = jnp.ones((N_HEADS, tk, HEAD_DIM), BF16)
    for h, hs in enumerate(heads):
        q = q_ref[:, hs].astype(F32)
        qn_sc[h] = jnp.broadcast_to(jnp.sqrt(jnp.sum(q * q, axis=-1, keepdims=True)), (tq, LANES))
    rep = tk // LANES

    def key_tile(j, masked, hlist):
        k0 = pl.multiple_of(j * tk, tk)
        if masked:
            keep = (lax.broadcasted_iota(jnp.int32, (tq, tk), 1)
                    <= lax.broadcasted_iota(jnp.int32, (tq, tk), 0))
        scores = [lax.dot_general(q_ref[:, heads[h]], k_ref[pl.ds(k0, tk), heads[h]],
                                  (((1,), (1,)), ((), ())), preferred_element_type=F32) for h in hlist]
        probs, alphas = [], []
        for h, s in zip(hlist, scores):
            s = s - cfk_ref[h:h + 1, pl.ds(k0, tk)]
            if masked:
                s = jnp.where(keep, s, NEG_BIG)
            m_old = m_sc[h]
            m_new = jnp.maximum(m_old, jnp.max(s, axis=-1, keepdims=True))
            alphas.append(jnp.exp2(m_old - m_new))
            probs.append(jnp.exp2(s - jnp.concatenate([m_new] * rep, axis=1)).astype(BF16))
            m_sc[h] = m_new
            vaug_sc[h, :, :HEAD_DIM] = v_ref[pl.ds(k0, tk), heads[h]]
        for h, p, alpha in zip(hlist, probs, alphas):
            pv = jnp.dot(p, vaug_sc[h], preferred_element_type=F32)
            acc_sc[h] = jnp.concatenate([alpha, alpha], axis=1) * acc_sc[h] + pv

    def alive(j, hlist):
        newest = pl.multiple_of(jnp.maximum(j, 0) * tk + tk - LANES, LANES)
        reach = None
        for h in hlist:
            decay = -cfk_ref[h:h + 1, pl.ds(newest, LANES)][:, LANES - 1:]
            bound = jnp.max(qn_sc[h] * kn_sc[h, 0:1, :] + decay - m_sc[h], axis=0, keepdims=True)
            reach = bound if reach is None else jnp.minimum(reach, bound)
        return (jnp.max(reach) > FOX_DEAD_LOG2).astype(jnp.int32)

    def walk_back(j_start, hlist):
        def earlier(state):
            j, _ = state
            key_tile(j, False, hlist)
            return j - 1, alive(j - 1, hlist)
        return lax.while_loop(lambda st: jnp.logical_and(st[0] >= 0, st[1] > 0), earlier,
                              (j_start, alive(j_start, hlist)))[0]

    all_heads = list(range(N_HEADS))
    key_tile(i, True, all_heads)
    j_split = walk_back(i - 1, all_heads)
    for h in all_heads:
        walk_back(j_split, [h])
    for h in range(N_HEADS):
        hs = slice(h * HEAD_DIM, (h + 1) * HEAD_DIM)
        o_ref[:, hs] = (acc_sc[h, :, :HEAD_DIM] / acc_sc[h, :, HEAD_DIM:]).astype(o_ref.dtype)


def _fox_attention(qkv, cfk, col0, layer, later_weights, col_blocks, tq=512):
    s = qkv.shape[0]
    tk = tq
    w = BRANCH_WIDTH
    steps = s // tq
    kern = functools.partial(_fox_kernel, tq=tq, tk=tk)
    resident = pl.Buffered(1)
    cast_in, cast_out, cast_shapes = [], [], []
    for arr in later_weights:
        _, r, c = arr.shape
        cast_in.append(pl.BlockSpec((None, r // steps, c), lambda i: (layer, i, 0)))
        cast_out.append(pl.BlockSpec((col_blocks, r // steps, c // col_blocks), lambda i: (0, i, 0)))
        cast_shapes.append(jax.ShapeDtypeStruct((col_blocks, r, c // col_blocks), BF16))
    return pl.pallas_call(
        kern,
        grid=(steps,),
        in_specs=[pl.BlockSpec((tq, w), lambda i: (i, col0)),
                  pl.BlockSpec((s, w), lambda i: (0, col0 + 1), pipeline_mode=resident),
                  pl.BlockSpec((s, w), lambda i: (0, col0 + 2), pipeline_mode=resident),
                  pl.BlockSpec((SUBLANES, s), lambda i: (0, 0), pipeline_mode=resident)] + cast_in,
        out_specs=[pl.BlockSpec((tq, w), lambda i: (i, 0))] + cast_out,
        out_shape=[jax.ShapeDtypeStruct((s, w), BF16)] + cast_shapes,
        scratch_shapes=[pltpu.VMEM((N_HEADS, tq, LANES), F32),
                        pltpu.VMEM((N_HEADS, tq, 2 * HEAD_DIM), F32),
                        pltpu.VMEM((N_HEADS, tk, 2 * HEAD_DIM), BF16),
                        pltpu.VMEM((N_HEADS, tq, LANES), F32),
                        pltpu.VMEM((N_HEADS, SUBLANES, LANES), F32)],
        compiler_params=_cparams(("arbitrary",)),
        name="fox_attention",
    )(qkv, qkv, qkv, cfk, *later_weights)


def _sb_kernel(q_ref, k_ref, v_ref, o_ref, run_sc, acc_sc, *, tq):
    i = pl.program_id(0)
    nsub = tq // LANES
    r = lax.broadcasted_iota(jnp.int32, (2 * LANES, 2 * LANES), 0) & (LANES - 1)
    c = lax.broadcasted_iota(jnp.int32, (2 * LANES, 2 * LANES), 1)
    tri_aug = ((c >= LANES) | (r > c)).astype(BF16)

    def sub_blocks(items, masked):
        scores = []
        for h, k0, r0 in items:
            hs = slice(h * HEAD_DIM, (h + 1) * HEAD_DIM)
            scores.append(lax.dot_general(q_ref[r0:, hs], k_ref[pl.ds(k0, LANES), hs],
                                          (((1,), (1,)), ((), ())), preferred_element_type=F32))
        log_beta, sums, keeps = [], [], []
        for (h, k0, r0), z in zip(items, scores):
            rows = tq - r0
            lp = jnp.minimum(z, 0.0) - jnp.log2(1.0 + jnp.exp2(-jnp.abs(z)))
            ln = lp - z
            keep = None
            if masked:
                keep = (lax.broadcasted_iota(jnp.int32, (rows, LANES), 1)
                        < lax.broadcasted_iota(jnp.int32, (rows, LANES), 0))
                ln = jnp.where(keep, ln, 0.0)
            ln_hi = ln.astype(BF16)
            ln_lo = (ln - ln_hi.astype(F32)).astype(BF16)
            log_beta.append(lp)
            keeps.append(keep)
            sums.append(jnp.dot(jnp.concatenate([ln_hi, ln_lo], axis=1), tri_aug,
                                preferred_element_type=F32))
        weights = []
        for (h, k0, r0), lp, la, keep in zip(items, log_beta, sums, keeps):
            run = run_sc[h, r0:, :]
            a = jnp.exp2(lp + la[:, :LANES] + run)
            if masked:
                a = jnp.where(keep, a, 0.0)
            run_sc[h, r0:, :] = run + la[:, LANES:]
            weights.append(a.astype(BF16))
        for (h, k0, r0), a in zip(items, weights):
            hs = slice(h * HEAD_DIM, (h + 1) * HEAD_DIM)
            acc_sc[h, r0:, :] += jnp.dot(a, v_ref[pl.ds(k0, LANES), hs], preferred_element_type=F32)

    run_sc[...] = jnp.zeros(run_sc.shape, F32)
    acc_sc[...] = jnp.zeros(acc_sc.shape, F32)
    sub_blocks([(h, pl.multiple_of(i * tq + cc * LANES, LANES), cc * LANES)
                for cc in range(nsub - 1, -1, -1) for h in range(N_HEADS)], True)

    def alive():
        return (jnp.max(run_sc[...]) > SB_DEAD_LOG2).astype(jnp.int32)

    def earlier(state):
        cb, _ = state
        sub_blocks([(h, pl.multiple_of((cb - back) * LANES, LANES), 0)
                    for back in range(2) for h in range(N_HEADS)], False)
        return cb - 2, alive()

    lax.while_loop(lambda st: jnp.logical_and(st[0] >= 0, st[1] > 0), earlier,
                   (i * nsub - 1, alive()))
    for h in range(N_HEADS):
        o_ref[:, h * HEAD_DIM:(h + 1) * HEAD_DIM] = acc_sc[h].astype(o_ref.dtype)


def _sb_attention(qkv, col0, tq=256):
    s = qkv.shape[0]
    w = BRANCH_WIDTH
    assert (tq // LANES) % 2 == 0
    kern = functools.partial(_sb_kernel, tq=tq)
    resident = pl.Buffered(1)
    return pl.pallas_call(
        kern,
        grid=(s // tq,),
        in_specs=[pl.BlockSpec((tq, w), lambda i: (i, col0)),
                  pl.BlockSpec((s, w), lambda i: (0, col0 + 1), pipeline_mode=resident),
                  pl.BlockSpec((s, w), lambda i: (0, col0 + 2), pipeline_mode=resident)],
        out_specs=pl.BlockSpec((tq, w), lambda i: (i, 0)),
        out_shape=jax.ShapeDtypeStruct((s, w), BF16),
        scratch_shapes=[pltpu.VMEM((N_HEADS, tq, LANES), F32),
                        pltpu.VMEM((N_HEADS, tq, HEAD_DIM), F32)],
        compiler_params=_cparams(("arbitrary",)),
        name="sb_attention",
    )(qkv, qkv, qkv)


def _band_kernel(q_ref, k2_ref, k1_ref, k0_ref, v2_ref, v1_ref, v0_ref, ext_ref, o_ref,
                 bias_sc, vaug_sc, *, tq):
    i = pl.program_id(0)
    width = 4 * tq

    @pl.when(i == 0)
    def _build_tables():
        t = lax.broadcasted_iota(jnp.int32, (tq, 3 * tq), 0)
        s = lax.broadcasted_iota(jnp.int32, (tq, 3 * tq), 1)
        shift = CHUNK.bit_length() - 1
        t_chunk = t >> shift
        s_chunk = (s >> shift) - (2 * tq) // CHUNK
        in_band = (t_chunk - s_chunk <= LOOKBACK_CHUNKS) & (s_chunk <= t_chunk)
        for h in range(N_HEADS):
            x = jnp.broadcast_to(ext_ref[h:h + 1, :], (tq, width))
            x = pltpu.roll(x, 0, axis=1, stride=1, stride_axis=0)
            bias_sc[h] = jnp.where(in_band, x[:, :3 * tq] * LOG2_E, NEG_BIG)
        vaug_sc[:, :, HEAD_DIM:] = jnp.ones((N_HEADS, 3 * tq, HEAD_DIM), BF16)

    k_refs = (k2_ref, k1_ref, k0_ref)
    v_refs = (v2_ref, v1_ref, v0_ref)
    heads = [slice(h * HEAD_DIM, (h + 1) * HEAD_DIM) for h in range(N_HEADS)]

    def tile(first_tiles):
        scores = [[lax.dot_general(q_ref[:, hs], k_refs[p][:, hs], (((1,), (1,)), ((), ())),
                                   preferred_element_type=F32) for p in range(3)] for hs in heads]
        probs = []
        for h, pieces in enumerate(scores):
            pieces = [s + bias_sc[h, :, p * tq:(p + 1) * tq] for p, s in enumerate(pieces)]
            if first_tiles:
                pieces = [jnp.where(i - 2 + p >= 0, s, NEG_BIG) for p, s in enumerate(pieces)]
            m = jnp.max(jnp.maximum(jnp.maximum(pieces[0], pieces[1]), pieces[2]),
                        axis=-1, keepdims=True)
            probs.append(jnp.concatenate([jnp.exp2(s - m).astype(BF16) for s in pieces], axis=1))
            for p in range(3):
                vaug_sc[h, p * tq:(p + 1) * tq, :HEAD_DIM] = v_refs[p][:, heads[h]]
        for h, p in enumerate(probs):
            acc = jnp.dot(p, vaug_sc[h], preferred_element_type=F32)
            o_ref[:, heads[h]] = (acc[:, :HEAD_DIM] / acc[:, HEAD_DIM:]).astype(o_ref.dtype)

    @pl.when(i < 2)
    def _first_tiles():
        tile(True)

    @pl.when(i >= 2)
    def _other_tiles():
        tile(False)


def _band_bias_vector(rel_bias_l, tq):
    n = np.arange(4 * tq)
    dist = np.where(n < 3 * tq, 2 * tq - n, 6 * tq - n)
    ridx = np.clip(dist, -(CHUNK - 1), REL_CLIP) + (CHUNK - 1)
    return rel_bias_l.astype(F32)[:, ridx]


def _band_attention(qkv, bias_ext, col0, tq=256):
    s = qkv.shape[0]
    w = BRANCH_WIDTH
    assert 2 * tq >= LOOKBACK_CHUNKS * CHUNK and tq % CHUNK == 0 and tq & (tq - 1) == 0
    kern = functools.partial(_band_kernel, tq=tq)

    def kv_spec(back, col):
        return pl.BlockSpec((tq, w), lambda i: (jnp.maximum(i - back, 0), col))

    return pl.pallas_call(
        kern,
        grid=(s // tq,),
        in_specs=[pl.BlockSpec((tq, w), lambda i: (i, col0)),
                  kv_spec(2, col0 + 1), kv_spec(1, col0 + 1), kv_spec(0, col0 + 1),
                  kv_spec(2, col0 + 2), kv_spec(1, col0 + 2), kv_spec(0, col0 + 2),
                  pl.BlockSpec((N_HEADS, 4 * tq), lambda i: (0, 0))],
        out_specs=pl.BlockSpec((tq, w), lambda i: (i, 0)),
        out_shape=jax.ShapeDtypeStruct((s, w), BF16),
        scratch_shapes=[pltpu.VMEM((N_HEADS, tq, 3 * tq), F32),
                        pltpu.VMEM((N_HEADS, 3 * tq, 2 * HEAD_DIM), BF16)],
        compiler_params=_cparams(("arbitrary",)),
        name="band_attention",
    )(qkv, qkv, qkv, qkv, qkv, qkv, qkv, bias_ext)


def _gelu_tanh(x):
    c = math.sqrt(2.0 / math.pi)
    return 0.5 * x * (1.0 + jnp.tanh(c * (x + 0.044715 * (x * x * x))))


def _lru_begin_tile(first, xext_sc, carry_sc, *, tm):
    halo = SUBLANES

    @pl.when(first)
    def _first():
        xext_sc[0:halo, :] = jnp.zeros((halo, BRANCH_WIDTH), F32)
        carry_sc[...] = jnp.zeros(carry_sc.shape, F32)

    @pl.when(jnp.logical_not(first))
    def _shift_halo():
        xext_sc[0:halo, :] = xext_sc[tm:tm + halo, :]


def _lru_gates(rx_ref, cw_ref, cb_ref, wr_ref, wi_ref, xext_sc, *, tm):
    halo = SUBLANES
    xext_sc[halo:halo + tm, :] = rx_ref[...]
    xext = xext_sc[0:halo + tm, :]
    xc = cb_ref[...] + rx_ref[...] * cw_ref[CONV_WIDTH - 1:CONV_WIDTH, :]
    for back in range(1, CONV_WIDTH):
        t = CONV_WIDTH - 1 - back
        xc = xc + pltpu.roll(xext, back, axis=0)[halo:, :] * cw_ref[t:t + 1, :]
    xcb = xc.astype(BF16)
    r_parts, i_parts = [], []
    for n in range(N_HEADS):
        ns = slice(n * HEAD_DIM, (n + 1) * HEAD_DIM)
        r_parts.append(jnp.dot(xcb[:, ns], wr_ref[n], preferred_element_type=F32))
        i_parts.append(jnp.dot(xcb[:, ns], wi_ref[n], preferred_element_type=F32))
    return xc, jnp.concatenate(r_parts, axis=1), jnp.concatenate(i_parts, axis=1)


def _lru_coefficients(xc, r_pre, i_pre, br_ref, bi_ref, lam_ref, a_sc, b_sc):
    r = jax.nn.sigmoid(r_pre + br_ref[...])
    gi = jax.nn.sigmoid(i_pre + bi_ref[...])
    log_a = LRU_C * r * _log_sigmoid(lam_ref[...])
    a = jnp.exp(log_a)
    a_sc[...] = a
    b_sc[...] = jnp.sqrt(-jnp.tanh(log_a) * (a * a + 1.0)) * (gi * xc)


def _lru_recurrence(groups, a_sc, b_sc, h_sc, carry_sc):
    w = BRANCH_WIDTH
    row = lax.broadcasted_iota(jnp.int32, (SUBLANES, w), 0)
    carry = carry_sc[...]
    for g in groups:
        rows = slice(g * SUBLANES, (g + 1) * SUBLANES)
        a = a_sc[rows, :]
        b = b_sc[rows, :]
        for k in (1, 2, 4):
            a_prev = pltpu.roll(a, k, axis=0)
            b_prev = pltpu.roll(b, k, axis=0)
            ok = row >= k
            b = jnp.where(ok, a * b_prev + b, b)
            a = jnp.where(ok, a * a_prev, a)
        hgrp = a * carry + b
        h_sc[rows, :] = hgrp
        carry = jnp.broadcast_to(hgrp[SUBLANES - 1:SUBLANES, :], (SUBLANES, w))
    carry_sc[...] = carry


def _proj_lru_kernel(x_ref, w_ref, s_ref, rx_ref, ry_ref, cw_ref, cb_ref, wr_ref, br_ref, wi_ref,
                     bi_ref, lam_ref, qkv_ref, olru_ref, xext_sc, a_sc, b_sc, h_sc, carry_sc, *, tm_lru):
    first = jnp.logical_and(pl.program_id(0) == 0, pl.program_id(1) == 0)
    tn = qkv_ref.shape[1]
    bounds = [0, tn // 3] + [tn // 3 + (k + 1) * (2 * tn // 9) for k in range(3)]
    groups = tm_lru // SUBLANES

    def project(k):
        cols = slice(bounds[k], bounds[k + 1])
        acc = jnp.dot(x_ref[...], w_ref[:, cols], preferred_element_type=F32)
        qkv_ref[:, cols] = (acc * s_ref[:, cols]).astype(qkv_ref.dtype)

    _lru_begin_tile(first, xext_sc, carry_sc, tm=tm_lru)
    project(0)
    xc, r_pre, i_pre = _lru_gates(rx_ref, cw_ref, cb_ref, wr_ref, wi_ref, xext_sc, tm=tm_lru)
    project(1)
    _lru_coefficients(xc, r_pre, i_pre, br_ref, bi_ref, lam_ref, a_sc, b_sc)
    project(2)
    _lru_recurrence(range(groups // 2), a_sc, b_sc, h_sc, carry_sc)
    project(3)
    _lru_recurrence(range(groups // 2, groups), a_sc, b_sc, h_sc, carry_sc)
    olru_ref[...] = (h_sc[...] * _gelu_tanh(ry_ref[...])).astype(olru_ref.dtype)


def _project_qkv_and_recur(xb, w_all, layer, colscale, uf, conv_w, conv_b, w_r, b_r, w_i, b_i, lam,
                           tm=1024, tn=2304, tm_lru=512):
    m, k = xb.shape
    n = w_all.shape[2]
    w = BRANCH_WIDTH
    nj = n // tn
    assert tm == nj * tm_lru
    kern = functools.partial(_proj_lru_kernel, tm_lru=tm_lru)
    row = lambda v: v.reshape(1, w)
    full2 = lambda shape: pl.BlockSpec(shape, lambda i, j: (0, 0))
    full3 = lambda shape: pl.BlockSpec(shape, lambda i, j: (0, 0, 0))
    return pl.pallas_call(
        kern,
        grid=(m // tm, nj),
        in_specs=[pl.BlockSpec((tm, k), lambda i, j: (i, 0)),
                  pl.BlockSpec((None, k, tn), lambda i, j: (layer, 0, j)),
                  pl.BlockSpec((1, tn), lambda i, j: (0, j)),
                  pl.BlockSpec((tm_lru, w), lambda i, j: (i * nj + j, 0)),
                  pl.BlockSpec((tm_lru, w), lambda i, j: (i * nj + j, 1)),
                  full2((CONV_WIDTH, w)), full2((1, w)),
                  full3((N_HEADS, HEAD_DIM, HEAD_DIM)), full2((1, w)),
                  full3((N_HEADS, HEAD_DIM, HEAD_DIM)), full2((1, w)),
                  full2((1, w))],
        out_specs=[pl.BlockSpec((tm, tn), lambda i, j: (i, j)),
                   pl.BlockSpec((tm_lru, w), lambda i, j: (i * nj + j, 0))],
        out_shape=[jax.ShapeDtypeStruct((m, n), BF16), jax.ShapeDtypeStruct((m, w), BF16)],
        scratch_shapes=[pltpu.VMEM((tm_lru + 2 * SUBLANES, w), F32),
                        pltpu.VMEM((tm_lru, w), F32),
                        pltpu.VMEM((tm_lru, w), F32),
                        pltpu.VMEM((tm_lru, w), F32),
                        pltpu.VMEM((SUBLANES, w), F32)],
        compiler_params=_cparams(("arbitrary", "arbitrary")),
        name="in_proj_qkv_recurrent",
    )(xb, w_all, colscale, uf, uf, conv_w, row(conv_b), w_r.astype(BF16), row(b_r),
      w_i.astype(BF16), row(b_i), row(lam))


def _merge_kernel(x_ref, o0_ref, o1_ref, o2_ref, o3_ref, wg_ref, bg_ref, wb_ref, wo_ref, w1_ref, w2_ref,
                  out_ref, wo_bf_ref, w1_bf_ref, w2_bf_ref):
    wo_bf_ref[...] = wo_ref[...].astype(BF16)
    w1_bf_ref[...] = w1_ref[...].astype(BF16)
    w2_bf_ref[...] = w2_ref[...].astype(BF16)
    x = x_ref[...]
    merged = None
    for g, o_ref in enumerate((o0_ref, o1_ref, o2_ref, o3_ref)):
        gate = jax.nn.sigmoid(jnp.dot(x, wg_ref[g], preferred_element_type=F32) + bg_ref[g:g + 1, :])
        term = gate * jnp.dot(o_ref[...], wb_ref[g], preferred_element_type=F32)
        merged = term if merged is None else merged + term
    out_ref[...] = merged.astype(out_ref.dtype)


def _merge(xb, branches, wg, bg_all, wb, layer, later_weights, tm=1024):
    s, d = xb.shape
    w = BRANCH_WIDTH
    nj, _, _, tn = wg.shape
    ni = s // tm
    o_spec = pl.BlockSpec((tm, w), lambda i, j: (i, 0))
    cast_in, cast_out, cast_shapes = [], [], []
    for arr, column_major in later_weights:
        _, r, c = arr.shape
        cast_in.append(pl.BlockSpec((None, r // ni, c // nj), lambda i, j: (layer, i, j)))
        if column_major:
            cast_out.append(pl.BlockSpec((None, r // ni, c // nj), lambda i, j: (j, i, 0)))
            cast_shapes.append(jax.ShapeDtypeStruct((nj, r, c // nj), BF16))
        else:
            cast_out.append(pl.BlockSpec((r // ni, c // nj), lambda i, j: (i, j)))
            cast_shapes.append(jax.ShapeDtypeStruct((r, c), BF16))
    return pl.pallas_call(
        _merge_kernel,
        grid=(ni, nj),
        in_specs=[pl.BlockSpec((tm, d), lambda i, j: (i, 0)),
                  o_spec, o_spec, o_spec, o_spec,
                  pl.BlockSpec((None, N_BRANCH, d, tn), lambda i, j: (j, 0, 0, 0)),
                  pl.BlockSpec((None, N_BRANCH, tn), lambda i, j: (layer, 0, j)),
                  pl.BlockSpec((None, N_BRANCH, w, tn), lambda i, j: (j, 0, 0, 0))] + cast_in,
        out_specs=[pl.BlockSpec((tm, tn), lambda i, j: (i, j))] + cast_out,
        out_shape=[jax.ShapeDtypeStruct((s, d), BF16)] + cast_shapes,
        compiler_params=_cparams(("parallel", "arbitrary")),
        name="gated_merge",
    )(xb, *branches, wg, bg_all, wb, *[arr for arr, _ in later_weights])


def _outproj_kernel(m_ref, w_ref, x_ref, pre_g_ref, pre_b_ref, g_ref, b_ref, of_ref, ob_ref, *, prenorm):
    half = m_ref.shape[0] // 2
    rows = (slice(0, half), slice(half, 2 * half))
    proj = [jnp.dot(m_ref[r, :], w_ref[...], preferred_element_type=F32) for r in rows]
    for r, p in zip(rows, proj):
        x = x_ref[r, :]
        if prenorm:
            x = _layer_norm_rows(x, pre_g_ref[...], pre_b_ref[...])
        y = _layer_norm_rows(ALPHA * x + p, g_ref[...], b_ref[...])
        of_ref[r, :] = y
        ob_ref[r, :] = y.astype(BF16)


def _outproj_ln(merged, w_out, x, pre_g, pre_b, prenorm, g, b, tm=512):
    s, d = x.shape
    row_spec = pl.BlockSpec((tm, d), lambda i: (i, 0))
    vec_spec = pl.BlockSpec((1, d), lambda i: (0, 0))
    return pl.pallas_call(
        functools.partial(_outproj_kernel, prenorm=prenorm),
        grid=(s // tm,),
        in_specs=[row_spec, pl.BlockSpec((d, d), lambda i: (0, 0)), row_spec, vec_spec, vec_spec,
                  vec_spec, vec_spec],
        out_specs=[row_spec, row_spec],
        out_shape=[jax.ShapeDtypeStruct((s, d), F32), jax.ShapeDtypeStruct((s, d), BF16)],
        compiler_params=_cparams(("parallel",)),
        name="outproj_ln",
    )(merged, w_out, x, pre_g.reshape(1, d), pre_b.reshape(1, d), g.reshape(1, d), b.reshape(1, d))


def _ffn_kernel(xb_ref, xf_ref, w1_ref, w2_ref, g_ref, b_ref, of_ref, ob_ref, acc_sc):
    f = pl.program_id(1)
    last = pl.num_programs(1) - 1

    def hidden():
        hid = jnp.maximum(jnp.dot(xb_ref[...], w1_ref[...], preferred_element_type=F32), 0.0)
        return (hid * hid).astype(BF16)

    @pl.when(f == 0)
    def _first():
        acc_sc[...] = jnp.dot(hidden(), w2_ref[...], preferred_element_type=F32)

    @pl.when(jnp.logical_and(f > 0, f < last))
    def _middle():
        acc_sc[...] += jnp.dot(hidden(), w2_ref[...], preferred_element_type=F32)

    @pl.when(f == last)
    def _last():
        hid = hidden()
        half = hid.shape[0] // 2
        rows = (slice(0, half), slice(half, 2 * half))
        down = [jnp.dot(hid[r, :], w2_ref[...], preferred_element_type=F32) for r in rows]
        for r, dn in zip(rows, down):
            y = _layer_norm_rows(ALPHA * xf_ref[r, :] + (acc_sc[r, :] + dn), g_ref[...], b_ref[...])
            of_ref[r, :] = y
            ob_ref[r, :] = y.astype(BF16)


def _ffn_ln(xb, xf, w1, w2, g, b, tm=512):
    s, d = xf.shape
    nf, _, tf = w1.shape
    row_spec = pl.BlockSpec((tm, d), lambda i, f: (i, 0))
    vec_spec = pl.BlockSpec((1, d), lambda i, f: (0, 0))
    return pl.pallas_call(
        _ffn_kernel,
        grid=(s // tm, nf),
        in_specs=[row_spec,
                  pl.BlockSpec((tm, d), lambda i, f: (jnp.where(f >= nf // 2, i, jnp.maximum(i - 1, 0)), 0)),
                  pl.BlockSpec((None, d, tf), lambda i, f: (f, 0, 0)),
                  pl.BlockSpec((tf, d), lambda i, f: (f, 0)),
                  vec_spec, vec_spec],
        out_specs=[row_spec, row_spec],
        out_shape=[jax.ShapeDtypeStruct((s, d), F32), jax.ShapeDtypeStruct((s, d), BF16)],
        scratch_shapes=[pltpu.VMEM((tm, d), F32)],
        compiler_params=_cparams(("parallel", "arbitrary")),
        name="ffn_ln",
    )(xb, xf, w1, w2, g.reshape(1, d), b.reshape(1, d))


def _split_in_proj(w_in):
    qkv = jnp.concatenate([w_in[:, :, _OFF_FQ:_OFF_FF], w_in[:, :, _OFF_SQ:_OFF_END]], axis=2)
    pad = jnp.zeros(w_in.shape[:2] + (LANES - N_HEADS,), w_in.dtype)
    rest = jnp.concatenate([w_in[:, :, _OFF_RX:_OFF_SQ], w_in[:, :, _OFF_FF:_OFF_RX], pad], axis=2)
    return qkv.astype(BF16), rest.astype(BF16)


def kernel(x, ln_in_g, ln_in_b, w_in, b_forget, conv_w, conv_b, w_r, b_r, w_i, b_i, lru_lambda,
           rel_bias, w_branch, w_gate, b_gate, w_out, ln1_g, ln1_b, w_ff1, w_ff2, ln2_g, ln2_b):
    batch, s, d = x.shape
    assert (batch, s, d) == (1, SEQ, D_MODEL)
    w = BRANCH_WIDTH
    band_tq = 256
    merge_blocks = 8

    col_scale = np.ones((1, 9 * w), np.float32)
    for q_block in (0, 3, 6):
        col_scale[:, q_block * w:(q_block + 1) * w] = QK_SCALE * LOG2_E
    qkv_scale = jnp.asarray(col_scale)
    rest_scale = jnp.ones((1, 2 * w + LANES), F32)

    w_qkv, w_rest = _split_in_proj(w_in)
    w_gate_rows = w_gate.reshape(DEPTH, N_BRANCH * d, d)
    w_branch_rows = w_branch.reshape(DEPTH, N_BRANCH * w, d)

    xf = x.reshape(s, d)
    xb = _entry_ln(xf, ln_in_g, ln_in_b)
    for l in range(DEPTH):
        uf = _project(xb, w_rest, l, rest_scale, F32, 1024, 1152, "in_proj_rest")
        qkv, o_lru = _project_qkv_and_recur(xb, w_qkv, l, qkv_scale, uf, conv_w[l], conv_b[l], w_r[l],
                                            b_r[l], w_i[l], b_i[l], lru_lambda[l])

        f_rows = uf[:, 2 * w:2 * w + N_HEADS].T.reshape(N_HEADS * (s // LANES), LANES)
        b_rows = jnp.repeat(b_forget[l].astype(F32), s // LANES).reshape(-1, 1)
        cf = _forget_cumsum(f_rows, b_rows).reshape(N_HEADS, s)
        cfk = jnp.pad(cf, ((0, SUBLANES - N_HEADS), (0, 0)))

        o_fox, wg_b, wb_b = _fox_attention(qkv, cfk, 0, l, (w_gate_rows, w_branch_rows), merge_blocks)
        o_sb = _sb_attention(qkv, 3)
        o_ch = _band_attention(qkv, _band_bias_vector(rel_bias[l], band_tq), 6, band_tq)

        merged, wo_b, w1_b, w2_b = _merge(
            xb, (o_fox, o_lru, o_sb, o_ch),
            wg_b.reshape(merge_blocks, N_BRANCH, d, d // merge_blocks), b_gate,
            wb_b.reshape(merge_blocks, N_BRANCH, w, d // merge_blocks), l,
            ((w_out, False), (w_ff1, True), (w_ff2, False)))
        xf, xb = _outproj_ln(merged, wo_b, xf, ln_in_g, ln_in_b, l == 0, ln1_g[l], ln1_b[l])
        xf, xb = _ffn_ln(xb, xf, w1_b, w2_b, ln2_g[l], ln2_b[l])
    return xf.reshape(batch, s, d)
```
